```python
import math
import jax, jax.numpy as jnp
from jax import lax
import numpy as np

D_MODEL = 2048
BATCH = 4
SEQ = 2048
DEPTH = 2
DEC_BATCH = 128
DEC_SEQ = 4
PAST_LEN = 16384
PAGE_SIZE = 128

N_MIXERS = 4
GROUP_W = D_MODEL // N_MIXERS
D_MIX = N_MIXERS * GROUP_W
N_IN_SLICES = 11
D_IN = N_IN_SLICES * GROUP_W
RET_HEADS = 4
RET_DK = GROUP_W // RET_HEADS
RET_DV = GROUP_W // RET_HEADS
RET_CHUNK = 128
ROPE_THETA = 10000.0
CONV_W = 31
SC_W = 3
SGU_GROUPS = 4
SGU_CH = GROUP_W // SGU_GROUPS
SGU_CHUNK = 128
D_FF = 5632
N_EXPERTS = 8
TOP_K = 2
MOE_BLOCK = 128
N_DENSE = (DEPTH + 1) // 2
N_MOE = DEPTH // 2
EPS = 1e-6

kernel_name = 'hybrid_retention_conv_sgu_moe_step'

F32 = jnp.float32


def rmsnorm(x, g):
    xf = x.astype(F32)
    y = xf * lax.rsqrt(jnp.mean(xf * xf, axis=-1, keepdims=True) + EPS)
    return (y * g.astype(F32)).astype(x.dtype)


def layernorm(x, g, b):
    xf = x.astype(F32)
    mu = jnp.mean(xf, axis=-1, keepdims=True)
    xc = xf - mu
    y = xc * lax.rsqrt(jnp.mean(xc * xc, axis=-1, keepdims=True) + EPS)
    return (y * g.astype(F32) + b.astype(F32)).astype(x.dtype)


def rotary(x, pos):
    half = x.shape[-1] // 2
    inv = ROPE_THETA ** (-jnp.arange(half, dtype=F32) / half)
    ang = pos[:, None] * inv[None, :]
    cos = jnp.cos(ang)[None, :, None, :]
    sin = jnp.sin(ang)[None, :, None, :]
    x1, x2 = x[..., :half], x[..., half:]
    return jnp.concatenate([x1 * cos - x2 * sin, x1 * sin + x2 * cos], axis=-1)


def retention(q, k, v, s0):
    Bt, L, H, _ = q.shape
    C = math.gcd(L, RET_CHUNK)
    n = L // C
    log_g = jnp.log1p(-jnp.exp2(-5.0 - jnp.arange(H, dtype=F32)))
    idx = jnp.arange(C, dtype=F32)
    rel = idx[:, None] - idx[None, :]
    decay_in = jnp.where(rel >= 0, jnp.exp(log_g[:, None, None] * jnp.maximum(rel, 0.0)), 0.0)
    decay_q = jnp.exp(log_g[:, None] * (idx + 1.0)[None, :]).T[None, :, :, None]
    decay_k = jnp.exp(log_g[:, None] * (C - 1.0 - idx)[None, :]).T[None, :, :, None]
    decay_c = jnp.exp(log_g * C)[None, :, None, None]

    def chunk_step(S, qkv):
        qc, kc, vc = qkv
        scores = jnp.einsum('bihd,bjhd->bhij', qc, kc) * decay_in[None]
        inner = jnp.einsum('bhij,bjhe->bihe', scores, vc)
        cross = jnp.einsum('bihd,bhde->bihe', qc, S) * decay_q
        S_new = S * decay_c + jnp.einsum('bjhd,bjhe->bhde', kc * decay_k, vc)
        return S_new, inner + cross

    def split(t):
        return jnp.moveaxis(t.reshape(Bt, n, C, H, t.shape[-1]), 1, 0)

    S, out = lax.scan(chunk_step, s0, (split(q), split(k), split(v)))
    out = jnp.moveaxis(out, 0, 1).reshape(Bt, L, H, v.shape[-1])
    return out, S


def causal_dwconv(ext, w):
    C = w.shape[-1]
    return lax.conv_general_dilated(
        ext, w.astype(ext.dtype)[:, None, :], window_strides=(1,), padding='VALID',
        dimension_numbers=('NWC', 'WIO', 'NWC'), feature_group_count=C)


def mixer_layer(h, s_ret, buf31, buf3, pos0, w_in, ret_norm_g, conv31_w, conv31_b,
                conv_ln_g, conv_ln_b, sconv_w, sgu_ln_g, sgu_ln_b, sgu_w, sgu_b, w_out):
    Bt, L, _ = h.shape
    z = h @ w_in
    (q, k, v, g, b_lin, b_gate, c_b, c_c, c_h, d_u, d_v) = jnp.split(z, N_IN_SLICES, axis=-1)

    pos = jnp.arange(L, dtype=F32) + pos0
    qh = rotary(q.reshape(Bt, L, RET_HEADS, RET_DK).astype(F32), pos)
    kh = rotary(k.reshape(Bt, L, RET_HEADS, RET_DK).astype(F32), pos) * (RET_DK ** -0.5)
    vh = v.reshape(Bt, L, RET_HEADS, RET_DV).astype(F32)
    r, s_new = retention(qh, kh, vh, s_ret.astype(F32))
    r = r * lax.rsqrt(jnp.mean(r * r, axis=-1, keepdims=True) + EPS)
    r = r.reshape(Bt, L, GROUP_W) * ret_norm_g.astype(F32)
    out_a = jax.nn.silu(g) * r.astype(h.dtype)

    glu = b_lin * jax.nn.sigmoid(b_gate)
    ext31 = jnp.concatenate([buf31.astype(glu.dtype), glu], axis=1)
    conv = causal_dwconv(ext31, conv31_w) + conv31_b
    out_b = jax.nn.silu(layernorm(conv, conv_ln_g, conv_ln_b))
    new_buf31 = ext31[:, -(CONV_W - 1):]

    gated = c_c * c_h
    ext3 = jnp.concatenate([buf3.astype(gated.dtype), gated], axis=1)
    out_c = c_b * causal_dwconv(ext3, sconv_w)
    new_buf3 = ext3[:, -(SC_W - 1):]

    vn = layernorm(d_v, sgu_ln_g, sgu_ln_b)
    Cs = min(L, SGU_CHUNK)
    nc = L // Cs
    wm = jnp.tril(sgu_w[:, :Cs, :Cs])
    vg = vn.reshape(Bt, nc, Cs, SGU_GROUPS, SGU_CH)
    mixed = jnp.einsum('gts,bnsgc->bntgc', wm, vg) + sgu_b[:, :Cs].T[None, None, :, :, None]
    out_d = d_u * mixed.reshape(Bt, L, GROUP_W)

    y = jnp.concatenate([out_a, out_b, out_c, out_d], axis=-1) @ w_out
    return y, s_new.astype(s_ret.dtype), new_buf31, new_buf3, vn


def swiglu(x, w1, w3, w2):
    return (jax.nn.silu(x @ w1) * (x @ w3)) @ w2


def moe_swiglu(x, router_w, w1, w3, w2):
    Bt, L, D = x.shape
    xt = x.reshape(-1, D)
    T = xt.shape[0]
    logits = (xt @ router_w).astype(F32)
    top_l, top_e = lax.top_k(logits, TOP_K)
    gates = jax.nn.softmax(top_l, axis=-1).astype(x.dtype)
    S = T * TOP_K
    flat_e = top_e.reshape(-1)
    flat_t = jnp.repeat(jnp.arange(T, dtype=jnp.int32), TOP_K, total_repeat_length=S)
    flat_g = gates.reshape(-1)
    order = jnp.argsort(flat_e)
    se = flat_e[order]
    counts = jnp.bincount(flat_e, length=N_EXPERTS)
    padded = (counts + MOE_BLOCK - 1) // MOE_BLOCK * MOE_BLOCK
    pad_end = jnp.cumsum(padded)
    pad_start = pad_end - padded
    start = jnp.cumsum(counts) - counts
    dest = pad_start[se] + jnp.arange(S, dtype=jnp.int32) - start[se]
    n_blocks = -(-S // MOE_BLOCK) + N_EXPERTS
    P = n_blocks * MOE_BLOCK
    slot_t = jnp.zeros((P,), jnp.int32).at[dest].set(flat_t[order])
    slot_g = jnp.zeros((P,), x.dtype).at[dest].set(flat_g[order])
    block_start = jnp.arange(n_blocks, dtype=jnp.int32) * MOE_BLOCK
    block_e = jnp.minimum(jnp.sum(block_start[:, None] >= pad_end[None, :], axis=-1), N_EXPERTS - 1)
    xb = xt[slot_t].reshape(n_blocks, MOE_BLOCK, D)

    def expert_block(args):
        xe, e = args
        return swiglu(xe, w1[e], w3[e], w2[e])

    yb = lax.map(expert_block, (xb, block_e)).reshape(P, D) * slot_g[:, None]
    out = jnp.zeros((T, D), yb.dtype).at[slot_t].add(yb)
    return out.reshape(Bt, L, D)


def setup_inputs(seed: int = 0) -> dict:
    key = jax.random.key(seed)
    ks = iter(jax.random.split(key, 40))
    nrm = lambda shape, scale: jax.random.normal(next(ks), shape, F32) * scale
    gain = lambda shape: 1.0 + nrm(shape, 0.01)
    return {
        'x_prompt': nrm((BATCH, SEQ, D_MODEL), 1.0),
        'x_sample': nrm((DEC_BATCH, DEC_SEQ, D_MODEL), 1.0),
        'state_ret': nrm((DEPTH, DEC_BATCH, RET_HEADS, RET_DK, RET_DV), 0.5),
        'state_conv31': nrm((DEPTH, DEC_BATCH, CONV_W - 1, GROUP_W), 0.5),
        'state_conv3': nrm((DEPTH, DEC_BATCH, SC_W - 1, GROUP_W), 0.5),
        'mix_norm_g': gain((DEPTH, D_MODEL)),
        'w_in': nrm((DEPTH, D_MODEL, D_IN), D_MODEL ** -0.5),
        'ret_norm_g': gain((DEPTH, GROUP_W)),
        'conv31_w': nrm((DEPTH, CONV_W, GROUP_W), CONV_W ** -0.5),
        'conv31_b': nrm((DEPTH, GROUP_W), 0.02),
        'conv_ln_g': gain((DEPTH, GROUP_W)),
        'conv_ln_b': nrm((DEPTH, GROUP_W), 0.02),
        'sconv_w': nrm((DEPTH, SC_W, GROUP_W), SC_W ** -0.5),
        'sgu_ln_g': gain((DEPTH, GROUP_W)),
        'sgu_ln_b': nrm((DEPTH, GROUP_W), 0.02),
        'sgu_w': nrm((DEPTH, SGU_GROUPS, SGU_CHUNK, SGU_CHUNK), SGU_CHUNK ** -0.5),
        'sgu_b': gain((DEPTH, SGU_GROUPS, SGU_CHUNK)),
        'w_out': nrm((DEPTH, D_MIX, D_MODEL), D_MIX ** -0.5),
        'ffn_norm_g': gain((DEPTH, D_MODEL)),
        'dense_w1': nrm((N_DENSE, D_MODEL, D_FF), D_MODEL ** -0.5),
        'dense_w3': nrm((N_DENSE, D_MODEL, D_FF), D_MODEL ** -0.5),
        'dense_w2': nrm((N_DENSE, D_FF, D_MODEL), D_FF ** -0.5),
        'router_w': nrm((N_MOE, D_MODEL, N_EXPERTS), D_MODEL ** -0.5),
        'moe_w1': nrm((N_MOE, N_EXPERTS, D_MODEL, D_FF), D_MODEL ** -0.5),
        'moe_w3': nrm((N_MOE, N_EXPERTS, D_MODEL, D_FF), D_MODEL ** -0.5),
        'moe_w2': nrm((N_MOE, N_EXPERTS, D_FF, D_MODEL), D_FF ** -0.5),
        'final_norm_g': gain((D_MODEL,)),
    }


def reference(x_prompt, x_sample, state_ret, state_conv31, state_conv3, mix_norm_g, w_in,
              ret_norm_g, conv31_w, conv31_b, conv_ln_g, conv_ln_b, sconv_w, sgu_ln_g, sgu_ln_b,
              sgu_w, sgu_b, w_out, ffn_norm_g, dense_w1, dense_w3, dense_w2, router_w,
              moe_w1, moe_w3, moe_w2, final_norm_g):
    def run(x, s_ret, b31, b3, pos0):
        rets, c31s, c3s, vrows = [], [], [], []
        for l in range(DEPTH):
            h = rmsnorm(x, mix_norm_g[l])
            y, s_new, nb31, nb3, vn = mixer_layer(
                h, s_ret[l], b31[l], b3[l], pos0, w_in[l], ret_norm_g[l], conv31_w[l], conv31_b[l],
                conv_ln_g[l], conv_ln_b[l], sconv_w[l], sgu_ln_g[l], sgu_ln_b[l], sgu_w[l], sgu_b[l],
                w_out[l])
            x = x + y
            h = rmsnorm(x, ffn_norm_g[l])
            j = l // 2
            if l % 2 == 0:
                x = x + swiglu(h, dense_w1[j], dense_w3[j], dense_w2[j])
            else:
                x = x + moe_swiglu(h, router_w[j], moe_w1[j], moe_w3[j], moe_w2[j])
            rets.append(s_new)
            c31s.append(nb31)
            c3s.append(nb3)
            vrows.append(vn)
        return (rmsnorm(x, final_norm_g), jnp.stack(rets), jnp.stack(c31s), jnp.stack(c3s),
                jnp.stack(vrows))

    dt = x_prompt.dtype
    zero_ret = jnp.zeros((DEPTH, BATCH, RET_HEADS, RET_DK, RET_DV), dt)
    zero_31 = jnp.zeros((DEPTH, BATCH, CONV_W - 1, GROUP_W), dt)
    zero_3 = jnp.zeros((DEPTH, BATCH, SC_W - 1, GROUP_W), dt)
    y_prompt, ret_p, c31_p, c3_p, _ = run(x_prompt, zero_ret, zero_31, zero_3, 0)
    y_sample, ret_s, c31_s, c3_s, v_s = run(x_sample, state_ret, state_conv31, state_conv3, PAST_LEN)
    return (y_prompt, y_sample, ret_p, ret_s, c31_p, c31_s, c3_p, c3_s, v_s)
```

```python
import functools
import math

import jax
import jax.numpy as jnp
from jax import lax
from jax.experimental import pallas as pl
from jax.experimental.pallas import tpu as pltpu

F32 = jnp.float32
BF16 = jnp.bfloat16

D_MODEL = 2048
BATCH = 4
SEQ = 2048
DEPTH = 2
DEC_BATCH = 128
DEC_SEQ = 4
PAST_LEN = 16384
GROUP_W = 512
D_IN = 11 * GROUP_W
RET_HEADS = 4
RET_D = 128
CHUNK = 128
ROPE_THETA = 10000.0
CONV_W = 31
SC_W = 3
SGU_GROUPS = 4
D_FF = 5632
N_EXPERTS = 8
TOP_K = 2
EPS = 1e-6

T_PROMPT = BATCH * SEQ
T_SAMPLE = DEC_BATCH * DEC_SEQ
T_ALL = T_PROMPT + T_SAMPLE
N_CHUNKS = SEQ // CHUNK

C_Q, C_K, C_V, C_G, C_BL, C_BG, C_CB, C_CC, C_CH, C_DU, C_DV = (i * GROUP_W for i in range(11))

V7X_VMEM_LIMIT = 56 * 1024 * 1024

TM = 1088
TM_FFN = 544
NORM_ROWS = 272
TN_IN = 512
TN_OUT = 512
TF = 256
MOE_BM = 512
MOE_NB = (T_ALL * TOP_K) // MOE_BM + N_EXPERTS
GATHER_ROWS = 256
SAMPLE_BT = 8


def _rms_rows(x, g):
    ms = jnp.mean(x * x, axis=-1, keepdims=True)
    return (x * lax.rsqrt(ms + EPS)) * g


def _layernorm_rows(x, g, b):
    mu = jnp.mean(x, axis=-1, keepdims=True)
    xc = x - mu
    return xc * lax.rsqrt(jnp.mean(xc * xc, axis=-1, keepdims=True) + EPS) * g + b


def _silu(x):
    return x * jax.nn.sigmoid(x)


def _norm_block_to(x_ref, g_ref, dst_ref):
    rows = x_ref.shape[0]
    g = g_ref[...]

    def body(c, carry):
        r0 = pl.multiple_of(c * NORM_ROWS, NORM_ROWS)
        x = x_ref[pl.ds(r0, NORM_ROWS), :]
        dst_ref[pl.ds(r0, NORM_ROWS), :] = _rms_rows(x, g).astype(dst_ref.dtype)
        return carry

    lax.fori_loop(0, rows // NORM_ROWS, body, 0)


def _norm_matmul_kernel(x_ref, g_ref, w_ref, o_ref, hn_ref):
    @pl.when(pl.program_id(1) == 0)
    def _():
        _norm_block_to(x_ref, g_ref, hn_ref)

    o_ref[...] = jnp.dot(hn_ref[...], w_ref[...].astype(BF16), preferred_element_type=F32)


def _norm_matmul(x, g, w):
    t, d = x.shape
    n = w.shape[1]
    return pl.pallas_call(
        _norm_matmul_kernel,
        grid=(t // TM, n // TN_IN),
        in_specs=[
            pl.BlockSpec((TM, d), lambda i, j: (i, 0)),
            pl.BlockSpec((1, d), lambda i, j: (0, 0)),
            pl.BlockSpec((d, TN_IN), lambda i, j: (0, j)),
        ],
        out_specs=pl.BlockSpec((TM, TN_IN), lambda i, j: (i, j)),
        out_shape=jax.ShapeDtypeStruct((t, n), F32),
        scratch_shapes=[pltpu.VMEM((TM, d), BF16)],
        compiler_params=pltpu.CompilerParams(
            dimension_semantics=("arbitrary", "arbitrary"), vmem_limit_bytes=V7X_VMEM_LIMIT),
        name="norm_in_proj",
    )(x, g.reshape(1, d), w)


def _matmul_residual_kernel(a_ref, w_ref, r_ref, o_ref):
    o_ref[...] = r_ref[...] + jnp.dot(a_ref[...], w_ref[...].astype(BF16), preferred_element_type=F32)


def _matmul_residual(a, w, r):
    t, k = a.shape
    n = w.shape[1]
    return pl.pallas_call(
        _matmul_residual_kernel,
        grid=(t // TM, n // TN_OUT),
        in_specs=[
            pl.BlockSpec((TM, k), lambda i, j: (i, 0)),
            pl.BlockSpec((k, TN_OUT), lambda i, j: (0, j)),
            pl.BlockSpec((TM, TN_OUT), lambda i, j: (i, j)),
        ],
        out_specs=pl.BlockSpec((TM, TN_OUT), lambda i, j: (i, j)),
        out_shape=jax.ShapeDtypeStruct((t, n), F32),
        compiler_params=pltpu.CompilerParams(
            dimension_semantics=("arbitrary", "arbitrary"), vmem_limit_bytes=V7X_VMEM_LIMIT),
        name="out_proj_residual",
    )(a, w, r)


def _swiglu_tile(h, w1_ref, w3_ref, w2_ref):
    a = jnp.dot(h, w1_ref[...].astype(BF16), preferred_element_type=F32)
    b = jnp.dot(h, w3_ref[...].astype(BF16), preferred_element_type=F32)
    u = (_silu(a) * b).astype(BF16)
    return jnp.dot(u, w2_ref[...].astype(BF16), preferred_element_type=F32)


def _ffn_kernel(x_ref, g_ref, w1_ref, w3_ref, w2_ref, o_ref, hn_ref):
    @pl.when(pl.program_id(1) == 0)
    def _():
        _norm_block_to(x_ref, g_ref, hn_ref)
        o_ref[...] = x_ref[...]

    o_ref[...] += _swiglu_tile(hn_ref[...], w1_ref, w3_ref, w2_ref)


def _ffn_dense(x, g, w1, w3, w2):
    t, d = x.shape
    f = w1.shape[1]
    return pl.pallas_call(
        _ffn_kernel,
        grid=(t // TM_FFN, f // TF),
        in_specs=[
            pl.BlockSpec((TM_FFN, d), lambda i, j: (i, 0)),
            pl.BlockSpec((1, d), lambda i, j: (0, 0)),
            pl.BlockSpec((d, TF), lambda i, j: (0, j)),
            pl.BlockSpec((d, TF), lambda i, j: (0, j)),
            pl.BlockSpec((TF, d), lambda i, j: (j, 0)),
        ],
        out_specs=pl.BlockSpec((TM_FFN, d), lambda i, j: (i, 0)),
        out_shape=jax.ShapeDtypeStruct((t, d), F32),
        scratch_shapes=[pltpu.VMEM((TM_FFN, d), BF16)],
        compiler_params=pltpu.CompilerParams(
            dimension_semantics=("arbitrary", "arbitrary"), vmem_limit_bytes=V7X_VMEM_LIMIT),
        name="ffn_dense",
    )(x, g.reshape(1, d), w1, w3, w2)


def _rotary_tables(length, pos0):
    half = RET_D // 2
    inv = ROPE_THETA ** (-jnp.arange(half, dtype=F32) / half)
    pos = jnp.arange(length, dtype=F32) + pos0
    ang = pos[:, None] * inv[None, :]
    cos, sin = jnp.cos(ang), jnp.sin(ang)
    return jnp.concatenate([cos, cos], axis=-1), jnp.concatenate([-sin, sin], axis=-1)


def _decay_tables(c):
    log_g = jnp.log1p(-jnp.exp2(-5.0 - jnp.arange(RET_HEADS, dtype=F32)))
    idx = jnp.arange(c, dtype=F32)
    rel = idx[:, None] - idx[None, :]
    d_in = jnp.where(rel >= 0, jnp.exp(log_g[:, None, None] * jnp.maximum(rel, 0.0)), 0.0)
    d_q = jnp.exp(log_g[:, None] * (idx + 1.0)[None, :])
    d_k = jnp.exp(log_g[:, None] * (c - 1.0 - idx)[None, :])
    d_c = jnp.exp(log_g * c)
    return d_in, d_q, d_k, d_c


HIST31 = 32
HIST3 = 8


def _mixer_prompt_kernel(z_ref, cos_ref, sin_ref, din_ref, dq_ref, dk_ref, dc_ref, retg_ref,
                         c31w_ref, c31b_ref, clng_ref, clnb_ref, scw_ref, slng_ref, slnb_ref,
                         sguw_ref, sgub_ref, mix_in_ref,
                         mix_ref, ret_ref, b31_ref, b3_ref,
                         s_scr, ext31_scr, ext3_scr):
    del mix_in_ref
    c = pl.program_id(1)

    @pl.when(c == 0)
    def _():
        s_scr[...] = jnp.zeros_like(s_scr)
        ext31_scr[pl.ds(0, HIST31), :] = jnp.zeros((HIST31, GROUP_W), F32)
        ext3_scr[pl.ds(0, HIST3), :] = jnp.zeros((HIST3, GROUP_W), F32)

    cos = cos_ref[...]
    sin = sin_ref[...]

    for h in range(RET_HEADS):
        lo = h * RET_D
        q = z_ref[:, pl.ds(C_Q + lo, RET_D)]
        k = z_ref[:, pl.ds(C_K + lo, RET_D)]
        v = z_ref[:, pl.ds(C_V + lo, RET_D)]
        g = z_ref[:, pl.ds(C_G + lo, RET_D)]
        q = q * cos + pltpu.roll(q, RET_D // 2, axis=1) * sin
        k = (k * cos + pltpu.roll(k, RET_D // 2, axis=1) * sin) * (RET_D ** -0.5)
        qb = q.astype(BF16)
        vb = v.astype(BF16)
        scores = lax.dot_general(qb, k.astype(BF16), (((1,), (1,)), ((), ())),
                                 preferred_element_type=F32) * din_ref[h]
        inner = jnp.dot(scores.astype(BF16), vb, preferred_element_type=F32)
        s_old = s_scr[h]
        cross = jnp.dot(qb, s_old.astype(BF16), preferred_element_type=F32) * dq_ref[h]
        kd = (k * dk_ref[h]).astype(BF16)
        s_scr[h] = s_old * dc_ref[h] + lax.dot_general(kd, vb, (((0,), (0,)), ((), ())),
                                                       preferred_element_type=F32)
        r = inner + cross
        r = r * lax.rsqrt(jnp.mean(r * r, axis=-1, keepdims=True) + EPS)
        r = r * retg_ref[:, pl.ds(lo, RET_D)]
        mix_ref[:, pl.ds(lo, RET_D)] = (_silu(g) * r).astype(mix_ref.dtype)

    glu = z_ref[:, pl.ds(C_BL, GROUP_W)] * jax.nn.sigmoid(z_ref[:, pl.ds(C_BG, GROUP_W)])
    ext31_scr[pl.ds(HIST31, CHUNK), :] = glu
    conv = ext31_scr[pl.ds(HIST31 - (CONV_W - 1), CHUNK), :] * c31w_ref[pl.ds(0, 1), :]
    for j in range(1, CONV_W):
        conv = conv + ext31_scr[pl.ds(HIST31 - (CONV_W - 1) + j, CHUNK), :] * c31w_ref[pl.ds(j, 1), :]
    conv = conv + c31b_ref[...]
    out_b = _silu(_layernorm_rows(conv, clng_ref[...], clnb_ref[...]))
    mix_ref[:, pl.ds(GROUP_W, GROUP_W)] = out_b.astype(mix_ref.dtype)
    ext31_scr[pl.ds(0, HIST31), :] = ext31_scr[pl.ds(CHUNK, HIST31), :]

    gated = z_ref[:, pl.ds(C_CC, GROUP_W)] * z_ref[:, pl.ds(C_CH, GROUP_W)]
    ext3_scr[pl.ds(HIST3, CHUNK), :] = gated
    conv3 = ext3_scr[pl.ds(HIST3 - (SC_W - 1), CHUNK), :] * scw_ref[pl.ds(0, 1), :]
    for j in range(1, SC_W):
        conv3 = conv3 + ext3_scr[pl.ds(HIST3 - (SC_W - 1) + j, CHUNK), :] * scw_ref[pl.ds(j, 1), :]
    mix_ref[:, pl.ds(2 * GROUP_W, GROUP_W)] = (z_ref[:, pl.ds(C_CB, GROUP_W)] * conv3).astype(mix_ref.dtype)
    ext3_scr[pl.ds(0, HIST3), :] = ext3_scr[pl.ds(CHUNK, HIST3), :]

    vn = _layernorm_rows(z_ref[:, pl.ds(C_DV, GROUP_W)], slng_ref[...], slnb_ref[...])
    row = lax.broadcasted_iota(jnp.int32, (CHUNK, CHUNK), 0)
    col = lax.broadcasted_iota(jnp.int32, (CHUNK, CHUNK), 1)
    ch = GROUP_W // SGU_GROUPS
    for gi in range(SGU_GROUPS):
        wm = jnp.where(row >= col, sguw_ref[gi], 0.0).astype(BF16)
        mixed = jnp.dot(wm, vn[:, gi * ch:(gi + 1) * ch].astype(BF16), preferred_element_type=F32)
        mixed = mixed + sgub_ref[:, pl.ds(gi, 1)]
        du = z_ref[:, pl.ds(C_DU + gi * ch, ch)]
        mix_ref[:, pl.ds(3 * GROUP_W + gi * ch, ch)] = (du * mixed).astype(mix_ref.dtype)

    @pl.when(c == pl.num_programs(1) - 1)
    def _():
        ret_ref[...] = s_scr[...]
        b31_ref[...] = ext31_scr[pl.ds(HIST31 - (CONV_W - 1), CONV_W - 1), :]
        b3_ref[...] = ext3_scr[pl.ds(HIST3 - (SC_W - 1), SC_W - 1), :]


def _mixer_prompt(z, p, mix_init):
    cos, sin = _rotary_tables(SEQ, 0)
    d_in, d_q, d_k, d_c = _decay_tables(CHUNK)
    dq_t = jnp.broadcast_to(d_q[:, :, None], (RET_HEADS, CHUNK, RET_D))
    dk_t = jnp.broadcast_to(d_k[:, :, None], (RET_HEADS, CHUNK, RET_D))
    dc_t = jnp.broadcast_to(d_c[:, None, None], (RET_HEADS, RET_D, RET_D))

    def full(shape):
        return pl.BlockSpec(shape, lambda b, c: (0,) * len(shape))

    row = lambda a: a.reshape(1, -1)
    return pl.pallas_call(
        _mixer_prompt_kernel,
        grid=(BATCH, N_CHUNKS),
        in_specs=[
            pl.BlockSpec((CHUNK, D_IN), lambda b, c: (b * N_CHUNKS + c, 0)),
            pl.BlockSpec((CHUNK, RET_D), lambda b, c: (c, 0)),
            pl.BlockSpec((CHUNK, RET_D), lambda b, c: (c, 0)),
            full((RET_HEADS, CHUNK, CHUNK)),
            full((RET_HEADS, CHUNK, RET_D)),
            full((RET_HEADS, CHUNK, RET_D)),
            full((RET_HEADS, RET_D, RET_D)),
            full((1, GROUP_W)),
            full((CONV_W, GROUP_W)),
            full((1, GROUP_W)),
            full((1, GROUP_W)),
            full((1, GROUP_W)),
            full((SC_W, GROUP_W)),
            full((1, GROUP_W)),
            full((1, GROUP_W)),
            full((SGU_GROUPS, CHUNK, CHUNK)),
            full((CHUNK, SGU_GROUPS)),
            pl.BlockSpec(memory_space=pl.ANY),
        ],
        input_output_aliases={17: 0},
        out_specs=[
            pl.BlockSpec((CHUNK, D_MODEL), lambda b, c: (b * N_CHUNKS + c, 0)),
            pl.BlockSpec((None, RET_HEADS, RET_D, RET_D), lambda b, c: (b, 0, 0, 0)),
            pl.BlockSpec((None, CONV_W - 1, GROUP_W), lambda b, c: (b, 0, 0)),
            pl.BlockSpec((None, SC_W - 1, GROUP_W), lambda b, c: (b, 0, 0)),
        ],
        out_shape=[
            jax.ShapeDtypeStruct((T_ALL, D_MODEL), BF16),
            jax.ShapeDtypeStruct((BATCH, RET_HEADS, RET_D, RET_D), F32),
            jax.ShapeDtypeStruct((BATCH, CONV_W - 1, GROUP_W), F32),
            jax.ShapeDtypeStruct((BATCH, SC_W - 1, GROUP_W), F32),
        ],
        scratch_shapes=[
            pltpu.VMEM((RET_HEADS, RET_D, RET_D), F32),
            pltpu.VMEM((HIST31 + CHUNK, GROUP_W), F32),
            pltpu.VMEM((HIST3 + CHUNK, GROUP_W), F32),
        ],
        compiler_params=pltpu.CompilerParams(
            dimension_semantics=("arbitrary", "arbitrary"), vmem_limit_bytes=V7X_VMEM_LIMIT),
        name="mixer_prompt",
    )(z, cos, sin, d_in, dq_t, dk_t, dc_t, row(p["ret_norm_g"]), p["conv31_w"], row(p["conv31_b"]),
      row(p["conv_ln_g"]), row(p["conv_ln_b"]), p["sconv_w"], row(p["sgu_ln_g"]), row(p["sgu_ln_b"]),
      p["sgu_w"], p["sgu_b"].T, mix_init)


EXT31_S = 40
EXT3_S = 8


def _mixer_sample_kernel(z_ref, cos_ref, sin_ref, din_ref, dq_ref, dk_ref, dc_ref, retg_ref,
                         c31w_ref, c31b_ref, clng_ref, clnb_ref, scw_ref, slng_ref, slnb_ref,
                         sguw_ref, sgub_ref, sret_ref, s31_ref, s3_ref,
                         mix_ref, ret_ref, b31_ref, b3_ref, vn_ref,
                         ext31_scr, ext3_scr):
    cos = cos_ref[...]
    sin = sin_ref[...]
    n_hist31 = CONV_W - 1
    n_hist3 = SC_W - 1

    for h in range(RET_HEADS):
        lo = h * RET_D
        q = z_ref[:, :, pl.ds(C_Q + lo, RET_D)]
        k = z_ref[:, :, pl.ds(C_K + lo, RET_D)]
        g = z_ref[:, :, pl.ds(C_G + lo, RET_D)]
        q = q * cos + pltpu.roll(q, RET_D // 2, axis=2) * sin
        k = (k * cos + pltpu.roll(k, RET_D // 2, axis=2) * sin) * (RET_D ** -0.5)
        s_old = sret_ref[:, h]
        r = lax.dot_general(q, s_old, (((2,), (1,)), ((0,), (0,))),
                            preferred_element_type=F32) * dq_ref[h]
        for s in range(DEC_SEQ):
            k_s = z_ref[:, pl.ds(s, 1), pl.ds(C_K + lo, RET_D)]
            k_s = (k_s * cos[s:s + 1] + pltpu.roll(k_s, RET_D // 2, axis=2) * sin[s:s + 1]) * (RET_D ** -0.5)
            v_s = z_ref[:, pl.ds(s, 1), pl.ds(C_V + lo, RET_D)]
            score = jnp.sum(q * k_s, axis=-1, keepdims=True) * din_ref[h, s]
            r = r + score * v_s
        kd = k * dk_ref[h]
        v = z_ref[:, :, pl.ds(C_V + lo, RET_D)]
        ret_ref[:, h] = s_old * dc_ref[h] + lax.dot_general(
            kd, v, (((1,), (1,)), ((0,), (0,))), preferred_element_type=F32)
        r = r * lax.rsqrt(jnp.mean(r * r, axis=-1, keepdims=True) + EPS)
        r = r * retg_ref[:, pl.ds(lo, RET_D)]
        mix_ref[:, :, pl.ds(lo, RET_D)] = _silu(g) * r

    glu = z_ref[:, :, pl.ds(C_BL, GROUP_W)] * jax.nn.sigmoid(z_ref[:, :, pl.ds(C_BG, GROUP_W)])
    ext31_scr[:, pl.ds(0, n_hist31), :] = s31_ref[...]
    ext31_scr[:, pl.ds(n_hist31, DEC_SEQ), :] = glu
    conv = ext31_scr[:, pl.ds(0, DEC_SEQ), :] * c31w_ref[pl.ds(0, 1), :]
    for j in range(1, CONV_W):
        conv = conv + ext31_scr[:, pl.ds(j, DEC_SEQ), :] * c31w_ref[pl.ds(j, 1), :]
    conv = conv + c31b_ref[...]
    mix_ref[:, :, pl.ds(GROUP_W, GROUP_W)] = _silu(_layernorm_rows(conv, clng_ref[...], clnb_ref[...]))
    b31_ref[...] = ext31_scr[:, pl.ds(DEC_SEQ, n_hist31), :]

    gated = z_ref[:, :, pl.ds(C_CC, GROUP_W)] * z_ref[:, :, pl.ds(C_CH, GROUP_W)]
    ext3_scr[:, pl.ds(0, n_hist3), :] = s3_ref[...]
    ext3_scr[:, pl.ds(n_hist3, DEC_SEQ), :] = gated
    conv3 = ext3_scr[:, pl.ds(0, DEC_SEQ), :] * scw_ref[pl.ds(0, 1), :]
    for j in range(1, SC_W):
        conv3 = conv3 + ext3_scr[:, pl.ds(j, DEC_SEQ), :] * scw_ref[pl.ds(j, 1), :]
    mix_ref[:, :, pl.ds(2 * GROUP_W, GROUP_W)] = z_ref[:, :, pl.ds(C_CB, GROUP_W)] * conv3
    b3_ref[...] = ext3_scr[:, pl.ds(DEC_SEQ, n_hist3), :]

    vn = _layernorm_rows(z_ref[:, :, pl.ds(C_DV, GROUP_W)], slng_ref[...], slnb_ref[...])
    vn_ref[...] = vn
    mixed = sgub_ref[...]
    for s in range(DEC_SEQ):
        mixed = mixed + sguw_ref[s] * vn_ref[:, pl.ds(s, 1), :]
    mix_ref[:, :, pl.ds(3 * GROUP_W, GROUP_W)] = z_ref[:, :, pl.ds(C_DU, GROUP_W)] * mixed


def _mixer_sample(z_s, p, s_ret, s_31, s_3):
    cos, sin = _rotary_tables(DEC_SEQ, PAST_LEN)
    d_in, d_q, d_k, d_c = _decay_tables(DEC_SEQ)
    din_t = jnp.swapaxes(d_in, 1, 2)[:, :, :, None]
    dq_t = jnp.broadcast_to(d_q[:, :, None], (RET_HEADS, DEC_SEQ, RET_D))
    dk_t = jnp.broadcast_to(d_k[:, :, None], (RET_HEADS, DEC_SEQ, RET_D))
    dc_t = jnp.broadcast_to(d_c[:, None, None], (RET_HEADS, RET_D, RET_D))
    ch = GROUP_W // SGU_GROUPS
    w4 = jnp.tril(p["sgu_w"][:, :DEC_SEQ, :DEC_SEQ])
    w_t = jnp.repeat(jnp.transpose(w4, (2, 1, 0)), ch, axis=-1)
    b_t = jnp.repeat(p["sgu_b"][:, :DEC_SEQ].T, ch, axis=-1)
    bt = SAMPLE_BT

    def full(shape):
        return pl.BlockSpec(shape, lambda i: (0,) * len(shape))

    row = lambda a: a.reshape(1, -1)
    return pl.pallas_call(
        _mixer_sample_kernel,
        grid=(DEC_BATCH // bt,),
        in_specs=[
            pl.BlockSpec((bt, DEC_SEQ, D_IN), lambda i: (i, 0, 0)),
            full((DEC_SEQ, RET_D)),
            full((DEC_SEQ, RET_D)),
            full((RET_HEADS, DEC_SEQ, DEC_SEQ, 1)),
            full((RET_HEADS, DEC_SEQ, RET_D)),
            full((RET_HEADS, DEC_SEQ, RET_D)),
            full((RET_HEADS, RET_D, RET_D)),
            full((1, GROUP_W)),
            full((CONV_W, GROUP_W)),
            full((1, GROUP_W)),
            full((1, GROUP_W)),
            full((1, GROUP_W)),
            full((SC_W, GROUP_W)),
            full((1, GROUP_W)),
            full((1, GROUP_W)),
            full((DEC_SEQ, DEC_SEQ, GROUP_W)),
            full((DEC_SEQ, GROUP_W)),
            pl.BlockSpec((bt, RET_HEADS, RET_D, RET_D), lambda i: (i, 0, 0, 0)),
            pl.BlockSpec((bt, CONV_W - 1, GROUP_W), lambda i: (i, 0, 0)),
            pl.BlockSpec((bt, SC_W - 1, GROUP_W), lambda i: (i, 0, 0)),
        ],
        out_specs=[
            pl.BlockSpec((bt, DEC_SEQ, D_MODEL), lambda i: (i, 0, 0)),
            pl.BlockSpec((bt, RET_HEADS, RET_D, RET_D), lambda i: (i, 0, 0, 0)),
            pl.BlockSpec((bt, CONV_W - 1, GROUP_W), lambda i: (i, 0, 0)),
            pl.BlockSpec((bt, SC_W - 1, GROUP_W), lambda i: (i, 0, 0)),
            pl.BlockSpec((bt, DEC_SEQ, GROUP_W), lambda i: (i, 0, 0)),
        ],
        out_shape=[
            jax.ShapeDtypeStruct((DEC_BATCH, DEC_SEQ, D_MODEL), F32),
            jax.ShapeDtypeStruct((DEC_BATCH, RET_HEADS, RET_D, RET_D), F32),
            jax.ShapeDtypeStruct((DEC_BATCH, CONV_W - 1, GROUP_W), F32),
            jax.ShapeDtypeStruct((DEC_BATCH, SC_W - 1, GROUP_W), F32),
            jax.ShapeDtypeStruct((DEC_BATCH, DEC_SEQ, GROUP_W), F32),
        ],
        scratch_shapes=[
            pltpu.VMEM((bt, EXT31_S, GROUP_W), F32),
            pltpu.VMEM((bt, EXT3_S, GROUP_W), F32),
        ],
        compiler_params=pltpu.CompilerParams(
            dimension_semantics=("arbitrary",), vmem_limit_bytes=V7X_VMEM_LIMIT),
        name="mixer_sample",
    )(z_s, cos, sin, din_t, dq_t, dk_t, dc_t, row(p["ret_norm_g"]), p["conv31_w"], row(p["conv31_b"]),
      row(p["conv_ln_g"]), row(p["conv_ln_b"]), p["sconv_w"], row(p["sgu_ln_g"]), row(p["sgu_ln_b"]),
      w_t, b_t, s_ret, s_31, s_3)


ROUTE_ROWS = 544


def _router_kernel(x_ref, g_ref, rw_ref, e_ref, p_ref):
    h = _rms_rows(x_ref[...], g_ref[...])
    logits = jnp.dot(h, rw_ref[...], preferred_element_type=F32, precision=lax.Precision.HIGHEST)
    lane = lax.broadcasted_iota(jnp.int32, logits.shape, 1)
    m1 = jnp.max(logits, axis=-1, keepdims=True)
    i1 = jnp.min(jnp.where(logits == m1, lane, N_EXPERTS), axis=-1, keepdims=True)
    rest = jnp.where(lane == i1, -jnp.inf, logits)
    m2 = jnp.max(rest, axis=-1, keepdims=True)
    i2 = jnp.min(jnp.where(rest == m2, lane, N_EXPERTS), axis=-1, keepdims=True)
    e2 = jnp.exp(m2 - m1)
    den = 1.0 + e2
    e_ref[...] = jnp.where(lane == 0, i1, jnp.where(lane == 1, i2, 0))
    p_ref[...] = jnp.where(lane == 0, 1.0 / den, jnp.where(lane == 1, e2 / den, 0.0))


def _router(x, g, rw):
    t, d = x.shape
    return pl.pallas_call(
        _router_kernel,
        grid=(t // ROUTE_ROWS,),
        in_specs=[
            pl.BlockSpec((ROUTE_ROWS, d), lambda i: (i, 0)),
            pl.BlockSpec((1, d), lambda i: (0, 0)),
            pl.BlockSpec((d, N_EXPERTS), lambda i: (0, 0)),
        ],
        out_specs=[
            pl.BlockSpec((ROUTE_ROWS, N_EXPERTS), lambda i: (i, 0)),
            pl.BlockSpec((ROUTE_ROWS, N_EXPERTS), lambda i: (i, 0)),
        ],
        out_shape=[
            jax.ShapeDtypeStruct((t, N_EXPERTS), jnp.int32),
            jax.ShapeDtypeStruct((t, N_EXPERTS), F32),
        ],
        compiler_params=pltpu.CompilerParams(
            dimension_semantics=("arbitrary",), vmem_limit_bytes=V7X_VMEM_LIMIT),
        name="moe_router",
    )(x, g.reshape(1, d), rw)


def _route_tables(eid, gate):
    t = eid.shape[0]
    s = t * TOP_K
    flat_e = eid.reshape(-1)
    onehot = (flat_e[:, None] == jnp.arange(N_EXPERTS, dtype=jnp.int32)[None, :]).astype(jnp.int32)
    csum = jnp.cumsum(onehot, axis=0)
    rank = jnp.take_along_axis(csum, flat_e[:, None], axis=1)[:, 0] - 1
    counts = csum[-1]
    padded = (counts + MOE_BM - 1) // MOE_BM * MOE_BM
    pad_end = jnp.cumsum(padded)
    pad_start = pad_end - padded
    dest = (pad_start[flat_e] + rank).astype(jnp.int32)
    rows = MOE_NB * MOE_BM
    slot_t = jnp.zeros((rows,), jnp.int32).at[dest].set(jnp.arange(s, dtype=jnp.int32) // TOP_K)
    slot_g = jnp.zeros((rows,), F32).at[dest].set(gate.reshape(-1))
    block_start = jnp.arange(MOE_NB, dtype=jnp.int32) * MOE_BM
    block_e = jnp.minimum(jnp.sum(block_start[:, None] >= pad_end[None, :], axis=-1), N_EXPERTS - 1)
    n_used = (pad_end[-1] // MOE_BM).astype(jnp.int32)
    return dest, slot_t, slot_g, block_e.astype(jnp.int32), n_used.reshape(1)


def _row_copy(src_hbm, row, dst, r, sem):
    return pltpu.make_async_copy(src_hbm.at[pl.ds(row, 1)], dst.at[pl.ds(r, 1)], sem)


def _gather_norm_kernel(slot_ref, x_hbm, g_ref, o_ref, buf, sem):
    def start(r, carry):
        _row_copy(x_hbm, slot_ref[0, 0, r], buf, r, sem).start()
        return carry

    lax.fori_loop(0, GATHER_ROWS, start, 0)

    def wait(r, carry):
        _row_copy(x_hbm, 0, buf, r, sem).wait()
        return carry

    lax.fori_loop(0, GATHER_ROWS, wait, 0)
    o_ref[...] = _rms_rows(buf[...], g_ref[...]).astype(o_ref.dtype)


def _gather_norm(x, g, slot_t):
    t, d = x.shape
    rows = slot_t.shape[0]
    nblk = rows // GATHER_ROWS
    return pl.pallas_call(
        _gather_norm_kernel,
        grid=(nblk,),
        in_specs=[
            pl.BlockSpec((1, 1, GATHER_ROWS), lambda i: (i, 0, 0), memory_space=pltpu.SMEM),
            pl.BlockSpec(memory_space=pl.ANY),
            pl.BlockSpec((1, d), lambda i: (0, 0)),
        ],
        out_specs=pl.BlockSpec((GATHER_ROWS, d), lambda i: (i, 0)),
        out_shape=jax.ShapeDtypeStruct((rows, d), BF16),
        scratch_shapes=[pltpu.VMEM((GATHER_ROWS, d), F32), pltpu.SemaphoreType.DMA(())],
        compiler_params=pltpu.CompilerParams(
            dimension_semantics=("arbitrary",), vmem_limit_bytes=V7X_VMEM_LIMIT),
        name="moe_gather_norm",
    )(slot_t.reshape(nblk, 1, GATHER_ROWS), x, g.reshape(1, d))


def _moe_ffn_kernel(be_ref, nu_ref, x_ref, w1_ref, w3_ref, w2_ref, sg_ref, o_ref):
    i = pl.program_id(0)
    j = pl.program_id(1)

    @pl.when(j == 0)
    def _():
        o_ref[...] = jnp.zeros_like(o_ref)

    @pl.when(i < nu_ref[0])
    def _():
        o_ref[...] += _swiglu_tile(x_ref[...], w1_ref, w3_ref, w2_ref)

        @pl.when(j == pl.num_programs(1) - 1)
        def _():
            o_ref[...] = o_ref[...] * sg_ref[...]


def _moe_ffn(xg, w1, w3, w2, slot_g, block_e, n_used):
    rows, d = xg.shape
    f = w1.shape[2]
    nj = f // TF

    def jmap(i, j, nu):
        return jnp.where(i < nu[0], j, nj - 1)

    return pl.pallas_call(
        _moe_ffn_kernel,
        grid_spec=pltpu.PrefetchScalarGridSpec(
            num_scalar_prefetch=2,
            grid=(MOE_NB, nj),
            in_specs=[
                pl.BlockSpec((MOE_BM, d), lambda i, j, be, nu: (i, 0)),
                pl.BlockSpec((None, d, TF), lambda i, j, be, nu: (be[i], 0, jmap(i, j, nu))),
                pl.BlockSpec((None, d, TF), lambda i, j, be, nu: (be[i], 0, jmap(i, j, nu))),
                pl.BlockSpec((None, TF, d), lambda i, j, be, nu: (be[i], jmap(i, j, nu), 0)),
                pl.BlockSpec((MOE_BM, 1), lambda i, j, be, nu: (i, 0)),
            ],
            out_specs=pl.BlockSpec((MOE_BM, d), lambda i, j, be, nu: (i, 0)),
        ),
        out_shape=jax.ShapeDtypeStruct((rows, d), F32),
        compiler_params=pltpu.CompilerParams(
            dimension_semantics=("arbitrary", "arbitrary"), vmem_limit_bytes=V7X_VMEM_LIMIT),
        name="moe_ffn",
    )(block_e, n_used, xg, w1, w3, w2, slot_g.reshape(rows, 1))


def _combine_kernel(dest_ref, x_ref, y_hbm, g_ref, o_ref, buf, sem, *, final_norm):
    def start(r, carry):
        for kk in range(TOP_K):
            _row_copy(y_hbm, dest_ref[0, 0, TOP_K * r + kk], buf.at[kk], r, sem).start()
        return carry

    lax.fori_loop(0, GATHER_ROWS, start, 0)

    def wait(r, carry):
        for kk in range(TOP_K):
            _row_copy(y_hbm, 0, buf.at[kk], r, sem).wait()
        return carry

    lax.fori_loop(0, GATHER_ROWS, wait, 0)
    y = x_ref[...] + (buf[0] + buf[1])
    if final_norm:
        y = _rms_rows(y, g_ref[...])
    o_ref[...] = y


def _moe_combine(x, yb, dest, g, final_norm):
    t, d = x.shape
    nblk = t // GATHER_ROWS
    return pl.pallas_call(
        functools.partial(_combine_kernel, final_norm=final_norm),
        grid=(nblk,),
        in_specs=[
            pl.BlockSpec((1, 1, TOP_K * GATHER_ROWS), lambda i: (i, 0, 0), memory_space=pltpu.SMEM),
            pl.BlockSpec((GATHER_ROWS, d), lambda i: (i, 0)),
            pl.BlockSpec(memory_space=pl.ANY),
            pl.BlockSpec((1, d), lambda i: (0, 0)),
        ],
        out_specs=pl.BlockSpec((GATHER_ROWS, d), lambda i: (i, 0)),
        out_shape=jax.ShapeDtypeStruct((t, d), F32),
        scratch_shapes=[pltpu.VMEM((TOP_K, GATHER_ROWS, d), F32), pltpu.SemaphoreType.DMA(())],
        compiler_params=pltpu.CompilerParams(
            dimension_semantics=("arbitrary",), vmem_limit_bytes=V7X_VMEM_LIMIT),
        name="moe_combine",
    )(dest.reshape(nblk, 1, TOP_K * GATHER_ROWS), x, yb, g.reshape(1, d))


def _moe(x, norm_g, rw, w1, w3, w2, out_g, final_norm):
    eid, gate = _router(x, norm_g, rw)
    dest, slot_t, slot_g, block_e, n_used = _route_tables(eid[:, :TOP_K], gate[:, :TOP_K])
    xg = _gather_norm(x, norm_g, slot_t)
    yb = _moe_ffn(xg, w1, w3, w2, slot_g, block_e, n_used)
    return _moe_combine(x, yb, dest, out_g, final_norm)


def kernel(x_prompt, x_sample, state_ret, state_conv31, state_conv3, mix_norm_g, w_in, ret_norm_g, conv31_w, conv31_b, conv_ln_g, conv_ln_b, sconv_w, sgu_ln_g, sgu_ln_b, sgu_w, sgu_b, w_out, ffn_norm_g, dense_w1, dense_w3, dense_w2, router_w, moe_w1, moe_w3, moe_w2, final_norm_g):
    assert DEPTH % 2 == 0, "the final norm is fused into the last (expert) channel mixer"
    x = jnp.concatenate([x_prompt.reshape(T_PROMPT, D_MODEL), x_sample.reshape(T_SAMPLE, D_MODEL)], axis=0)
    rets_p, rets_s, c31_p, c31_s, c3_p, c3_s, vns = [], [], [], [], [], [], []
    for l in range(DEPTH):
        p = dict(ret_norm_g=ret_norm_g[l], conv31_w=conv31_w[l], conv31_b=conv31_b[l],
                 conv_ln_g=conv_ln_g[l], conv_ln_b=conv_ln_b[l], sconv_w=sconv_w[l],
                 sgu_ln_g=sgu_ln_g[l], sgu_ln_b=sgu_ln_b[l], sgu_w=sgu_w[l], sgu_b=sgu_b[l])
        z = _norm_matmul(x, mix_norm_g[l], w_in[l])
        z_s = z[T_PROMPT:].reshape(DEC_BATCH, DEC_SEQ, D_IN)
        mix_s, ret_s, b31_s, b3_s, vn_s = _mixer_sample(z_s, p, state_ret[l], state_conv31[l], state_conv3[l])
        mix_init = jnp.pad(mix_s.reshape(T_SAMPLE, D_MODEL).astype(BF16), ((T_PROMPT, 0), (0, 0)))
        mix, ret_p, b31_p, b3_p = _mixer_prompt(z, p, mix_init)
        x = _matmul_residual(mix, w_out[l], x)
        j = l // 2
        if l % 2 == 0:
            x = _ffn_dense(x, ffn_norm_g[l], dense_w1[j], dense_w3[j], dense_w2[j])
        else:
            x = _moe(x, ffn_norm_g[l], router_w[j], moe_w1[j], moe_w3[j], moe_w2[j],
                     final_norm_g, final_norm=(l == DEPTH - 1))
        rets_p.append(ret_p); rets_s.append(ret_s)
        c31_p.append(b31_p); c31_s.append(b31_s)
        c3_p.append(b3_p); c3_s.append(b3_s)
        vns.append(vn_s)
    y_prompt = x[:T_PROMPT].reshape(BATCH, SEQ, D_MODEL)
    y_sample = x[T_PROMPT:].reshape(DEC_BATCH, DEC_SEQ, D_MODEL)
    return (y_prompt, y_sample, jnp.stack(rets_p), jnp.stack(rets_s), jnp.stack(c31_p), jnp.stack(c31_s),
            jnp.stack(c3_p), jnp.stack(c3_s), jnp.stack(vns))
```

```python
import functools
import math

import jax
import jax.numpy as jnp
from jax import lax
from jax.experimental import pallas as pl
from jax.experimental.pallas import tpu as pltpu

F32 = jnp.float32
BF16 = jnp.bfloat16

D_MODEL = 2048
BATCH = 4
SEQ = 2048
DEPTH = 2
DEC_BATCH = 128
DEC_SEQ = 4
PAST_LEN = 16384
GROUP_W = 512
D_IN = 11 * GROUP_W
RET_HEADS = 4
RET_D = 128
CHUNK = 128
ROPE_THETA = 10000.0
CONV_W = 31
SC_W = 3
SGU_GROUPS = 4
D_FF = 5632
N_EXPERTS = 8
TOP_K = 2
EPS = 1e-6

T_PROMPT = BATCH * SEQ
T_SAMPLE = DEC_BATCH * DEC_SEQ
T_ALL = T_PROMPT + T_SAMPLE
N_CHUNKS = SEQ // CHUNK

C_Q, C_K, C_V, C_G, C_BL, C_BG, C_CB, C_CC, C_CH, C_DU, C_DV = (i * GROUP_W for i in range(11))

V7X_VMEM_LIMIT = 56 * 1024 * 1024

TM = 1088
NORM_ROWS = 272
TN_IN = 512
TN_OUT = 512
TF = 256
SAMPLE_BT = 8

LANES = 128
TOK_ROWS = D_MODEL // LANES
MOE_TB = 256
N_TB = T_ALL // MOE_TB
N_SLOTS = T_ALL * TOP_K
SUB = 256
SB = 9 * SUB
NS_MAX = -(-N_SLOTS // SB) + N_EXPERTS


def _rms_rows(x, g):
    ms = jnp.mean(x * x, axis=-1, keepdims=True)
    return (x * lax.rsqrt(ms + EPS)) * g


def _layernorm_rows(x, g, b):
    mu = jnp.mean(x, axis=-1, keepdims=True)
    xc = x - mu
    return xc * lax.rsqrt(jnp.mean(xc * xc, axis=-1, keepdims=True) + EPS) * g + b


def _silu(x):
    return x * jax.nn.sigmoid(x)


def _norm_block_to(x_ref, g_ref, dst_ref):
    rows = x_ref.shape[0]
    g = g_ref[...]

    def body(c, carry):
        r0 = pl.multiple_of(c * NORM_ROWS, NORM_ROWS)
        x = x_ref[pl.ds(r0, NORM_ROWS), :]
        dst_ref[pl.ds(r0, NORM_ROWS), :] = _rms_rows(x, g).astype(dst_ref.dtype)
        return carry

    lax.fori_loop(0, rows // NORM_ROWS, body, 0)


def _norm_matmul_kernel(x_ref, g_ref, w_ref, o_ref, hn_ref):
    @pl.when(pl.program_id(1) == 0)
    def _():
        _norm_block_to(x_ref, g_ref, hn_ref)

    o_ref[...] = jnp.dot(hn_ref[...], w_ref[...].astype(BF16), preferred_element_type=F32)


def _norm_matmul(x, g, w):
    t, d = x.shape
    n = w.shape[1]
    return pl.pallas_call(
        _norm_matmul_kernel,
        grid=(t // TM, n // TN_IN),
        in_specs=[
            pl.BlockSpec((TM, d), lambda i, j: (i, 0)),
            pl.BlockSpec((1, d), lambda i, j: (0, 0)),
            pl.BlockSpec((d, TN_IN), lambda i, j: (0, j)),
        ],
        out_specs=pl.BlockSpec((TM, TN_IN), lambda i, j: (i, j)),
        out_shape=jax.ShapeDtypeStruct((t, n), F32),
        scratch_shapes=[pltpu.VMEM((TM, d), BF16)],
        compiler_params=pltpu.CompilerParams(
            dimension_semantics=("arbitrary", "arbitrary"), vmem_limit_bytes=V7X_VMEM_LIMIT),
        name="norm_in_proj",
    )(x, g.reshape(1, d), w)


def _matmul_residual_kernel(a_ref, w_ref, r_ref, o_ref):
    o_ref[...] = r_ref[...] + jnp.dot(a_ref[...], w_ref[...].astype(BF16), preferred_element_type=F32)


def _matmul_residual(a, w, r):
    t, k = a.shape
    n = w.shape[1]
    return pl.pallas_call(
        _matmul_residual_kernel,
        grid=(t // TM, n // TN_OUT),
        in_specs=[
            pl.BlockSpec((TM, k), lambda i, j: (i, 0)),
            pl.BlockSpec((k, TN_OUT), lambda i, j: (0, j)),
            pl.BlockSpec((TM, TN_OUT), lambda i, j: (i, j)),
        ],
        out_specs=pl.BlockSpec((TM, TN_OUT), lambda i, j: (i, j)),
        out_shape=jax.ShapeDtypeStruct((t, n), F32),
        compiler_params=pltpu.CompilerParams(
            dimension_semantics=("arbitrary", "arbitrary"), vmem_limit_bytes=V7X_VMEM_LIMIT),
        name="out_proj_residual",
    )(a, w, r)


def _swiglu_tile(h, w1_ref, w3_ref, w2_ref):
    a = jnp.dot(h, w1_ref[...].astype(BF16), preferred_element_type=F32)
    b = jnp.dot(h, w3_ref[...].astype(BF16), preferred_element_type=F32)
    u = (_silu(a) * b).astype(BF16)
    return jnp.dot(u, w2_ref[...].astype(BF16), preferred_element_type=F32)


def _ffn_kernel(x_hbm, g_ref, w1_ref, w3_ref, w2_ref, o_ref, hn_ref, sem):
    @pl.when(pl.program_id(1) == 0)
    def _():
        r0 = pl.multiple_of(pl.program_id(0) * TM, TM)
        cp = pltpu.make_async_copy(x_hbm.at[pl.ds(r0, TM)], o_ref, sem)
        cp.start()
        cp.wait()
        _norm_block_to(o_ref, g_ref, hn_ref)

    o_ref[...] += _swiglu_tile(hn_ref[...], w1_ref, w3_ref, w2_ref)


def _ffn_dense(x, g, w1, w3, w2):
    t, d = x.shape
    f = w1.shape[1]
    return pl.pallas_call(
        _ffn_kernel,
        grid=(t // TM, f // TF),
        in_specs=[
            pl.BlockSpec(memory_space=pl.ANY),
            pl.BlockSpec((1, d), lambda i, j: (0, 0)),
            pl.BlockSpec((d, TF), lambda i, j: (0, j)),
            pl.BlockSpec((d, TF), lambda i, j: (0, j)),
            pl.BlockSpec((TF, d), lambda i, j: (j, 0)),
        ],
        out_specs=pl.BlockSpec((TM, d), lambda i, j: (i, 0)),
        out_shape=jax.ShapeDtypeStruct((t, d), F32),
        scratch_shapes=[pltpu.VMEM((TM, d), BF16), pltpu.SemaphoreType.DMA(())],
        compiler_params=pltpu.CompilerParams(
            dimension_semantics=("arbitrary", "arbitrary"), vmem_limit_bytes=V7X_VMEM_LIMIT),
        name="ffn_dense",
    )(x, g.reshape(1, d), w1, w3, w2)


def _rotary_tables(length, pos0):
    half = RET_D // 2
    inv = ROPE_THETA ** (-jnp.arange(half, dtype=F32) / half)
    pos = jnp.arange(length, dtype=F32) + pos0
    ang = pos[:, None] * inv[None, :]
    cos, sin = jnp.cos(ang), jnp.sin(ang)
    return jnp.concatenate([cos, cos], axis=-1), jnp.concatenate([-sin, sin], axis=-1)


def _decay_tables(c):
    log_g = jnp.log1p(-jnp.exp2(-5.0 - jnp.arange(RET_HEADS, dtype=F32)))
    idx = jnp.arange(c, dtype=F32)
    rel = idx[:, None] - idx[None, :]
    d_in = jnp.where(rel >= 0, jnp.exp(log_g[:, None, None] * jnp.maximum(rel, 0.0)), 0.0)
    d_q = jnp.exp(log_g[:, None] * (idx + 1.0)[None, :])
    d_k = jnp.exp(log_g[:, None] * (c - 1.0 - idx)[None, :])
    d_c = jnp.exp(log_g * c)
    return d_in, d_q, d_k, d_c


HIST31 = 32
HIST3 = 8


def _mixer_prompt_kernel(z_ref, cos_ref, sin_ref, din_ref, dq_ref, dk_ref, dc_ref, retg_ref,
                         c31w_ref, c31b_ref, clng_ref, clnb_ref, scw_ref, slng_ref, slnb_ref,
                         sguw_ref, sgub_ref, mix_in_ref,
                         mix_ref, ret_ref, b31_ref, b3_ref,
                         s_scr, ext31_scr, ext3_scr):
    del mix_in_ref
    c = pl.program_id(1)

    @pl.when(c == 0)
    def _():
        s_scr[...] = jnp.zeros_like(s_scr)
        ext31_scr[pl.ds(0, HIST31), :] = jnp.zeros((HIST31, GROUP_W), F32)
        ext3_scr[pl.ds(0, HIST3), :] = jnp.zeros((HIST3, GROUP_W), F32)

    cos = cos_ref[...]
    sin = sin_ref[...]

    for h in range(RET_HEADS):
        lo = h * RET_D
        q = z_ref[:, pl.ds(C_Q + lo, RET_D)]
        k = z_ref[:, pl.ds(C_K + lo, RET_D)]
        v = z_ref[:, pl.ds(C_V + lo, RET_D)]
        g = z_ref[:, pl.ds(C_G + lo, RET_D)]
        q = q * cos + pltpu.roll(q, RET_D // 2, axis=1) * sin
        k = (k * cos + pltpu.roll(k, RET_D // 2, axis=1) * sin) * (RET_D ** -0.5)
        qb = q.astype(BF16)
        vb = v.astype(BF16)
        scores = lax.dot_general(qb, k.astype(BF16), (((1,), (1,)), ((), ())),
                                 preferred_element_type=F32) * din_ref[h]
        inner = jnp.dot(scores.astype(BF16), vb, preferred_element_type=F32)
        s_old = s_scr[h]
        cross = jnp.dot(qb, s_old.astype(BF16), preferred_element_type=F32) * dq_ref[h]
        kd = (k * dk_ref[h]).astype(BF16)
        s_scr[h] = s_old * dc_ref[h] + lax.dot_general(kd, vb, (((0,), (0,)), ((), ())),
                                                       preferred_element_type=F32)
        r = inner + cross
        r = r * lax.rsqrt(jnp.mean(r * r, axis=-1, keepdims=True) + EPS)
        r = r * retg_ref[:, pl.ds(lo, RET_D)]
        mix_ref[:, pl.ds(lo, RET_D)] = (_silu(g) * r).astype(mix_ref.dtype)

    glu = z_ref[:, pl.ds(C_BL, GROUP_W)] * jax.nn.sigmoid(z_ref[:, pl.ds(C_BG, GROUP_W)])
    ext31_scr[pl.ds(HIST31, CHUNK), :] = glu
    conv = ext31_scr[pl.ds(HIST31 - (CONV_W - 1), CHUNK), :] * c31w_ref[pl.ds(0, 1), :]
    for j in range(1, CONV_W):
        conv = conv + ext31_scr[pl.ds(HIST31 - (CONV_W - 1) + j, CHUNK), :] * c31w_ref[pl.ds(j, 1), :]
    conv = conv + c31b_ref[...]
    out_b = _silu(_layernorm_rows(conv, clng_ref[...], clnb_ref[...]))
    mix_ref[:, pl.ds(GROUP_W, GROUP_W)] = out_b.astype(mix_ref.dtype)
    ext31_scr[pl.ds(0, HIST31), :] = ext31_scr[pl.ds(CHUNK, HIST31), :]

    gated = z_ref[:, pl.ds(C_CC, GROUP_W)] * z_ref[:, pl.ds(C_CH, GROUP_W)]
    ext3_scr[pl.ds(HIST3, CHUNK), :] = gated
    conv3 = ext3_scr[pl.ds(HIST3 - (SC_W - 1), CHUNK), :] * scw_ref[pl.ds(0, 1), :]
    for j in range(1, SC_W):
        conv3 = conv3 + ext3_scr[pl.ds(HIST3 - (SC_W - 1) + j, CHUNK), :] * scw_ref[pl.ds(j, 1), :]
    mix_ref[:, pl.ds(2 * GROUP_W, GROUP_W)] = (z_ref[:, pl.ds(C_CB, GROUP_W)] * conv3).astype(mix_ref.dtype)
    ext3_scr[pl.ds(0, HIST3), :] = ext3_scr[pl.ds(CHUNK, HIST3), :]

    vn = _layernorm_rows(z_ref[:, pl.ds(C_DV, GROUP_W)], slng_ref[...], slnb_ref[...])
    row = lax.broadcasted_iota(jnp.int32, (CHUNK, CHUNK), 0)
    col = lax.broadcasted_iota(jnp.int32, (CHUNK, CHUNK), 1)
    ch = GROUP_W // SGU_GROUPS
    for gi in range(SGU_GROUPS):
        wm = jnp.where(row >= col, sguw_ref[gi], 0.0).astype(BF16)
        mixed = jnp.dot(wm, vn[:, gi * ch:(gi + 1) * ch].astype(BF16), preferred_element_type=F32)
        mixed = mixed + sgub_ref[:, pl.ds(gi, 1)]
        du = z_ref[:, pl.ds(C_DU + gi * ch, ch)]
        mix_ref[:, pl.ds(3 * GROUP_W + gi * ch, ch)] = (du * mixed).astype(mix_ref.dtype)

    @pl.when(c == pl.num_programs(1) - 1)
    def _():
        ret_ref[...] = s_scr[...]
        b31_ref[...] = ext31_scr[pl.ds(HIST31 - (CONV_W - 1), CONV_W - 1), :]
        b3_ref[...] = ext3_scr[pl.ds(HIST3 - (SC_W - 1), SC_W - 1), :]


def _mixer_prompt(z, p, mix_init):
    cos, sin = _rotary_tables(SEQ, 0)
    d_in, d_q, d_k, d_c = _decay_tables(CHUNK)
    dq_t = jnp.broadcast_to(d_q[:, :, None], (RET_HEADS, CHUNK, RET_D))
    dk_t = jnp.broadcast_to(d_k[:, :, None], (RET_HEADS, CHUNK, RET_D))
    dc_t = jnp.broadcast_to(d_c[:, None, None], (RET_HEADS, RET_D, RET_D))

    def full(shape):
        return pl.BlockSpec(shape, lambda b, c: (0,) * len(shape))

    row = lambda a: a.reshape(1, -1)
    return pl.pallas_call(
        _mixer_prompt_kernel,
        grid=(BATCH, N_CHUNKS),
        in_specs=[
            pl.BlockSpec((CHUNK, D_IN), lambda b, c: (b * N_CHUNKS + c, 0)),
            pl.BlockSpec((CHUNK, RET_D), lambda b, c: (c, 0)),
            pl.BlockSpec((CHUNK, RET_D), lambda b, c: (c, 0)),
            full((RET_HEADS, CHUNK, CHUNK)),
            full((RET_HEADS, CHUNK, RET_D)),
            full((RET_HEADS, CHUNK, RET_D)),
            full((RET_HEADS, RET_D, RET_D)),
            full((1, GROUP_W)),
            full((CONV_W, GROUP_W)),
            full((1, GROUP_W)),
            full((1, GROUP_W)),
            full((1, GROUP_W)),
            full((SC_W, GROUP_W)),
            full((1, GROUP_W)),
            full((1, GROUP_W)),
            full((SGU_GROUPS, CHUNK, CHUNK)),
            full((CHUNK, SGU_GROUPS)),
            pl.BlockSpec(memory_space=pl.ANY),
        ],
        input_output_aliases={17: 0},
        out_specs=[
            pl.BlockSpec((CHUNK, D_MODEL), lambda b, c: (b * N_CHUNKS + c, 0)),
            pl.BlockSpec((None, RET_HEADS, RET_D, RET_D), lambda b, c: (b, 0, 0, 0)),
            pl.BlockSpec((None, CONV_W - 1, GROUP_W), lambda b, c: (b, 0, 0)),
            pl.BlockSpec((None, SC_W - 1, GROUP_W), lambda b, c: (b, 0, 0)),
        ],
        out_shape=[
            jax.ShapeDtypeStruct((T_ALL, D_MODEL), BF16),
            jax.ShapeDtypeStruct((BATCH, RET_HEADS, RET_D, RET_D), F32),
            jax.ShapeDtypeStruct((BATCH, CONV_W - 1, GROUP_W), F32),
            jax.ShapeDtypeStruct((BATCH, SC_W - 1, GROUP_W), F32),
        ],
        scratch_shapes=[
            pltpu.VMEM((RET_HEADS, RET_D, RET_D), F32),
            pltpu.VMEM((HIST31 + CHUNK, GROUP_W), F32),
            pltpu.VMEM((HIST3 + CHUNK, GROUP_W), F32),
        ],
        compiler_params=pltpu.CompilerParams(
            dimension_semantics=("arbitrary", "arbitrary"), vmem_limit_bytes=V7X_VMEM_LIMIT),
        name="mixer_prompt",
    )(z, cos, sin, d_in, dq_t, dk_t, dc_t, row(p["ret_norm_g"]), p["conv31_w"], row(p["conv31_b"]),
      row(p["conv_ln_g"]), row(p["conv_ln_b"]), p["sconv_w"], row(p["sgu_ln_g"]), row(p["sgu_ln_b"]),
      p["sgu_w"], p["sgu_b"].T, mix_init)


EXT31_S = 40
EXT3_S = 8


def _mixer_sample_kernel(z_ref, cos_ref, sin_ref, din_ref, dq_ref, dk_ref, dc_ref, retg_ref,
                         c31w_ref, c31b_ref, clng_ref, clnb_ref, scw_ref, slng_ref, slnb_ref,
                         sguw_ref, sgub_ref, sret_ref, s31_ref, s3_ref,
                         mix_ref, ret_ref, b31_ref, b3_ref, vn_ref,
                         ext31_scr, ext3_scr):
    cos = cos_ref[...]
    sin = sin_ref[...]
    n_hist31 = CONV_W - 1
    n_hist3 = SC_W - 1

    for h in range(RET_HEADS):
        lo = h * RET_D
        q = z_ref[:, :, pl.ds(C_Q + lo, RET_D)]
        k = z_ref[:, :, pl.ds(C_K + lo, RET_D)]
        g = z_ref[:, :, pl.ds(C_G + lo, RET_D)]
        q = q * cos + pltpu.roll(q, RET_D // 2, axis=2) * sin
        k = (k * cos + pltpu.roll(k, RET_D // 2, axis=2) * sin) * (RET_D ** -0.5)
        s_old = sret_ref[:, h]
        r = lax.dot_general(q, s_old, (((2,), (1,)), ((0,), (0,))),
                            preferred_element_type=F32) * dq_ref[h]
        for s in range(DEC_SEQ):
            k_s = z_ref[:, pl.ds(s, 1), pl.ds(C_K + lo, RET_D)]
            k_s = (k_s * cos[s:s + 1] + pltpu.roll(k_s, RET_D // 2, axis=2) * sin[s:s + 1]) * (RET_D ** -0.5)
            v_s = z_ref[:, pl.ds(s, 1), pl.ds(C_V + lo, RET_D)]
            score = jnp.sum(q * k_s, axis=-1, keepdims=True) * din_ref[h, s]
            r = r + score * v_s
        kd = k * dk_ref[h]
        v = z_ref[:, :, pl.ds(C_V + lo, RET_D)]
        ret_ref[:, h] = s_old * dc_ref[h] + lax.dot_general(
            kd, v, (((1,), (1,)), ((0,), (0,))), preferred_element_type=F32)
        r = r * lax.rsqrt(jnp.mean(r * r, axis=-1, keepdims=True) + EPS)
        r = r * retg_ref[:, pl.ds(lo, RET_D)]
        mix_ref[:, :, pl.ds(lo, RET_D)] = _silu(g) * r

    glu = z_ref[:, :, pl.ds(C_BL, GROUP_W)] * jax.nn.sigmoid(z_ref[:, :, pl.ds(C_BG, GROUP_W)])
    ext31_scr[:, pl.ds(0, n_hist31), :] = s31_ref[...]
    ext31_scr[:, pl.ds(n_hist31, DEC_SEQ), :] = glu
    conv = ext31_scr[:, pl.ds(0, DEC_SEQ), :] * c31w_ref[pl.ds(0, 1), :]
    for j in range(1, CONV_W):
        conv = conv + ext31_scr[:, pl.ds(j, DEC_SEQ), :] * c31w_ref[pl.ds(j, 1), :]
    conv = conv + c31b_ref[...]
    mix_ref[:, :, pl.ds(GROUP_W, GROUP_W)] = _silu(_layernorm_rows(conv, clng_ref[...], clnb_ref[...]))
    b31_ref[...] = ext31_scr[:, pl.ds(DEC_SEQ, n_hist31), :]

    gated = z_ref[:, :, pl.ds(C_CC, GROUP_W)] * z_ref[:, :, pl.ds(C_CH, GROUP_W)]
    ext3_scr[:, pl.ds(0, n_hist3), :] = s3_ref[...]
    ext3_scr[:, pl.ds(n_hist3, DEC_SEQ), :] = gated
    conv3 = ext3_scr[:, pl.ds(0, DEC_SEQ), :] * scw_ref[pl.ds(0, 1), :]
    for j in range(1, SC_W):
        conv3 = conv3 + ext3_scr[:, pl.ds(j, DEC_SEQ), :] * scw_ref[pl.ds(j, 1), :]
    mix_ref[:, :, pl.ds(2 * GROUP_W, GROUP_W)] = z_ref[:, :, pl.ds(C_CB, GROUP_W)] * conv3
    b3_ref[...] = ext3_scr[:, pl.ds(DEC_SEQ, n_hist3), :]

    vn = _layernorm_rows(z_ref[:, :, pl.ds(C_DV, GROUP_W)], slng_ref[...], slnb_ref[...])
    vn_ref[...] = vn
    mixed = sgub_ref[...]
    for s in range(DEC_SEQ):
        mixed = mixed + sguw_ref[s] * vn_ref[:, pl.ds(s, 1), :]
    mix_ref[:, :, pl.ds(3 * GROUP_W, GROUP_W)] = z_ref[:, :, pl.ds(C_DU, GROUP_W)] * mixed


def _mixer_sample(z_s, p, s_ret, s_31, s_3):
    cos, sin = _rotary_tables(DEC_SEQ, PAST_LEN)
    d_in, d_q, d_k, d_c = _decay_tables(DEC_SEQ)
    din_t = jnp.swapaxes(d_in, 1, 2)[:, :, :, None]
    dq_t = jnp.broadcast_to(d_q[:, :, None], (RET_HEADS, DEC_SEQ, RET_D))
    dk_t = jnp.broadcast_to(d_k[:, :, None], (RET_HEADS, DEC_SEQ, RET_D))
    dc_t = jnp.broadcast_to(d_c[:, None, None], (RET_HEADS, RET_D, RET_D))
    ch = GROUP_W // SGU_GROUPS
    w4 = jnp.tril(p["sgu_w"][:, :DEC_SEQ, :DEC_SEQ])
    w_t = jnp.repeat(jnp.transpose(w4, (2, 1, 0)), ch, axis=-1)
    b_t = jnp.repeat(p["sgu_b"][:, :DEC_SEQ].T, ch, axis=-1)
    bt = SAMPLE_BT

    def full(shape):
        return pl.BlockSpec(shape, lambda i: (0,) * len(shape))

    row = lambda a: a.reshape(1, -1)
    return pl.pallas_call(
        _mixer_sample_kernel,
        grid=(DEC_BATCH // bt,),
        in_specs=[
            pl.BlockSpec((bt, DEC_SEQ, D_IN), lambda i: (i, 0, 0)),
            full((DEC_SEQ, RET_D)),
            full((DEC_SEQ, RET_D)),
            full((RET_HEADS, DEC_SEQ, DEC_SEQ, 1)),
            full((RET_HEADS, DEC_SEQ, RET_D)),
            full((RET_HEADS, DEC_SEQ, RET_D)),
            full((RET_HEADS, RET_D, RET_D)),
            full((1, GROUP_W)),
            full((CONV_W, GROUP_W)),
            full((1, GROUP_W)),
            full((1, GROUP_W)),
            full((1, GROUP_W)),
            full((SC_W, GROUP_W)),
            full((1, GROUP_W)),
            full((1, GROUP_W)),
            full((DEC_SEQ, DEC_SEQ, GROUP_W)),
            full((DEC_SEQ, GROUP_W)),
            pl.BlockSpec((bt, RET_HEADS, RET_D, RET_D), lambda i: (i, 0, 0, 0)),
            pl.BlockSpec((bt, CONV_W - 1, GROUP_W), lambda i: (i, 0, 0)),
            pl.BlockSpec((bt, SC_W - 1, GROUP_W), lambda i: (i, 0, 0)),
        ],
        out_specs=[
            pl.BlockSpec((bt, DEC_SEQ, D_MODEL), lambda i: (i, 0, 0)),
            pl.BlockSpec((bt, RET_HEADS, RET_D, RET_D), lambda i: (i, 0, 0, 0)),
            pl.BlockSpec((bt, CONV_W - 1, GROUP_W), lambda i: (i, 0, 0)),
            pl.BlockSpec((bt, SC_W - 1, GROUP_W), lambda i: (i, 0, 0)),
            pl.BlockSpec((bt, DEC_SEQ, GROUP_W), lambda i: (i, 0, 0)),
        ],
        out_shape=[
            jax.ShapeDtypeStruct((DEC_BATCH, DEC_SEQ, D_MODEL), F32),
            jax.ShapeDtypeStruct((DEC_BATCH, RET_HEADS, RET_D, RET_D), F32),
            jax.ShapeDtypeStruct((DEC_BATCH, CONV_W - 1, GROUP_W), F32),
            jax.ShapeDtypeStruct((DEC_BATCH, SC_W - 1, GROUP_W), F32),
            jax.ShapeDtypeStruct((DEC_BATCH, DEC_SEQ, GROUP_W), F32),
        ],
        scratch_shapes=[
            pltpu.VMEM((bt, EXT31_S, GROUP_W), F32),
            pltpu.VMEM((bt, EXT3_S, GROUP_W), F32),
        ],
        compiler_params=pltpu.CompilerParams(
            dimension_semantics=("arbitrary",), vmem_limit_bytes=V7X_VMEM_LIMIT),
        name="mixer_sample",
    )(z_s, cos, sin, din_t, dq_t, dk_t, dc_t, row(p["ret_norm_g"]), p["conv31_w"], row(p["conv31_b"]),
      row(p["conv_ln_g"]), row(p["conv_ln_b"]), p["sconv_w"], row(p["sgu_ln_g"]), row(p["sgu_ln_b"]),
      w_t, b_t, s_ret, s_31, s_3)


def _to_token_major(dst, val):
    n = val.shape[0]
    for a in range(TOK_ROWS):
        dst[pl.ds(a, n, stride=TOK_ROWS), :] = val[:, a * LANES:(a + 1) * LANES]


def _router_kernel(x_ref, g_ref, rw_ref, e_ref, p_ref, c_ref):
    h = _rms_rows(x_ref[...], g_ref[...])
    logits = jnp.dot(h, rw_ref[...], preferred_element_type=F32, precision=lax.Precision.HIGHEST)
    lane = lax.broadcasted_iota(jnp.int32, logits.shape, 1)
    m1 = jnp.max(logits, axis=-1, keepdims=True)
    i1 = jnp.min(jnp.where(logits == m1, lane, N_EXPERTS), axis=-1, keepdims=True)
    rest = jnp.where(lane == i1, -jnp.inf, logits)
    m2 = jnp.max(rest, axis=-1, keepdims=True)
    i2 = jnp.min(jnp.where(rest == m2, lane, N_EXPERTS), axis=-1, keepdims=True)
    e2 = jnp.exp(m2 - m1)
    den = 1.0 + e2
    e_ref[...] = jnp.where(lane == 0, i1, jnp.where(lane == 1, i2, 0))
    p_ref[...] = jnp.where(lane == 0, 1.0 / den, jnp.where(lane == 1, e2 / den, 0.0))
    c_ref[pl.ds(0, 1), :] = jnp.sum(jnp.where(lane == i1, 1.0, 0.0), axis=0, keepdims=True).astype(jnp.int32)
    c_ref[pl.ds(1, 1), :] = jnp.sum(jnp.where(lane == i2, 1.0, 0.0), axis=0, keepdims=True).astype(jnp.int32)


def _router(x, g, rw):
    t, d = x.shape
    return pl.pallas_call(
        _router_kernel,
        grid=(N_TB,),
        in_specs=[
            pl.BlockSpec((MOE_TB, d), lambda i: (i, 0)),
            pl.BlockSpec((1, d), lambda i: (0, 0)),
            pl.BlockSpec((d, N_EXPERTS), lambda i: (0, 0)),
        ],
        out_specs=[
            pl.BlockSpec((MOE_TB, N_EXPERTS), lambda i: (i, 0)),
            pl.BlockSpec((MOE_TB, N_EXPERTS), lambda i: (i, 0)),
            pl.BlockSpec((None, TOP_K, N_EXPERTS), lambda i: (i, 0, 0)),
        ],
        out_shape=[
            jax.ShapeDtypeStruct((t, N_EXPERTS), jnp.int32),
            jax.ShapeDtypeStruct((t, N_EXPERTS), F32),
            jax.ShapeDtypeStruct((N_TB, TOP_K, N_EXPERTS), jnp.int32),
        ],
        compiler_params=pltpu.CompilerParams(
            dimension_semantics=("arbitrary",), vmem_limit_bytes=V7X_VMEM_LIMIT),
        name="moe_router",
    )(x, g.reshape(1, d), rw)


def _pick(table, idx):
    lanes = jnp.arange(N_EXPERTS, dtype=jnp.int32)
    return jnp.sum(jnp.where(idx[..., None] == lanes, table, 0), axis=-1)


def _expert_layout(cnt):
    tot = jnp.sum(cnt, axis=(0, 1))
    start = jnp.cumsum(tot) - tot
    blk = jnp.sum(cnt, axis=1)
    blk_base = start[None, :] + jnp.cumsum(blk, axis=0) - blk
    base = jnp.stack([blk_base, blk_base + cnt[:, 0]], axis=1)
    n_sb = (tot + SB - 1) // SB
    sb_end = jnp.cumsum(n_sb)
    n_used = sb_end[-1]
    s_idx = jnp.arange(NS_MAX, dtype=jnp.int32)
    s_cl = jnp.minimum(s_idx, n_used - 1)
    sb_e = jnp.sum(s_cl[:, None] >= sb_end[None, :], axis=-1).astype(jnp.int32)
    local = s_cl - _pick(sb_end - n_sb, sb_e)
    sb_row0 = _pick(start, sb_e) + local * SB
    sb_rows = jnp.where(s_idx < n_used, jnp.clip(_pick(tot, sb_e) - local * SB, 0, SB), 0)
    i32 = lambda a: a.astype(jnp.int32)
    return i32(base), sb_e, i32(sb_row0), i32(sb_rows), i32(n_used).reshape(1)


def _dest_kernel(e_ref, b_ref, d_ref):
    eid = e_ref[...]
    lane = lax.broadcasted_iota(jnp.int32, eid.shape, 1)
    row = lax.broadcasted_iota(jnp.int32, (MOE_TB, MOE_TB), 0)
    col = lax.broadcasted_iota(jnp.int32, (MOE_TB, MOE_TB), 1)
    earlier = jnp.where(row > col, 1.0, 0.0).astype(BF16)
    base = b_ref[...].astype(F32)

    def dest_of(k):
        mine = lane == eid[:, k:k + 1]
        before = jnp.dot(earlier, jnp.where(mine, 1.0, 0.0).astype(BF16), preferred_element_type=F32)
        return jnp.sum(jnp.where(mine, before + base[k:k + 1], 0.0), axis=-1, keepdims=True)

    d_ref[...] = jnp.where(lane == 0, dest_of(0), jnp.where(lane == 1, dest_of(1), 0.0)).astype(jnp.int32)


def _dest_rows(eid, base):
    return pl.pallas_call(
        _dest_kernel,
        grid=(N_TB,),
        in_specs=[
            pl.BlockSpec((MOE_TB, N_EXPERTS), lambda i: (i, 0)),
            pl.BlockSpec((None, TOP_K, N_EXPERTS), lambda i: (i, 0, 0)),
        ],
        out_specs=pl.BlockSpec((MOE_TB, N_EXPERTS), lambda i: (i, 0)),
        out_shape=jax.ShapeDtypeStruct((T_ALL, N_EXPERTS), jnp.int32),
        compiler_params=pltpu.CompilerParams(
            dimension_semantics=("arbitrary",), vmem_limit_bytes=V7X_VMEM_LIMIT),
        name="moe_dest_rows",
    )(eid, base)


def _dispatch_kernel(d_ref, x_ref, g_ref, xg_hbm, stage, zeros, sem, zsem):
    i = pl.program_id(0)
    n = pl.num_programs(0)
    slot = i % 2
    blk_rows = MOE_TB * TOK_ROWS

    def wait_slot(sl):
        for _ in range(TOP_K):
            pltpu.make_async_copy(stage.at[sl], xg_hbm.at[pl.ds(0, blk_rows)], sem.at[sl]).wait()

    @pl.when(i >= 2)
    def _():
        wait_slot(slot)

    _to_token_major(stage.at[slot], _rms_rows(x_ref[...], g_ref[...]))

    def body(r, carry):
        src = stage.at[slot, pl.ds(pl.multiple_of(r * TOK_ROWS, TOK_ROWS), TOK_ROWS)]
        for k in range(TOP_K):
            row = d_ref[0, 0, TOP_K * r + k]
            dst = xg_hbm.at[pl.ds(pl.multiple_of(row * TOK_ROWS, TOK_ROWS), TOK_ROWS)]
            pltpu.make_async_copy(src, dst, sem.at[slot]).start()
        return carry

    lax.fori_loop(0, MOE_TB, body, 0)

    @pl.when(i == n - 1)
    def _():
        zeros[...] = jnp.zeros_like(zeros)
        tail = [pltpu.make_async_copy(zeros, xg_hbm.at[pl.ds((N_SLOTS + q * (SUB // TOK_ROWS)) * TOK_ROWS, SUB)], zsem)
                for q in range(TOK_ROWS)]
        for cp in tail:
            cp.start()
        for cp in tail:
            cp.wait()
        wait_slot(slot)
        wait_slot(1 - slot)


def _dispatch(x, g, dest_flat):
    t, d = x.shape
    return pl.pallas_call(
        _dispatch_kernel,
        grid=(N_TB,),
        in_specs=[
            pl.BlockSpec((1, 1, TOP_K * MOE_TB), lambda i: (i, 0, 0), memory_space=pltpu.SMEM),
            pl.BlockSpec((MOE_TB, d), lambda i: (i, 0)),
            pl.BlockSpec((1, d), lambda i: (0, 0)),
        ],
        out_specs=pl.BlockSpec(memory_space=pl.ANY),
        out_shape=jax.ShapeDtypeStruct(((N_SLOTS + SUB) * TOK_ROWS, LANES), F32),
        scratch_shapes=[
            pltpu.VMEM((2, MOE_TB * TOK_ROWS, LANES), F32),
            pltpu.VMEM((SUB, LANES), F32),
            pltpu.SemaphoreType.DMA((2,)),
            pltpu.SemaphoreType.DMA(()),
        ],
        compiler_params=pltpu.CompilerParams(
            dimension_semantics=("arbitrary",), vmem_limit_bytes=V7X_VMEM_LIMIT),
        name="moe_dispatch",
    )(dest_flat, x, g.reshape(1, d))


def _moe_ffn_kernel(se_ref, r0_ref, nr_ref, nu_ref, xg_hbm, w1_ref, w3_ref, w2_ref, y_hbm,
                    x_scr, acc, stage, w1b, w3b, w2b, sem_in, sem_out):
    s = pl.program_id(0)
    j = pl.program_id(1)
    nj = pl.num_programs(1)
    rows = nr_ref[s]
    row0 = r0_ref[s]
    nsub = lax.shift_right_logical(rows + (SUB - 1), int(math.log2(SUB)))
    sub_rows = SUB * TOK_ROWS

    @pl.when(nsub > 0)
    def _():
        w1b[...] = w1_ref[...].astype(BF16)
        w3b[...] = w3_ref[...].astype(BF16)
        w2b[...] = w2_ref[...].astype(BF16)

    def sub(r):
        return pl.ds(pl.multiple_of(r * SUB, SUB), SUB)

    def tile(r):
        xs = x_scr[sub(r), :]
        a = jnp.dot(xs, w1b[...], preferred_element_type=F32)
        b = jnp.dot(xs, w3b[...], preferred_element_type=F32)
        u = (_silu(a) * b).astype(BF16)
        return jnp.dot(u, w2b[...], preferred_element_type=F32)

    def in_copy(r, slot):
        first = pl.multiple_of((row0 + r * SUB) * TOK_ROWS, TOK_ROWS)
        return pltpu.make_async_copy(xg_hbm.at[pl.ds(first, sub_rows)], stage.at[slot], sem_in.at[slot])

    @pl.when(j == 0)
    def _():
        @pl.when(nsub > 0)
        def _():
            in_copy(0, 0).start()

        def body(r, carry):
            slot = r % 2
            in_copy(r, slot).wait()

            @pl.when(r + 1 < nsub)
            def _():
                in_copy(r + 1, 1 - slot).start()

            for a in range(TOK_ROWS):
                x_scr[sub(r), pl.ds(a * LANES, LANES)] = stage[slot, pl.ds(a, SUB, stride=TOK_ROWS), :].astype(BF16)
            acc[sub(r), :] = tile(r)
            return carry

        lax.fori_loop(0, nsub, body, 0)

    @pl.when((j > 0) & (j < nj - 1))
    def _():
        def body(r, carry):
            acc[sub(r), :] += tile(r)
            return carry

        lax.fori_loop(0, nsub, body, 0)

    @pl.when(j == nj - 1)
    def _():
        def out_copies(r, act):
            nv = jnp.clip(rows - r * SUB, 0, SUB)
            bit = SUB
            while bit >= 1:
                off = nv & (-2 * bit)
                src = stage.at[0, pl.ds(pl.multiple_of(off * TOK_ROWS, TOK_ROWS), bit * TOK_ROWS)]
                first = pl.multiple_of((row0 + r * SUB + off) * TOK_ROWS, TOK_ROWS)
                cp = pltpu.make_async_copy(src, y_hbm.at[pl.ds(first, bit * TOK_ROWS)], sem_out)
                pl.when((nv & bit) != 0)(functools.partial(act, cp))
                bit //= 2

        def body(r, carry):
            y = acc[sub(r), :] + tile(r)

            @pl.when(r > 0)
            def _():
                out_copies(r - 1, lambda cp: cp.wait())

            _to_token_major(stage.at[0], y)
            out_copies(r, lambda cp: cp.start())
            return carry

        lax.fori_loop(0, nsub, body, 0)

        @pl.when(nsub > 0)
        def _():
            out_copies(nsub - 1, lambda cp: cp.wait())


def _moe_ffn(xg, w1, w3, w2, sb_e, sb_row0, sb_rows, n_used):
    d = w1.shape[1]
    f = w1.shape[2]
    nj = f // TF

    def jmap(s, j, nu):
        return jnp.where(s < nu[0], j, nj - 1)

    return pl.pallas_call(
        _moe_ffn_kernel,
        grid_spec=pltpu.PrefetchScalarGridSpec(
            num_scalar_prefetch=4,
            grid=(NS_MAX, nj),
            in_specs=[
                pl.BlockSpec(memory_space=pl.ANY),
                pl.BlockSpec((None, d, TF), lambda s, j, se, r0, nr, nu: (se[s], 0, jmap(s, j, nu))),
                pl.BlockSpec((None, d, TF), lambda s, j, se, r0, nr, nu: (se[s], 0, jmap(s, j, nu))),
                pl.BlockSpec((None, TF, d), lambda s, j, se, r0, nr, nu: (se[s], jmap(s, j, nu), 0)),
            ],
            out_specs=pl.BlockSpec(memory_space=pl.ANY),
            scratch_shapes=[
                pltpu.VMEM((SB, d), BF16),
                pltpu.VMEM((SB, d), F32),
                pltpu.VMEM((2, SUB * TOK_ROWS, LANES), F32),
                pltpu.VMEM((d, TF), BF16),
                pltpu.VMEM((d, TF), BF16),
                pltpu.VMEM((TF, d), BF16),
                pltpu.SemaphoreType.DMA((2,)),
                pltpu.SemaphoreType.DMA(()),
            ],
        ),
        out_shape=jax.ShapeDtypeStruct((N_SLOTS * TOK_ROWS, LANES), F32),
        compiler_params=pltpu.CompilerParams(
            dimension_semantics=("arbitrary", "arbitrary"), vmem_limit_bytes=V7X_VMEM_LIMIT),
        name="moe_ffn",
    )(sb_e, sb_row0, sb_rows, n_used, xg, w1, w3, w2)


N_TB_PROMPT = T_PROMPT // MOE_TB


def _combine_kernel(d_ref, dn_ref, x_ref, p_ref, g_ref, y_hbm, op_ref, os_ref, stage, sem, *, final_norm):
    i = pl.program_id(0)
    n = pl.num_programs(0)
    slot = i % 2
    blk_rows = TOP_K * MOE_TB * TOK_ROWS

    def start_block(dref, sl):
        def body(q, carry):
            src = y_hbm.at[pl.ds(pl.multiple_of(dref[0, 0, q] * TOK_ROWS, TOK_ROWS), TOK_ROWS)]
            dst = stage.at[sl, pl.ds(pl.multiple_of(q * TOK_ROWS, TOK_ROWS), TOK_ROWS)]
            pltpu.make_async_copy(src, dst, sem.at[sl]).start()
            return carry

        lax.fori_loop(0, TOP_K * MOE_TB, body, 0)

    @pl.when(i == 0)
    def _():
        start_block(d_ref, 0)

    @pl.when(i + 1 < n)
    def _():
        start_block(dn_ref, 1 - slot)

    pltpu.make_async_copy(y_hbm.at[pl.ds(0, blk_rows)], stage.at[slot], sem.at[slot]).wait()
    g0 = p_ref[:, 0:1]
    g1 = p_ref[:, 1:2]
    pieces = []
    for a in range(TOK_ROWS):
        y0 = stage[slot, pl.ds(a, MOE_TB, stride=TOP_K * TOK_ROWS), :]
        y1 = stage[slot, pl.ds(TOK_ROWS + a, MOE_TB, stride=TOP_K * TOK_ROWS), :]
        pieces.append(x_ref[:, pl.ds(a * LANES, LANES)] + (y0 * g0 + y1 * g1))
    y = jnp.concatenate(pieces, axis=1)
    if final_norm:
        y = _rms_rows(y, g_ref[...])

    @pl.when(i < N_TB_PROMPT)
    def _():
        op_ref[...] = y

    @pl.when(i >= N_TB_PROMPT)
    def _():
        os_ref[...] = y


def _moe_combine(x, y, dest_flat, gate, g, final_norm):
    t, d = x.shape
    return pl.pallas_call(
        functools.partial(_combine_kernel, final_norm=final_norm),
        grid=(N_TB,),
        in_specs=[
            pl.BlockSpec((1, 1, TOP_K * MOE_TB), lambda i: (i, 0, 0), memory_space=pltpu.SMEM),
            pl.BlockSpec((1, 1, TOP_K * MOE_TB), lambda i: (jnp.minimum(i + 1, N_TB - 1), 0, 0),
                         memory_space=pltpu.SMEM),
            pl.BlockSpec((MOE_TB, d), lambda i: (i, 0)),
            pl.BlockSpec((MOE_TB, N_EXPERTS), lambda i: (i, 0)),
            pl.BlockSpec((1, d), lambda i: (0, 0)),
            pl.BlockSpec(memory_space=pl.ANY),
        ],
        out_specs=[
            pl.BlockSpec((MOE_TB, d), lambda i: (jnp.minimum(i, N_TB_PROMPT - 1), 0)),
            pl.BlockSpec((MOE_TB, d), lambda i: (jnp.maximum(i - N_TB_PROMPT, 0), 0)),
        ],
        out_shape=[
            jax.ShapeDtypeStruct((T_PROMPT, d), F32),
            jax.ShapeDtypeStruct((T_SAMPLE, d), F32),
        ],
        scratch_shapes=[
            pltpu.VMEM((2, TOP_K * MOE_TB * TOK_ROWS, LANES), F32),
            pltpu.SemaphoreType.DMA((2,)),
        ],
        compiler_params=pltpu.CompilerParams(
            dimension_semantics=("arbitrary",), vmem_limit_bytes=V7X_VMEM_LIMIT),
        name="moe_combine",
    )(dest_flat, dest_flat, x, gate, g.reshape(1, d), y)


def _moe(x, norm_g, rw, w1, w3, w2, out_g, final_norm):
    eid, gate, cnt = _router(x, norm_g, rw)
    base, sb_e, sb_row0, sb_rows, n_used = _expert_layout(cnt)
    dest = _dest_rows(eid, base)
    dest_flat = dest[:, :TOP_K].reshape(N_TB, 1, TOP_K * MOE_TB)
    xg = _dispatch(x, norm_g, dest_flat)
    y = _moe_ffn(xg, w1, w3, w2, sb_e, sb_row0, sb_rows, n_used)
    return _moe_combine(x, y, dest_flat, gate, out_g, final_norm)


def kernel(x_prompt, x_sample, state_ret, state_conv31, state_conv3, mix_norm_g, w_in, ret_norm_g, conv31_w, conv31_b, conv_ln_g, conv_ln_b, sconv_w, sgu_ln_g, sgu_ln_b, sgu_w, sgu_b, w_out, ffn_norm_g, dense_w1, dense_w3, dense_w2, router_w, moe_w1, moe_w3, moe_w2, final_norm_g):
    assert DEPTH == 2, "layer 0 has the dense channel mixer; layer 1's expert mixer ends with the final norm"
    x = jnp.concatenate([x_prompt.reshape(T_PROMPT, D_MODEL), x_sample.reshape(T_SAMPLE, D_MODEL)], axis=0)
    rets_p, rets_s, c31_p, c31_s, c3_p, c3_s, vns = [], [], [], [], [], [], []
    for l in range(DEPTH):
        p = dict(ret_norm_g=ret_norm_g[l], conv31_w=conv31_w[l], conv31_b=conv31_b[l],
                 conv_ln_g=conv_ln_g[l], conv_ln_b=conv_ln_b[l], sconv_w=sconv_w[l],
                 sgu_ln_g=sgu_ln_g[l], sgu_ln_b=sgu_ln_b[l], sgu_w=sgu_w[l], sgu_b=sgu_b[l])
        z = _norm_matmul(x, mix_norm_g[l], w_in[l])
        z_s = z[T_PROMPT:].reshape(DEC_BATCH, DEC_SEQ, D_IN)
        mix_s, ret_s, b31_s, b3_s, vn_s = _mixer_sample(z_s, p, state_ret[l], state_conv31[l], state_conv3[l])
        mix_init = jnp.pad(mix_s.reshape(T_SAMPLE, D_MODEL).astype(BF16), ((T_PROMPT, 0), (0, 0)))
        mix, ret_p, b31_p, b3_p = _mixer_prompt(z, p, mix_init)
        x = _matmul_residual(mix, w_out[l], x)
        j = l // 2
        if l % 2 == 0:
            x = _ffn_dense(x, ffn_norm_g[l], dense_w1[j], dense_w3[j], dense_w2[j])
        else:
            y_p, y_s = _moe(x, ffn_norm_g[l], router_w[j], moe_w1[j], moe_w3[j], moe_w2[j],
                            final_norm_g, final_norm=True)
        rets_p.append(ret_p); rets_s.append(ret_s)
        c31_p.append(b31_p); c31_s.append(b31_s)
        c3_p.append(b3_p); c3_s.append(b3_s)
        vns.append(vn_s)
    y_prompt = y_p.reshape(BATCH, SEQ, D_MODEL)
    y_sample = y_s.reshape(DEC_BATCH, DEC_SEQ, D_MODEL)
    return (y_prompt, y_sample, jnp.stack(rets_p), jnp.stack(rets_s), jnp.stack(c31_p), jnp.stack(c31_s),
            jnp.stack(c3_p), jnp.stack(c3_s), jnp.stack(vns))
```

```python
import functools
import math

import jax
import jax.numpy as jnp
from jax import lax
from jax.experimental import pallas as pl
from jax.experimental.pallas import tpu as pltpu

F32 = jnp.float32
BF16 = jnp.bfloat16

D_MODEL = 2048
BATCH = 4
SEQ = 2048
DEPTH = 2
DEC_BATCH = 128
DEC_SEQ = 4
PAST_LEN = 16384
GROUP_W = 512
D_IN = 11 * GROUP_W
RET_HEADS = 4
RET_D = 128
CHUNK = 128
ROPE_THETA = 10000.0
CONV_W = 31
SC_W = 3
SGU_GROUPS = 4
D_FF = 5632
N_EXPERTS = 8
TOP_K = 2
EPS = 1e-6

T_PROMPT = BATCH * SEQ
T_SAMPLE = DEC_BATCH * DEC_SEQ
T_ALL = T_PROMPT + T_SAMPLE
N_CHUNKS = SEQ // CHUNK

C_Q, C_K, C_V, C_G, C_BL, C_BG, C_CB, C_CC, C_CH, C_DU, C_DV = (i * GROUP_W for i in range(11))

V7X_VMEM_LIMIT = 56 * 1024 * 1024

TM = 1088
NORM_ROWS = 272
TN_IN = 512
TN_OUT = 512
TF = 256
SAMPLE_BT = 8

LANES = 128
TOK_ROWS = D_MODEL // LANES
MOE_TB = 256
N_TB = T_ALL // MOE_TB
N_SLOTS = T_ALL * TOP_K
SUB = 256
CHUNK_ROWS = 3 * SUB
SB = 3 * CHUNK_ROWS
NS_MAX = -(-N_SLOTS // SB) + N_EXPERTS


def _rms_rows(x, g):
    ms = jnp.mean(x * x, axis=-1, keepdims=True)
    return (x * lax.rsqrt(ms + EPS)) * g


def _layernorm_rows(x, g, b):
    mu = jnp.mean(x, axis=-1, keepdims=True)
    xc = x - mu
    return xc * lax.rsqrt(jnp.mean(xc * xc, axis=-1, keepdims=True) + EPS) * g + b


def _silu(x):
    return x * jax.nn.sigmoid(x)


def _norm_block_to(x_ref, g_ref, dst_ref):
    rows = x_ref.shape[0]
    g = g_ref[...]

    def body(c, carry):
        r0 = pl.multiple_of(c * NORM_ROWS, NORM_ROWS)
        x = x_ref[pl.ds(r0, NORM_ROWS), :]
        dst_ref[pl.ds(r0, NORM_ROWS), :] = _rms_rows(x, g).astype(dst_ref.dtype)
        return carry

    lax.fori_loop(0, rows // NORM_ROWS, body, 0)


def _norm_matmul_kernel(x_ref, g_ref, w_ref, o_ref, hn_ref):
    @pl.when(pl.program_id(1) == 0)
    def _():
        _norm_block_to(x_ref, g_ref, hn_ref)

    o_ref[...] = jnp.dot(hn_ref[...], w_ref[...].astype(BF16), preferred_element_type=F32)


def _norm_matmul(x, g, w_all, layer):
    t, d = x.shape
    n = w_all.shape[2]
    return pl.pallas_call(
        _norm_matmul_kernel,
        grid=(t // TM, n // TN_IN),
        in_specs=[
            pl.BlockSpec((TM, d), lambda i, j: (i, 0)),
            pl.BlockSpec((1, d), lambda i, j: (0, 0)),
            pl.BlockSpec((None, d, TN_IN), lambda i, j: (layer, 0, j)),
        ],
        out_specs=pl.BlockSpec((TM, TN_IN), lambda i, j: (i, j)),
        out_shape=jax.ShapeDtypeStruct((t, n), F32),
        scratch_shapes=[pltpu.VMEM((TM, d), BF16)],
        compiler_params=pltpu.CompilerParams(
            dimension_semantics=("arbitrary", "arbitrary"), vmem_limit_bytes=V7X_VMEM_LIMIT),
        name="norm_in_proj",
    )(x, g.reshape(1, d), w_all)


def _matmul_residual_kernel(a_ref, w_ref, r_ref, o_ref):
    o_ref[...] = r_ref[...] + jnp.dot(a_ref[...], w_ref[...].astype(BF16), preferred_element_type=F32)


def _matmul_residual(a, w_all, layer, r):
    t, k = a.shape
    n = w_all.shape[2]
    return pl.pallas_call(
        _matmul_residual_kernel,
        grid=(t // TM, n // TN_OUT),
        in_specs=[
            pl.BlockSpec((TM, k), lambda i, j: (i, 0)),
            pl.BlockSpec((None, k, TN_OUT), lambda i, j: (layer, 0, j)),
            pl.BlockSpec((TM, TN_OUT), lambda i, j: (i, j)),
        ],
        out_specs=pl.BlockSpec((TM, TN_OUT), lambda i, j: (i, j)),
        out_shape=jax.ShapeDtypeStruct((t, n), F32),
        compiler_params=pltpu.CompilerParams(
            dimension_semantics=("arbitrary", "arbitrary"), vmem_limit_bytes=V7X_VMEM_LIMIT),
        name="out_proj_residual",
    )(a, w_all, r)


def _swiglu_tile(h, w1_ref, w3_ref, w2_ref):
    a = jnp.dot(h, w1_ref[...].astype(BF16), preferred_element_type=F32)
    b = jnp.dot(h, w3_ref[...].astype(BF16), preferred_element_type=F32)
    u = (_silu(a) * b).astype(BF16)
    return jnp.dot(u, w2_ref[...].astype(BF16), preferred_element_type=F32)


def _ffn_kernel(x_hbm, g_ref, w1_ref, w3_ref, w2_ref, o_ref, hn_ref, sem):
    @pl.when(pl.program_id(1) == 0)
    def _():
        r0 = pl.multiple_of(pl.program_id(0) * TM, TM)
        cp = pltpu.make_async_copy(x_hbm.at[pl.ds(r0, TM)], o_ref, sem)
        cp.start()
        cp.wait()
        _norm_block_to(o_ref, g_ref, hn_ref)

    o_ref[...] += _swiglu_tile(hn_ref[...], w1_ref, w3_ref, w2_ref)


def _ffn_dense(x, g, w1, w3, w2):
    t, d = x.shape
    f = w1.shape[1]
    return pl.pallas_call(
        _ffn_kernel,
        grid=(t // TM, f // TF),
        in_specs=[
            pl.BlockSpec(memory_space=pl.ANY),
            pl.BlockSpec((1, d), lambda i, j: (0, 0)),
            pl.BlockSpec((d, TF), lambda i, j: (0, j)),
            pl.BlockSpec((d, TF), lambda i, j: (0, j)),
            pl.BlockSpec((TF, d), lambda i, j: (j, 0)),
        ],
        out_specs=pl.BlockSpec((TM, d), lambda i, j: (i, 0)),
        out_shape=jax.ShapeDtypeStruct((t, d), F32),
        scratch_shapes=[pltpu.VMEM((TM, d), BF16), pltpu.SemaphoreType.DMA(())],
        compiler_params=pltpu.CompilerParams(
            dimension_semantics=("arbitrary", "arbitrary"), vmem_limit_bytes=V7X_VMEM_LIMIT),
        name="ffn_dense",
    )(x, g.reshape(1, d), w1, w3, w2)


def _rotary_tables(length, pos0):
    half = RET_D // 2
    inv = ROPE_THETA ** (-jnp.arange(half, dtype=F32) / half)
    pos = jnp.arange(length, dtype=F32) + pos0
    ang = pos[:, None] * inv[None, :]
    cos, sin = jnp.cos(ang), jnp.sin(ang)
    return jnp.concatenate([cos, cos], axis=-1), jnp.concatenate([-sin, sin], axis=-1)


def _decay_tables(c):
    log_g = jnp.log1p(-jnp.exp2(-5.0 - jnp.arange(RET_HEADS, dtype=F32)))
    idx = jnp.arange(c, dtype=F32)
    rel = idx[:, None] - idx[None, :]
    d_in = jnp.where(rel >= 0, jnp.exp(log_g[:, None, None] * jnp.maximum(rel, 0.0)), 0.0)
    d_q = jnp.exp(log_g[:, None] * (idx + 1.0)[None, :])
    d_k = jnp.exp(log_g[:, None] * (c - 1.0 - idx)[None, :])
    d_c = jnp.exp(log_g * c)
    return d_in, d_q, d_k, d_c


HIST31 = 32
HIST3 = 8


def _mixer_prompt_kernel(z_ref, cos_ref, sin_ref, din_ref, dq_ref, dk_ref, dc_ref, retg_ref,
                         c31w_ref, c31b_ref, clng_ref, clnb_ref, scw_ref, slng_ref, slnb_ref,
                         sguw_ref, sgub_ref, mix_in_ref,
                         mix_ref, ret_ref, b31_ref, b3_ref,
                         s_scr, ext31_scr, ext3_scr):
    del mix_in_ref
    c = pl.program_id(1)

    @pl.when(c == 0)
    def _():
        s_scr[...] = jnp.zeros_like(s_scr)
        ext31_scr[pl.ds(0, HIST31), :] = jnp.zeros((HIST31, GROUP_W), F32)
        ext3_scr[pl.ds(0, HIST3), :] = jnp.zeros((HIST3, GROUP_W), F32)

    cos = cos_ref[...]
    sin = sin_ref[...]

    for h in range(RET_HEADS):
        lo = h * RET_D
        q = z_ref[:, pl.ds(C_Q + lo, RET_D)]
        k = z_ref[:, pl.ds(C_K + lo, RET_D)]
        v = z_ref[:, pl.ds(C_V + lo, RET_D)]
        g = z_ref[:, pl.ds(C_G + lo, RET_D)]
        q = q * cos + pltpu.roll(q, RET_D // 2, axis=1) * sin
        k = (k * cos + pltpu.roll(k, RET_D // 2, axis=1) * sin) * (RET_D ** -0.5)
        qb = q.astype(BF16)
        vb = v.astype(BF16)
        scores = lax.dot_general(qb, k.astype(BF16), (((1,), (1,)), ((), ())),
                                 preferred_element_type=F32) * din_ref[h]
        inner = jnp.dot(scores.astype(BF16), vb, preferred_element_type=F32)
        s_old = s_scr[h]
        cross = jnp.dot(qb, s_old.astype(BF16), preferred_element_type=F32) * dq_ref[h]
        kd = (k * dk_ref[h]).astype(BF16)
        s_scr[h] = s_old * dc_ref[h] + lax.dot_general(kd, vb, (((0,), (0,)), ((), ())),
                                                       preferred_element_type=F32)
        r = inner + cross
        r = r * lax.rsqrt(jnp.mean(r * r, axis=-1, keepdims=True) + EPS)
        r = r * retg_ref[:, pl.ds(lo, RET_D)]
        mix_ref[:, pl.ds(lo, RET_D)] = (_silu(g) * r).astype(mix_ref.dtype)

    glu = z_ref[:, pl.ds(C_BL, GROUP_W)] * jax.nn.sigmoid(z_ref[:, pl.ds(C_BG, GROUP_W)])
    ext31_scr[pl.ds(HIST31, CHUNK), :] = glu
    conv = ext31_scr[pl.ds(HIST31 - (CONV_W - 1), CHUNK), :] * c31w_ref[pl.ds(0, 1), :]
    for j in range(1, CONV_W):
        conv = conv + ext31_scr[pl.ds(HIST31 - (CONV_W - 1) + j, CHUNK), :] * c31w_ref[pl.ds(j, 1), :]
    conv = conv + c31b_ref[...]
    out_b = _silu(_layernorm_rows(conv, clng_ref[...], clnb_ref[...]))
    mix_ref[:, pl.ds(GROUP_W, GROUP_W)] = out_b.astype(mix_ref.dtype)
    ext31_scr[pl.ds(0, HIST31), :] = ext31_scr[pl.ds(CHUNK, HIST31), :]

    gated = z_ref[:, pl.ds(C_CC, GROUP_W)] * z_ref[:, pl.ds(C_CH, GROUP_W)]
    ext3_scr[pl.ds(HIST3, CHUNK), :] = gated
    conv3 = ext3_scr[pl.ds(HIST3 - (SC_W - 1), CHUNK), :] * scw_ref[pl.ds(0, 1), :]
    for j in range(1, SC_W):
        conv3 = conv3 + ext3_scr[pl.ds(HIST3 - (SC_W - 1) + j, CHUNK), :] * scw_ref[pl.ds(j, 1), :]
    mix_ref[:, pl.ds(2 * GROUP_W, GROUP_W)] = (z_ref[:, pl.ds(C_CB, GROUP_W)] * conv3).astype(mix_ref.dtype)
    ext3_scr[pl.ds(0, HIST3), :] = ext3_scr[pl.ds(CHUNK, HIST3), :]

    vn = _layernorm_rows(z_ref[:, pl.ds(C_DV, GROUP_W)], slng_ref[...], slnb_ref[...])
    row = lax.broadcasted_iota(jnp.int32, (CHUNK, CHUNK), 0)
    col = lax.broadcasted_iota(jnp.int32, (CHUNK, CHUNK), 1)
    ch = GROUP_W // SGU_GROUPS
    for gi in range(SGU_GROUPS):
        wm = jnp.where(row >= col, sguw_ref[gi], 0.0).astype(BF16)
        mixed = jnp.dot(wm, vn[:, gi * ch:(gi + 1) * ch].astype(BF16), preferred_element_type=F32)
        mixed = mixed + sgub_ref[:, pl.ds(gi, 1)]
        du = z_ref[:, pl.ds(C_DU + gi * ch, ch)]
        mix_ref[:, pl.ds(3 * GROUP_W + gi * ch, ch)] = (du * mixed).astype(mix_ref.dtype)

    @pl.when(c == pl.num_programs(1) - 1)
    def _():
        ret_ref[...] = s_scr[...]
        b31_ref[...] = ext31_scr[pl.ds(HIST31 - (CONV_W - 1), CONV_W - 1), :]
        b3_ref[...] = ext3_scr[pl.ds(HIST3 - (SC_W - 1), SC_W - 1), :]


def _mixer_prompt(z, p, mix_init):
    cos, sin = _rotary_tables(SEQ, 0)
    d_in, d_q, d_k, d_c = _decay_tables(CHUNK)
    dq_t = jnp.broadcast_to(d_q[:, :, None], (RET_HEADS, CHUNK, RET_D))
    dk_t = jnp.broadcast_to(d_k[:, :, None], (RET_HEADS, CHUNK, RET_D))
    dc_t = jnp.broadcast_to(d_c[:, None, None], (RET_HEADS, RET_D, RET_D))

    def full(shape):
        return pl.BlockSpec(shape, lambda b, c: (0,) * len(shape))

    row = lambda a: a.reshape(1, -1)
    return pl.pallas_call(
        _mixer_prompt_kernel,
        grid=(BATCH, N_CHUNKS),
        in_specs=[
            pl.BlockSpec((CHUNK, D_IN), lambda b, c: (b * N_CHUNKS + c, 0)),
            pl.BlockSpec((CHUNK, RET_D), lambda b, c: (c, 0)),
            pl.BlockSpec((CHUNK, RET_D), lambda b, c: (c, 0)),
            full((RET_HEADS, CHUNK, CHUNK)),
            full((RET_HEADS, CHUNK, RET_D)),
            full((RET_HEADS, CHUNK, RET_D)),
            full((RET_HEADS, RET_D, RET_D)),
            full((1, GROUP_W)),
            full((CONV_W, GROUP_W)),
            full((1, GROUP_W)),
            full((1, GROUP_W)),
            full((1, GROUP_W)),
            full((SC_W, GROUP_W)),
            full((1, GROUP_W)),
            full((1, GROUP_W)),
            full((SGU_GROUPS, CHUNK, CHUNK)),
            full((CHUNK, SGU_GROUPS)),
            pl.BlockSpec(memory_space=pl.ANY),
        ],
        input_output_aliases={17: 0},
        out_specs=[
            pl.BlockSpec((CHUNK, D_MODEL), lambda b, c: (b * N_CHUNKS + c, 0)),
            pl.BlockSpec((None, RET_HEADS, RET_D, RET_D), lambda b, c: (b, 0, 0, 0)),
            pl.BlockSpec((None, CONV_W - 1, GROUP_W), lambda b, c: (b, 0, 0)),
            pl.BlockSpec((None, SC_W - 1, GROUP_W), lambda b, c: (b, 0, 0)),
        ],
        out_shape=[
            jax.ShapeDtypeStruct((T_ALL, D_MODEL), BF16),
            jax.ShapeDtypeStruct((BATCH, RET_HEADS, RET_D, RET_D), F32),
            jax.ShapeDtypeStruct((BATCH, CONV_W - 1, GROUP_W), F32),
            jax.ShapeDtypeStruct((BATCH, SC_W - 1, GROUP_W), F32),
        ],
        scratch_shapes=[
            pltpu.VMEM((RET_HEADS, RET_D, RET_D), F32),
            pltpu.VMEM((HIST31 + CHUNK, GROUP_W), F32),
            pltpu.VMEM((HIST3 + CHUNK, GROUP_W), F32),
        ],
        compiler_params=pltpu.CompilerParams(
            dimension_semantics=("arbitrary", "arbitrary"), vmem_limit_bytes=V7X_VMEM_LIMIT),
        name="mixer_prompt",
    )(z, cos, sin, d_in, dq_t, dk_t, dc_t, row(p["ret_norm_g"]), p["conv31_w"], row(p["conv31_b"]),
      row(p["conv_ln_g"]), row(p["conv_ln_b"]), p["sconv_w"], row(p["sgu_ln_g"]), row(p["sgu_ln_b"]),
      p["sgu_w"], p["sgu_b"].T, mix_init)


EXT31_S = 40
EXT3_S = 8


def _mixer_sample_kernel(z_ref, cos_ref, sin_ref, din_ref, dq_ref, dk_ref, dc_ref, retg_ref,
                         c31w_ref, c31b_ref, clng_ref, clnb_ref, scw_ref, slng_ref, slnb_ref,
                         sguw_ref, sgub_ref, sret_ref, s31_ref, s3_ref,
                         mix_ref, ret_ref, b31_ref, b3_ref, vn_ref,
                         ext31_scr, ext3_scr):
    cos = cos_ref[...]
    sin = sin_ref[...]
    n_hist31 = CONV_W - 1
    n_hist3 = SC_W - 1

    for h in range(RET_HEADS):
        lo = h * RET_D
        q = z_ref[:, :, pl.ds(C_Q + lo, RET_D)]
        k = z_ref[:, :, pl.ds(C_K + lo, RET_D)]
        g = z_ref[:, :, pl.ds(C_G + lo, RET_D)]
        q = q * cos + pltpu.roll(q, RET_D // 2, axis=2) * sin
        k = (k * cos + pltpu.roll(k, RET_D // 2, axis=2) * sin) * (RET_D ** -0.5)
        s_old = sret_ref[:, h]
        r = lax.dot_general(q, s_old, (((2,), (1,)), ((0,), (0,))),
                            preferred_element_type=F32) * dq_ref[h]
        for s in range(DEC_SEQ):
            k_s = z_ref[:, pl.ds(s, 1), pl.ds(C_K + lo, RET_D)]
            k_s = (k_s * cos[s:s + 1] + pltpu.roll(k_s, RET_D // 2, axis=2) * sin[s:s + 1]) * (RET_D ** -0.5)
            v_s = z_ref[:, pl.ds(s, 1), pl.ds(C_V + lo, RET_D)]
            score = jnp.sum(q * k_s, axis=-1, keepdims=True) * din_ref[h, s]
            r = r + score * v_s
        kd = k * dk_ref[h]
        v = z_ref[:, :, pl.ds(C_V + lo, RET_D)]
        ret_ref[:, h] = s_old * dc_ref[h] + lax.dot_general(
            kd, v, (((1,), (1,)), ((0,), (0,))), preferred_element_type=F32)
        r = r * lax.rsqrt(jnp.mean(r * r, axis=-1, keepdims=True) + EPS)
        r = r * retg_ref[:, pl.ds(lo, RET_D)]
        mix_ref[:, :, pl.ds(lo, RET_D)] = _silu(g) * r

    glu = z_ref[:, :, pl.ds(C_BL, GROUP_W)] * jax.nn.sigmoid(z_ref[:, :, pl.ds(C_BG, GROUP_W)])
    ext31_scr[:, pl.ds(0, n_hist31), :] = s31_ref[...]
    ext31_scr[:, pl.ds(n_hist31, DEC_SEQ), :] = glu
    conv = ext31_scr[:, pl.ds(0, DEC_SEQ), :] * c31w_ref[pl.ds(0, 1), :]
    for j in range(1, CONV_W):
        conv = conv + ext31_scr[:, pl.ds(j, DEC_SEQ), :] * c31w_ref[pl.ds(j, 1), :]
    conv = conv + c31b_ref[...]
    mix_ref[:, :, pl.ds(GROUP_W, GROUP_W)] = _silu(_layernorm_rows(conv, clng_ref[...], clnb_ref[...]))
    b31_ref[...] = ext31_scr[:, pl.ds(DEC_SEQ, n_hist31), :]

    gated = z_ref[:, :, pl.ds(C_CC, GROUP_W)] * z_ref[:, :, pl.ds(C_CH, GROUP_W)]
    ext3_scr[:, pl.ds(0, n_hist3), :] = s3_ref[...]
    ext3_scr[:, pl.ds(n_hist3, DEC_SEQ), :] = gated
    conv3 = ext3_scr[:, pl.ds(0, DEC_SEQ), :] * scw_ref[pl.ds(0, 1), :]
    for j in range(1, SC_W):
        conv3 = conv3 + ext3_scr[:, pl.ds(j, DEC_SEQ), :] * scw_ref[pl.ds(j, 1), :]
    mix_ref[:, :, pl.ds(2 * GROUP_W, GROUP_W)] = z_ref[:, :, pl.ds(C_CB, GROUP_W)] * conv3
    b3_ref[...] = ext3_scr[:, pl.ds(DEC_SEQ, n_hist3), :]

    vn = _layernorm_rows(z_ref[:, :, pl.ds(C_DV, GROUP_W)], slng_ref[...], slnb_ref[...])
    vn_ref[...] = vn
    mixed = sgub_ref[...]
    for s in range(DEC_SEQ):
        mixed = mixed + sguw_ref[s] * vn_ref[:, pl.ds(s, 1), :]
    mix_ref[:, :, pl.ds(3 * GROUP_W, GROUP_W)] = z_ref[:, :, pl.ds(C_DU, GROUP_W)] * mixed


def _mixer_sample(z_s, p, layer, s_ret, s_31, s_3):
    cos, sin = _rotary_tables(DEC_SEQ, PAST_LEN)
    d_in, d_q, d_k, d_c = _decay_tables(DEC_SEQ)
    din_t = jnp.swapaxes(d_in, 1, 2)[:, :, :, None]
    dq_t = jnp.broadcast_to(d_q[:, :, None], (RET_HEADS, DEC_SEQ, RET_D))
    dk_t = jnp.broadcast_to(d_k[:, :, None], (RET_HEADS, DEC_SEQ, RET_D))
    dc_t = jnp.broadcast_to(d_c[:, None, None], (RET_HEADS, RET_D, RET_D))
    ch = GROUP_W // SGU_GROUPS
    w4 = jnp.tril(p["sgu_w"][:, :DEC_SEQ, :DEC_SEQ])
    w_t = jnp.repeat(jnp.transpose(w4, (2, 1, 0)), ch, axis=-1)
    b_t = jnp.repeat(p["sgu_b"][:, :DEC_SEQ].T, ch, axis=-1)
    bt = SAMPLE_BT

    def full(shape):
        return pl.BlockSpec(shape, lambda i: (0,) * len(shape))

    row = lambda a: a.reshape(1, -1)
    return pl.pallas_call(
        _mixer_sample_kernel,
        grid=(DEC_BATCH // bt,),
        in_specs=[
            pl.BlockSpec((bt, DEC_SEQ, D_IN), lambda i: (i, 0, 0)),
            full((DEC_SEQ, RET_D)),
            full((DEC_SEQ, RET_D)),
            full((RET_HEADS, DEC_SEQ, DEC_SEQ, 1)),
            full((RET_HEADS, DEC_SEQ, RET_D)),
            full((RET_HEADS, DEC_SEQ, RET_D)),
            full((RET_HEADS, RET_D, RET_D)),
            full((1, GROUP_W)),
            full((CONV_W, GROUP_W)),
            full((1, GROUP_W)),
            full((1, GROUP_W)),
            full((1, GROUP_W)),
            full((SC_W, GROUP_W)),
            full((1, GROUP_W)),
            full((1, GROUP_W)),
            full((DEC_SEQ, DEC_SEQ, GROUP_W)),
            full((DEC_SEQ, GROUP_W)),
            pl.BlockSpec((None, bt, RET_HEADS, RET_D, RET_D), lambda i: (layer, i, 0, 0, 0)),
            pl.BlockSpec((None, bt, CONV_W - 1, GROUP_W), lambda i: (layer, i, 0, 0)),
            pl.BlockSpec((None, bt, SC_W - 1, GROUP_W), lambda i: (layer, i, 0, 0)),
        ],
        out_specs=[
            pl.BlockSpec((bt, DEC_SEQ, D_MODEL), lambda i: (i, 0, 0)),
            pl.BlockSpec((bt, RET_HEADS, RET_D, RET_D), lambda i: (i, 0, 0, 0)),
            pl.BlockSpec((bt, CONV_W - 1, GROUP_W), lambda i: (i, 0, 0)),
            pl.BlockSpec((bt, SC_W - 1, GROUP_W), lambda i: (i, 0, 0)),
            pl.BlockSpec((bt, DEC_SEQ, GROUP_W), lambda i: (i, 0, 0)),
        ],
        out_shape=[
            jax.ShapeDtypeStruct((DEC_BATCH, DEC_SEQ, D_MODEL), F32),
            jax.ShapeDtypeStruct((DEC_BATCH, RET_HEADS, RET_D, RET_D), F32),
            jax.ShapeDtypeStruct((DEC_BATCH, CONV_W - 1, GROUP_W), F32),
            jax.ShapeDtypeStruct((DEC_BATCH, SC_W - 1, GROUP_W), F32),
            jax.ShapeDtypeStruct((DEC_BATCH, DEC_SEQ, GROUP_W), F32),
        ],
        scratch_shapes=[
            pltpu.VMEM((bt, EXT31_S, GROUP_W), F32),
            pltpu.VMEM((bt, EXT3_S, GROUP_W), F32),
        ],
        compiler_params=pltpu.CompilerParams(
            dimension_semantics=("arbitrary",), vmem_limit_bytes=V7X_VMEM_LIMIT),
        name="mixer_sample",
    )(z_s, cos, sin, din_t, dq_t, dk_t, dc_t, row(p["ret_norm_g"]), p["conv31_w"], row(p["conv31_b"]),
      row(p["conv_ln_g"]), row(p["conv_ln_b"]), p["sconv_w"], row(p["sgu_ln_g"]), row(p["sgu_ln_b"]),
      w_t, b_t, s_ret, s_31, s_3)


def _to_token_major(dst, val, pitch=TOK_ROWS):
    n = val.shape[0]
    for a in range(TOK_ROWS):
        dst[pl.ds(a, n, stride=pitch), :] = val[:, a * LANES:(a + 1) * LANES]


STAGE_PITCH = 24
DMA_UNROLL = 8


def _router_kernel(x_ref, g_ref, rw_ref, e_ref, p_ref, c_ref):
    h = _rms_rows(x_ref[...], g_ref[...])
    logits = jnp.dot(h, rw_ref[...], preferred_element_type=F32, precision=lax.Precision.HIGHEST)
    lane = lax.broadcasted_iota(jnp.int32, logits.shape, 1)
    m1 = jnp.max(logits, axis=-1, keepdims=True)
    i1 = jnp.min(jnp.where(logits == m1, lane, N_EXPERTS), axis=-1, keepdims=True)
    rest = jnp.where(lane == i1, -jnp.inf, logits)
    m2 = jnp.max(rest, axis=-1, keepdims=True)
    i2 = jnp.min(jnp.where(rest == m2, lane, N_EXPERTS), axis=-1, keepdims=True)
    e2 = jnp.exp(m2 - m1)
    den = 1.0 + e2
    e_ref[...] = jnp.where(lane == 0, i1, jnp.where(lane == 1, i2, 0))
    p_ref[...] = jnp.where(lane == 0, 1.0 / den, jnp.where(lane == 1, e2 / den, 0.0))
    c_ref[pl.ds(0, 1), :] = jnp.sum(jnp.where(lane == i1, 1.0, 0.0), axis=0, keepdims=True).astype(jnp.int32)
    c_ref[pl.ds(1, 1), :] = jnp.sum(jnp.where(lane == i2, 1.0, 0.0), axis=0, keepdims=True).astype(jnp.int32)


def _router(x, g, rw):
    t, d = x.shape
    return pl.pallas_call(
        _router_kernel,
        grid=(N_TB,),
        in_specs=[
            pl.BlockSpec((MOE_TB, d), lambda i: (i, 0)),
            pl.BlockSpec((1, d), lambda i: (0, 0)),
            pl.BlockSpec((d, N_EXPERTS), lambda i: (0, 0)),
        ],
        out_specs=[
            pl.BlockSpec((MOE_TB, N_EXPERTS), lambda i: (i, 0)),
            pl.BlockSpec((MOE_TB, N_EXPERTS), lambda i: (i, 0)),
            pl.BlockSpec((None, TOP_K, N_EXPERTS), lambda i: (i, 0, 0)),
        ],
        out_shape=[
            jax.ShapeDtypeStruct((t, N_EXPERTS), jnp.int32),
            jax.ShapeDtypeStruct((t, N_EXPERTS), F32),
            jax.ShapeDtypeStruct((N_TB, TOP_K, N_EXPERTS), jnp.int32),
        ],
        compiler_params=pltpu.CompilerParams(
            dimension_semantics=("arbitrary",), vmem_limit_bytes=V7X_VMEM_LIMIT),
        name="moe_router",
    )(x, g.reshape(1, d), rw)


def _pick(table, idx):
    lanes = jnp.arange(N_EXPERTS, dtype=jnp.int32)
    return jnp.sum(jnp.where(idx[..., None] == lanes, table, 0), axis=-1)


def _expert_layout(cnt):
    tot = jnp.sum(cnt, axis=(0, 1))
    start = jnp.cumsum(tot) - tot
    blk = jnp.sum(cnt, axis=1)
    blk_base = start[None, :] + jnp.cumsum(blk, axis=0) - blk
    base = jnp.stack([blk_base, blk_base + cnt[:, 0]], axis=1)
    n_sb = (tot + SB - 1) // SB
    sb_end = jnp.cumsum(n_sb)
    n_used = sb_end[-1]
    s_idx = jnp.arange(NS_MAX, dtype=jnp.int32)
    s_cl = jnp.minimum(s_idx, n_used - 1)
    sb_e = jnp.sum(s_cl[:, None] >= sb_end[None, :], axis=-1).astype(jnp.int32)
    local = s_cl - _pick(sb_end - n_sb, sb_e)
    sb_row0 = _pick(start, sb_e) + local * SB
    sb_rows = jnp.where(s_idx < n_used, jnp.clip(_pick(tot, sb_e) - local * SB, 0, SB), 0)
    i32 = lambda a: a.astype(jnp.int32)
    return i32(base), sb_e, i32(sb_row0), i32(sb_rows), i32(n_used).reshape(1)


def _dest_kernel(e_ref, b_ref, d_ref):
    eid = e_ref[...]
    lane = lax.broadcasted_iota(jnp.int32, eid.shape, 1)
    row = lax.broadcasted_iota(jnp.int32, (MOE_TB, MOE_TB), 0)
    col = lax.broadcasted_iota(jnp.int32, (MOE_TB, MOE_TB), 1)
    earlier = jnp.where(row > col, 1.0, 0.0).astype(BF16)
    base = b_ref[...].astype(F32)

    def dest_of(k):
        mine = lane == eid[:, k:k + 1]
        before = jnp.dot(earlier, jnp.where(mine, 1.0, 0.0).astype(BF16), preferred_element_type=F32)
        return jnp.sum(jnp.where(mine, before + base[k:k + 1], 0.0), axis=-1, keepdims=True)

    d_ref[...] = jnp.where(lane == 0, dest_of(0), jnp.where(lane == 1, dest_of(1), 0.0)).astype(jnp.int32)


def _dest_rows(eid, base):
    return pl.pallas_call(
        _dest_kernel,
        grid=(N_TB,),
        in_specs=[
            pl.BlockSpec((MOE_TB, N_EXPERTS), lambda i: (i, 0)),
            pl.BlockSpec((None, TOP_K, N_EXPERTS), lambda i: (i, 0, 0)),
        ],
        out_specs=pl.BlockSpec((MOE_TB, N_EXPERTS), lambda i: (i, 0)),
        out_shape=jax.ShapeDtypeStruct((T_ALL, N_EXPERTS), jnp.int32),
        compiler_params=pltpu.CompilerParams(
            dimension_semantics=("arbitrary",), vmem_limit_bytes=V7X_VMEM_LIMIT),
        name="moe_dest_rows",
    )(eid, base)


def _dispatch_kernel(d_ref, x_ref, g_ref, xg_hbm, stage, zeros, sem, zsem):
    i = pl.program_id(0)
    n = pl.num_programs(0)
    slot = i % 2
    blk_rows = MOE_TB * TOK_ROWS

    def wait_slot(sl):
        for _ in range(TOP_K):
            pltpu.make_async_copy(stage.at[sl, pl.ds(0, blk_rows)], xg_hbm.at[pl.ds(0, blk_rows)], sem.at[sl]).wait()

    @pl.when(i >= 2)
    def _():
        wait_slot(slot)

    _to_token_major(stage.at[slot], _rms_rows(x_ref[...], g_ref[...]), STAGE_PITCH)

    def body(r, carry):
        src = stage.at[slot, pl.ds(pl.multiple_of(r * STAGE_PITCH, 8), TOK_ROWS)]
        for k in range(TOP_K):
            row = d_ref[0, 0, TOP_K * r + k]
            dst = xg_hbm.at[pl.ds(pl.multiple_of(row * TOK_ROWS, TOK_ROWS), TOK_ROWS)]
            pltpu.make_async_copy(src, dst, sem.at[slot]).start()
        return carry

    lax.fori_loop(0, MOE_TB, body, 0, unroll=DMA_UNROLL)

    @pl.when(i == n - 1)
    def _():
        zeros[...] = jnp.zeros_like(zeros)
        tail = [pltpu.make_async_copy(zeros, xg_hbm.at[pl.ds((N_SLOTS + q * (SUB // TOK_ROWS)) * TOK_ROWS, SUB)], zsem)
                for q in range(TOK_ROWS)]
        for cp in tail:
            cp.start()
        for cp in tail:
            cp.wait()
        wait_slot(slot)
        wait_slot(1 - slot)


def _dispatch(x, g, dest_flat):
    t, d = x.shape
    return pl.pallas_call(
        _dispatch_kernel,
        grid=(N_TB,),
        in_specs=[
            pl.BlockSpec((1, 1, TOP_K * MOE_TB), lambda i: (i, 0, 0), memory_space=pltpu.SMEM),
            pl.BlockSpec((MOE_TB, d), lambda i: (i, 0)),
            pl.BlockSpec((1, d), lambda i: (0, 0)),
        ],
        out_specs=pl.BlockSpec(memory_space=pl.ANY),
        out_shape=jax.ShapeDtypeStruct(((N_SLOTS + SUB) * TOK_ROWS, LANES), F32),
        scratch_shapes=[
            pltpu.VMEM((2, MOE_TB * STAGE_PITCH, LANES), F32),
            pltpu.VMEM((SUB, LANES), F32),
            pltpu.SemaphoreType.DMA((2,)),
            pltpu.SemaphoreType.DMA(()),
        ],
        compiler_params=pltpu.CompilerParams(
            dimension_semantics=("arbitrary",), vmem_limit_bytes=V7X_VMEM_LIMIT),
        name="moe_dispatch",
    )(dest_flat, x, g.reshape(1, d))


def _moe_ffn_kernel(se_ref, r0_ref, nr_ref, nu_ref, xg_hbm, w1_ref, w3_ref, w2_ref, y_hbm,
                    x_scr, acc, stage, w1b, w3b, w2b, sem_in, sem_out):
    s = pl.program_id(0)
    j = pl.program_id(1)
    nj = pl.num_programs(1)
    rows = nr_ref[s]
    row0 = r0_ref[s]
    nsub = lax.shift_right_logical(rows + (SUB - 1), int(math.log2(SUB)))
    nchunk = (nsub + (CHUNK_ROWS // SUB - 1)) // (CHUNK_ROWS // SUB)
    sub_rows = SUB * TOK_ROWS

    @pl.when(nsub > 0)
    def _():
        w1b[...] = w1_ref[...].astype(BF16)
        w3b[...] = w3_ref[...].astype(BF16)
        w2b[...] = w2_ref[...].astype(BF16)

    def sub(r):
        return pl.ds(pl.multiple_of(r * SUB, SUB), SUB)

    def swiglu(xs):
        a = jnp.dot(xs, w1b[...], preferred_element_type=F32)
        b = jnp.dot(xs, w3b[...], preferred_element_type=F32)
        u = (_silu(a) * b).astype(BF16)
        return jnp.dot(u, w2b[...], preferred_element_type=F32)

    def tile(r):
        return swiglu(x_scr[sub(r), :])

    def in_copy(r, slot):
        first = pl.multiple_of((row0 + r * SUB) * TOK_ROWS, TOK_ROWS)
        return pltpu.make_async_copy(xg_hbm.at[pl.ds(first, sub_rows)], stage.at[slot], sem_in.at[slot])

    @pl.when(j == 0)
    def _():
        @pl.when(nsub > 0)
        def _():
            in_copy(0, 0).start()

        def body(r, carry):
            slot = r % 2
            in_copy(r, slot).wait()

            @pl.when(r + 1 < nsub)
            def _():
                in_copy(r + 1, 1 - slot).start()

            for a in range(TOK_ROWS):
                x_scr[sub(r), pl.ds(a * LANES, LANES)] = stage[slot, pl.ds(a, SUB, stride=TOK_ROWS), :].astype(BF16)
            acc[sub(r), :] = tile(r)
            return carry

        lax.fori_loop(0, nsub, body, 0)

        def clear(r, carry):
            x_scr[sub(r), :] = jnp.zeros((SUB, D_MODEL), BF16)
            acc[sub(r), :] = jnp.zeros((SUB, D_MODEL), F32)
            return carry

        lax.fori_loop(nsub, nchunk * (CHUNK_ROWS // SUB), clear, 0)

    @pl.when((j > 0) & (j < nj - 1))
    def _():
        for c in range(SB // CHUNK_ROWS):
            @pl.when(c < nchunk)
            def _():
                rows_c = pl.ds(c * CHUNK_ROWS, CHUNK_ROWS)
                acc[rows_c, :] += swiglu(x_scr[rows_c, :])

    @pl.when(j == nj - 1)
    def _():
        def out_copies(r, act):
            nv = jnp.clip(rows - r * SUB, 0, SUB)
            bit = SUB
            while bit >= 1:
                off = nv & (-2 * bit)
                src = stage.at[0, pl.ds(pl.multiple_of(off * TOK_ROWS, TOK_ROWS), bit * TOK_ROWS)]
                first = pl.multiple_of((row0 + r * SUB + off) * TOK_ROWS, TOK_ROWS)
                cp = pltpu.make_async_copy(src, y_hbm.at[pl.ds(first, bit * TOK_ROWS)], sem_out)
                pl.when((nv & bit) != 0)(functools.partial(act, cp))
                bit //= 2

        def body(r, carry):
            y = acc[sub(r), :] + tile(r)

            @pl.when(r > 0)
            def _():
                out_copies(r - 1, lambda cp: cp.wait())

            _to_token_major(stage.at[0], y)
            out_copies(r, lambda cp: cp.start())
            return carry

        lax.fori_loop(0, nsub, body, 0)

        @pl.when(nsub > 0)
        def _():
            out_copies(nsub - 1, lambda cp: cp.wait())


def _moe_ffn(xg, w1, w3, w2, sb_e, sb_row0, sb_rows, n_used):
    d = w1.shape[1]
    f = w1.shape[2]
    nj = f // TF

    def jmap(s, j, nu):
        return jnp.where(s < nu[0], j, nj - 1)

    return pl.pallas_call(
        _moe_ffn_kernel,
        grid_spec=pltpu.PrefetchScalarGridSpec(
            num_scalar_prefetch=4,
            grid=(NS_MAX, nj),
            in_specs=[
                pl.BlockSpec(memory_space=pl.ANY),
                pl.BlockSpec((None, d, TF), lambda s, j, se, r0, nr, nu: (se[s], 0, jmap(s, j, nu))),
                pl.BlockSpec((None, d, TF), lambda s, j, se, r0, nr, nu: (se[s], 0, jmap(s, j, nu))),
                pl.BlockSpec((None, TF, d), lambda s, j, se, r0, nr, nu: (se[s], jmap(s, j, nu), 0)),
            ],
            out_specs=pl.BlockSpec(memory_space=pl.ANY),
            scratch_shapes=[
                pltpu.VMEM((SB, d), BF16),
                pltpu.VMEM((SB, d), F32),
                pltpu.VMEM((2, SUB * TOK_ROWS, LANES), F32),
                pltpu.VMEM((d, TF), BF16),
                pltpu.VMEM((d, TF), BF16),
                pltpu.VMEM((TF, d), BF16),
                pltpu.SemaphoreType.DMA((2,)),
                pltpu.SemaphoreType.DMA(()),
            ],
        ),
        out_shape=jax.ShapeDtypeStruct((N_SLOTS * TOK_ROWS, LANES), F32),
        compiler_params=pltpu.CompilerParams(
            dimension_semantics=("arbitrary", "arbitrary"), vmem_limit_bytes=V7X_VMEM_LIMIT),
        name="moe_ffn",
    )(sb_e, sb_row0, sb_rows, n_used, xg, w1, w3, w2)


N_TB_PROMPT = T_PROMPT // MOE_TB


def _combine_kernel(d_ref, dn_ref, x_ref, p_ref, g_ref, y_hbm, op_ref, os_ref, stage, sem, *, final_norm):
    i = pl.program_id(0)
    n = pl.num_programs(0)
    slot = i % 2
    blk_rows = MOE_TB * TOK_ROWS

    def start_block(dref, sl):
        def body(r, carry):
            for k in range(TOP_K):
                row = dref[0, 0, TOP_K * r + k]
                src = y_hbm.at[pl.ds(pl.multiple_of(row * TOK_ROWS, TOK_ROWS), TOK_ROWS)]
                dst = stage.at[sl, k, pl.ds(pl.multiple_of(r * STAGE_PITCH, 8), TOK_ROWS)]
                pltpu.make_async_copy(src, dst, sem.at[sl]).start()
            return carry

        lax.fori_loop(0, MOE_TB, body, 0, unroll=DMA_UNROLL)

    @pl.when(i == 0)
    def _():
        start_block(d_ref, 0)

    @pl.when(i + 1 < n)
    def _():
        start_block(dn_ref, 1 - slot)

    for k in range(TOP_K):
        pltpu.make_async_copy(y_hbm.at[pl.ds(0, blk_rows)], stage.at[slot, k, pl.ds(0, blk_rows)], sem.at[slot]).wait()
    g0 = p_ref[:, 0:1]
    g1 = p_ref[:, 1:2]
    pieces = []
    for a in range(TOK_ROWS):
        y0 = stage[slot, 0, pl.ds(a, MOE_TB, stride=STAGE_PITCH), :]
        y1 = stage[slot, 1, pl.ds(a, MOE_TB, stride=STAGE_PITCH), :]
        pieces.append(x_ref[:, pl.ds(a * LANES, LANES)] + (y0 * g0 + y1 * g1))
    y = jnp.concatenate(pieces, axis=1)
    if final_norm:
        y = _rms_rows(y, g_ref[...])

    @pl.when(i < N_TB_PROMPT)
    def _():
        op_ref[...] = y

    @pl.when(i >= N_TB_PROMPT)
    def _():
        os_ref[...] = y


def _moe_combine(x, y, dest_flat, gate, g, final_norm):
    t, d = x.shape
    return pl.pallas_call(
        functools.partial(_combine_kernel, final_norm=final_norm),
        grid=(N_TB,),
        in_specs=[
            pl.BlockSpec((1, 1, TOP_K * MOE_TB), lambda i: (i, 0, 0), memory_space=pltpu.SMEM),
            pl.BlockSpec((1, 1, TOP_K * MOE_TB), lambda i: (jnp.minimum(i + 1, N_TB - 1), 0, 0),
                         memory_space=pltpu.SMEM),
            pl.BlockSpec((MOE_TB, d), lambda i: (i, 0)),
            pl.BlockSpec((MOE_TB, N_EXPERTS), lambda i: (i, 0)),
            pl.BlockSpec((1, d), lambda i: (0, 0)),
            pl.BlockSpec(memory_space=pl.ANY),
        ],
        out_specs=[
            pl.BlockSpec((MOE_TB, d), lambda i: (jnp.minimum(i, N_TB_PROMPT - 1), 0)),
            pl.BlockSpec((MOE_TB, d), lambda i: (jnp.maximum(i - N_TB_PROMPT, 0), 0)),
        ],
        out_shape=[
            jax.ShapeDtypeStruct((T_PROMPT, d), F32),
            jax.ShapeDtypeStruct((T_SAMPLE, d), F32),
        ],
        scratch_shapes=[
            pltpu.VMEM((2, TOP_K, MOE_TB * STAGE_PITCH, LANES), F32),
            pltpu.SemaphoreType.DMA((2,)),
        ],
        compiler_params=pltpu.CompilerParams(
            dimension_semantics=("arbitrary",), vmem_limit_bytes=V7X_VMEM_LIMIT),
        name="moe_combine",
    )(dest_flat, dest_flat, x, gate, g.reshape(1, d), y)


def _moe(x, norm_g, rw, w1, w3, w2, out_g, final_norm):
    eid, gate, cnt = _router(x, norm_g, rw)
    base, sb_e, sb_row0, sb_rows, n_used = _expert_layout(cnt)
    dest = _dest_rows(eid, base)
    dest_flat = dest[:, :TOP_K].reshape(N_TB, 1, TOP_K * MOE_TB)
    xg = _dispatch(x, norm_g, dest_flat)
    y = _moe_ffn(xg, w1, w3, w2, sb_e, sb_row0, sb_rows, n_used)
    return _moe_combine(x, y, dest_flat, gate, out_g, final_norm)


def kernel(x_prompt, x_sample, state_ret, state_conv31, state_conv3, mix_norm_g, w_in, ret_norm_g, conv31_w, conv31_b, conv_ln_g, conv_ln_b, sconv_w, sgu_ln_g, sgu_ln_b, sgu_w, sgu_b, w_out, ffn_norm_g, dense_w1, dense_w3, dense_w2, router_w, moe_w1, moe_w3, moe_w2, final_norm_g):
    assert DEPTH == 2, "layer 0 has the dense channel mixer; layer 1's expert mixer ends with the final norm"
    x = jnp.concatenate([x_prompt.reshape(T_PROMPT, D_MODEL), x_sample.reshape(T_SAMPLE, D_MODEL)], axis=0)
    rets_p, rets_s, c31_p, c31_s, c3_p, c3_s, vns = [], [], [], [], [], [], []
    for l in range(DEPTH):
        p = dict(ret_norm_g=ret_norm_g[l], conv31_w=conv31_w[l], conv31_b=conv31_b[l],
                 conv_ln_g=conv_ln_g[l], conv_ln_b=conv_ln_b[l], sconv_w=sconv_w[l],
                 sgu_ln_g=sgu_ln_g[l], sgu_ln_b=sgu_ln_b[l], sgu_w=sgu_w[l], sgu_b=sgu_b[l])
        z = _norm_matmul(x, mix_norm_g[l], w_in, l)
        z_s = z[T_PROMPT:].reshape(DEC_BATCH, DEC_SEQ, D_IN)
        mix_s, ret_s, b31_s, b3_s, vn_s = _mixer_sample(z_s, p, l, state_ret, state_conv31, state_conv3)
        mix_init = jnp.pad(mix_s.reshape(T_SAMPLE, D_MODEL).astype(BF16), ((T_PROMPT, 0), (0, 0)))
        mix, ret_p, b31_p, b3_p = _mixer_prompt(z, p, mix_init)
        x = _matmul_residual(mix, w_out, l, x)
        j = l // 2
        if l % 2 == 0:
            x = _ffn_dense(x, ffn_norm_g[l], dense_w1[j], dense_w3[j], dense_w2[j])
        else:
            y_p, y_s = _moe(x, ffn_norm_g[l], router_w[j], moe_w1[j], moe_w3[j], moe_w2[j],
                            final_norm_g, final_norm=True)
        rets_p.append(ret_p); rets_s.append(ret_s)
        c31_p.append(b31_p); c31_s.append(b31_s)
        c3_p.append(b3_p); c3_s.append(b3_s)
        vns.append(vn_s)
    y_prompt = y_p.reshape(BATCH, SEQ, D_MODEL)
    y_sample = y_s.reshape(DEC_BATCH, DEC_SEQ, D_MODEL)
    return (y_prompt, y_sample, jnp.stack(rets_p), jnp.stack(rets_s), jnp.stack(c31_p), jnp.stack(c31_s),
            jnp.stack(c3_p), jnp.stack(c3_s), jnp.stack(vns))
```

```python
import functools
import math

import jax
import jax.numpy as jnp
from jax import lax
from jax.experimental import pallas as pl
from jax.experimental.pallas import tpu as pltpu

F32 = jnp.float32
BF16 = jnp.bfloat16

D_MODEL = 2048
BATCH = 4
SEQ = 2048
DEPTH = 2
DEC_BATCH = 128
DEC_SEQ = 4
PAST_LEN = 16384
GROUP_W = 512
D_IN = 11 * GROUP_W
RET_HEADS = 4
RET_D = 128
CHUNK = 128
ROPE_THETA = 10000.0
CONV_W = 31
SC_W = 3
SGU_GROUPS = 4
D_FF = 5632
N_EXPERTS = 8
TOP_K = 2
EPS = 1e-6

T_PROMPT = BATCH * SEQ
T_SAMPLE = DEC_BATCH * DEC_SEQ
T_ALL = T_PROMPT + T_SAMPLE
N_CHUNKS = SEQ // CHUNK

C_Q, C_K, C_V, C_G, C_BL, C_BG, C_CB, C_CC, C_CH, C_DU, C_DV = (i * GROUP_W for i in range(11))

V7X_VMEM_LIMIT = 56 * 1024 * 1024

TM = 1088
TM_IN = 2176
TM_OUT = 544
NORM_ROWS = 272
TN_IN = 512
TF = 256
SAMPLE_BT = 8

LANES = 128
TOK_ROWS = D_MODEL // LANES
MOE_TB = 256
N_TB = T_ALL // MOE_TB
N_SLOTS = T_ALL * TOP_K
SUB = 256
CHUNK_ROWS = 3 * SUB
SB = 3 * CHUNK_ROWS
NS_MAX = -(-N_SLOTS // SB) + N_EXPERTS


def _rms_rows(x, g):
    ms = jnp.mean(x * x, axis=-1, keepdims=True)
    return (x * lax.rsqrt(ms + EPS)) * g


def _layernorm_rows(x, g, b):
    mu = jnp.mean(x, axis=-1, keepdims=True)
    xc = x - mu
    return xc * lax.rsqrt(jnp.mean(xc * xc, axis=-1, keepdims=True) + EPS) * g + b


def _silu(x):
    return x * jax.nn.sigmoid(x)


def _norm_block_to(x_ref, g_ref, dst_ref):
    rows = x_ref.shape[0]
    g = g_ref[...]

    def body(c, carry):
        r0 = pl.multiple_of(c * NORM_ROWS, NORM_ROWS)
        x = x_ref[pl.ds(r0, NORM_ROWS), :]
        dst_ref[pl.ds(r0, NORM_ROWS), :] = _rms_rows(x, g).astype(dst_ref.dtype)
        return carry

    lax.fori_loop(0, rows // NORM_ROWS, body, 0)


PREP_ROWS = 512
N_PREP_PROMPT = T_PROMPT // PREP_ROWS


def _prep_kernel(xp_ref, xs_ref, g_ref, x_ref, hn_ref):
    def emit(src_ref):
        x = src_ref[...]
        x_ref[...] = x
        hn_ref[...] = _rms_rows(x, g_ref[...]).astype(hn_ref.dtype)

    is_prompt = pl.program_id(0) < N_PREP_PROMPT
    pl.when(is_prompt)(functools.partial(emit, xp_ref))
    pl.when(jnp.logical_not(is_prompt))(functools.partial(emit, xs_ref))


def _prep(x_prompt, x_sample, g):
    d = x_prompt.shape[1]
    blk = lambda f: pl.BlockSpec((PREP_ROWS, d), f)
    return pl.pallas_call(
        _prep_kernel,
        grid=(T_ALL // PREP_ROWS,),
        in_specs=[
            blk(lambda i: (jnp.minimum(i, N_PREP_PROMPT - 1), 0)),
            blk(lambda i: (jnp.maximum(i - N_PREP_PROMPT, 0), 0)),
            pl.BlockSpec((1, d), lambda i: (0, 0)),
        ],
        out_specs=[blk(lambda i: (i, 0)), blk(lambda i: (i, 0))],
        out_shape=[jax.ShapeDtypeStruct((T_ALL, d), F32), jax.ShapeDtypeStruct((T_ALL, d), BF16)],
        compiler_params=pltpu.CompilerParams(
            dimension_semantics=("arbitrary",), vmem_limit_bytes=V7X_VMEM_LIMIT),
        name="prep_rows",
    )(x_prompt, x_sample, g.reshape(1, d))


def _in_proj_kernel(h_ref, w_ref, o_ref):
    o_ref[...] = jnp.dot(h_ref[...], w_ref[...].astype(BF16), preferred_element_type=F32)


def _in_proj(hn, w_all, layer):
    t, d = hn.shape
    n = w_all.shape[2]
    return pl.pallas_call(
        _in_proj_kernel,
        grid=(t // TM_IN, n // TN_IN),
        in_specs=[
            pl.BlockSpec((TM_IN, d), lambda i, j: (i, 0)),
            pl.BlockSpec((None, d, TN_IN), lambda i, j: (layer, 0, j)),
        ],
        out_specs=pl.BlockSpec((TM_IN, TN_IN), lambda i, j: (i, j)),
        out_shape=jax.ShapeDtypeStruct((t, n), F32),
        compiler_params=pltpu.CompilerParams(
            dimension_semantics=("arbitrary", "arbitrary"), vmem_limit_bytes=V7X_VMEM_LIMIT),
        name="in_proj",
    )(hn, w_all)


W_LOAD_ROWS = 512


def _out_proj_kernel(a_ref, w_hbm, r_ref, o_ref, wb, stage, sem, *, layer):
    @pl.when(pl.program_id(0) == 0)
    def _():
        for c in range(wb.shape[0] // W_LOAD_ROWS):
            rows = pl.ds(c * W_LOAD_ROWS, W_LOAD_ROWS)
            cp = pltpu.make_async_copy(w_hbm.at[layer, rows], stage, sem)
            cp.start()
            cp.wait()
            wb[rows, :] = stage[...].astype(BF16)

    o_ref[...] = r_ref[...] + jnp.dot(a_ref[...], wb[...], preferred_element_type=F32)


def _out_proj_residual(a, w_all, layer, r):
    t, k = a.shape
    n = w_all.shape[2]
    return pl.pallas_call(
        functools.partial(_out_proj_kernel, layer=layer),
        grid=(t // TM_OUT,),
        in_specs=[
            pl.BlockSpec((TM_OUT, k), lambda i: (i, 0)),
            pl.BlockSpec(memory_space=pl.ANY),
            pl.BlockSpec((TM_OUT, n), lambda i: (i, 0)),
        ],
        out_specs=pl.BlockSpec((TM_OUT, n), lambda i: (i, 0)),
        out_shape=jax.ShapeDtypeStruct((t, n), F32),
        scratch_shapes=[
            pltpu.VMEM((k, n), BF16),
            pltpu.VMEM((W_LOAD_ROWS, n), F32),
            pltpu.SemaphoreType.DMA(()),
        ],
        compiler_params=pltpu.CompilerParams(
            dimension_semantics=("arbitrary",), vmem_limit_bytes=V7X_VMEM_LIMIT),
        name="out_proj_residual",
    )(a, w_all, r)


def _swiglu_tile(h, w1_ref, w3_ref, w2_ref):
    a = jnp.dot(h, w1_ref[...].astype(BF16), preferred_element_type=F32)
    b = jnp.dot(h, w3_ref[...].astype(BF16), preferred_element_type=F32)
    u = (_silu(a) * b).astype(BF16)
    return jnp.dot(u, w2_ref[...].astype(BF16), preferred_element_type=F32)


def _ffn_kernel(x_hbm, g_ref, w1_ref, w3_ref, w2_ref, gn_ref, o_ref, hnext_ref, hn_ref, sem):
    @pl.when(pl.program_id(1) == 0)
    def _():
        r0 = pl.multiple_of(pl.program_id(0) * TM, TM)
        cp = pltpu.make_async_copy(x_hbm.at[pl.ds(r0, TM)], o_ref, sem)
        cp.start()
        cp.wait()
        _norm_block_to(o_ref, g_ref, hn_ref)

    o_ref[...] += _swiglu_tile(hn_ref[...], w1_ref, w3_ref, w2_ref)

    @pl.when(pl.program_id(1) == pl.num_programs(1) - 1)
    def _():
        _norm_block_to(o_ref, gn_ref, hnext_ref)


def _ffn_dense(x, g, w1, w3, w2, g_next):
    t, d = x.shape
    f = w1.shape[1]
    return pl.pallas_call(
        _ffn_kernel,
        grid=(t // TM, f // TF),
        in_specs=[
            pl.BlockSpec(memory_space=pl.ANY),
            pl.BlockSpec((1, d), lambda i, j: (0, 0)),
            pl.BlockSpec((d, TF), lambda i, j: (0, j)),
            pl.BlockSpec((d, TF), lambda i, j: (0, j)),
            pl.BlockSpec((TF, d), lambda i, j: (j, 0)),
            pl.BlockSpec((1, d), lambda i, j: (0, 0)),
        ],
        out_specs=[
            pl.BlockSpec((TM, d), lambda i, j: (i, 0)),
            pl.BlockSpec((TM, d), lambda i, j: (i, 0)),
        ],
        out_shape=[jax.ShapeDtypeStruct((t, d), F32), jax.ShapeDtypeStruct((t, d), BF16)],
        scratch_shapes=[pltpu.VMEM((TM, d), BF16), pltpu.SemaphoreType.DMA(())],
        compiler_params=pltpu.CompilerParams(
            dimension_semantics=("arbitrary", "arbitrary"), vmem_limit_bytes=V7X_VMEM_LIMIT),
        name="ffn_dense",
    )(x, g.reshape(1, d), w1, w3, w2, g_next.reshape(1, d))


def _rotary_tables(length, pos0):
    half = RET_D // 2
    inv = ROPE_THETA ** (-jnp.arange(half, dtype=F32) / half)
    pos = jnp.arange(length, dtype=F32) + pos0
    ang = pos[:, None] * inv[None, :]
    cos, sin = jnp.cos(ang), jnp.sin(ang)
    return jnp.concatenate([cos, cos], axis=-1), jnp.concatenate([-sin, sin], axis=-1)


def _decay_tables(c):
    log_g = jnp.log1p(-jnp.exp2(-5.0 - jnp.arange(RET_HEADS, dtype=F32)))
    idx = jnp.arange(c, dtype=F32)
    rel = idx[:, None] - idx[None, :]
    d_in = jnp.where(rel >= 0, jnp.exp(log_g[:, None, None] * jnp.maximum(rel, 0.0)), 0.0)
    d_q = jnp.exp(log_g[:, None] * (idx + 1.0)[None, :])
    d_k = jnp.exp(log_g[:, None] * (c - 1.0 - idx)[None, :])
    d_c = jnp.exp(log_g * c)
    return d_in, d_q, d_k, d_c


HIST31 = 32
HIST3 = 8


def _mixer_prompt_kernel(z_ref, cos_ref, sin_ref, din_ref, dq_ref, dk_ref, dc_ref, retg_ref,
                         c31w_ref, c31b_ref, clng_ref, clnb_ref, scw_ref, slng_ref, slnb_ref,
                         sguw_ref, sgub_ref, mix_in_ref,
                         mix_ref, ret_ref, b31_ref, b3_ref,
                         s_scr, ext31_scr, ext3_scr):
    del mix_in_ref
    c = pl.program_id(1)

    @pl.when(c == 0)
    def _():
        s_scr[...] = jnp.zeros_like(s_scr)
        ext31_scr[pl.ds(0, HIST31), :] = jnp.zeros((HIST31, GROUP_W), F32)
        ext3_scr[pl.ds(0, HIST3), :] = jnp.zeros((HIST3, GROUP_W), F32)

    cos = cos_ref[...]
    sin = sin_ref[...]

    for h in range(RET_HEADS):
        lo = h * RET_D
        q = z_ref[:, pl.ds(C_Q + lo, RET_D)]
        k = z_ref[:, pl.ds(C_K + lo, RET_D)]
        v = z_ref[:, pl.ds(C_V + lo, RET_D)]
        g = z_ref[:, pl.ds(C_G + lo, RET_D)]
        q = q * cos + pltpu.roll(q, RET_D // 2, axis=1) * sin
        k = (k * cos + pltpu.roll(k, RET_D // 2, axis=1) * sin) * (RET_D ** -0.5)
        qb = q.astype(BF16)
        vb = v.astype(BF16)
        scores = lax.dot_general(qb, k.astype(BF16), (((1,), (1,)), ((), ())),
                                 preferred_element_type=F32) * din_ref[h]
        inner = jnp.dot(scores.astype(BF16), vb, preferred_element_type=F32)
        s_old = s_scr[h]
        cross = jnp.dot(qb, s_old.astype(BF16), preferred_element_type=F32) * dq_ref[h]
        kd = (k * dk_ref[h]).astype(BF16)
        s_scr[h] = s_old * dc_ref[h] + lax.dot_general(kd, vb, (((0,), (0,)), ((), ())),
                                                       preferred_element_type=F32)
        r = inner + cross
        r = r * lax.rsqrt(jnp.mean(r * r, axis=-1, keepdims=True) + EPS)
        r = r * retg_ref[:, pl.ds(lo, RET_D)]
        mix_ref[:, pl.ds(lo, RET_D)] = (_silu(g) * r).astype(mix_ref.dtype)

    glu = z_ref[:, pl.ds(C_BL, GROUP_W)] * jax.nn.sigmoid(z_ref[:, pl.ds(C_BG, GROUP_W)])
    ext31_scr[pl.ds(HIST31, CHUNK), :] = glu
    conv = ext31_scr[pl.ds(HIST31 - (CONV_W - 1), CHUNK), :] * c31w_ref[pl.ds(0, 1), :]
    for j in range(1, CONV_W):
        conv = conv + ext31_scr[pl.ds(HIST31 - (CONV_W - 1) + j, CHUNK), :] * c31w_ref[pl.ds(j, 1), :]
    conv = conv + c31b_ref[...]
    out_b = _silu(_layernorm_rows(conv, clng_ref[...], clnb_ref[...]))
    mix_ref[:, pl.ds(GROUP_W, GROUP_W)] = out_b.astype(mix_ref.dtype)
    ext31_scr[pl.ds(0, HIST31), :] = ext31_scr[pl.ds(CHUNK, HIST31), :]

    gated = z_ref[:, pl.ds(C_CC, GROUP_W)] * z_ref[:, pl.ds(C_CH, GROUP_W)]
    ext3_scr[pl.ds(HIST3, CHUNK), :] = gated
    conv3 = ext3_scr[pl.ds(HIST3 - (SC_W - 1), CHUNK), :] * scw_ref[pl.ds(0, 1), :]
    for j in range(1, SC_W):
        conv3 = conv3 + ext3_scr[pl.ds(HIST3 - (SC_W - 1) + j, CHUNK), :] * scw_ref[pl.ds(j, 1), :]
    mix_ref[:, pl.ds(2 * GROUP_W, GROUP_W)] = (z_ref[:, pl.ds(C_CB, GROUP_W)] * conv3).astype(mix_ref.dtype)
    ext3_scr[pl.ds(0, HIST3), :] = ext3_scr[pl.ds(CHUNK, HIST3), :]

    vn = _layernorm_rows(z_ref[:, pl.ds(C_DV, GROUP_W)], slng_ref[...], slnb_ref[...])
    row = lax.broadcasted_iota(jnp.int32, (CHUNK, CHUNK), 0)
    col = lax.broadcasted_iota(jnp.int32, (CHUNK, CHUNK), 1)
    ch = GROUP_W // SGU_GROUPS
    for gi in range(SGU_GROUPS):
        wm = jnp.where(row >= col, sguw_ref[gi], 0.0).astype(BF16)
        mixed = jnp.dot(wm, vn[:, gi * ch:(gi + 1) * ch].astype(BF16), preferred_element_type=F32)
        mixed = mixed + sgub_ref[:, pl.ds(gi, 1)]
        du = z_ref[:, pl.ds(C_DU + gi * ch, ch)]
        mix_ref[:, pl.ds(3 * GROUP_W + gi * ch, ch)] = (du * mixed).astype(mix_ref.dtype)

    @pl.when(c == pl.num_programs(1) - 1)
    def _():
        ret_ref[...] = s_scr[...]
        b31_ref[...] = ext31_scr[pl.ds(HIST31 - (CONV_W - 1), CONV_W - 1), :]
        b3_ref[...] = ext3_scr[pl.ds(HIST3 - (SC_W - 1), SC_W - 1), :]


def _mixer_prompt(z, p, mix_init):
    cos, sin = _rotary_tables(SEQ, 0)
    d_in, d_q, d_k, d_c = _decay_tables(CHUNK)
    dq_t = jnp.broadcast_to(d_q[:, :, None], (RET_HEADS, CHUNK, RET_D))
    dk_t = jnp.broadcast_to(d_k[:, :, None], (RET_HEADS, CHUNK, RET_D))
    dc_t = jnp.broadcast_to(d_c[:, None, None], (RET_HEADS, RET_D, RET_D))

    def full(shape):
        return pl.BlockSpec(shape, lambda b, c: (0,) * len(shape))

    row = lambda a: a.reshape(1, -1)
    return pl.pallas_call(
        _mixer_prompt_kernel,
        grid=(BATCH, N_CHUNKS),
        in_specs=[
            pl.BlockSpec((CHUNK, D_IN), lambda b, c: (b * N_CHUNKS + c, 0)),
            pl.BlockSpec((CHUNK, RET_D), lambda b, c: (c, 0)),
            pl.BlockSpec((CHUNK, RET_D), lambda b, c: (c, 0)),
            full((RET_HEADS, CHUNK, CHUNK)),
            full((RET_HEADS, CHUNK, RET_D)),
            full((RET_HEADS, CHUNK, RET_D)),
            full((RET_HEADS, RET_D, RET_D)),
            full((1, GROUP_W)),
            full((CONV_W, GROUP_W)),
            full((1, GROUP_W)),
            full((1, GROUP_W)),
            full((1, GROUP_W)),
            full((SC_W, GROUP_W)),
            full((1, GROUP_W)),
            full((1, GROUP_W)),
            full((SGU_GROUPS, CHUNK, CHUNK)),
            full((CHUNK, SGU_GROUPS)),
            pl.BlockSpec(memory_space=pl.ANY),
        ],
        input_output_aliases={17: 0},
        out_specs=[
            pl.BlockSpec((CHUNK, D_MODEL), lambda b, c: (b * N_CHUNKS + c, 0)),
            pl.BlockSpec((None, RET_HEADS, RET_D, RET_D), lambda b, c: (b, 0, 0, 0)),
            pl.BlockSpec((None, CONV_W - 1, GROUP_W), lambda b, c: (b, 0, 0)),
            pl.BlockSpec((None, SC_W - 1, GROUP_W), lambda b, c: (b, 0, 0)),
        ],
        out_shape=[
            jax.ShapeDtypeStruct((T_ALL, D_MODEL), BF16),
            jax.ShapeDtypeStruct((BATCH, RET_HEADS, RET_D, RET_D), F32),
            jax.ShapeDtypeStruct((BATCH, CONV_W - 1, GROUP_W), F32),
            jax.ShapeDtypeStruct((BATCH, SC_W - 1, GROUP_W), F32),
        ],
        scratch_shapes=[
            pltpu.VMEM((RET_HEADS, RET_D, RET_D), F32),
            pltpu.VMEM((HIST31 + CHUNK, GROUP_W), F32),
            pltpu.VMEM((HIST3 + CHUNK, GROUP_W), F32),
        ],
        compiler_params=pltpu.CompilerParams(
            dimension_semantics=("arbitrary", "arbitrary"), vmem_limit_bytes=V7X_VMEM_LIMIT),
        name="mixer_prompt",
    )(z, cos, sin, d_in, dq_t, dk_t, dc_t, row(p["ret_norm_g"]), p["conv31_w"], row(p["conv31_b"]),
      row(p["conv_ln_g"]), row(p["conv_ln_b"]), p["sconv_w"], row(p["sgu_ln_g"]), row(p["sgu_ln_b"]),
      p["sgu_w"], p["sgu_b"].T, mix_init)


EXT31_S = 40
EXT3_S = 8


def _mixer_sample_kernel(z_ref, cos_ref, sin_ref, din_ref, dq_ref, dk_ref, dc_ref, retg_ref,
                         c31w_ref, c31b_ref, clng_ref, clnb_ref, scw_ref, slng_ref, slnb_ref,
                         sguw_ref, sgub_ref, sret_ref, s31_ref, s3_ref,
                         mix_ref, ret_ref, b31_ref, b3_ref, vn_ref,
                         ext31_scr, ext3_scr):
    cos = cos_ref[...]
    sin = sin_ref[...]
    n_hist31 = CONV_W - 1
    n_hist3 = SC_W - 1

    for h in range(RET_HEADS):
        lo = h * RET_D
        q = z_ref[:, :, pl.ds(C_Q + lo, RET_D)]
        k = z_ref[:, :, pl.ds(C_K + lo, RET_D)]
        g = z_ref[:, :, pl.ds(C_G + lo, RET_D)]
        q = q * cos + pltpu.roll(q, RET_D // 2, axis=2) * sin
        k = (k * cos + pltpu.roll(k, RET_D // 2, axis=2) * sin) * (RET_D ** -0.5)
        s_old = sret_ref[:, h]
        r = lax.dot_general(q, s_old, (((2,), (1,)), ((0,), (0,))),
                            preferred_element_type=F32) * dq_ref[h]
        for s in range(DEC_SEQ):
            k_s = z_ref[:, pl.ds(s, 1), pl.ds(C_K + lo, RET_D)]
            k_s = (k_s * cos[s:s + 1] + pltpu.roll(k_s, RET_D // 2, axis=2) * sin[s:s + 1]) * (RET_D ** -0.5)
            v_s = z_ref[:, pl.ds(s, 1), pl.ds(C_V + lo, RET_D)]
            score = jnp.sum(q * k_s, axis=-1, keepdims=True) * din_ref[h, s]
            r = r + score * v_s
        kd = k * dk_ref[h]
        v = z_ref[:, :, pl.ds(C_V + lo, RET_D)]
        ret_ref[:, h] = s_old * dc_ref[h] + lax.dot_general(
            kd, v, (((1,), (1,)), ((0,), (0,))), preferred_element_type=F32)
        r = r * lax.rsqrt(jnp.mean(r * r, axis=-1, keepdims=True) + EPS)
        r = r * retg_ref[:, pl.ds(lo, RET_D)]
        mix_ref[:, :, pl.ds(lo, RET_D)] = _silu(g) * r

    glu = z_ref[:, :, pl.ds(C_BL, GROUP_W)] * jax.nn.sigmoid(z_ref[:, :, pl.ds(C_BG, GROUP_W)])
    ext31_scr[:, pl.ds(0, n_hist31), :] = s31_ref[...]
    ext31_scr[:, pl.ds(n_hist31, DEC_SEQ), :] = glu
    conv = ext31_scr[:, pl.ds(0, DEC_SEQ), :] * c31w_ref[pl.ds(0, 1), :]
    for j in range(1, CONV_W):
        conv = conv + ext31_scr[:, pl.ds(j, DEC_SEQ), :] * c31w_ref[pl.ds(j, 1), :]
    conv = conv + c31b_ref[...]
    mix_ref[:, :, pl.ds(GROUP_W, GROUP_W)] = _silu(_layernorm_rows(conv, clng_ref[...], clnb_ref[...]))
    b31_ref[...] = ext31_scr[:, pl.ds(DEC_SEQ, n_hist31), :]

    gated = z_ref[:, :, pl.ds(C_CC, GROUP_W)] * z_ref[:, :, pl.ds(C_CH, GROUP_W)]
    ext3_scr[:, pl.ds(0, n_hist3), :] = s3_ref[...]
    ext3_scr[:, pl.ds(n_hist3, DEC_SEQ), :] = gated
    conv3 = ext3_scr[:, pl.ds(0, DEC_SEQ), :] * scw_ref[pl.ds(0, 1), :]
    for j in range(1, SC_W):
        conv3 = conv3 + ext3_scr[:, pl.ds(j, DEC_SEQ), :] * scw_ref[pl.ds(j, 1), :]
    mix_ref[:, :, pl.ds(2 * GROUP_W, GROUP_W)] = z_ref[:, :, pl.ds(C_CB, GROUP_W)] * conv3
    b3_ref[...] = ext3_scr[:, pl.ds(DEC_SEQ, n_hist3), :]

    vn = _layernorm_rows(z_ref[:, :, pl.ds(C_DV, GROUP_W)], slng_ref[...], slnb_ref[...])
    vn_ref[...] = vn
    mixed = sgub_ref[...]
    for s in range(DEC_SEQ):
        mixed = mixed + sguw_ref[s] * vn_ref[:, pl.ds(s, 1), :]
    mix_ref[:, :, pl.ds(3 * GROUP_W, GROUP_W)] = z_ref[:, :, pl.ds(C_DU, GROUP_W)] * mixed


def _mixer_sample(z_s, p, layer, s_ret, s_31, s_3):
    cos, sin = _rotary_tables(DEC_SEQ, PAST_LEN)
    d_in, d_q, d_k, d_c = _decay_tables(DEC_SEQ)
    din_t = jnp.swapaxes(d_in, 1, 2)[:, :, :, None]
    dq_t = jnp.broadcast_to(d_q[:, :, None], (RET_HEADS, DEC_SEQ, RET_D))
    dk_t = jnp.broadcast_to(d_k[:, :, None], (RET_HEADS, DEC_SEQ, RET_D))
    dc_t = jnp.broadcast_to(d_c[:, None, None], (RET_HEADS, RET_D, RET_D))
    ch = GROUP_W // SGU_GROUPS
    w4 = jnp.tril(p["sgu_w"][:, :DEC_SEQ, :DEC_SEQ])
    w_t = jnp.repeat(jnp.transpose(w4, (2, 1, 0)), ch, axis=-1)
    b_t = jnp.repeat(p["sgu_b"][:, :DEC_SEQ].T, ch, axis=-1)
    bt = SAMPLE_BT

    def full(shape):
        return pl.BlockSpec(shape, lambda i: (0,) * len(shape))

    row = lambda a: a.reshape(1, -1)
    return pl.pallas_call(
        _mixer_sample_kernel,
        grid=(DEC_BATCH // bt,),
        in_specs=[
            pl.BlockSpec((bt, DEC_SEQ, D_IN), lambda i: (i, 0, 0)),
            full((DEC_SEQ, RET_D)),
            full((DEC_SEQ, RET_D)),
            full((RET_HEADS, DEC_SEQ, DEC_SEQ, 1)),
            full((RET_HEADS, DEC_SEQ, RET_D)),
            full((RET_HEADS, DEC_SEQ, RET_D)),
            full((RET_HEADS, RET_D, RET_D)),
            full((1, GROUP_W)),
            full((CONV_W, GROUP_W)),
            full((1, GROUP_W)),
            full((1, GROUP_W)),
            full((1, GROUP_W)),
            full((SC_W, GROUP_W)),
            full((1, GROUP_W)),
            full((1, GROUP_W)),
            full((DEC_SEQ, DEC_SEQ, GROUP_W)),
            full((DEC_SEQ, GROUP_W)),
            pl.BlockSpec((None, bt, RET_HEADS, RET_D, RET_D), lambda i: (layer, i, 0, 0, 0)),
            pl.BlockSpec((None, bt, CONV_W - 1, GROUP_W), lambda i: (layer, i, 0, 0)),
            pl.BlockSpec((None, bt, SC_W - 1, GROUP_W), lambda i: (layer, i, 0, 0)),
        ],
        out_specs=[
            pl.BlockSpec((bt, DEC_SEQ, D_MODEL), lambda i: (i, 0, 0)),
            pl.BlockSpec((bt, RET_HEADS, RET_D, RET_D), lambda i: (i, 0, 0, 0)),
            pl.BlockSpec((bt, CONV_W - 1, GROUP_W), lambda i: (i, 0, 0)),
            pl.BlockSpec((bt, SC_W - 1, GROUP_W), lambda i: (i, 0, 0)),
            pl.BlockSpec((bt, DEC_SEQ, GROUP_W), lambda i: (i, 0, 0)),
        ],
        out_shape=[
            jax.ShapeDtypeStruct((DEC_BATCH, DEC_SEQ, D_MODEL), F32),
            jax.ShapeDtypeStruct((DEC_BATCH, RET_HEADS, RET_D, RET_D), F32),
            jax.ShapeDtypeStruct((DEC_BATCH, CONV_W - 1, GROUP_W), F32),
            jax.ShapeDtypeStruct((DEC_BATCH, SC_W - 1, GROUP_W), F32),
            jax.ShapeDtypeStruct((DEC_BATCH, DEC_SEQ, GROUP_W), F32),
        ],
        scratch_shapes=[
            pltpu.VMEM((bt, EXT31_S, GROUP_W), F32),
            pltpu.VMEM((bt, EXT3_S, GROUP_W), F32),
        ],
        compiler_params=pltpu.CompilerParams(
            dimension_semantics=("arbitrary",), vmem_limit_bytes=V7X_VMEM_LIMIT),
        name="mixer_sample",
    )(z_s, cos, sin, din_t, dq_t, dk_t, dc_t, row(p["ret_norm_g"]), p["conv31_w"], row(p["conv31_b"]),
      row(p["conv_ln_g"]), row(p["conv_ln_b"]), p["sconv_w"], row(p["sgu_ln_g"]), row(p["sgu_ln_b"]),
      w_t, b_t, s_ret, s_31, s_3)


def _to_token_major(dst, val, pitch=TOK_ROWS):
    n = val.shape[0]
    for a in range(TOK_ROWS):
        dst[pl.ds(a, n, stride=pitch), :] = val[:, a * LANES:(a + 1) * LANES]


STAGE_PITCH = 24
DMA_UNROLL = 8


def _router_kernel(x_ref, g_ref, rw_ref, e_ref, p_ref, c_ref):
    h = _rms_rows(x_ref[...], g_ref[...])
    lane = lax.broadcasted_iota(jnp.int32, (h.shape[0], N_EXPERTS), 1)
    logits = jnp.zeros((h.shape[0], N_EXPERTS), F32)
    for e in range(N_EXPERTS):
        col = jnp.sum(h * rw_ref[pl.ds(e, 1), :], axis=-1, keepdims=True)
        logits = jnp.where(lane == e, col, logits)
    m1 = jnp.max(logits, axis=-1, keepdims=True)
    i1 = jnp.min(jnp.where(logits == m1, lane, N_EXPERTS), axis=-1, keepdims=True)
    rest = jnp.where(lane == i1, -jnp.inf, logits)
    m2 = jnp.max(rest, axis=-1, keepdims=True)
    i2 = jnp.min(jnp.where(rest == m2, lane, N_EXPERTS), axis=-1, keepdims=True)
    e2 = jnp.exp(m2 - m1)
    den = 1.0 + e2
    e_ref[...] = jnp.where(lane == 0, i1, jnp.where(lane == 1, i2, 0))
    p_ref[...] = jnp.where(lane == 0, 1.0 / den, jnp.where(lane == 1, e2 / den, 0.0))
    c_ref[pl.ds(0, 1), :] = jnp.sum(jnp.where(lane == i1, 1.0, 0.0), axis=0, keepdims=True).astype(jnp.int32)
    c_ref[pl.ds(1, 1), :] = jnp.sum(jnp.where(lane == i2, 1.0, 0.0), axis=0, keepdims=True).astype(jnp.int32)


def _router(x, g, rw):
    t, d = x.shape
    return pl.pallas_call(
        _router_kernel,
        grid=(N_TB,),
        in_specs=[
            pl.BlockSpec((MOE_TB, d), lambda i: (i, 0)),
            pl.BlockSpec((1, d), lambda i: (0, 0)),
            pl.BlockSpec((N_EXPERTS, d), lambda i: (0, 0)),
        ],
        out_specs=[
            pl.BlockSpec((MOE_TB, N_EXPERTS), lambda i: (i, 0)),
            pl.BlockSpec((MOE_TB, N_EXPERTS), lambda i: (i, 0)),
            pl.BlockSpec((None, TOP_K, N_EXPERTS), lambda i: (i, 0, 0)),
        ],
        out_shape=[
            jax.ShapeDtypeStruct((t, N_EXPERTS), jnp.int32),
            jax.ShapeDtypeStruct((t, N_EXPERTS), F32),
            jax.ShapeDtypeStruct((N_TB, TOP_K, N_EXPERTS), jnp.int32),
        ],
        compiler_params=pltpu.CompilerParams(
            dimension_semantics=("arbitrary",), vmem_limit_bytes=V7X_VMEM_LIMIT),
        name="moe_router",
    )(x, g.reshape(1, d), rw.T)


def _pick(table, idx):
    lanes = jnp.arange(N_EXPERTS, dtype=jnp.int32)
    return jnp.sum(jnp.where(idx[..., None] == lanes, table, 0), axis=-1)


def _expert_layout(cnt):
    tot = jnp.sum(cnt, axis=(0, 1))
    start = jnp.cumsum(tot) - tot
    blk = jnp.sum(cnt, axis=1)
    blk_base = start[None, :] + jnp.cumsum(blk, axis=0) - blk
    base = jnp.stack([blk_base, blk_base + cnt[:, 0]], axis=1)
    n_sb = (tot + SB - 1) // SB
    sb_end = jnp.cumsum(n_sb)
    n_used = sb_end[-1]
    s_idx = jnp.arange(NS_MAX, dtype=jnp.int32)
    s_cl = jnp.minimum(s_idx, n_used - 1)
    sb_e = jnp.sum(s_cl[:, None] >= sb_end[None, :], axis=-1).astype(jnp.int32)
    local = s_cl - _pick(sb_end - n_sb, sb_e)
    sb_row0 = _pick(start, sb_e) + local * SB
    sb_rows = jnp.where(s_idx < n_used, jnp.clip(_pick(tot, sb_e) - local * SB, 0, SB), 0)
    i32 = lambda a: a.astype(jnp.int32)
    return i32(base), sb_e, i32(sb_row0), i32(sb_rows), i32(n_used).reshape(1)


def _dest_kernel(e_ref, b_ref, d_ref):
    eid = e_ref[...]
    lane = lax.broadcasted_iota(jnp.int32, eid.shape, 1)
    row = lax.broadcasted_iota(jnp.int32, (MOE_TB, MOE_TB), 0)
    col = lax.broadcasted_iota(jnp.int32, (MOE_TB, MOE_TB), 1)
    earlier = jnp.where(row > col, 1.0, 0.0).astype(BF16)
    base = b_ref[...].astype(F32)

    def dest_of(k):
        mine = lane == eid[:, k:k + 1]
        before = jnp.dot(earlier, jnp.where(mine, 1.0, 0.0).astype(BF16), preferred_element_type=F32)
        return jnp.sum(jnp.where(mine, before + base[k:k + 1], 0.0), axis=-1, keepdims=True)

    d_ref[...] = jnp.where(lane == 0, dest_of(0), jnp.where(lane == 1, dest_of(1), 0.0)).astype(jnp.int32)


def _dest_rows(eid, base):
    return pl.pallas_call(
        _dest_kernel,
        grid=(N_TB,),
        in_specs=[
            pl.BlockSpec((MOE_TB, N_EXPERTS), lambda i: (i, 0)),
            pl.BlockSpec((None, TOP_K, N_EXPERTS), lambda i: (i, 0, 0)),
        ],
        out_specs=pl.BlockSpec((MOE_TB, N_EXPERTS), lambda i: (i, 0)),
        out_shape=jax.ShapeDtypeStruct((T_ALL, N_EXPERTS), jnp.int32),
        compiler_params=pltpu.CompilerParams(
            dimension_semantics=("arbitrary",), vmem_limit_bytes=V7X_VMEM_LIMIT),
        name="moe_dest_rows",
    )(eid, base)


def _dispatch_kernel(d_ref, x_ref, g_ref, xg_hbm, stage, zeros, sem, zsem):
    i = pl.program_id(0)
    n = pl.num_programs(0)
    slot = i % 2
    blk_rows = MOE_TB * TOK_ROWS

    def wait_slot(sl):
        for _ in range(TOP_K):
            pltpu.make_async_copy(stage.at[sl, pl.ds(0, blk_rows)], xg_hbm.at[pl.ds(0, blk_rows)], sem.at[sl]).wait()

    @pl.when(i >= 2)
    def _():
        wait_slot(slot)

    _to_token_major(stage.at[slot], _rms_rows(x_ref[...], g_ref[...]), STAGE_PITCH)

    def body(r, carry):
        src = stage.at[slot, pl.ds(pl.multiple_of(r * STAGE_PITCH, 8), TOK_ROWS)]
        for k in range(TOP_K):
            row = d_ref[0, 0, TOP_K * r + k]
            dst = xg_hbm.at[pl.ds(pl.multiple_of(row * TOK_ROWS, TOK_ROWS), TOK_ROWS)]
            pltpu.make_async_copy(src, dst, sem.at[slot]).start()
        return carry

    lax.fori_loop(0, MOE_TB, body, 0, unroll=DMA_UNROLL)

    @pl.when(i == n - 1)
    def _():
        zeros[...] = jnp.zeros_like(zeros)
        tail = [pltpu.make_async_copy(zeros, xg_hbm.at[pl.ds((N_SLOTS + q * (SUB // TOK_ROWS)) * TOK_ROWS, SUB)], zsem)
                for q in range(TOK_ROWS)]
        for cp in tail:
            cp.start()
        for cp in tail:
            cp.wait()
        wait_slot(slot)
        wait_slot(1 - slot)


def _dispatch(x, g, dest_flat):
    t, d = x.shape
    return pl.pallas_call(
        _dispatch_kernel,
        grid=(N_TB,),
        in_specs=[
            pl.BlockSpec((1, 1, TOP_K * MOE_TB), lambda i: (i, 0, 0), memory_space=pltpu.SMEM),
            pl.BlockSpec((MOE_TB, d), lambda i: (i, 0)),
            pl.BlockSpec((1, d), lambda i: (0, 0)),
        ],
        out_specs=pl.BlockSpec(memory_space=pl.ANY),
        out_shape=jax.ShapeDtypeStruct(((N_SLOTS + SUB) * TOK_ROWS, LANES), F32),
        scratch_shapes=[
            pltpu.VMEM((2, MOE_TB * STAGE_PITCH, LANES), F32),
            pltpu.VMEM((SUB, LANES), F32),
            pltpu.SemaphoreType.DMA((2,)),
            pltpu.SemaphoreType.DMA(()),
        ],
        compiler_params=pltpu.CompilerParams(
            dimension_semantics=("arbitrary",), vmem_limit_bytes=V7X_VMEM_LIMIT),
        name="moe_dispatch",
    )(dest_flat, x, g.reshape(1, d))


def _moe_ffn_kernel(se_ref, r0_ref, nr_ref, nu_ref, xg_hbm, w1_ref, w3_ref, w2_ref, y_hbm,
                    x_scr, acc, stage, w1b, w3b, w2b, sem_in, sem_out):
    s = pl.program_id(0)
    j = pl.program_id(1)
    nj = pl.num_programs(1)
    rows = nr_ref[s]
    row0 = r0_ref[s]
    nsub = lax.shift_right_logical(rows + (SUB - 1), int(math.log2(SUB)))
    nchunk = (nsub + (CHUNK_ROWS // SUB - 1)) // (CHUNK_ROWS // SUB)
    sub_rows = SUB * TOK_ROWS

    is_middle = (j > 0) & (j < nj - 1)

    def cast_weights():
        ws = (w1_ref[...].astype(BF16), w3_ref[...].astype(BF16), w2_ref[...].astype(BF16))
        w1b[...], w3b[...], w2b[...] = ws
        return ws

    @pl.when((nsub > 0) & jnp.logical_not(is_middle))
    def _():
        cast_weights()

    def sub(r):
        return pl.ds(pl.multiple_of(r * SUB, SUB), SUB)

    def swiglu(xs, ws=None):
        w1, w3, w2 = ws if ws is not None else (w1b[...], w3b[...], w2b[...])
        a = jnp.dot(xs, w1, preferred_element_type=F32)
        b = jnp.dot(xs, w3, preferred_element_type=F32)
        u = (_silu(a) * b).astype(BF16)
        return jnp.dot(u, w2, preferred_element_type=F32)

    def tile(r):
        return swiglu(x_scr[sub(r), :])

    def in_copy(r, slot):
        first = pl.multiple_of((row0 + r * SUB) * TOK_ROWS, TOK_ROWS)
        return pltpu.make_async_copy(xg_hbm.at[pl.ds(first, sub_rows)], stage.at[slot], sem_in.at[slot])

    @pl.when(j == 0)
    def _():
        @pl.when(nsub > 0)
        def _():
            in_copy(0, 0).start()

        def body(r, carry):
            slot = r % 2
            in_copy(r, slot).wait()

            @pl.when(r + 1 < nsub)
            def _():
                in_copy(r + 1, 1 - slot).start()

            for a in range(TOK_ROWS):
                x_scr[sub(r), pl.ds(a * LANES, LANES)] = stage[slot, pl.ds(a, SUB, stride=TOK_ROWS), :].astype(BF16)
            acc[sub(r), :] = tile(r)
            return carry

        lax.fori_loop(0, nsub, body, 0)

        def clear(r, carry):
            x_scr[sub(r), :] = jnp.zeros((SUB, D_MODEL), BF16)
            acc[sub(r), :] = jnp.zeros((SUB, D_MODEL), F32)
            return carry

        lax.fori_loop(nsub, nchunk * (CHUNK_ROWS // SUB), clear, 0)

    @pl.when(is_middle)
    def _():
        for c in range(SB // CHUNK_ROWS):
            @pl.when(c < nchunk)
            def _():
                ws = cast_weights() if c == 0 else None
                rows_c = pl.ds(c * CHUNK_ROWS, CHUNK_ROWS)
                acc[rows_c, :] += swiglu(x_scr[rows_c, :], ws)

    @pl.when(j == nj - 1)
    def _():
        def out_copies(r, act):
            nv = jnp.clip(rows - r * SUB, 0, SUB)
            bit = SUB
            while bit >= 1:
                off = nv & (-2 * bit)
                src = stage.at[0, pl.ds(pl.multiple_of(off * TOK_ROWS, TOK_ROWS), bit * TOK_ROWS)]
                first = pl.multiple_of((row0 + r * SUB + off) * TOK_ROWS, TOK_ROWS)
                cp = pltpu.make_async_copy(src, y_hbm.at[pl.ds(first, bit * TOK_ROWS)], sem_out)
                pl.when((nv & bit) != 0)(functools.partial(act, cp))
                bit //= 2

        def body(r, carry):
            y = acc[sub(r), :] + tile(r)

            @pl.when(r > 0)
            def _():
                out_copies(r - 1, lambda cp: cp.wait())

            _to_token_major(stage.at[0], y)
            out_copies(r, lambda cp: cp.start())
            return carry

        lax.fori_loop(0, nsub, body, 0)

        @pl.when(nsub > 0)
        def _():
            out_copies(nsub - 1, lambda cp: cp.wait())


def _moe_ffn(xg, w1, w3, w2, sb_e, sb_row0, sb_rows, n_used):
    d = w1.shape[1]
    f = w1.shape[2]
    nj = f // TF

    def jmap(s, j, nu):
        return jnp.where(s < nu[0], j, nj - 1)

    return pl.pallas_call(
        _moe_ffn_kernel,
        grid_spec=pltpu.PrefetchScalarGridSpec(
            num_scalar_prefetch=4,
            grid=(NS_MAX, nj),
            in_specs=[
                pl.BlockSpec(memory_space=pl.ANY),
                pl.BlockSpec((None, d, TF), lambda s, j, se, r0, nr, nu: (se[s], 0, jmap(s, j, nu))),
                pl.BlockSpec((None, d, TF), lambda s, j, se, r0, nr, nu: (se[s], 0, jmap(s, j, nu))),
                pl.BlockSpec((None, TF, d), lambda s, j, se, r0, nr, nu: (se[s], jmap(s, j, nu), 0)),
            ],
            out_specs=pl.BlockSpec(memory_space=pl.ANY),
            scratch_shapes=[
                pltpu.VMEM((SB, d), BF16),
                pltpu.VMEM((SB, d), F32),
                pltpu.VMEM((2, SUB * TOK_ROWS, LANES), F32),
                pltpu.VMEM((d, TF), BF16),
                pltpu.VMEM((d, TF), BF16),
                pltpu.VMEM((TF, d), BF16),
                pltpu.SemaphoreType.DMA((2,)),
                pltpu.SemaphoreType.DMA(()),
            ],
        ),
        out_shape=jax.ShapeDtypeStruct((N_SLOTS * TOK_ROWS, LANES), F32),
        compiler_params=pltpu.CompilerParams(
            dimension_semantics=("arbitrary", "arbitrary"), vmem_limit_bytes=V7X_VMEM_LIMIT),
        name="moe_ffn",
    )(sb_e, sb_row0, sb_rows, n_used, xg, w1, w3, w2)


N_TB_PROMPT = T_PROMPT // MOE_TB


def _combine_kernel(d_ref, dn_ref, x_ref, p_ref, g_ref, y_hbm, op_ref, os_ref, stage, sem, *, final_norm):
    i = pl.program_id(0)
    n = pl.num_programs(0)
    slot = i % 2
    blk_rows = MOE_TB * TOK_ROWS

    def start_block(dref, sl):
        def body(r, carry):
            for k in range(TOP_K):
                row = dref[0, 0, TOP_K * r + k]
                src = y_hbm.at[pl.ds(pl.multiple_of(row * TOK_ROWS, TOK_ROWS), TOK_ROWS)]
                dst = stage.at[sl, k, pl.ds(pl.multiple_of(r * STAGE_PITCH, 8), TOK_ROWS)]
                pltpu.make_async_copy(src, dst, sem.at[sl]).start()
            return carry

        lax.fori_loop(0, MOE_TB, body, 0, unroll=DMA_UNROLL)

    @pl.when(i == 0)
    def _():
        start_block(d_ref, 0)

    @pl.when(i + 1 < n)
    def _():
        start_block(dn_ref, 1 - slot)

    for k in range(TOP_K):
        pltpu.make_async_copy(y_hbm.at[pl.ds(0, blk_rows)], stage.at[slot, k, pl.ds(0, blk_rows)], sem.at[slot]).wait()
    g0 = p_ref[:, 0:1]
    g1 = p_ref[:, 1:2]
    pieces = []
    for a in range(TOK_ROWS):
        y0 = stage[slot, 0, pl.ds(a, MOE_TB, stride=STAGE_PITCH), :]
        y1 = stage[slot, 1, pl.ds(a, MOE_TB, stride=STAGE_PITCH), :]
        pieces.append(x_ref[:, pl.ds(a * LANES, LANES)] + (y0 * g0 + y1 * g1))
    y = jnp.concatenate(pieces, axis=1)
    if final_norm:
        y = _rms_rows(y, g_ref[...])

    @pl.when(i < N_TB_PROMPT)
    def _():
        op_ref[...] = y

    @pl.when(i >= N_TB_PROMPT)
    def _():
        os_ref[...] = y


def _moe_combine(x, y, dest_flat, gate, g, final_norm):
    t, d = x.shape
    return pl.pallas_call(
        functools.partial(_combine_kernel, final_norm=final_norm),
        grid=(N_TB,),
        in_specs=[
            pl.BlockSpec((1, 1, TOP_K * MOE_TB), lambda i: (i, 0, 0), memory_space=pltpu.SMEM),
            pl.BlockSpec((1, 1, TOP_K * MOE_TB), lambda i: (jnp.minimum(i + 1, N_TB - 1), 0, 0),
                         memory_space=pltpu.SMEM),
            pl.BlockSpec((MOE_TB, d), lambda i: (i, 0)),
            pl.BlockSpec((MOE_TB, N_EXPERTS), lambda i: (i, 0)),
            pl.BlockSpec((1, d), lambda i: (0, 0)),
            pl.BlockSpec(memory_space=pl.ANY),
        ],
        out_specs=[
            pl.BlockSpec((MOE_TB, d), lambda i: (jnp.minimum(i, N_TB_PROMPT - 1), 0)),
            pl.BlockSpec((MOE_TB, d), lambda i: (jnp.maximum(i - N_TB_PROMPT, 0), 0)),
        ],
        out_shape=[
            jax.ShapeDtypeStruct((T_PROMPT, d), F32),
            jax.ShapeDtypeStruct((T_SAMPLE, d), F32),
        ],
        scratch_shapes=[
            pltpu.VMEM((2, TOP_K, MOE_TB * STAGE_PITCH, LANES), F32),
            pltpu.SemaphoreType.DMA((2,)),
        ],
        compiler_params=pltpu.CompilerParams(
            dimension_semantics=("arbitrary",), vmem_limit_bytes=V7X_VMEM_LIMIT),
        name="moe_combine",
    )(dest_flat, dest_flat, x, gate, g.reshape(1, d), y)


def _moe(x, norm_g, rw, w1, w3, w2, out_g, final_norm):
    eid, gate, cnt = _router(x, norm_g, rw)
    base, sb_e, sb_row0, sb_rows, n_used = _expert_layout(cnt)
    dest = _dest_rows(eid, base)
    dest_flat = dest[:, :TOP_K].reshape(N_TB, 1, TOP_K * MOE_TB)
    xg = _dispatch(x, norm_g, dest_flat)
    y = _moe_ffn(xg, w1, w3, w2, sb_e, sb_row0, sb_rows, n_used)
    return _moe_combine(x, y, dest_flat, gate, out_g, final_norm)


def kernel(x_prompt, x_sample, state_ret, state_conv31, state_conv3, mix_norm_g, w_in, ret_norm_g, conv31_w, conv31_b, conv_ln_g, conv_ln_b, sconv_w, sgu_ln_g, sgu_ln_b, sgu_w, sgu_b, w_out, ffn_norm_g, dense_w1, dense_w3, dense_w2, router_w, moe_w1, moe_w3, moe_w2, final_norm_g):
    assert DEPTH == 2, "layer 0 has the dense channel mixer; layer 1's expert mixer ends with the final norm"
    x, hn = _prep(x_prompt.reshape(T_PROMPT, D_MODEL), x_sample.reshape(T_SAMPLE, D_MODEL), mix_norm_g[0])
    rets_p, rets_s, c31_p, c31_s, c3_p, c3_s, vns = [], [], [], [], [], [], []
    for l in range(DEPTH):
        p = dict(ret_norm_g=ret_norm_g[l], conv31_w=conv31_w[l], conv31_b=conv31_b[l],
                 conv_ln_g=conv_ln_g[l], conv_ln_b=conv_ln_b[l], sconv_w=sconv_w[l],
                 sgu_ln_g=sgu_ln_g[l], sgu_ln_b=sgu_ln_b[l], sgu_w=sgu_w[l], sgu_b=sgu_b[l])
        z = _in_proj(hn, w_in, l)
        z_s = z[T_PROMPT:].reshape(DEC_BATCH, DEC_SEQ, D_IN)
        mix_s, ret_s, b31_s, b3_s, vn_s = _mixer_sample(z_s, p, l, state_ret, state_conv31, state_conv3)
        mix_init = jnp.pad(mix_s.reshape(T_SAMPLE, D_MODEL).astype(BF16), ((T_PROMPT, 0), (0, 0)))
        mix, ret_p, b31_p, b3_p = _mixer_prompt(z, p, mix_init)
        x = _out_proj_residual(mix, w_out, l, x)
        j = l // 2
        if l % 2 == 0:
            x, hn = _ffn_dense(x, ffn_norm_g[l], dense_w1[j], dense_w3[j], dense_w2[j], mix_norm_g[l + 1])
        else:
            y_p, y_s = _moe(x, ffn_norm_g[l], router_w[j], moe_w1[j], moe_w3[j], moe_w2[j],
                            final_norm_g, final_norm=True)
        rets_p.append(ret_p); rets_s.append(ret_s)
        c31_p.append(b31_p); c31_s.append(b31_s)
        c3_p.append(b3_p); c3_s.append(b3_s)
        vns.append(vn_s)
    y_prompt = y_p.reshape(BATCH, SEQ, D_MODEL)
    y_sample = y_s.reshape(DEC_BATCH, DEC_SEQ, D_MODEL)
    return (y_prompt, y_sample, jnp.stack(rets_p), jnp.stack(rets_s), jnp.stack(c31_p), jnp.stack(c31_s),
            jnp.stack(c3_p), jnp.stack(c3_s), jnp.stack(vns))
```

```python
import functools
import math

import jax
import jax.numpy as jnp
from jax import lax
from jax.experimental import pallas as pl
from jax.experimental.pallas import tpu as pltpu

F32 = jnp.float32
BF16 = jnp.bfloat16

D_MODEL = 2048
BATCH = 4
SEQ = 2048
DEPTH = 2
DEC_BATCH = 128
DEC_SEQ = 4
PAST_LEN = 16384
GROUP_W = 512
D_IN = 11 * GROUP_W
RET_HEADS = 4
RET_D = 128
CHUNK = 128
ROPE_THETA = 10000.0
CONV_W = 31
SC_W = 3
SGU_GROUPS = 4
D_FF = 5632
N_EXPERTS = 8
TOP_K = 2
EPS = 1e-6

T_PROMPT = BATCH * SEQ
T_SAMPLE = DEC_BATCH * DEC_SEQ
T_ALL = T_PROMPT + T_SAMPLE
N_CHUNKS = SEQ // CHUNK

C_Q, C_K, C_V, C_G, C_BL, C_BG, C_CB, C_CC, C_CH, C_DU, C_DV = (i * GROUP_W for i in range(11))

V7X_VMEM_LIMIT = 56 * 1024 * 1024

TM = 1088
TM_IN = 2176
TM_OUT = 544
NORM_ROWS = 272
TN_IN = 512
TF = 256
SAMPLE_BT = 8

LANES = 128
SUBLANES = 8
TOK_ROWS = D_MODEL // LANES
MOE_TB = 256
N_TB = T_ALL // MOE_TB
N_SLOTS = T_ALL * TOP_K
SUB = 256
SB = 9 * SUB
CHUNK_ROWS = SB // 2
NS_MAX = N_SLOTS // SB + N_EXPERTS


def _rms_rows(x, g):
    ms = jnp.mean(x * x, axis=-1, keepdims=True)
    return (x * lax.rsqrt(ms + EPS)) * g


def _layernorm_rows(x, g, b):
    mu = jnp.mean(x, axis=-1, keepdims=True)
    xc = x - mu
    return xc * lax.rsqrt(jnp.mean(xc * xc, axis=-1, keepdims=True) + EPS) * g + b


def _silu(x):
    return x * jax.nn.sigmoid(x)


def _norm_block_to(x_ref, g_ref, dst_ref):
    rows = x_ref.shape[0]
    g = g_ref[...]

    def body(c, carry):
        r0 = pl.multiple_of(c * NORM_ROWS, NORM_ROWS)
        x = x_ref[pl.ds(r0, NORM_ROWS), :]
        dst_ref[pl.ds(r0, NORM_ROWS), :] = _rms_rows(x, g).astype(dst_ref.dtype)
        return carry

    lax.fori_loop(0, rows // NORM_ROWS, body, 0)


PREP_ROWS = 512
N_PREP_PROMPT = T_PROMPT // PREP_ROWS


def _prep_kernel(xp_ref, xs_ref, g_ref, x_ref, hn_ref):
    def emit(src_ref):
        x = src_ref[...]
        x_ref[...] = x
        hn_ref[...] = _rms_rows(x, g_ref[...]).astype(hn_ref.dtype)

    is_prompt = pl.program_id(0) < N_PREP_PROMPT
    pl.when(is_prompt)(functools.partial(emit, xp_ref))
    pl.when(jnp.logical_not(is_prompt))(functools.partial(emit, xs_ref))


def _prep(x_prompt, x_sample, g):
    d = x_prompt.shape[1]
    blk = lambda f: pl.BlockSpec((PREP_ROWS, d), f)
    return pl.pallas_call(
        _prep_kernel,
        grid=(T_ALL // PREP_ROWS,),
        in_specs=[
            blk(lambda i: (jnp.minimum(i, N_PREP_PROMPT - 1), 0)),
            blk(lambda i: (jnp.maximum(i - N_PREP_PROMPT, 0), 0)),
            pl.BlockSpec((1, d), lambda i: (0, 0)),
        ],
        out_specs=[blk(lambda i: (i, 0)), blk(lambda i: (i, 0))],
        out_shape=[jax.ShapeDtypeStruct((T_ALL, d), F32), jax.ShapeDtypeStruct((T_ALL, d), BF16)],
        compiler_params=pltpu.CompilerParams(
            dimension_semantics=("arbitrary",), vmem_limit_bytes=V7X_VMEM_LIMIT),
        name="prep_rows",
    )(x_prompt, x_sample, g.reshape(1, d))


def _in_proj_kernel(h_ref, w_ref, o_ref):
    o_ref[...] = jnp.dot(h_ref[...], w_ref[...].astype(BF16), preferred_element_type=F32)


def _in_proj(hn, w_all, layer):
    t, d = hn.shape
    n = w_all.shape[2]
    return pl.pallas_call(
        _in_proj_kernel,
        grid=(t // TM_IN, n // TN_IN),
        in_specs=[
            pl.BlockSpec((TM_IN, d), lambda i, j: (i, 0)),
            pl.BlockSpec((None, d, TN_IN), lambda i, j: (layer, 0, j)),
        ],
        out_specs=pl.BlockSpec((TM_IN, TN_IN), lambda i, j: (i, j)),
        out_shape=jax.ShapeDtypeStruct((t, n), F32),
        compiler_params=pltpu.CompilerParams(
            dimension_semantics=("arbitrary", "arbitrary"), vmem_limit_bytes=V7X_VMEM_LIMIT),
        name="in_proj",
    )(hn, w_all)


W_LOAD_ROWS = 512


def _out_proj_kernel(a_ref, w_hbm, r_ref, o_ref, wb, stage, sem, *, layer):
    @pl.when(pl.program_id(0) == 0)
    def _():
        for c in range(wb.shape[0] // W_LOAD_ROWS):
            rows = pl.ds(c * W_LOAD_ROWS, W_LOAD_ROWS)
            cp = pltpu.make_async_copy(w_hbm.at[layer, rows], stage, sem)
            cp.start()
            cp.wait()
            wb[rows, :] = stage[...].astype(BF16)

    o_ref[...] = r_ref[...] + jnp.dot(a_ref[...], wb[...], preferred_element_type=F32)


def _out_proj_residual(a, w_all, layer, r):
    t, k = a.shape
    n = w_all.shape[2]
    return pl.pallas_call(
        functools.partial(_out_proj_kernel, layer=layer),
        grid=(t // TM_OUT,),
        in_specs=[
            pl.BlockSpec((TM_OUT, k), lambda i: (i, 0)),
            pl.BlockSpec(memory_space=pl.ANY),
            pl.BlockSpec((TM_OUT, n), lambda i: (i, 0)),
        ],
        out_specs=pl.BlockSpec((TM_OUT, n), lambda i: (i, 0)),
        out_shape=jax.ShapeDtypeStruct((t, n), F32),
        scratch_shapes=[
            pltpu.VMEM((k, n), BF16),
            pltpu.VMEM((W_LOAD_ROWS, n), F32),
            pltpu.SemaphoreType.DMA(()),
        ],
        compiler_params=pltpu.CompilerParams(
            dimension_semantics=("arbitrary",), vmem_limit_bytes=V7X_VMEM_LIMIT),
        name="out_proj_residual",
    )(a, w_all, r)


W13_TILES = 2


def _swiglu_tile(h, w1_ref, w3_ref, w2_ref, j):
    cols = pl.ds(pl.multiple_of((j % W13_TILES) * TF, TF), TF)
    a = jnp.dot(h, w1_ref[:, cols].astype(BF16), preferred_element_type=F32)
    b = jnp.dot(h, w3_ref[:, cols].astype(BF16), preferred_element_type=F32)
    u = (_silu(a) * b).astype(BF16)
    return jnp.dot(u, w2_ref[...].astype(BF16), preferred_element_type=F32)


def _ffn_kernel(x_hbm, g_ref, w1_ref, w3_ref, w2_ref, gn_ref, o_ref, hnext_ref, hn_ref, sem):
    @pl.when(pl.program_id(1) == 0)
    def _():
        r0 = pl.multiple_of(pl.program_id(0) * TM, TM)
        cp = pltpu.make_async_copy(x_hbm.at[pl.ds(r0, TM)], o_ref, sem)
        cp.start()
        cp.wait()
        _norm_block_to(o_ref, g_ref, hn_ref)

    o_ref[...] += _swiglu_tile(hn_ref[...], w1_ref, w3_ref, w2_ref, pl.program_id(1))

    @pl.when(pl.program_id(1) == pl.num_programs(1) - 1)
    def _():
        _norm_block_to(o_ref, gn_ref, hnext_ref)


def _ffn_dense(x, g, w1, w3, w2, g_next):
    t, d = x.shape
    f = w1.shape[1]
    return pl.pallas_call(
        _ffn_kernel,
        grid=(t // TM, f // TF),
        in_specs=[
            pl.BlockSpec(memory_space=pl.ANY),
            pl.BlockSpec((1, d), lambda i, j: (0, 0)),
            pl.BlockSpec((d, W13_TILES * TF), lambda i, j: (0, j // W13_TILES)),
            pl.BlockSpec((d, W13_TILES * TF), lambda i, j: (0, j // W13_TILES)),
            pl.BlockSpec((TF, d), lambda i, j: (j, 0)),
            pl.BlockSpec((1, d), lambda i, j: (0, 0)),
        ],
        out_specs=[
            pl.BlockSpec((TM, d), lambda i, j: (i, 0)),
            pl.BlockSpec((TM, d), lambda i, j: (i, 0)),
        ],
        out_shape=[jax.ShapeDtypeStruct((t, d), F32), jax.ShapeDtypeStruct((t, d), BF16)],
        scratch_shapes=[pltpu.VMEM((TM, d), BF16), pltpu.SemaphoreType.DMA(())],
        compiler_params=pltpu.CompilerParams(
            dimension_semantics=("arbitrary", "arbitrary"), vmem_limit_bytes=V7X_VMEM_LIMIT),
        name="ffn_dense",
    )(x, g.reshape(1, d), w1, w3, w2, g_next.reshape(1, d))


def _rotary_tables(length, pos0):
    half = RET_D // 2
    inv = ROPE_THETA ** (-jnp.arange(half, dtype=F32) / half)
    pos = jnp.arange(length, dtype=F32) + pos0
    ang = pos[:, None] * inv[None, :]
    cos, sin = jnp.cos(ang), jnp.sin(ang)
    return jnp.concatenate([cos, cos], axis=-1), jnp.concatenate([-sin, sin], axis=-1)


def _decay_tables(c):
    log_g = jnp.log1p(-jnp.exp2(-5.0 - jnp.arange(RET_HEADS, dtype=F32)))
    idx = jnp.arange(c, dtype=F32)
    rel = idx[:, None] - idx[None, :]
    d_in = jnp.where(rel >= 0, jnp.exp(log_g[:, None, None] * jnp.maximum(rel, 0.0)), 0.0)
    d_q = jnp.exp(log_g[:, None] * (idx + 1.0)[None, :])
    d_k = jnp.exp(log_g[:, None] * (c - 1.0 - idx)[None, :])
    d_c = jnp.exp(log_g * c)
    return d_in, d_q, d_k, d_c


HIST31 = 32
HIST3 = 8


def _mixer_prompt_kernel(z_ref, cos_ref, sin_ref, din_ref, dq_ref, dk_ref, dc_ref, retg_ref,
                         c31w_ref, c31b_ref, clng_ref, clnb_ref, scw_ref, slng_ref, slnb_ref,
                         sguw_ref, sgub_ref, mix_in_ref,
                         mix_ref, ret_ref, b31_ref, b3_ref,
                         s_scr, ext31_scr, ext3_scr, ysh_scr):
    del mix_in_ref
    c = pl.program_id(1)

    @pl.when(c == 0)
    def _():
        s_scr[...] = jnp.zeros_like(s_scr)
        ext31_scr[pl.ds(0, HIST31), :] = jnp.zeros((HIST31, GROUP_W), F32)
        ext3_scr[pl.ds(0, HIST3), :] = jnp.zeros((HIST3, GROUP_W), F32)

    cos = cos_ref[...]
    sin = sin_ref[...]

    for h in range(RET_HEADS):
        lo = h * RET_D
        q = z_ref[:, pl.ds(C_Q + lo, RET_D)]
        k = z_ref[:, pl.ds(C_K + lo, RET_D)]
        v = z_ref[:, pl.ds(C_V + lo, RET_D)]
        g = z_ref[:, pl.ds(C_G + lo, RET_D)]
        q = q * cos + pltpu.roll(q, RET_D // 2, axis=1) * sin
        k = (k * cos + pltpu.roll(k, RET_D // 2, axis=1) * sin) * (RET_D ** -0.5)
        qb = q.astype(BF16)
        vb = v.astype(BF16)
        scores = lax.dot_general(qb, k.astype(BF16), (((1,), (1,)), ((), ())),
                                 preferred_element_type=F32) * din_ref[h]
        inner = jnp.dot(scores.astype(BF16), vb, preferred_element_type=F32)
        s_old = s_scr[h]
        cross = jnp.dot(qb, s_old.astype(BF16), preferred_element_type=F32) * dq_ref[h]
        kd = (k * dk_ref[h]).astype(BF16)
        s_scr[h] = s_old * dc_ref[h] + lax.dot_general(kd, vb, (((0,), (0,)), ((), ())),
                                                       preferred_element_type=F32)
        r = inner + cross
        r = r * lax.rsqrt(jnp.mean(r * r, axis=-1, keepdims=True) + EPS)
        r = r * retg_ref[:, pl.ds(lo, RET_D)]
        mix_ref[:, pl.ds(lo, RET_D)] = (_silu(g) * r).astype(mix_ref.dtype)

    glu = z_ref[:, pl.ds(C_BL, GROUP_W)] * jax.nn.sigmoid(z_ref[:, pl.ds(C_BG, GROUP_W)])
    ext31_scr[pl.ds(HIST31, CHUNK), :] = glu
    s0 = HIST31 - (CONV_W - 1)
    conv = None
    for i in range(SUBLANES):
        n_rows = CHUNK if i == 0 else CHUNK + SUBLANES
        y = None
        for m in range(HIST31 // SUBLANES + 1):
            j = SUBLANES * m + i - s0
            if 0 <= j < CONV_W:
                term = ext31_scr[pl.ds(SUBLANES * m, n_rows), :] * c31w_ref[pl.ds(j, 1), :]
                y = term if y is None else y + term
        if i == 0:
            conv = y
        else:
            ysh_scr[i - 1] = y
            conv = conv + ysh_scr[i - 1, pl.ds(i, CHUNK), :]
    conv = conv + c31b_ref[...]
    out_b = _silu(_layernorm_rows(conv, clng_ref[...], clnb_ref[...]))
    mix_ref[:, pl.ds(GROUP_W, GROUP_W)] = out_b.astype(mix_ref.dtype)
    ext31_scr[pl.ds(0, HIST31), :] = ext31_scr[pl.ds(CHUNK, HIST31), :]

    gated = z_ref[:, pl.ds(C_CC, GROUP_W)] * z_ref[:, pl.ds(C_CH, GROUP_W)]
    ext3_scr[pl.ds(HIST3, CHUNK), :] = gated
    conv3 = ext3_scr[pl.ds(HIST3 - (SC_W - 1), CHUNK), :] * scw_ref[pl.ds(0, 1), :]
    for j in range(1, SC_W):
        conv3 = conv3 + ext3_scr[pl.ds(HIST3 - (SC_W - 1) + j, CHUNK), :] * scw_ref[pl.ds(j, 1), :]
    mix_ref[:, pl.ds(2 * GROUP_W, GROUP_W)] = (z_ref[:, pl.ds(C_CB, GROUP_W)] * conv3).astype(mix_ref.dtype)
    ext3_scr[pl.ds(0, HIST3), :] = ext3_scr[pl.ds(CHUNK, HIST3), :]

    vn = _layernorm_rows(z_ref[:, pl.ds(C_DV, GROUP_W)], slng_ref[...], slnb_ref[...])
    row = lax.broadcasted_iota(jnp.int32, (CHUNK, CHUNK), 0)
    col = lax.broadcasted_iota(jnp.int32, (CHUNK, CHUNK), 1)
    ch = GROUP_W // SGU_GROUPS
    for gi in range(SGU_GROUPS):
        wm = jnp.where(row >= col, sguw_ref[gi], 0.0).astype(BF16)
        mixed = jnp.dot(wm, vn[:, gi * ch:(gi + 1) * ch].astype(BF16), preferred_element_type=F32)
        mixed = mixed + sgub_ref[:, pl.ds(gi, 1)]
        du = z_ref[:, pl.ds(C_DU + gi * ch, ch)]
        mix_ref[:, pl.ds(3 * GROUP_W + gi * ch, ch)] = (du * mixed).astype(mix_ref.dtype)

    @pl.when(c == pl.num_programs(1) - 1)
    def _():
        ret_ref[...] = s_scr[...]
        b31_ref[...] = ext31_scr[pl.ds(HIST31 - (CONV_W - 1), CONV_W - 1), :]
        b3_ref[...] = ext3_scr[pl.ds(HIST3 - (SC_W - 1), SC_W - 1), :]


def _mixer_prompt(z, p, mix_init):
    cos, sin = _rotary_tables(SEQ, 0)
    d_in, d_q, d_k, d_c = _decay_tables(CHUNK)
    dq_t = jnp.broadcast_to(d_q[:, :, None], (RET_HEADS, CHUNK, RET_D))
    dk_t = jnp.broadcast_to(d_k[:, :, None], (RET_HEADS, CHUNK, RET_D))
    dc_t = jnp.broadcast_to(d_c[:, None, None], (RET_HEADS, RET_D, RET_D))

    def full(shape):
        return pl.BlockSpec(shape, lambda b, c: (0,) * len(shape))

    row = lambda a: a.reshape(1, -1)
    return pl.pallas_call(
        _mixer_prompt_kernel,
        grid=(BATCH, N_CHUNKS),
        in_specs=[
            pl.BlockSpec((CHUNK, D_IN), lambda b, c: (b * N_CHUNKS + c, 0)),
            pl.BlockSpec((CHUNK, RET_D), lambda b, c: (c, 0)),
            pl.BlockSpec((CHUNK, RET_D), lambda b, c: (c, 0)),
            full((RET_HEADS, CHUNK, CHUNK)),
            full((RET_HEADS, CHUNK, RET_D)),
            full((RET_HEADS, CHUNK, RET_D)),
            full((RET_HEADS, RET_D, RET_D)),
            full((1, GROUP_W)),
            full((CONV_W, GROUP_W)),
            full((1, GROUP_W)),
            full((1, GROUP_W)),
            full((1, GROUP_W)),
            full((SC_W, GROUP_W)),
            full((1, GROUP_W)),
            full((1, GROUP_W)),
            full((SGU_GROUPS, CHUNK, CHUNK)),
            full((CHUNK, SGU_GROUPS)),
            pl.BlockSpec(memory_space=pl.ANY),
        ],
        input_output_aliases={17: 0},
        out_specs=[
            pl.BlockSpec((CHUNK, D_MODEL), lambda b, c: (b * N_CHUNKS + c, 0)),
            pl.BlockSpec((None, RET_HEADS, RET_D, RET_D), lambda b, c: (b, 0, 0, 0)),
            pl.BlockSpec((None, CONV_W - 1, GROUP_W), lambda b, c: (b, 0, 0)),
            pl.BlockSpec((None, SC_W - 1, GROUP_W), lambda b, c: (b, 0, 0)),
        ],
        out_shape=[
            jax.ShapeDtypeStruct((T_ALL, D_MODEL), BF16),
            jax.ShapeDtypeStruct((BATCH, RET_HEADS, RET_D, RET_D), F32),
            jax.ShapeDtypeStruct((BATCH, CONV_W - 1, GROUP_W), F32),
            jax.ShapeDtypeStruct((BATCH, SC_W - 1, GROUP_W), F32),
        ],
        scratch_shapes=[
            pltpu.VMEM((RET_HEADS, RET_D, RET_D), F32),
            pltpu.VMEM((HIST31 + CHUNK, GROUP_W), F32),
            pltpu.VMEM((HIST3 + CHUNK, GROUP_W), F32),
            pltpu.VMEM((SUBLANES - 1, CHUNK + SUBLANES, GROUP_W), F32),
        ],
        compiler_params=pltpu.CompilerParams(
            dimension_semantics=("arbitrary", "arbitrary"), vmem_limit_bytes=V7X_VMEM_LIMIT),
        name="mixer_prompt",
    )(z, cos, sin, d_in, dq_t, dk_t, dc_t, row(p["ret_norm_g"]), p["conv31_w"], row(p["conv31_b"]),
      row(p["conv_ln_g"]), row(p["conv_ln_b"]), p["sconv_w"], row(p["sgu_ln_g"]), row(p["sgu_ln_b"]),
      p["sgu_w"], p["sgu_b"].T, mix_init)


EXT31_S = 40
EXT3_S = 8


def _mixer_sample_kernel(z_ref, cos_ref, sin_ref, din_ref, dq_ref, dk_ref, dc_ref, retg_ref,
                         c31w_ref, c31b_ref, clng_ref, clnb_ref, scw_ref, slng_ref, slnb_ref,
                         sguw_ref, sgub_ref, sret_ref, s31_ref, s3_ref,
                         mix_ref, ret_ref, b31_ref, b3_ref, vn_ref,
                         ext31_scr, ext3_scr):
    cos = cos_ref[...]
    sin = sin_ref[...]
    n_hist31 = CONV_W - 1
    n_hist3 = SC_W - 1

    for h in range(RET_HEADS):
        lo = h * RET_D
        q = z_ref[:, :, pl.ds(C_Q + lo, RET_D)]
        k = z_ref[:, :, pl.ds(C_K + lo, RET_D)]
        g = z_ref[:, :, pl.ds(C_G + lo, RET_D)]
        q = q * cos + pltpu.roll(q, RET_D // 2, axis=2) * sin
        k = (k * cos + pltpu.roll(k, RET_D // 2, axis=2) * sin) * (RET_D ** -0.5)
        s_old = sret_ref[:, h]
        r = lax.dot_general(q, s_old, (((2,), (1,)), ((0,), (0,))),
                            preferred_element_type=F32) * dq_ref[h]
        for s in range(DEC_SEQ):
            k_s = z_ref[:, pl.ds(s, 1), pl.ds(C_K + lo, RET_D)]
            k_s = (k_s * cos[s:s + 1] + pltpu.roll(k_s, RET_D // 2, axis=2) * sin[s:s + 1]) * (RET_D ** -0.5)
            v_s = z_ref[:, pl.ds(s, 1), pl.ds(C_V + lo, RET_D)]
            score = jnp.sum(q * k_s, axis=-1, keepdims=True) * din_ref[h, s]
            r = r + score * v_s
        kd = k * dk_ref[h]
        v = z_ref[:, :, pl.ds(C_V + lo, RET_D)]
        ret_ref[:, h] = s_old * dc_ref[h] + lax.dot_general(
            kd, v, (((1,), (1,)), ((0,), (0,))), preferred_element_type=F32)
        r = r * lax.rsqrt(jnp.mean(r * r, axis=-1, keepdims=True) + EPS)
        r = r * retg_ref[:, pl.ds(lo, RET_D)]
        mix_ref[:, :, pl.ds(lo, RET_D)] = _silu(g) * r

    glu = z_ref[:, :, pl.ds(C_BL, GROUP_W)] * jax.nn.sigmoid(z_ref[:, :, pl.ds(C_BG, GROUP_W)])
    ext31_scr[:, pl.ds(0, n_hist31), :] = s31_ref[...]
    ext31_scr[:, pl.ds(n_hist31, DEC_SEQ), :] = glu
    conv = ext31_scr[:, pl.ds(0, DEC_SEQ), :] * c31w_ref[pl.ds(0, 1), :]
    for j in range(1, CONV_W):
        conv = conv + ext31_scr[:, pl.ds(j, DEC_SEQ), :] * c31w_ref[pl.ds(j, 1), :]
    conv = conv + c31b_ref[...]
    mix_ref[:, :, pl.ds(GROUP_W, GROUP_W)] = _silu(_layernorm_rows(conv, clng_ref[...], clnb_ref[...]))
    b31_ref[...] = ext31_scr[:, pl.ds(DEC_SEQ, n_hist31), :]

    gated = z_ref[:, :, pl.ds(C_CC, GROUP_W)] * z_ref[:, :, pl.ds(C_CH, GROUP_W)]
    ext3_scr[:, pl.ds(0, n_hist3), :] = s3_ref[...]
    ext3_scr[:, pl.ds(n_hist3, DEC_SEQ), :] = gated
    conv3 = ext3_scr[:, pl.ds(0, DEC_SEQ), :] * scw_ref[pl.ds(0, 1), :]
    for j in range(1, SC_W):
        conv3 = conv3 + ext3_scr[:, pl.ds(j, DEC_SEQ), :] * scw_ref[pl.ds(j, 1), :]
    mix_ref[:, :, pl.ds(2 * GROUP_W, GROUP_W)] = z_ref[:, :, pl.ds(C_CB, GROUP_W)] * conv3
    b3_ref[...] = ext3_scr[:, pl.ds(DEC_SEQ, n_hist3), :]

    vn = _layernorm_rows(z_ref[:, :, pl.ds(C_DV, GROUP_W)], slng_ref[...], slnb_ref[...])
    vn_ref[...] = vn
    mixed = sgub_ref[...]
    for s in range(DEC_SEQ):
        mixed = mixed + sguw_ref[s] * vn_ref[:, pl.ds(s, 1), :]
    mix_ref[:, :, pl.ds(3 * GROUP_W, GROUP_W)] = z_ref[:, :, pl.ds(C_DU, GROUP_W)] * mixed


def _mixer_sample(z_s, p, layer, s_ret, s_31, s_3):
    cos, sin = _rotary_tables(DEC_SEQ, PAST_LEN)
    d_in, d_q, d_k, d_c = _decay_tables(DEC_SEQ)
    din_t = jnp.swapaxes(d_in, 1, 2)[:, :, :, None]
    dq_t = jnp.broadcast_to(d_q[:, :, None], (RET_HEADS, DEC_SEQ, RET_D))
    dk_t = jnp.broadcast_to(d_k[:, :, None], (RET_HEADS, DEC_SEQ, RET_D))
    dc_t = jnp.broadcast_to(d_c[:, None, None], (RET_HEADS, RET_D, RET_D))
    ch = GROUP_W // SGU_GROUPS
    w4 = jnp.tril(p["sgu_w"][:, :DEC_SEQ, :DEC_SEQ])
    w_t = jnp.repeat(jnp.transpose(w4, (2, 1, 0)), ch, axis=-1)
    b_t = jnp.repeat(p["sgu_b"][:, :DEC_SEQ].T, ch, axis=-1)
    bt = SAMPLE_BT

    def full(shape):
        return pl.BlockSpec(shape, lambda i: (0,) * len(shape))

    row = lambda a: a.reshape(1, -1)
    return pl.pallas_call(
        _mixer_sample_kernel,
        grid=(DEC_BATCH // bt,),
        in_specs=[
            pl.BlockSpec((bt, DEC_SEQ, D_IN), lambda i: (i, 0, 0)),
            full((DEC_SEQ, RET_D)),
            full((DEC_SEQ, RET_D)),
            full((RET_HEADS, DEC_SEQ, DEC_SEQ, 1)),
            full((RET_HEADS, DEC_SEQ, RET_D)),
            full((RET_HEADS, DEC_SEQ, RET_D)),
            full((RET_HEADS, RET_D, RET_D)),
            full((1, GROUP_W)),
            full((CONV_W, GROUP_W)),
            full((1, GROUP_W)),
            full((1, GROUP_W)),
            full((1, GROUP_W)),
            full((SC_W, GROUP_W)),
            full((1, GROUP_W)),
            full((1, GROUP_W)),
            full((DEC_SEQ, DEC_SEQ, GROUP_W)),
            full((DEC_SEQ, GROUP_W)),
            pl.BlockSpec((None, bt, RET_HEADS, RET_D, RET_D), lambda i: (layer, i, 0, 0, 0)),
            pl.BlockSpec((None, bt, CONV_W - 1, GROUP_W), lambda i: (layer, i, 0, 0)),
            pl.BlockSpec((None, bt, SC_W - 1, GROUP_W), lambda i: (layer, i, 0, 0)),
        ],
        out_specs=[
            pl.BlockSpec((bt, DEC_SEQ, D_MODEL), lambda i: (i, 0, 0)),
            pl.BlockSpec((bt, RET_HEADS, RET_D, RET_D), lambda i: (i, 0, 0, 0)),
            pl.BlockSpec((bt, CONV_W - 1, GROUP_W), lambda i: (i, 0, 0)),
            pl.BlockSpec((bt, SC_W - 1, GROUP_W), lambda i: (i, 0, 0)),
            pl.BlockSpec((bt, DEC_SEQ, GROUP_W), lambda i: (i, 0, 0)),
        ],
        out_shape=[
            jax.ShapeDtypeStruct((DEC_BATCH, DEC_SEQ, D_MODEL), F32),
            jax.ShapeDtypeStruct((DEC_BATCH, RET_HEADS, RET_D, RET_D), F32),
            jax.ShapeDtypeStruct((DEC_BATCH, CONV_W - 1, GROUP_W), F32),
            jax.ShapeDtypeStruct((DEC_BATCH, SC_W - 1, GROUP_W), F32),
            jax.ShapeDtypeStruct((DEC_BATCH, DEC_SEQ, GROUP_W), F32),
        ],
        scratch_shapes=[
            pltpu.VMEM((bt, EXT31_S, GROUP_W), F32),
            pltpu.VMEM((bt, EXT3_S, GROUP_W), F32),
        ],
        compiler_params=pltpu.CompilerParams(
            dimension_semantics=("arbitrary",), vmem_limit_bytes=V7X_VMEM_LIMIT),
        name="mixer_sample",
    )(z_s, cos, sin, din_t, dq_t, dk_t, dc_t, row(p["ret_norm_g"]), p["conv31_w"], row(p["conv31_b"]),
      row(p["conv_ln_g"]), row(p["conv_ln_b"]), p["sconv_w"], row(p["sgu_ln_g"]), row(p["sgu_ln_b"]),
      w_t, b_t, s_ret, s_31, s_3)


def _to_token_major(dst, val, pitch=TOK_ROWS):
    n = val.shape[0]
    for a in range(TOK_ROWS):
        dst[pl.ds(a, n, stride=pitch), :] = val[:, a * LANES:(a + 1) * LANES]


STAGE_PITCH = 24
DMA_UNROLL = 8


def _router_kernel(x_ref, g_ref, rw_ref, e_ref, p_ref, c_ref):
    h = _rms_rows(x_ref[...], g_ref[...])
    lane = lax.broadcasted_iota(jnp.int32, (h.shape[0], N_EXPERTS), 1)
    logits = jnp.zeros((h.shape[0], N_EXPERTS), F32)
    for e in range(N_EXPERTS):
        col = jnp.sum(h * rw_ref[pl.ds(e, 1), :], axis=-1, keepdims=True)
        logits = jnp.where(lane == e, col, logits)
    m1 = jnp.max(logits, axis=-1, keepdims=True)
    i1 = jnp.min(jnp.where(logits == m1, lane, N_EXPERTS), axis=-1, keepdims=True)
    rest = jnp.where(lane == i1, -jnp.inf, logits)
    m2 = jnp.max(rest, axis=-1, keepdims=True)
    i2 = jnp.min(jnp.where(rest == m2, lane, N_EXPERTS), axis=-1, keepdims=True)
    e2 = jnp.exp(m2 - m1)
    den = 1.0 + e2
    e_ref[...] = jnp.where(lane == 0, i1, jnp.where(lane == 1, i2, 0))
    p_ref[...] = jnp.where(lane == 0, 1.0 / den, jnp.where(lane == 1, e2 / den, 0.0))
    c_ref[pl.ds(0, 1), :] = jnp.sum(jnp.where(lane == i1, 1.0, 0.0), axis=0, keepdims=True).astype(jnp.int32)
    c_ref[pl.ds(1, 1), :] = jnp.sum(jnp.where(lane == i2, 1.0, 0.0), axis=0, keepdims=True).astype(jnp.int32)


def _router(x, g, rw):
    t, d = x.shape
    return pl.pallas_call(
        _router_kernel,
        grid=(N_TB,),
        in_specs=[
            pl.BlockSpec((MOE_TB, d), lambda i: (i, 0)),
            pl.BlockSpec((1, d), lambda i: (0, 0)),
            pl.BlockSpec((N_EXPERTS, d), lambda i: (0, 0)),
        ],
        out_specs=[
            pl.BlockSpec((MOE_TB, N_EXPERTS), lambda i: (i, 0)),
            pl.BlockSpec((MOE_TB, N_EXPERTS), lambda i: (i, 0)),
            pl.BlockSpec((None, TOP_K, N_EXPERTS), lambda i: (i, 0, 0)),
        ],
        out_shape=[
            jax.ShapeDtypeStruct((t, N_EXPERTS), jnp.int32),
            jax.ShapeDtypeStruct((t, N_EXPERTS), F32),
            jax.ShapeDtypeStruct((N_TB, TOP_K, N_EXPERTS), jnp.int32),
        ],
        compiler_params=pltpu.CompilerParams(
            dimension_semantics=("arbitrary",), vmem_limit_bytes=V7X_VMEM_LIMIT),
        name="moe_router",
    )(x, g.reshape(1, d), rw.T)


def _pick(table, idx):
    lanes = jnp.arange(N_EXPERTS, dtype=jnp.int32)
    return jnp.sum(jnp.where(idx[..., None] == lanes, table, 0), axis=-1)


def _expert_layout(cnt):
    tot = jnp.sum(cnt, axis=(0, 1))
    start = jnp.cumsum(tot) - tot
    blk = jnp.sum(cnt, axis=1)
    blk_base = start[None, :] + jnp.cumsum(blk, axis=0) - blk
    base = jnp.stack([blk_base, blk_base + cnt[:, 0]], axis=1)
    n_sb = (tot + SB - 1) // SB
    sb_end = jnp.cumsum(n_sb)
    n_used = sb_end[-1]
    s_idx = jnp.arange(NS_MAX, dtype=jnp.int32)
    s_cl = jnp.minimum(s_idx, n_used - 1)
    sb_e = jnp.sum(s_cl[:, None] >= sb_end[None, :], axis=-1).astype(jnp.int32)
    local = s_cl - _pick(sb_end - n_sb, sb_e)
    sb_row0 = _pick(start, sb_e) + local * SB
    sb_rows = jnp.where(s_idx < n_used, jnp.clip(_pick(tot, sb_e) - local * SB, 0, SB), 0)
    i32 = lambda a: a.astype(jnp.int32)
    return i32(base), sb_e, i32(sb_row0), i32(sb_rows), i32(n_used).reshape(1)


def _dest_kernel(e_ref, b_ref, d_ref):
    eid = e_ref[...]
    lane = lax.broadcasted_iota(jnp.int32, eid.shape, 1)
    row = lax.broadcasted_iota(jnp.int32, (MOE_TB, MOE_TB), 0)
    col = lax.broadcasted_iota(jnp.int32, (MOE_TB, MOE_TB), 1)
    earlier = jnp.where(row > col, 1.0, 0.0).astype(BF16)
    base = b_ref[...].astype(F32)

    def dest_of(k):
        mine = lane == eid[:, k:k + 1]
        before = jnp.dot(earlier, jnp.where(mine, 1.0, 0.0).astype(BF16), preferred_element_type=F32)
        return jnp.sum(jnp.where(mine, before + base[k:k + 1], 0.0), axis=-1, keepdims=True)

    d_ref[...] = jnp.where(lane == 0, dest_of(0), jnp.where(lane == 1, dest_of(1), 0.0)).astype(jnp.int32)


def _dest_rows(eid, base):
    return pl.pallas_call(
        _dest_kernel,
        grid=(N_TB,),
        in_specs=[
            pl.BlockSpec((MOE_TB, N_EXPERTS), lambda i: (i, 0)),
            pl.BlockSpec((None, TOP_K, N_EXPERTS), lambda i: (i, 0, 0)),
        ],
        out_specs=pl.BlockSpec((MOE_TB, N_EXPERTS), lambda i: (i, 0)),
        out_shape=jax.ShapeDtypeStruct((T_ALL, N_EXPERTS), jnp.int32),
        compiler_params=pltpu.CompilerParams(
            dimension_semantics=("arbitrary",), vmem_limit_bytes=V7X_VMEM_LIMIT),
        name="moe_dest_rows",
    )(eid, base)


def _dispatch_kernel(d_ref, x_ref, g_ref, xg_hbm, stage, zeros, sem, zsem):
    i = pl.program_id(0)
    n = pl.num_programs(0)
    slot = i % 2
    blk_rows = MOE_TB * TOK_ROWS

    def wait_slot(sl):
        for _ in range(TOP_K):
            pltpu.make_async_copy(stage.at[sl, pl.ds(0, blk_rows)], xg_hbm.at[pl.ds(0, blk_rows)], sem.at[sl]).wait()

    @pl.when(i >= 2)
    def _():
        wait_slot(slot)

    _to_token_major(stage.at[slot], _rms_rows(x_ref[...], g_ref[...]), STAGE_PITCH)

    def body(r, carry):
        src = stage.at[slot, pl.ds(pl.multiple_of(r * STAGE_PITCH, 8), TOK_ROWS)]
        for k in range(TOP_K):
            row = d_ref[0, 0, TOP_K * r + k]
            dst = xg_hbm.at[pl.ds(pl.multiple_of(row * TOK_ROWS, TOK_ROWS), TOK_ROWS)]
            pltpu.make_async_copy(src, dst, sem.at[slot]).start()
        return carry

    lax.fori_loop(0, MOE_TB, body, 0, unroll=DMA_UNROLL)

    @pl.when(i == n - 1)
    def _():
        zeros[...] = jnp.zeros_like(zeros)
        tail = [pltpu.make_async_copy(zeros, xg_hbm.at[pl.ds((N_SLOTS + q * (SUB // TOK_ROWS)) * TOK_ROWS, SUB)], zsem)
                for q in range(TOK_ROWS)]
        for cp in tail:
            cp.start()
        for cp in tail:
            cp.wait()
        wait_slot(slot)
        wait_slot(1 - slot)


def _dispatch(x, g, dest_flat):
    t, d = x.shape
    return pl.pallas_call(
        _dispatch_kernel,
        grid=(N_TB,),
        in_specs=[
            pl.BlockSpec((1, 1, TOP_K * MOE_TB), lambda i: (i, 0, 0), memory_space=pltpu.SMEM),
            pl.BlockSpec((MOE_TB, d), lambda i: (i, 0)),
            pl.BlockSpec((1, d), lambda i: (0, 0)),
        ],
        out_specs=pl.BlockSpec(memory_space=pl.ANY),
        out_shape=jax.ShapeDtypeStruct(((N_SLOTS + SUB) * TOK_ROWS, LANES), F32),
        scratch_shapes=[
            pltpu.VMEM((2, MOE_TB * STAGE_PITCH, LANES), F32),
            pltpu.VMEM((SUB, LANES), F32),
            pltpu.SemaphoreType.DMA((2,)),
            pltpu.SemaphoreType.DMA(()),
        ],
        compiler_params=pltpu.CompilerParams(
            dimension_semantics=("arbitrary",), vmem_limit_bytes=V7X_VMEM_LIMIT),
        name="moe_dispatch",
    )(dest_flat, x, g.reshape(1, d))


def _moe_ffn_kernel(se_ref, r0_ref, nr_ref, nu_ref, xg_hbm, w1_ref, w3_ref, w2_ref, y_hbm,
                    x_scr, acc, stage, w1b, w3b, w2b, sem_in, sem_out):
    s = pl.program_id(0)
    j = pl.program_id(1)
    nj = pl.num_programs(1)
    rows = nr_ref[s]
    row0 = r0_ref[s]
    nsub = lax.shift_right_logical(rows + (SUB - 1), int(math.log2(SUB)))
    sub_rows = SUB * TOK_ROWS

    is_middle = (j > 0) & (j < nj - 1)

    def cast_weights():
        ws = (w1_ref[...].astype(BF16), w3_ref[...].astype(BF16), w2_ref[...].astype(BF16))
        w1b[...], w3b[...], w2b[...] = ws
        return ws

    @pl.when((nsub > 0) & jnp.logical_not(is_middle))
    def _():
        cast_weights()

    def sub(r):
        return pl.ds(pl.multiple_of(r * SUB, SUB), SUB)

    def swiglu(xs, ws=None):
        w1, w3, w2 = ws if ws is not None else (w1b[...], w3b[...], w2b[...])
        a = jnp.dot(xs, w1, preferred_element_type=F32)
        b = jnp.dot(xs, w3, preferred_element_type=F32)
        u = (_silu(a) * b).astype(BF16)
        return jnp.dot(u, w2, preferred_element_type=F32)

    def tile(r):
        return swiglu(x_scr[sub(r), :])

    def in_copy(r, slot):
        first = pl.multiple_of((row0 + r * SUB) * TOK_ROWS, TOK_ROWS)
        return pltpu.make_async_copy(xg_hbm.at[pl.ds(first, sub_rows)], stage.at[slot], sem_in.at[slot])

    @pl.when(j == 0)
    def _():
        @pl.when(nsub > 0)
        def _():
            in_copy(0, 0).start()

        def body(r, carry):
            slot = r % 2
            in_copy(r, slot).wait()

            @pl.when(r + 1 < nsub)
            def _():
                in_copy(r + 1, 1 - slot).start()

            for a in range(TOK_ROWS):
                x_scr[sub(r), pl.ds(a * LANES, LANES)] = stage[slot, pl.ds(a, SUB, stride=TOK_ROWS), :].astype(BF16)
            acc[sub(r), :] = tile(r)
            return carry

        lax.fori_loop(0, nsub, body, 0)

        def clear(r, carry):
            x_scr[sub(r), :] = jnp.zeros((SUB, D_MODEL), BF16)
            acc[sub(r), :] = jnp.zeros((SUB, D_MODEL), F32)
            return carry

        lax.fori_loop(nsub, SB // SUB, clear, 0)

    @pl.when(is_middle)
    def _():
        for c in range(SB // CHUNK_ROWS):
            @pl.when(rows > c * CHUNK_ROWS)
            def _():
                ws = cast_weights() if c == 0 else None
                rows_c = pl.ds(c * CHUNK_ROWS, CHUNK_ROWS)
                acc[rows_c, :] += swiglu(x_scr[rows_c, :], ws)

    @pl.when(j == nj - 1)
    def _():
        def out_copies(r, act):
            nv = jnp.clip(rows - r * SUB, 0, SUB)
            bit = SUB
            while bit >= 1:
                off = nv & (-2 * bit)
                src = stage.at[0, pl.ds(pl.multiple_of(off * TOK_ROWS, TOK_ROWS), bit * TOK_ROWS)]
                first = pl.multiple_of((row0 + r * SUB + off) * TOK_ROWS, TOK_ROWS)
                cp = pltpu.make_async_copy(src, y_hbm.at[pl.ds(first, bit * TOK_ROWS)], sem_out)
                pl.when((nv & bit) != 0)(functools.partial(act, cp))
                bit //= 2

        def body(r, carry):
            y = acc[sub(r), :] + tile(r)

            @pl.when(r > 0)
            def _():
                out_copies(r - 1, lambda cp: cp.wait())

            _to_token_major(stage.at[0], y)
            out_copies(r, lambda cp: cp.start())
            return carry

        lax.fori_loop(0, nsub, body, 0)

        @pl.when(nsub > 0)
        def _():
            out_copies(nsub - 1, lambda cp: cp.wait())


def _moe_ffn(xg, w1, w3, w2, sb_e, sb_row0, sb_rows, n_used):
    d = w1.shape[1]
    f = w1.shape[2]
    nj = f // TF

    def jmap(s, j, nu):
        return jnp.where(s < nu[0], j, nj - 1)

    return pl.pallas_call(
        _moe_ffn_kernel,
        grid_spec=pltpu.PrefetchScalarGridSpec(
            num_scalar_prefetch=4,
            grid=(NS_MAX, nj),
            in_specs=[
                pl.BlockSpec(memory_space=pl.ANY),
                pl.BlockSpec((None, d, TF), lambda s, j, se, r0, nr, nu: (se[s], 0, jmap(s, j, nu))),
                pl.BlockSpec((None, d, TF), lambda s, j, se, r0, nr, nu: (se[s], 0, jmap(s, j, nu))),
                pl.BlockSpec((None, TF, d), lambda s, j, se, r0, nr, nu: (se[s], jmap(s, j, nu), 0)),
            ],
            out_specs=pl.BlockSpec(memory_space=pl.ANY),
            scratch_shapes=[
                pltpu.VMEM((SB, d), BF16),
                pltpu.VMEM((SB, d), F32),
                pltpu.VMEM((2, SUB * TOK_ROWS, LANES), F32),
                pltpu.VMEM((d, TF), BF16),
                pltpu.VMEM((d, TF), BF16),
                pltpu.VMEM((TF, d), BF16),
                pltpu.SemaphoreType.DMA((2,)),
                pltpu.SemaphoreType.DMA(()),
            ],
        ),
        out_shape=jax.ShapeDtypeStruct((N_SLOTS * TOK_ROWS, LANES), F32),
        compiler_params=pltpu.CompilerParams(
            dimension_semantics=("arbitrary", "arbitrary"), vmem_limit_bytes=V7X_VMEM_LIMIT),
        name="moe_ffn",
    )(sb_e, sb_row0, sb_rows, n_used, xg, w1, w3, w2)


N_TB_PROMPT = T_PROMPT // MOE_TB


def _combine_kernel(d_ref, dn_ref, x_ref, p_ref, g_ref, y_hbm, op_ref, os_ref, stage, sem, *, final_norm):
    i = pl.program_id(0)
    n = pl.num_programs(0)
    slot = i % 2
    blk_rows = MOE_TB * TOK_ROWS

    def start_block(dref, sl):
        def body(r, carry):
            for k in range(TOP_K):
                row = dref[0, 0, TOP_K * r + k]
                src = y_hbm.at[pl.ds(pl.multiple_of(row * TOK_ROWS, TOK_ROWS), TOK_ROWS)]
                dst = stage.at[sl, k, pl.ds(pl.multiple_of(r * STAGE_PITCH, 8), TOK_ROWS)]
                pltpu.make_async_copy(src, dst, sem.at[sl]).start()
            return carry

        lax.fori_loop(0, MOE_TB, body, 0, unroll=DMA_UNROLL)

    @pl.when(i == 0)
    def _():
        start_block(d_ref, 0)

    @pl.when(i + 1 < n)
    def _():
        start_block(dn_ref, 1 - slot)

    for k in range(TOP_K):
        pltpu.make_async_copy(y_hbm.at[pl.ds(0, blk_rows)], stage.at[slot, k, pl.ds(0, blk_rows)], sem.at[slot]).wait()
    g0 = p_ref[:, 0:1]
    g1 = p_ref[:, 1:2]
    pieces = []
    for a in range(TOK_ROWS):
        y0 = stage[slot, 0, pl.ds(a, MOE_TB, stride=STAGE_PITCH), :]
        y1 = stage[slot, 1, pl.ds(a, MOE_TB, stride=STAGE_PITCH), :]
        pieces.append(x_ref[:, pl.ds(a * LANES, LANES)] + (y0 * g0 + y1 * g1))
    y = jnp.concatenate(pieces, axis=1)
    if final_norm:
        y = _rms_rows(y, g_ref[...])

    @pl.when(i < N_TB_PROMPT)
    def _():
        op_ref[...] = y

    @pl.when(i >= N_TB_PROMPT)
    def _():
        os_ref[...] = y


def _moe_combine(x, y, dest_flat, gate, g, final_norm):
    t, d = x.shape
    return pl.pallas_call(
        functools.partial(_combine_kernel, final_norm=final_norm),
        grid=(N_TB,),
        in_specs=[
            pl.BlockSpec((1, 1, TOP_K * MOE_TB), lambda i: (i, 0, 0), memory_space=pltpu.SMEM),
            pl.BlockSpec((1, 1, TOP_K * MOE_TB), lambda i: (jnp.minimum(i + 1, N_TB - 1), 0, 0),
                         memory_space=pltpu.SMEM),
            pl.BlockSpec((MOE_TB, d), lambda i: (i, 0)),
            pl.BlockSpec((MOE_TB, N_EXPERTS), lambda i: (i, 0)),
            pl.BlockSpec((1, d), lambda i: (0, 0)),
            pl.BlockSpec(memory_space=pl.ANY),
        ],
        out_specs=[
            pl.BlockSpec((MOE_TB, d), lambda i: (jnp.minimum(i, N_TB_PROMPT - 1), 0)),
            pl.BlockSpec((MOE_TB, d), lambda i: (jnp.maximum(i - N_TB_PROMPT, 0), 0)),
        ],
        out_shape=[
            jax.ShapeDtypeStruct((T_PROMPT, d), F32),
            jax.ShapeDtypeStruct((T_SAMPLE, d), F32),
        ],
        scratch_shapes=[
            pltpu.VMEM((2, TOP_K, MOE_TB * STAGE_PITCH, LANES), F32),
            pltpu.SemaphoreType.DMA((2,)),
        ],
        compiler_params=pltpu.CompilerParams(
            dimension_semantics=("arbitrary",), vmem_limit_bytes=V7X_VMEM_LIMIT),
        name="moe_combine",
    )(dest_flat, dest_flat, x, gate, g.reshape(1, d), y)


def _moe(x, norm_g, rw, w1, w3, w2, out_g, final_norm):
    eid, gate, cnt = _router(x, norm_g, rw)
    base, sb_e, sb_row0, sb_rows, n_used = _expert_layout(cnt)
    dest = _dest_rows(eid, base)
    dest_flat = dest[:, :TOP_K].reshape(N_TB, 1, TOP_K * MOE_TB)
    xg = _dispatch(x, norm_g, dest_flat)
    y = _moe_ffn(xg, w1, w3, w2, sb_e, sb_row0, sb_rows, n_used)
    return _moe_combine(x, y, dest_flat, gate, out_g, final_norm)


def kernel(x_prompt, x_sample, state_ret, state_conv31, state_conv3, mix_norm_g, w_in, ret_norm_g, conv31_w, conv31_b, conv_ln_g, conv_ln_b, sconv_w, sgu_ln_g, sgu_ln_b, sgu_w, sgu_b, w_out, ffn_norm_g, dense_w1, dense_w3, dense_w2, router_w, moe_w1, moe_w3, moe_w2, final_norm_g):
    assert DEPTH == 2, "layer 0 has the dense channel mixer; layer 1's expert mixer ends with the final norm"
    x, hn = _prep(x_prompt.reshape(T_PROMPT, D_MODEL), x_sample.reshape(T_SAMPLE, D_MODEL), mix_norm_g[0])
    rets_p, rets_s, c31_p, c31_s, c3_p, c3_s, vns = [], [], [], [], [], [], []
    for l in range(DEPTH):
        p = dict(ret_norm_g=ret_norm_g[l], conv31_w=conv31_w[l], conv31_b=conv31_b[l],
                 conv_ln_g=conv_ln_g[l], conv_ln_b=conv_ln_b[l], sconv_w=sconv_w[l],
                 sgu_ln_g=sgu_ln_g[l], sgu_ln_b=sgu_ln_b[l], sgu_w=sgu_w[l], sgu_b=sgu_b[l])
        z = _in_proj(hn, w_in, l)
        z_s = z[T_PROMPT:].reshape(DEC_BATCH, DEC_SEQ, D_IN)
        mix_s, ret_s, b31_s, b3_s, vn_s = _mixer_sample(z_s, p, l, state_ret, state_conv31, state_conv3)
        mix_init = jnp.pad(mix_s.reshape(T_SAMPLE, D_MODEL).astype(BF16), ((T_PROMPT, 0), (0, 0)))
        mix, ret_p, b31_p, b3_p = _mixer_prompt(z, p, mix_init)
        x = _out_proj_residual(mix, w_out, l, x)
        j = l // 2
        if l % 2 == 0:
            x, hn = _ffn_dense(x, ffn_norm_g[l], dense_w1[j], dense_w3[j], dense_w2[j], mix_norm_g[l + 1])
        else:
            y_p, y_s = _moe(x, ffn_norm_g[l], router_w[j], moe_w1[j], moe_w3[j], moe_w2[j],
                            final_norm_g, final_norm=True)
        rets_p.append(ret_p); rets_s.append(ret_s)
        c31_p.append(b31_p); c31_s.append(b31_s)
        c3_p.append(b3_p); c3_s.append(b3_s)
        vns.append(vn_s)
    y_prompt = y_p.reshape(BATCH, SEQ, D_MODEL)
    y_sample = y_s.reshape(DEC_BATCH, DEC_SEQ, D_MODEL)
    return (y_prompt, y_sample, jnp.stack(rets_p), jnp.stack(rets_s), jnp.stack(c31_p), jnp.stack(c31_s),
            jnp.stack(c3_p), jnp.stack(c3_s), jnp.stack(vns))
```

```python
import functools
import math

import jax
import jax.numpy as jnp
from jax import lax
from jax.experimental import pallas as pl
from jax.experimental.pallas import tpu as pltpu

F32 = jnp.float32
BF16 = jnp.bfloat16

D_MODEL = 2048
BATCH = 4
SEQ = 2048
DEPTH = 2
DEC_BATCH = 128
DEC_SEQ = 4
PAST_LEN = 16384
GROUP_W = 512
D_IN = 11 * GROUP_W
RET_HEADS = 4
RET_D = 128
CHUNK = 128
ROPE_THETA = 10000.0
CONV_W = 31
SC_W = 3
SGU_GROUPS = 4
D_FF = 5632
N_EXPERTS = 8
TOP_K = 2
EPS = 1e-6

T_PROMPT = BATCH * SEQ
T_SAMPLE = DEC_BATCH * DEC_SEQ
T_ALL = T_PROMPT + T_SAMPLE
N_CHUNKS = SEQ // CHUNK

C_Q, C_K, C_V, C_G, C_BL, C_BG, C_CB, C_CC, C_CH, C_DU, C_DV = (i * GROUP_W for i in range(11))

V7X_VMEM_LIMIT = 56 * 1024 * 1024

TM = 1088
TM_IN = 2176
TM_OUT = 512
N_OUT_PROMPT = T_PROMPT // TM_OUT
NORM_ROWS = 272
TN_IN = 512
TF = 256
SAMPLE_BT = 8

LANES = 128
SUBLANES = 8
TOK_ROWS = D_MODEL // LANES
MOE_TB = 256
N_TB = T_ALL // MOE_TB
N_SLOTS = T_ALL * TOP_K
SUB = 256
SB = 9 * SUB
CHUNK_ROWS = SB // 3
NS_MAX = N_SLOTS // SB + N_EXPERTS


def _rms_rows(x, g):
    ms = jnp.mean(x * x, axis=-1, keepdims=True)
    return (x * lax.rsqrt(ms + EPS)) * g


def _layernorm_rows(x, g, b):
    mu = jnp.mean(x, axis=-1, keepdims=True)
    xc = x - mu
    return xc * lax.rsqrt(jnp.mean(xc * xc, axis=-1, keepdims=True) + EPS) * g + b


def _silu(x):
    return x * jax.nn.sigmoid(x)


def _norm_block_to(x_ref, g_ref, dst_ref):
    rows = x_ref.shape[0]
    g = g_ref[...]

    def body(c, carry):
        r0 = pl.multiple_of(c * NORM_ROWS, NORM_ROWS)
        x = x_ref[pl.ds(r0, NORM_ROWS), :]
        dst_ref[pl.ds(r0, NORM_ROWS), :] = _rms_rows(x, g).astype(dst_ref.dtype)
        return carry

    lax.fori_loop(0, rows // NORM_ROWS, body, 0)


PREP_ROWS = 512
N_PREP_PROMPT = T_PROMPT // PREP_ROWS


def _prep_kernel(xp_ref, xs_ref, g_ref, x_ref, hn_ref):
    def emit(src_ref):
        x = src_ref[...]
        x_ref[...] = x
        hn_ref[...] = _rms_rows(x, g_ref[...]).astype(hn_ref.dtype)

    is_prompt = pl.program_id(0) < N_PREP_PROMPT
    pl.when(is_prompt)(functools.partial(emit, xp_ref))
    pl.when(jnp.logical_not(is_prompt))(functools.partial(emit, xs_ref))


def _prep(x_prompt, x_sample, g):
    d = x_prompt.shape[1]
    blk = lambda f: pl.BlockSpec((PREP_ROWS, d), f)
    return pl.pallas_call(
        _prep_kernel,
        grid=(T_ALL // PREP_ROWS,),
        in_specs=[
            blk(lambda i: (jnp.minimum(i, N_PREP_PROMPT - 1), 0)),
            blk(lambda i: (jnp.maximum(i - N_PREP_PROMPT, 0), 0)),
            pl.BlockSpec((1, d), lambda i: (0, 0)),
        ],
        out_specs=[blk(lambda i: (i, 0)), blk(lambda i: (i, 0))],
        out_shape=[jax.ShapeDtypeStruct((T_ALL, d), F32), jax.ShapeDtypeStruct((T_ALL, d), BF16)],
        compiler_params=pltpu.CompilerParams(
            dimension_semantics=("arbitrary",), vmem_limit_bytes=V7X_VMEM_LIMIT),
        name="prep_rows",
    )(x_prompt, x_sample, g.reshape(1, d))


def _in_proj_kernel(h_ref, w_ref, o_ref):
    o_ref[...] = jnp.dot(h_ref[...], w_ref[...].astype(BF16), preferred_element_type=F32)


def _in_proj(hn, w_all, layer):
    t, d = hn.shape
    n = w_all.shape[2]
    return pl.pallas_call(
        _in_proj_kernel,
        grid=(t // TM_IN, n // TN_IN),
        in_specs=[
            pl.BlockSpec((TM_IN, d), lambda i, j: (i, 0)),
            pl.BlockSpec((None, d, TN_IN), lambda i, j: (layer, 0, j)),
        ],
        out_specs=pl.BlockSpec((TM_IN, TN_IN), lambda i, j: (i, j)),
        out_shape=jax.ShapeDtypeStruct((t, n), F32),
        compiler_params=pltpu.CompilerParams(
            dimension_semantics=("arbitrary", "arbitrary"), vmem_limit_bytes=V7X_VMEM_LIMIT),
        name="in_proj",
    )(hn, w_all)


W_LOAD_ROWS = 512


def _out_proj_kernel(ap_ref, as_ref, w_hbm, r_ref, o_ref, wb, stage, sem, *, layer):
    i = pl.program_id(0)

    @pl.when(i == 0)
    def _():
        for c in range(wb.shape[0] // W_LOAD_ROWS):
            rows = pl.ds(c * W_LOAD_ROWS, W_LOAD_ROWS)
            cp = pltpu.make_async_copy(w_hbm.at[layer, rows], stage, sem)
            cp.start()
            cp.wait()
            wb[rows, :] = stage[...].astype(BF16)

    def emit(a):
        o_ref[...] = r_ref[...] + jnp.dot(a, wb[...], preferred_element_type=F32)

    pl.when(i < N_OUT_PROMPT)(lambda: emit(ap_ref[...]))
    pl.when(i >= N_OUT_PROMPT)(lambda: emit(as_ref[...].astype(BF16)))


def _out_proj_residual(a_prompt, a_sample, w_all, layer, r):
    k = a_prompt.shape[1]
    t = r.shape[0]
    n = w_all.shape[2]
    return pl.pallas_call(
        functools.partial(_out_proj_kernel, layer=layer),
        grid=(t // TM_OUT,),
        in_specs=[
            pl.BlockSpec((TM_OUT, k), lambda i: (jnp.minimum(i, N_OUT_PROMPT - 1), 0)),
            pl.BlockSpec((TM_OUT, k), lambda i: (0, 0)),
            pl.BlockSpec(memory_space=pl.ANY),
            pl.BlockSpec((TM_OUT, n), lambda i: (i, 0)),
        ],
        out_specs=pl.BlockSpec((TM_OUT, n), lambda i: (i, 0)),
        out_shape=jax.ShapeDtypeStruct((t, n), F32),
        scratch_shapes=[
            pltpu.VMEM((k, n), BF16),
            pltpu.VMEM((W_LOAD_ROWS, n), F32),
            pltpu.SemaphoreType.DMA(()),
        ],
        compiler_params=pltpu.CompilerParams(
            dimension_semantics=("arbitrary",), vmem_limit_bytes=V7X_VMEM_LIMIT),
        name="out_proj_residual",
    )(a_prompt, a_sample, w_all, r)


def _swiglu_tile(h, w1_ref, w3_ref, w2_ref):
    a = jnp.dot(h, w1_ref[...].astype(BF16), preferred_element_type=F32)
    b = jnp.dot(h, w3_ref[...].astype(BF16), preferred_element_type=F32)
    u = (_silu(a) * b).astype(BF16)
    return jnp.dot(u, w2_ref[...].astype(BF16), preferred_element_type=F32)


def _ffn_kernel(x_hbm, g_ref, w1_ref, w3_ref, w2_ref, gn_ref, o_ref, hnext_ref, hn_ref, sem):
    @pl.when(pl.program_id(1) == 0)
    def _():
        r0 = pl.multiple_of(pl.program_id(0) * TM, TM)
        cp = pltpu.make_async_copy(x_hbm.at[pl.ds(r0, TM)], o_ref, sem)
        cp.start()
        cp.wait()
        _norm_block_to(o_ref, g_ref, hn_ref)

    o_ref[...] += _swiglu_tile(hn_ref[...], w1_ref, w3_ref, w2_ref)

    @pl.when(pl.program_id(1) == pl.num_programs(1) - 1)
    def _():
        _norm_block_to(o_ref, gn_ref, hnext_ref)


def _ffn_dense(x, g, w1, w3, w2, g_next):
    t, d = x.shape
    f = w1.shape[1]
    return pl.pallas_call(
        _ffn_kernel,
        grid=(t // TM, f // TF),
        in_specs=[
            pl.BlockSpec(memory_space=pl.ANY),
            pl.BlockSpec((1, d), lambda i, j: (0, 0)),
            pl.BlockSpec((d, TF), lambda i, j: (0, j)),
            pl.BlockSpec((d, TF), lambda i, j: (0, j)),
            pl.BlockSpec((TF, d), lambda i, j: (j, 0)),
            pl.BlockSpec((1, d), lambda i, j: (0, 0)),
        ],
        out_specs=[
            pl.BlockSpec((TM, d), lambda i, j: (i, 0)),
            pl.BlockSpec((TM, d), lambda i, j: (i, 0)),
        ],
        out_shape=[jax.ShapeDtypeStruct((t, d), F32), jax.ShapeDtypeStruct((t, d), BF16)],
        scratch_shapes=[pltpu.VMEM((TM, d), BF16), pltpu.SemaphoreType.DMA(())],
        compiler_params=pltpu.CompilerParams(
            dimension_semantics=("arbitrary", "arbitrary"), vmem_limit_bytes=V7X_VMEM_LIMIT),
        name="ffn_dense",
    )(x, g.reshape(1, d), w1, w3, w2, g_next.reshape(1, d))


def _rotary_tables(length, pos0):
    half = RET_D // 2
    inv = ROPE_THETA ** (-jnp.arange(half, dtype=F32) / half)
    pos = jnp.arange(length, dtype=F32) + pos0
    ang = pos[:, None] * inv[None, :]
    cos, sin = jnp.cos(ang), jnp.sin(ang)
    return jnp.concatenate([cos, cos], axis=-1), jnp.concatenate([-sin, sin], axis=-1)


def _decay_tables(c):
    log_g = jnp.log1p(-jnp.exp2(-5.0 - jnp.arange(RET_HEADS, dtype=F32)))
    idx = jnp.arange(c, dtype=F32)
    rel = idx[:, None] - idx[None, :]
    d_in = jnp.where(rel >= 0, jnp.exp(log_g[:, None, None] * jnp.maximum(rel, 0.0)), 0.0)
    d_q = jnp.exp(log_g[:, None] * (idx + 1.0)[None, :])
    d_k = jnp.exp(log_g[:, None] * (c - 1.0 - idx)[None, :])
    d_c = jnp.exp(log_g * c)
    return d_in, d_q, d_k, d_c


HIST31 = 32
HIST3 = 8


def _mixer_prompt_kernel(z_ref, cos_ref, sin_ref, din_ref, dq_ref, dk_ref, dc_ref, retg_ref,
                         c31w_ref, c31b_ref, clng_ref, clnb_ref, scw_ref, slng_ref, slnb_ref,
                         sguw_ref, sgub_ref,
                         mix_ref, ret_ref, b31_ref, b3_ref,
                         s_scr, ext31_scr, ext3_scr, ysh_scr):
    c = pl.program_id(1)

    @pl.when(c == 0)
    def _():
        s_scr[...] = jnp.zeros_like(s_scr)
        ext31_scr[pl.ds(0, HIST31), :] = jnp.zeros((HIST31, GROUP_W), F32)
        ext3_scr[pl.ds(0, HIST3), :] = jnp.zeros((HIST3, GROUP_W), F32)

    cos = cos_ref[...]
    sin = sin_ref[...]

    for h in range(RET_HEADS):
        lo = h * RET_D
        q = z_ref[:, pl.ds(C_Q + lo, RET_D)]
        k = z_ref[:, pl.ds(C_K + lo, RET_D)]
        v = z_ref[:, pl.ds(C_V + lo, RET_D)]
        g = z_ref[:, pl.ds(C_G + lo, RET_D)]
        q = q * cos + pltpu.roll(q, RET_D // 2, axis=1) * sin
        k = (k * cos + pltpu.roll(k, RET_D // 2, axis=1) * sin) * (RET_D ** -0.5)
        qb = q.astype(BF16)
        vb = v.astype(BF16)
        scores = lax.dot_general(qb, k.astype(BF16), (((1,), (1,)), ((), ())),
                                 preferred_element_type=F32) * din_ref[h]
        inner = jnp.dot(scores.astype(BF16), vb, preferred_element_type=F32)
        s_old = s_scr[h]
        cross = jnp.dot(qb, s_old.astype(BF16), preferred_element_type=F32) * dq_ref[h]
        kd = (k * dk_ref[h]).astype(BF16)
        s_scr[h] = s_old * dc_ref[h] + lax.dot_general(kd, vb, (((0,), (0,)), ((), ())),
                                                       preferred_element_type=F32)
        r = inner + cross
        r = r * lax.rsqrt(jnp.mean(r * r, axis=-1, keepdims=True) + EPS)
        r = r * retg_ref[:, pl.ds(lo, RET_D)]
        mix_ref[:, pl.ds(lo, RET_D)] = (_silu(g) * r).astype(mix_ref.dtype)

    glu = z_ref[:, pl.ds(C_BL, GROUP_W)] * jax.nn.sigmoid(z_ref[:, pl.ds(C_BG, GROUP_W)])
    ext31_scr[pl.ds(HIST31, CHUNK), :] = glu
    s0 = HIST31 - (CONV_W - 1)
    conv = None
    for i in range(SUBLANES):
        n_rows = CHUNK if i == 0 else CHUNK + SUBLANES
        y = None
        for m in range(HIST31 // SUBLANES + 1):
            j = SUBLANES * m + i - s0
            if 0 <= j < CONV_W:
                term = ext31_scr[pl.ds(SUBLANES * m, n_rows), :] * c31w_ref[pl.ds(j, 1), :]
                y = term if y is None else y + term
        if i == 0:
            conv = y
        else:
            ysh_scr[i - 1] = y
            conv = conv + ysh_scr[i - 1, pl.ds(i, CHUNK), :]
    conv = conv + c31b_ref[...]
    out_b = _silu(_layernorm_rows(conv, clng_ref[...], clnb_ref[...]))
    mix_ref[:, pl.ds(GROUP_W, GROUP_W)] = out_b.astype(mix_ref.dtype)
    ext31_scr[pl.ds(0, HIST31), :] = ext31_scr[pl.ds(CHUNK, HIST31), :]

    gated = z_ref[:, pl.ds(C_CC, GROUP_W)] * z_ref[:, pl.ds(C_CH, GROUP_W)]
    ext3_scr[pl.ds(HIST3, CHUNK), :] = gated
    conv3 = ext3_scr[pl.ds(HIST3 - (SC_W - 1), CHUNK), :] * scw_ref[pl.ds(0, 1), :]
    for j in range(1, SC_W):
        conv3 = conv3 + ext3_scr[pl.ds(HIST3 - (SC_W - 1) + j, CHUNK), :] * scw_ref[pl.ds(j, 1), :]
    mix_ref[:, pl.ds(2 * GROUP_W, GROUP_W)] = (z_ref[:, pl.ds(C_CB, GROUP_W)] * conv3).astype(mix_ref.dtype)
    ext3_scr[pl.ds(0, HIST3), :] = ext3_scr[pl.ds(CHUNK, HIST3), :]

    vn = _layernorm_rows(z_ref[:, pl.ds(C_DV, GROUP_W)], slng_ref[...], slnb_ref[...])
    row = lax.broadcasted_iota(jnp.int32, (CHUNK, CHUNK), 0)
    col = lax.broadcasted_iota(jnp.int32, (CHUNK, CHUNK), 1)
    ch = GROUP_W // SGU_GROUPS
    for gi in range(SGU_GROUPS):
        wm = jnp.where(row >= col, sguw_ref[gi], 0.0).astype(BF16)
        mixed = jnp.dot(wm, vn[:, gi * ch:(gi + 1) * ch].astype(BF16), preferred_element_type=F32)
        mixed = mixed + sgub_ref[:, pl.ds(gi, 1)]
        du = z_ref[:, pl.ds(C_DU + gi * ch, ch)]
        mix_ref[:, pl.ds(3 * GROUP_W + gi * ch, ch)] = (du * mixed).astype(mix_ref.dtype)

    @pl.when(c == pl.num_programs(1) - 1)
    def _():
        ret_ref[...] = s_scr[...]
        b31_ref[...] = ext31_scr[pl.ds(HIST31 - (CONV_W - 1), CONV_W - 1), :]
        b3_ref[...] = ext3_scr[pl.ds(HIST3 - (SC_W - 1), SC_W - 1), :]


def _mixer_prompt(z, p):
    cos, sin = _rotary_tables(SEQ, 0)
    d_in, d_q, d_k, d_c = _decay_tables(CHUNK)
    dq_t = jnp.broadcast_to(d_q[:, :, None], (RET_HEADS, CHUNK, RET_D))
    dk_t = jnp.broadcast_to(d_k[:, :, None], (RET_HEADS, CHUNK, RET_D))
    dc_t = jnp.broadcast_to(d_c[:, None, None], (RET_HEADS, RET_D, RET_D))

    def full(shape):
        return pl.BlockSpec(shape, lambda b, c: (0,) * len(shape))

    row = lambda a: a.reshape(1, -1)
    return pl.pallas_call(
        _mixer_prompt_kernel,
        grid=(BATCH, N_CHUNKS),
        in_specs=[
            pl.BlockSpec((CHUNK, D_IN), lambda b, c: (b * N_CHUNKS + c, 0)),
            pl.BlockSpec((CHUNK, RET_D), lambda b, c: (c, 0)),
            pl.BlockSpec((CHUNK, RET_D), lambda b, c: (c, 0)),
            full((RET_HEADS, CHUNK, CHUNK)),
            full((RET_HEADS, CHUNK, RET_D)),
            full((RET_HEADS, CHUNK, RET_D)),
            full((RET_HEADS, RET_D, RET_D)),
            full((1, GROUP_W)),
            full((CONV_W, GROUP_W)),
            full((1, GROUP_W)),
            full((1, GROUP_W)),
            full((1, GROUP_W)),
            full((SC_W, GROUP_W)),
            full((1, GROUP_W)),
            full((1, GROUP_W)),
            full((SGU_GROUPS, CHUNK, CHUNK)),
            full((CHUNK, SGU_GROUPS)),
        ],
        out_specs=[
            pl.BlockSpec((CHUNK, D_MODEL), lambda b, c: (b * N_CHUNKS + c, 0)),
            pl.BlockSpec((None, RET_HEADS, RET_D, RET_D), lambda b, c: (b, 0, 0, 0)),
            pl.BlockSpec((None, CONV_W - 1, GROUP_W), lambda b, c: (b, 0, 0)),
            pl.BlockSpec((None, SC_W - 1, GROUP_W), lambda b, c: (b, 0, 0)),
        ],
        out_shape=[
            jax.ShapeDtypeStruct((T_PROMPT, D_MODEL), BF16),
            jax.ShapeDtypeStruct((BATCH, RET_HEADS, RET_D, RET_D), F32),
            jax.ShapeDtypeStruct((BATCH, CONV_W - 1, GROUP_W), F32),
            jax.ShapeDtypeStruct((BATCH, SC_W - 1, GROUP_W), F32),
        ],
        scratch_shapes=[
            pltpu.VMEM((RET_HEADS, RET_D, RET_D), F32),
            pltpu.VMEM((HIST31 + CHUNK, GROUP_W), F32),
            pltpu.VMEM((HIST3 + CHUNK, GROUP_W), F32),
            pltpu.VMEM((SUBLANES - 1, CHUNK + SUBLANES, GROUP_W), F32),
        ],
        compiler_params=pltpu.CompilerParams(
            dimension_semantics=("arbitrary", "arbitrary"), vmem_limit_bytes=V7X_VMEM_LIMIT),
        name="mixer_prompt",
    )(z, cos, sin, d_in, dq_t, dk_t, dc_t, row(p["ret_norm_g"]), p["conv31_w"], row(p["conv31_b"]),
      row(p["conv_ln_g"]), row(p["conv_ln_b"]), p["sconv_w"], row(p["sgu_ln_g"]), row(p["sgu_ln_b"]),
      p["sgu_w"], p["sgu_b"].T)


EXT31_S = 40
EXT3_S = 8


def _mixer_sample_kernel(z_ref, cos_ref, sin_ref, din_ref, dq_ref, dk_ref, dc_ref, retg_ref,
                         c31w_ref, c31b_ref, clng_ref, clnb_ref, scw_ref, slng_ref, slnb_ref,
                         sguw_ref, sgub_ref, sret_ref, s31_ref, s3_ref,
                         mix_ref, ret_ref, b31_ref, b3_ref, vn_ref,
                         ext31_scr, ext3_scr):
    cos = cos_ref[...]
    sin = sin_ref[...]
    n_hist31 = CONV_W - 1
    n_hist3 = SC_W - 1
    bt = sret_ref.shape[0]

    def zs(col, width):
        return jnp.stack([z_ref[pl.ds(DEC_SEQ * b, DEC_SEQ), pl.ds(col, width)] for b in range(bt)])

    def zrow(s, col, width):
        return jnp.stack([z_ref[pl.ds(DEC_SEQ * b + s, 1), pl.ds(col, width)] for b in range(bt)])

    def put(col, val):
        for b in range(bt):
            mix_ref[pl.ds(DEC_SEQ * b, DEC_SEQ), pl.ds(col, val.shape[-1])] = val[b]

    for h in range(RET_HEADS):
        lo = h * RET_D
        q = zs(C_Q + lo, RET_D)
        k = zs(C_K + lo, RET_D)
        g = zs(C_G + lo, RET_D)
        q = q * cos + pltpu.roll(q, RET_D // 2, axis=2) * sin
        k = (k * cos + pltpu.roll(k, RET_D // 2, axis=2) * sin) * (RET_D ** -0.5)
        s_old = sret_ref[:, h]
        r = lax.dot_general(q, s_old, (((2,), (1,)), ((0,), (0,))),
                            preferred_element_type=F32) * dq_ref[h]
        for s in range(DEC_SEQ):
            k_s = zrow(s, C_K + lo, RET_D)
            k_s = (k_s * cos[s:s + 1] + pltpu.roll(k_s, RET_D // 2, axis=2) * sin[s:s + 1]) * (RET_D ** -0.5)
            v_s = zrow(s, C_V + lo, RET_D)
            score = jnp.sum(q * k_s, axis=-1, keepdims=True) * din_ref[h, s]
            r = r + score * v_s
        kd = k * dk_ref[h]
        v = zs(C_V + lo, RET_D)
        ret_ref[:, h] = s_old * dc_ref[h] + lax.dot_general(
            kd, v, (((1,), (1,)), ((0,), (0,))), preferred_element_type=F32)
        r = r * lax.rsqrt(jnp.mean(r * r, axis=-1, keepdims=True) + EPS)
        r = r * retg_ref[:, pl.ds(lo, RET_D)]
        put(lo, _silu(g) * r)

    glu = zs(C_BL, GROUP_W) * jax.nn.sigmoid(zs(C_BG, GROUP_W))
    ext31_scr[:, pl.ds(0, n_hist31), :] = s31_ref[...]
    ext31_scr[:, pl.ds(n_hist31, DEC_SEQ), :] = glu
    conv = ext31_scr[:, pl.ds(0, DEC_SEQ), :] * c31w_ref[pl.ds(0, 1), :]
    for j in range(1, CONV_W):
        conv = conv + ext31_scr[:, pl.ds(j, DEC_SEQ), :] * c31w_ref[pl.ds(j, 1), :]
    conv = conv + c31b_ref[...]
    put(GROUP_W, _silu(_layernorm_rows(conv, clng_ref[...], clnb_ref[...])))
    b31_ref[...] = ext31_scr[:, pl.ds(DEC_SEQ, n_hist31), :]

    gated = zs(C_CC, GROUP_W) * zs(C_CH, GROUP_W)
    ext3_scr[:, pl.ds(0, n_hist3), :] = s3_ref[...]
    ext3_scr[:, pl.ds(n_hist3, DEC_SEQ), :] = gated
    conv3 = ext3_scr[:, pl.ds(0, DEC_SEQ), :] * scw_ref[pl.ds(0, 1), :]
    for j in range(1, SC_W):
        conv3 = conv3 + ext3_scr[:, pl.ds(j, DEC_SEQ), :] * scw_ref[pl.ds(j, 1), :]
    put(2 * GROUP_W, zs(C_CB, GROUP_W) * conv3)
    b3_ref[...] = ext3_scr[:, pl.ds(DEC_SEQ, n_hist3), :]

    vn = _layernorm_rows(zs(C_DV, GROUP_W), slng_ref[...], slnb_ref[...])
    vn_ref[...] = vn
    mixed = sgub_ref[...]
    for s in range(DEC_SEQ):
        mixed = mixed + sguw_ref[s] * vn_ref[:, pl.ds(s, 1), :]
    put(3 * GROUP_W, zs(C_DU, GROUP_W) * mixed)


def _mixer_sample(z, p, layer, s_ret, s_31, s_3):
    cos, sin = _rotary_tables(DEC_SEQ, PAST_LEN)
    d_in, d_q, d_k, d_c = _decay_tables(DEC_SEQ)
    din_t = jnp.swapaxes(d_in, 1, 2)[:, :, :, None]
    dq_t = jnp.broadcast_to(d_q[:, :, None], (RET_HEADS, DEC_SEQ, RET_D))
    dk_t = jnp.broadcast_to(d_k[:, :, None], (RET_HEADS, DEC_SEQ, RET_D))
    dc_t = jnp.broadcast_to(d_c[:, None, None], (RET_HEADS, RET_D, RET_D))
    ch = GROUP_W // SGU_GROUPS
    w4 = jnp.tril(p["sgu_w"][:, :DEC_SEQ, :DEC_SEQ])
    w_t = jnp.repeat(jnp.transpose(w4, (2, 1, 0)), ch, axis=-1)
    b_t = jnp.repeat(p["sgu_b"][:, :DEC_SEQ].T, ch, axis=-1)
    bt = SAMPLE_BT

    def full(shape):
        return pl.BlockSpec(shape, lambda i: (0,) * len(shape))

    row = lambda a: a.reshape(1, -1)
    return pl.pallas_call(
        _mixer_sample_kernel,
        grid=(DEC_BATCH // bt,),
        in_specs=[
            pl.BlockSpec((bt * DEC_SEQ, D_IN), lambda i: (T_PROMPT // (bt * DEC_SEQ) + i, 0)),
            full((DEC_SEQ, RET_D)),
            full((DEC_SEQ, RET_D)),
            full((RET_HEADS, DEC_SEQ, DEC_SEQ, 1)),
            full((RET_HEADS, DEC_SEQ, RET_D)),
            full((RET_HEADS, DEC_SEQ, RET_D)),
            full((RET_HEADS, RET_D, RET_D)),
            full((1, GROUP_W)),
            full((CONV_W, GROUP_W)),
            full((1, GROUP_W)),
            full((1, GROUP_W)),
            full((1, GROUP_W)),
            full((SC_W, GROUP_W)),
            full((1, GROUP_W)),
            full((1, GROUP_W)),
            full((DEC_SEQ, DEC_SEQ, GROUP_W)),
            full((DEC_SEQ, GROUP_W)),
            pl.BlockSpec((None, bt, RET_HEADS, RET_D, RET_D), lambda i: (layer, i, 0, 0, 0)),
            pl.BlockSpec((None, bt, CONV_W - 1, GROUP_W), lambda i: (layer, i, 0, 0)),
            pl.BlockSpec((None, bt, SC_W - 1, GROUP_W), lambda i: (layer, i, 0, 0)),
        ],
        out_specs=[
            pl.BlockSpec((bt * DEC_SEQ, D_MODEL), lambda i: (i, 0)),
            pl.BlockSpec((bt, RET_HEADS, RET_D, RET_D), lambda i: (i, 0, 0, 0)),
            pl.BlockSpec((bt, CONV_W - 1, GROUP_W), lambda i: (i, 0, 0)),
            pl.BlockSpec((bt, SC_W - 1, GROUP_W), lambda i: (i, 0, 0)),
            pl.BlockSpec((bt, DEC_SEQ, GROUP_W), lambda i: (i, 0, 0)),
        ],
        out_shape=[
            jax.ShapeDtypeStruct((T_SAMPLE, D_MODEL), F32),
            jax.ShapeDtypeStruct((DEC_BATCH, RET_HEADS, RET_D, RET_D), F32),
            jax.ShapeDtypeStruct((DEC_BATCH, CONV_W - 1, GROUP_W), F32),
            jax.ShapeDtypeStruct((DEC_BATCH, SC_W - 1, GROUP_W), F32),
            jax.ShapeDtypeStruct((DEC_BATCH, DEC_SEQ, GROUP_W), F32),
        ],
        scratch_shapes=[
            pltpu.VMEM((bt, EXT31_S, GROUP_W), F32),
            pltpu.VMEM((bt, EXT3_S, GROUP_W), F32),
        ],
        compiler_params=pltpu.CompilerParams(
            dimension_semantics=("arbitrary",), vmem_limit_bytes=V7X_VMEM_LIMIT),
        name="mixer_sample",
    )(z, cos, sin, din_t, dq_t, dk_t, dc_t, row(p["ret_norm_g"]), p["conv31_w"], row(p["conv31_b"]),
      row(p["conv_ln_g"]), row(p["conv_ln_b"]), p["sconv_w"], row(p["sgu_ln_g"]), row(p["sgu_ln_b"]),
      w_t, b_t, s_ret, s_31, s_3)


def _to_token_major(dst, val, pitch=TOK_ROWS):
    n = val.shape[0]
    for a in range(TOK_ROWS):
        dst[pl.ds(a, n, stride=pitch), :] = val[:, a * LANES:(a + 1) * LANES]


STAGE_PITCH = 24
DMA_UNROLL = 8


def _router_kernel(x_ref, g_ref, rw_ref, e_ref, p_ref, c_ref):
    h = _rms_rows(x_ref[...], g_ref[...])
    lane = lax.broadcasted_iota(jnp.int32, (h.shape[0], N_EXPERTS), 1)
    logits = jnp.zeros((h.shape[0], N_EXPERTS), F32)
    for e in range(N_EXPERTS):
        col = jnp.sum(h * rw_ref[pl.ds(e, 1), :], axis=-1, keepdims=True)
        logits = jnp.where(lane == e, col, logits)
    m1 = jnp.max(logits, axis=-1, keepdims=True)
    i1 = jnp.min(jnp.where(logits == m1, lane, N_EXPERTS), axis=-1, keepdims=True)
    rest = jnp.where(lane == i1, -jnp.inf, logits)
    m2 = jnp.max(rest, axis=-1, keepdims=True)
    i2 = jnp.min(jnp.where(rest == m2, lane, N_EXPERTS), axis=-1, keepdims=True)
    e2 = jnp.exp(m2 - m1)
    den = 1.0 + e2
    e_ref[...] = jnp.where(lane == 0, i1, jnp.where(lane == 1, i2, 0))
    p_ref[...] = jnp.where(lane == 0, 1.0 / den, jnp.where(lane == 1, e2 / den, 0.0))
    c_ref[pl.ds(0, 1), :] = jnp.sum(jnp.where(lane == i1, 1.0, 0.0), axis=0, keepdims=True).astype(jnp.int32)
    c_ref[pl.ds(1, 1), :] = jnp.sum(jnp.where(lane == i2, 1.0, 0.0), axis=0, keepdims=True).astype(jnp.int32)


def _router(x, g, rw):
    t, d = x.shape
    return pl.pallas_call(
        _router_kernel,
        grid=(N_TB,),
        in_specs=[
            pl.BlockSpec((MOE_TB, d), lambda i: (i, 0)),
            pl.BlockSpec((1, d), lambda i: (0, 0)),
            pl.BlockSpec((N_EXPERTS, d), lambda i: (0, 0)),
        ],
        out_specs=[
            pl.BlockSpec((MOE_TB, N_EXPERTS), lambda i: (i, 0)),
            pl.BlockSpec((MOE_TB, N_EXPERTS), lambda i: (i, 0)),
            pl.BlockSpec((None, TOP_K, N_EXPERTS), lambda i: (i, 0, 0)),
        ],
        out_shape=[
            jax.ShapeDtypeStruct((t, N_EXPERTS), jnp.int32),
            jax.ShapeDtypeStruct((t, N_EXPERTS), F32),
            jax.ShapeDtypeStruct((N_TB, TOP_K, N_EXPERTS), jnp.int32),
        ],
        compiler_params=pltpu.CompilerParams(
            dimension_semantics=("arbitrary",), vmem_limit_bytes=V7X_VMEM_LIMIT),
        name="moe_router",
    )(x, g.reshape(1, d), rw.T)


def _pick(table, idx):
    lanes = jnp.arange(N_EXPERTS, dtype=jnp.int32)
    return jnp.sum(jnp.where(idx[..., None] == lanes, table, 0), axis=-1)


def _expert_layout(cnt):
    tot = jnp.sum(cnt, axis=(0, 1))
    start = jnp.cumsum(tot) - tot
    blk = jnp.sum(cnt, axis=1)
    blk_base = start[None, :] + jnp.cumsum(blk, axis=0) - blk
    base = jnp.stack([blk_base, blk_base + cnt[:, 0]], axis=1)
    n_sb = (tot + SB - 1) // SB
    sb_end = jnp.cumsum(n_sb)
    n_used = sb_end[-1]
    s_idx = jnp.arange(NS_MAX, dtype=jnp.int32)
    s_cl = jnp.minimum(s_idx, n_used - 1)
    sb_e = jnp.sum(s_cl[:, None] >= sb_end[None, :], axis=-1).astype(jnp.int32)
    local = s_cl - _pick(sb_end - n_sb, sb_e)
    sb_row0 = _pick(start, sb_e) + local * SB
    sb_rows = jnp.where(s_idx < n_used, jnp.clip(_pick(tot, sb_e) - local * SB, 0, SB), 0)
    i32 = lambda a: a.astype(jnp.int32)
    return i32(base), sb_e, i32(sb_row0), i32(sb_rows), i32(n_used).reshape(1)


def _dest_kernel(e_ref, b_ref, d_ref):
    eid = e_ref[...]
    lane = lax.broadcasted_iota(jnp.int32, eid.shape, 1)
    row = lax.broadcasted_iota(jnp.int32, (MOE_TB, MOE_TB), 0)
    col = lax.broadcasted_iota(jnp.int32, (MOE_TB, MOE_TB), 1)
    earlier = jnp.where(row > col, 1.0, 0.0).astype(BF16)
    base = b_ref[...].astype(F32)

    def dest_of(k):
        mine = lane == eid[:, k:k + 1]
        before = jnp.dot(earlier, jnp.where(mine, 1.0, 0.0).astype(BF16), preferred_element_type=F32)
        return jnp.sum(jnp.where(mine, before + base[k:k + 1], 0.0), axis=-1, keepdims=True)

    d_ref[...] = jnp.where(lane == 0, dest_of(0), jnp.where(lane == 1, dest_of(1), 0.0)).astype(jnp.int32)


def _dest_rows(eid, base):
    return pl.pallas_call(
        _dest_kernel,
        grid=(N_TB,),
        in_specs=[
            pl.BlockSpec((MOE_TB, N_EXPERTS), lambda i: (i, 0)),
            pl.BlockSpec((None, TOP_K, N_EXPERTS), lambda i: (i, 0, 0)),
        ],
        out_specs=pl.BlockSpec((MOE_TB, N_EXPERTS), lambda i: (i, 0)),
        out_shape=jax.ShapeDtypeStruct((T_ALL, N_EXPERTS), jnp.int32),
        compiler_params=pltpu.CompilerParams(
            dimension_semantics=("arbitrary",), vmem_limit_bytes=V7X_VMEM_LIMIT),
        name="moe_dest_rows",
    )(eid, base)


def _dispatch_kernel(d_ref, x_ref, g_ref, xg_hbm, stage, zeros, sem, zsem):
    i = pl.program_id(0)
    n = pl.num_programs(0)
    slot = i % 2
    blk_rows = MOE_TB * TOK_ROWS

    def wait_slot(sl):
        for _ in range(TOP_K):
            pltpu.make_async_copy(stage.at[sl, pl.ds(0, blk_rows)], xg_hbm.at[pl.ds(0, blk_rows)], sem.at[sl]).wait()

    @pl.when(i >= 2)
    def _():
        wait_slot(slot)

    _to_token_major(stage.at[slot], _rms_rows(x_ref[...], g_ref[...]), STAGE_PITCH)

    def body(r, carry):
        src = stage.at[slot, pl.ds(pl.multiple_of(r * STAGE_PITCH, 8), TOK_ROWS)]
        for k in range(TOP_K):
            row = d_ref[0, 0, TOP_K * r + k]
            dst = xg_hbm.at[pl.ds(pl.multiple_of(row * TOK_ROWS, TOK_ROWS), TOK_ROWS)]
            pltpu.make_async_copy(src, dst, sem.at[slot]).start()
        return carry

    lax.fori_loop(0, MOE_TB, body, 0, unroll=DMA_UNROLL)

    @pl.when(i == n - 1)
    def _():
        zeros[...] = jnp.zeros_like(zeros)
        tail = [pltpu.make_async_copy(zeros, xg_hbm.at[pl.ds((N_SLOTS + q * (SUB // TOK_ROWS)) * TOK_ROWS, SUB)], zsem)
                for q in range(TOK_ROWS)]
        for cp in tail:
            cp.start()
        for cp in tail:
            cp.wait()
        wait_slot(slot)
        wait_slot(1 - slot)


def _dispatch(x, g, dest_flat):
    t, d = x.shape
    return pl.pallas_call(
        _dispatch_kernel,
        grid=(N_TB,),
        in_specs=[
            pl.BlockSpec((1, 1, TOP_K * MOE_TB), lambda i: (i, 0, 0), memory_space=pltpu.SMEM),
            pl.BlockSpec((MOE_TB, d), lambda i: (i, 0)),
            pl.BlockSpec((1, d), lambda i: (0, 0)),
        ],
        out_specs=pl.BlockSpec(memory_space=pl.ANY),
        out_shape=jax.ShapeDtypeStruct(((N_SLOTS + SUB) * TOK_ROWS, LANES), F32),
        scratch_shapes=[
            pltpu.VMEM((2, MOE_TB * STAGE_PITCH, LANES), F32),
            pltpu.VMEM((SUB, LANES), F32),
            pltpu.SemaphoreType.DMA((2,)),
            pltpu.SemaphoreType.DMA(()),
        ],
        compiler_params=pltpu.CompilerParams(
            dimension_semantics=("arbitrary",), vmem_limit_bytes=V7X_VMEM_LIMIT),
        name="moe_dispatch",
    )(dest_flat, x, g.reshape(1, d))


def _moe_ffn_kernel(se_ref, r0_ref, nr_ref, nu_ref, xg_hbm, w1_ref, w3_ref, w2_ref, y_hbm,
                    x_scr, acc, stage, w1b, w3b, w2b, sem_in, sem_out):
    s = pl.program_id(0)
    j = pl.program_id(1)
    nj = pl.num_programs(1)
    rows = nr_ref[s]
    row0 = r0_ref[s]
    nsub = lax.shift_right_logical(rows + (SUB - 1), int(math.log2(SUB)))
    sub_rows = SUB * TOK_ROWS

    is_middle = (j > 0) & (j < nj - 1)

    def cast_weights():
        ws = (w1_ref[...].astype(BF16), w3_ref[...].astype(BF16), w2_ref[...].astype(BF16))
        w1b[...], w3b[...], w2b[...] = ws
        return ws

    @pl.when((nsub > 0) & jnp.logical_not(is_middle))
    def _():
        cast_weights()

    def sub(r):
        return pl.ds(pl.multiple_of(r * SUB, SUB), SUB)

    def swiglu(xs, ws=None):
        w1, w3, w2 = ws if ws is not None else (w1b[...], w3b[...], w2b[...])
        a = jnp.dot(xs, w1, preferred_element_type=F32)
        b = jnp.dot(xs, w3, preferred_element_type=F32)
        u = (_silu(a) * b).astype(BF16)
        return jnp.dot(u, w2, preferred_element_type=F32)

    def tile(r):
        return swiglu(x_scr[sub(r), :])

    def in_copy(r, slot):
        first = pl.multiple_of((row0 + r * SUB) * TOK_ROWS, TOK_ROWS)
        return pltpu.make_async_copy(xg_hbm.at[pl.ds(first, sub_rows)], stage.at[slot], sem_in.at[slot])

    @pl.when(j == 0)
    def _():
        @pl.when(nsub > 0)
        def _():
            in_copy(0, 0).start()

        def body(r, carry):
            slot = r % 2
            in_copy(r, slot).wait()

            @pl.when(r + 1 < nsub)
            def _():
                in_copy(r + 1, 1 - slot).start()

            for a in range(TOK_ROWS):
                x_scr[sub(r), pl.ds(a * LANES, LANES)] = stage[slot, pl.ds(a, SUB, stride=TOK_ROWS), :].astype(BF16)
            acc[sub(r), :] = tile(r)
            return carry

        lax.fori_loop(0, nsub, body, 0)

        def clear(r, carry):
            x_scr[sub(r), :] = jnp.zeros((SUB, D_MODEL), BF16)
            acc[sub(r), :] = jnp.zeros((SUB, D_MODEL), F32)
            return carry

        lax.fori_loop(nsub, SB // SUB, clear, 0)

    @pl.when(is_middle)
    def _():
        for c in range(SB // CHUNK_ROWS):
            @pl.when(rows > c * CHUNK_ROWS)
            def _():
                ws = cast_weights() if c == 0 else None
                rows_c = pl.ds(c * CHUNK_ROWS, CHUNK_ROWS)
                acc[rows_c, :] += swiglu(x_scr[rows_c, :], ws)

    @pl.when(j == nj - 1)
    def _():
        def out_copies(r, act):
            nv = jnp.clip(rows - r * SUB, 0, SUB)
            bit = SUB
            while bit >= 1:
                off = nv & (-2 * bit)
                src = stage.at[0, pl.ds(pl.multiple_of(off * TOK_ROWS, TOK_ROWS), bit * TOK_ROWS)]
                first = pl.multiple_of((row0 + r * SUB + off) * TOK_ROWS, TOK_ROWS)
                cp = pltpu.make_async_copy(src, y_hbm.at[pl.ds(first, bit * TOK_ROWS)], sem_out)
                pl.when((nv & bit) != 0)(functools.partial(act, cp))
                bit //= 2

        def body(r, carry):
            y = acc[sub(r), :] + tile(r)

            @pl.when(r > 0)
            def _():
                out_copies(r - 1, lambda cp: cp.wait())

            _to_token_major(stage.at[0], y)
            out_copies(r, lambda cp: cp.start())
            return carry

        lax.fori_loop(0, nsub, body, 0)

        @pl.when(nsub > 0)
        def _():
            out_copies(nsub - 1, lambda cp: cp.wait())


def _moe_ffn(xg, w1, w3, w2, sb_e, sb_row0, sb_rows, n_used):
    d = w1.shape[1]
    f = w1.shape[2]
    nj = f // TF

    def jmap(s, j, nu):
        return jnp.where(s < nu[0], j, nj - 1)

    return pl.pallas_call(
        _moe_ffn_kernel,
        grid_spec=pltpu.PrefetchScalarGridSpec(
            num_scalar_prefetch=4,
            grid=(NS_MAX, nj),
            in_specs=[
                pl.BlockSpec(memory_space=pl.ANY),
                pl.BlockSpec((None, d, TF), lambda s, j, se, r0, nr, nu: (se[s], 0, jmap(s, j, nu))),
                pl.BlockSpec((None, d, TF), lambda s, j, se, r0, nr, nu: (se[s], 0, jmap(s, j, nu))),
                pl.BlockSpec((None, TF, d), lambda s, j, se, r0, nr, nu: (se[s], jmap(s, j, nu), 0)),
            ],
            out_specs=pl.BlockSpec(memory_space=pl.ANY),
            scratch_shapes=[
                pltpu.VMEM((SB, d), BF16),
                pltpu.VMEM((SB, d), F32),
                pltpu.VMEM((2, SUB * TOK_ROWS, LANES), F32),
                pltpu.VMEM((d, TF), BF16),
                pltpu.VMEM((d, TF), BF16),
                pltpu.VMEM((TF, d), BF16),
                pltpu.SemaphoreType.DMA((2,)),
                pltpu.SemaphoreType.DMA(()),
            ],
        ),
        out_shape=jax.ShapeDtypeStruct((N_SLOTS * TOK_ROWS, LANES), F32),
        compiler_params=pltpu.CompilerParams(
            dimension_semantics=("arbitrary", "arbitrary"), vmem_limit_bytes=V7X_VMEM_LIMIT),
        name="moe_ffn",
    )(sb_e, sb_row0, sb_rows, n_used, xg, w1, w3, w2)


N_TB_PROMPT = T_PROMPT // MOE_TB


def _combine_kernel(d_ref, dn_ref, x_ref, p_ref, g_ref, y_hbm, op_ref, os_ref, stage, sem, *, final_norm):
    i = pl.program_id(0)
    n = pl.num_programs(0)
    slot = i % 2
    blk_rows = MOE_TB * TOK_ROWS

    def start_block(dref, sl):
        def body(r, carry):
            for k in range(TOP_K):
                row = dref[0, 0, TOP_K * r + k]
                src = y_hbm.at[pl.ds(pl.multiple_of(row * TOK_ROWS, TOK_ROWS), TOK_ROWS)]
                dst = stage.at[sl, k, pl.ds(pl.multiple_of(r * STAGE_PITCH, 8), TOK_ROWS)]
                pltpu.make_async_copy(src, dst, sem.at[sl]).start()
            return carry

        lax.fori_loop(0, MOE_TB, body, 0, unroll=DMA_UNROLL)

    @pl.when(i == 0)
    def _():
        start_block(d_ref, 0)

    @pl.when(i + 1 < n)
    def _():
        start_block(dn_ref, 1 - slot)

    for k in range(TOP_K):
        pltpu.make_async_copy(y_hbm.at[pl.ds(0, blk_rows)], stage.at[slot, k, pl.ds(0, blk_rows)], sem.at[slot]).wait()
    g0 = p_ref[:, 0:1]
    g1 = p_ref[:, 1:2]
    pieces = []
    for a in range(TOK_ROWS):
        y0 = stage[slot, 0, pl.ds(a, MOE_TB, stride=STAGE_PITCH), :]
        y1 = stage[slot, 1, pl.ds(a, MOE_TB, stride=STAGE_PITCH), :]
        pieces.append(x_ref[:, pl.ds(a * LANES, LANES)] + (y0 * g0 + y1 * g1))
    y = jnp.concatenate(pieces, axis=1)
    if final_norm:
        y = _rms_rows(y, g_ref[...])

    @pl.when(i < N_TB_PROMPT)
    def _():
        op_ref[...] = y

    @pl.when(i >= N_TB_PROMPT)
    def _():
        os_ref[...] = y


def _moe_combine(x, y, dest_flat, gate, g, final_norm):
    t, d = x.shape
    return pl.pallas_call(
        functools.partial(_combine_kernel, final_norm=final_norm),
        grid=(N_TB,),
        in_specs=[
            pl.BlockSpec((1, 1, TOP_K * MOE_TB), lambda i: (i, 0, 0), memory_space=pltpu.SMEM),
            pl.BlockSpec((1, 1, TOP_K * MOE_TB), lambda i: (jnp.minimum(i + 1, N_TB - 1), 0, 0),
                         memory_space=pltpu.SMEM),
            pl.BlockSpec((MOE_TB, d), lambda i: (i, 0)),
            pl.BlockSpec((MOE_TB, N_EXPERTS), lambda i: (i, 0)),
            pl.BlockSpec((1, d), lambda i: (0, 0)),
            pl.BlockSpec(memory_space=pl.ANY),
        ],
        out_specs=[
            pl.BlockSpec((MOE_TB, d), lambda i: (jnp.minimum(i, N_TB_PROMPT - 1), 0)),
            pl.BlockSpec((MOE_TB, d), lambda i: (jnp.maximum(i - N_TB_PROMPT, 0), 0)),
        ],
        out_shape=[
            jax.ShapeDtypeStruct((T_PROMPT, d), F32),
            jax.ShapeDtypeStruct((T_SAMPLE, d), F32),
        ],
        scratch_shapes=[
            pltpu.VMEM((2, TOP_K, MOE_TB * STAGE_PITCH, LANES), F32),
            pltpu.SemaphoreType.DMA((2,)),
        ],
        compiler_params=pltpu.CompilerParams(
            dimension_semantics=("arbitrary",), vmem_limit_bytes=V7X_VMEM_LIMIT),
        name="moe_combine",
    )(dest_flat, dest_flat, x, gate, g.reshape(1, d), y)


def _moe(x, norm_g, rw, w1, w3, w2, out_g, final_norm):
    eid, gate, cnt = _router(x, norm_g, rw)
    base, sb_e, sb_row0, sb_rows, n_used = _expert_layout(cnt)
    dest = _dest_rows(eid, base)
    dest_flat = dest[:, :TOP_K].reshape(N_TB, 1, TOP_K * MOE_TB)
    xg = _dispatch(x, norm_g, dest_flat)
    y = _moe_ffn(xg, w1, w3, w2, sb_e, sb_row0, sb_rows, n_used)
    return _moe_combine(x, y, dest_flat, gate, out_g, final_norm)


def kernel(x_prompt, x_sample, state_ret, state_conv31, state_conv3, mix_norm_g, w_in, ret_norm_g, conv31_w, conv31_b, conv_ln_g, conv_ln_b, sconv_w, sgu_ln_g, sgu_ln_b, sgu_w, sgu_b, w_out, ffn_norm_g, dense_w1, dense_w3, dense_w2, router_w, moe_w1, moe_w3, moe_w2, final_norm_g):
    assert DEPTH == 2, "layer 0 has the dense channel mixer; layer 1's expert mixer ends with the final norm"
    x, hn = _prep(x_prompt.reshape(T_PROMPT, D_MODEL), x_sample.reshape(T_SAMPLE, D_MODEL), mix_norm_g[0])
    rets_p, rets_s, c31_p, c31_s, c3_p, c3_s, vns = [], [], [], [], [], [], []
    for l in range(DEPTH):
        p = dict(ret_norm_g=ret_norm_g[l], conv31_w=conv31_w[l], conv31_b=conv31_b[l],
                 conv_ln_g=conv_ln_g[l], conv_ln_b=conv_ln_b[l], sconv_w=sconv_w[l],
                 sgu_ln_g=sgu_ln_g[l], sgu_ln_b=sgu_ln_b[l], sgu_w=sgu_w[l], sgu_b=sgu_b[l])
        z = _in_proj(hn, w_in, l)
        mix_s, ret_s, b31_s, b3_s, vn_s = _mixer_sample(z, p, l, state_ret, state_conv31, state_conv3)
        mix_p, ret_p, b31_p, b3_p = _mixer_prompt(z, p)
        x = _out_proj_residual(mix_p, mix_s, w_out, l, x)
        j = l // 2
        if l % 2 == 0:
            x, hn = _ffn_dense(x, ffn_norm_g[l], dense_w1[j], dense_w3[j], dense_w2[j], mix_norm_g[l + 1])
        else:
            y_p, y_s = _moe(x, ffn_norm_g[l], router_w[j], moe_w1[j], moe_w3[j], moe_w2[j],
                            final_norm_g, final_norm=True)
        rets_p.append(ret_p); rets_s.append(ret_s)
        c31_p.append(b31_p); c31_s.append(b31_s)
        c3_p.append(b3_p); c3_s.append(b3_s)
        vns.append(vn_s)
    y_prompt = y_p.reshape(BATCH, SEQ, D_MODEL)
    y_sample = y_s.reshape(DEC_BATCH, DEC_SEQ, D_MODEL)
    return (y_prompt, y_sample, jnp.stack(rets_p), jnp.stack(rets_s), jnp.stack(c31_p), jnp.stack(c31_s),
            jnp.stack(c3_p), jnp.stack(c3_s), jnp.stack(vns))
```

```python
import functools
import math

import jax
import jax.numpy as jnp
from jax import lax
from jax.experimental import pallas as pl
from jax.experimental.pallas import tpu as pltpu

F32 = jnp.float32
BF16 = jnp.bfloat16

D_MODEL = 2048
BATCH = 4
SEQ = 2048
DEPTH = 2
DEC_BATCH = 128
DEC_SEQ = 4
PAST_LEN = 16384
GROUP_W = 512
D_IN = 11 * GROUP_W
RET_HEADS = 4
RET_D = 128
CHUNK = 128
ROPE_THETA = 10000.0
CONV_W = 31
SC_W = 3
SGU_GROUPS = 4
D_FF = 5632
N_EXPERTS = 8
TOP_K = 2
EPS = 1e-6

T_PROMPT = BATCH * SEQ
T_SAMPLE = DEC_BATCH * DEC_SEQ
T_ALL = T_PROMPT + T_SAMPLE
N_CHUNKS = SEQ // CHUNK

C_Q, C_K, C_V, C_G, C_BL, C_BG, C_CB, C_CC, C_CH, C_DU, C_DV = (i * GROUP_W for i in range(11))

V7X_VMEM_LIMIT = 56 * 1024 * 1024

TM = 1088
TM_IN = 2176
TM_OUT = 512
N_OUT_PROMPT = T_PROMPT // TM_OUT
NORM_ROWS = 272
TN_IN = 512
TF = 256
SAMPLE_BT = 8

LANES = 128
SUBLANES = 8
TOK_ROWS = D_MODEL // LANES
MOE_TB = 256
N_TB = T_ALL // MOE_TB
N_SLOTS = T_ALL * TOP_K
SUB = 256
SB = 9 * SUB
CHUNK_ROWS = SB // 3
NS_MAX = N_SLOTS // SB + N_EXPERTS


def _rms_rows(x, g):
    ms = jnp.mean(x * x, axis=-1, keepdims=True)
    return (x * lax.rsqrt(ms + EPS)) * g


def _layernorm_rows(x, g, b):
    mu = jnp.mean(x, axis=-1, keepdims=True)
    xc = x - mu
    return xc * lax.rsqrt(jnp.mean(xc * xc, axis=-1, keepdims=True) + EPS) * g + b


def _silu(x):
    return x * jax.nn.sigmoid(x)


def _norm_block_to(x_ref, g_ref, dst_ref):
    rows = x_ref.shape[0]
    g = g_ref[...]

    def body(c, carry):
        r0 = pl.multiple_of(c * NORM_ROWS, NORM_ROWS)
        x = x_ref[pl.ds(r0, NORM_ROWS), :]
        dst_ref[pl.ds(r0, NORM_ROWS), :] = _rms_rows(x, g).astype(dst_ref.dtype)
        return carry

    lax.fori_loop(0, rows // NORM_ROWS, body, 0)


PREP_ROWS = 512
N_PREP_PROMPT = T_PROMPT // PREP_ROWS


def _prep_kernel(xp_ref, xs_ref, g_ref, x_ref, hn_ref):
    def emit(src_ref):
        x = src_ref[...]
        x_ref[...] = x
        hn_ref[...] = _rms_rows(x, g_ref[...]).astype(hn_ref.dtype)

    is_prompt = pl.program_id(0) < N_PREP_PROMPT
    pl.when(is_prompt)(functools.partial(emit, xp_ref))
    pl.when(jnp.logical_not(is_prompt))(functools.partial(emit, xs_ref))


def _prep(x_prompt, x_sample, g):
    d = x_prompt.shape[1]
    blk = lambda f: pl.BlockSpec((PREP_ROWS, d), f)
    return pl.pallas_call(
        _prep_kernel,
        grid=(T_ALL // PREP_ROWS,),
        in_specs=[
            blk(lambda i: (jnp.minimum(i, N_PREP_PROMPT - 1), 0)),
            blk(lambda i: (jnp.maximum(i - N_PREP_PROMPT, 0), 0)),
            pl.BlockSpec((1, d), lambda i: (0, 0)),
        ],
        out_specs=[blk(lambda i: (i, 0)), blk(lambda i: (i, 0))],
        out_shape=[jax.ShapeDtypeStruct((T_ALL, d), F32), jax.ShapeDtypeStruct((T_ALL, d), BF16)],
        compiler_params=pltpu.CompilerParams(
            dimension_semantics=("arbitrary",), vmem_limit_bytes=V7X_VMEM_LIMIT),
        name="prep_rows",
    )(x_prompt, x_sample, g.reshape(1, d))


def _in_proj_kernel(h_ref, w_ref, o_ref):
    o_ref[...] = jnp.dot(h_ref[...], w_ref[...].astype(BF16), preferred_element_type=F32)


def _in_proj(hn, w_all, layer):
    t, d = hn.shape
    n = w_all.shape[2]
    return pl.pallas_call(
        _in_proj_kernel,
        grid=(t // TM_IN, n // TN_IN),
        in_specs=[
            pl.BlockSpec((TM_IN, d), lambda i, j: (i, 0)),
            pl.BlockSpec((None, d, TN_IN), lambda i, j: (layer, 0, j)),
        ],
        out_specs=pl.BlockSpec((TM_IN, TN_IN), lambda i, j: (i, j)),
        out_shape=jax.ShapeDtypeStruct((t, n), F32),
        compiler_params=pltpu.CompilerParams(
            dimension_semantics=("arbitrary", "arbitrary"), vmem_limit_bytes=V7X_VMEM_LIMIT),
        name="in_proj",
    )(hn, w_all)


W_LOAD_ROWS = 512


def _out_proj_kernel(ap_ref, as_ref, w_hbm, r_ref, o_ref, wb, stage, sem, *, layer):
    i = pl.program_id(0)

    @pl.when(i == 0)
    def _():
        for c in range(wb.shape[0] // W_LOAD_ROWS):
            rows = pl.ds(c * W_LOAD_ROWS, W_LOAD_ROWS)
            cp = pltpu.make_async_copy(w_hbm.at[layer, rows], stage, sem)
            cp.start()
            cp.wait()
            wb[rows, :] = stage[...].astype(BF16)

    def emit(a):
        o_ref[...] = r_ref[...] + jnp.dot(a, wb[...], preferred_element_type=F32)

    pl.when(i < N_OUT_PROMPT)(lambda: emit(ap_ref[...]))
    pl.when(i >= N_OUT_PROMPT)(lambda: emit(as_ref[...].astype(BF16)))


def _out_proj_residual(a_prompt, a_sample, w_all, layer, r):
    k = a_prompt.shape[1]
    t = r.shape[0]
    n = w_all.shape[2]
    return pl.pallas_call(
        functools.partial(_out_proj_kernel, layer=layer),
        grid=(t // TM_OUT,),
        in_specs=[
            pl.BlockSpec((TM_OUT, k), lambda i: (jnp.minimum(i, N_OUT_PROMPT - 1), 0)),
            pl.BlockSpec((TM_OUT, k), lambda i: (0, 0)),
            pl.BlockSpec(memory_space=pl.ANY),
            pl.BlockSpec((TM_OUT, n), lambda i: (i, 0)),
        ],
        out_specs=pl.BlockSpec((TM_OUT, n), lambda i: (i, 0)),
        out_shape=jax.ShapeDtypeStruct((t, n), F32),
        scratch_shapes=[
            pltpu.VMEM((k, n), BF16),
            pltpu.VMEM((W_LOAD_ROWS, n), F32),
            pltpu.SemaphoreType.DMA(()),
        ],
        compiler_params=pltpu.CompilerParams(
            dimension_semantics=("arbitrary",), vmem_limit_bytes=V7X_VMEM_LIMIT),
        name="out_proj_residual",
    )(a_prompt, a_sample, w_all, r)


def _gated(h, w1, w3):
    a = jnp.dot(h, w1, preferred_element_type=F32)
    b = jnp.dot(h, w3, preferred_element_type=F32)
    return (_silu(a) * b).astype(BF16)


def _stash_w2(w2b, w2_ref, j):
    w2b[pl.ds(pl.multiple_of((j & 1) * TF, TF), TF), :] = w2_ref[...].astype(BF16)


def _pair_out(u_even, u_odd, w2b):
    return jnp.dot(jnp.concatenate([u_even, u_odd], axis=1), w2b[...], preferred_element_type=F32)


def _ffn_kernel(x_hbm, g_ref, w1_ref, w3_ref, w2_ref, gn_ref, o_ref, hnext_ref, hn_ref, u_scr, w2b, sem):
    j = pl.program_id(1)

    @pl.when(j == 0)
    def _():
        r0 = pl.multiple_of(pl.program_id(0) * TM, TM)
        cp = pltpu.make_async_copy(x_hbm.at[pl.ds(r0, TM)], o_ref, sem)
        cp.start()
        cp.wait()
        _norm_block_to(o_ref, g_ref, hn_ref)

    u = _gated(hn_ref[...], w1_ref[...].astype(BF16), w3_ref[...].astype(BF16))
    _stash_w2(w2b, w2_ref, j)

    @pl.when((j & 1) == 0)
    def _():
        u_scr[...] = u

    @pl.when((j & 1) == 1)
    def _():
        o_ref[...] += _pair_out(u_scr[...], u, w2b)

    @pl.when(j == pl.num_programs(1) - 1)
    def _():
        _norm_block_to(o_ref, gn_ref, hnext_ref)


def _ffn_dense(x, g, w1, w3, w2, g_next):
    t, d = x.shape
    f = w1.shape[1]
    assert (f // TF) % 2 == 0, "d_ff tiles are consumed in pairs"
    return pl.pallas_call(
        _ffn_kernel,
        grid=(t // TM, f // TF),
        in_specs=[
            pl.BlockSpec(memory_space=pl.ANY),
            pl.BlockSpec((1, d), lambda i, j: (0, 0)),
            pl.BlockSpec((d, TF), lambda i, j: (0, j)),
            pl.BlockSpec((d, TF), lambda i, j: (0, j)),
            pl.BlockSpec((TF, d), lambda i, j: (j, 0)),
            pl.BlockSpec((1, d), lambda i, j: (0, 0)),
        ],
        out_specs=[
            pl.BlockSpec((TM, d), lambda i, j: (i, 0)),
            pl.BlockSpec((TM, d), lambda i, j: (i, 0)),
        ],
        out_shape=[jax.ShapeDtypeStruct((t, d), F32), jax.ShapeDtypeStruct((t, d), BF16)],
        scratch_shapes=[
            pltpu.VMEM((TM, d), BF16),
            pltpu.VMEM((TM, TF), BF16),
            pltpu.VMEM((2 * TF, d), BF16),
            pltpu.SemaphoreType.DMA(()),
        ],
        compiler_params=pltpu.CompilerParams(
            dimension_semantics=("arbitrary", "arbitrary"), vmem_limit_bytes=V7X_VMEM_LIMIT),
        name="ffn_dense",
    )(x, g.reshape(1, d), w1, w3, w2, g_next.reshape(1, d))


def _rotary_tables(length, pos0):
    half = RET_D // 2
    inv = ROPE_THETA ** (-jnp.arange(half, dtype=F32) / half)
    pos = jnp.arange(length, dtype=F32) + pos0
    ang = pos[:, None] * inv[None, :]
    cos, sin = jnp.cos(ang), jnp.sin(ang)
    return jnp.concatenate([cos, cos], axis=-1), jnp.concatenate([-sin, sin], axis=-1)


def _decay_tables(c):
    log_g = jnp.log1p(-jnp.exp2(-5.0 - jnp.arange(RET_HEADS, dtype=F32)))
    idx = jnp.arange(c, dtype=F32)
    rel = idx[:, None] - idx[None, :]
    d_in = jnp.where(rel >= 0, jnp.exp(log_g[:, None, None] * jnp.maximum(rel, 0.0)), 0.0)
    d_q = jnp.exp(log_g[:, None] * (idx + 1.0)[None, :])
    d_k = jnp.exp(log_g[:, None] * (c - 1.0 - idx)[None, :])
    d_c = jnp.exp(log_g * c)
    return d_in, d_q, d_k, d_c


HIST31 = 32
HIST3 = 8


def _mixer_prompt_kernel(z_ref, cos_ref, sin_ref, din_ref, dq_ref, dk_ref, dc_ref, retg_ref,
                         c31w_ref, c31b_ref, clng_ref, clnb_ref, scw_ref, slng_ref, slnb_ref,
                         sguw_ref, sgub_ref,
                         mix_ref, ret_ref, b31_ref, b3_ref,
                         s_scr, ext31_scr, ext3_scr, ysh_scr):
    c = pl.program_id(1)

    @pl.when(c == 0)
    def _():
        s_scr[...] = jnp.zeros_like(s_scr)
        ext31_scr[pl.ds(0, HIST31), :] = jnp.zeros((HIST31, GROUP_W), F32)
        ext3_scr[pl.ds(0, HIST3), :] = jnp.zeros((HIST3, GROUP_W), F32)

    cos = cos_ref[...]
    sin = sin_ref[...]

    for h in range(RET_HEADS):
        lo = h * RET_D
        q = z_ref[:, pl.ds(C_Q + lo, RET_D)]
        k = z_ref[:, pl.ds(C_K + lo, RET_D)]
        v = z_ref[:, pl.ds(C_V + lo, RET_D)]
        g = z_ref[:, pl.ds(C_G + lo, RET_D)]
        q = q * cos + pltpu.roll(q, RET_D // 2, axis=1) * sin
        k = (k * cos + pltpu.roll(k, RET_D // 2, axis=1) * sin) * (RET_D ** -0.5)
        qb = q.astype(BF16)
        vb = v.astype(BF16)
        scores = lax.dot_general(qb, k.astype(BF16), (((1,), (1,)), ((), ())),
                                 preferred_element_type=F32) * din_ref[h]
        inner = jnp.dot(scores.astype(BF16), vb, preferred_element_type=F32)
        s_old = s_scr[h]
        cross = jnp.dot(qb, s_old.astype(BF16), preferred_element_type=F32) * dq_ref[h]
        kd = (k * dk_ref[h]).astype(BF16)
        s_scr[h] = s_old * dc_ref[h] + lax.dot_general(kd, vb, (((0,), (0,)), ((), ())),
                                                       preferred_element_type=F32)
        r = inner + cross
        r = r * lax.rsqrt(jnp.mean(r * r, axis=-1, keepdims=True) + EPS)
        r = r * retg_ref[:, pl.ds(lo, RET_D)]
        mix_ref[:, pl.ds(lo, RET_D)] = (_silu(g) * r).astype(mix_ref.dtype)

    glu = z_ref[:, pl.ds(C_BL, GROUP_W)] * jax.nn.sigmoid(z_ref[:, pl.ds(C_BG, GROUP_W)])
    ext31_scr[pl.ds(HIST31, CHUNK), :] = glu
    s0 = HIST31 - (CONV_W - 1)
    conv = None
    for i in range(SUBLANES):
        n_rows = CHUNK if i == 0 else CHUNK + SUBLANES
        y = None
        for m in range(HIST31 // SUBLANES + 1):
            j = SUBLANES * m + i - s0
            if 0 <= j < CONV_W:
                term = ext31_scr[pl.ds(SUBLANES * m, n_rows), :] * c31w_ref[pl.ds(j, 1), :]
                y = term if y is None else y + term
        if i == 0:
            conv = y
        else:
            ysh_scr[i - 1] = y
            conv = conv + ysh_scr[i - 1, pl.ds(i, CHUNK), :]
    conv = conv + c31b_ref[...]
    out_b = _silu(_layernorm_rows(conv, clng_ref[...], clnb_ref[...]))
    mix_ref[:, pl.ds(GROUP_W, GROUP_W)] = out_b.astype(mix_ref.dtype)
    ext31_scr[pl.ds(0, HIST31), :] = ext31_scr[pl.ds(CHUNK, HIST31), :]

    gated = z_ref[:, pl.ds(C_CC, GROUP_W)] * z_ref[:, pl.ds(C_CH, GROUP_W)]
    ext3_scr[pl.ds(HIST3, CHUNK), :] = gated
    conv3 = ext3_scr[pl.ds(HIST3 - (SC_W - 1), CHUNK), :] * scw_ref[pl.ds(0, 1), :]
    for j in range(1, SC_W):
        conv3 = conv3 + ext3_scr[pl.ds(HIST3 - (SC_W - 1) + j, CHUNK), :] * scw_ref[pl.ds(j, 1), :]
    mix_ref[:, pl.ds(2 * GROUP_W, GROUP_W)] = (z_ref[:, pl.ds(C_CB, GROUP_W)] * conv3).astype(mix_ref.dtype)
    ext3_scr[pl.ds(0, HIST3), :] = ext3_scr[pl.ds(CHUNK, HIST3), :]

    vn = _layernorm_rows(z_ref[:, pl.ds(C_DV, GROUP_W)], slng_ref[...], slnb_ref[...])
    row = lax.broadcasted_iota(jnp.int32, (CHUNK, CHUNK), 0)
    col = lax.broadcasted_iota(jnp.int32, (CHUNK, CHUNK), 1)
    ch = GROUP_W // SGU_GROUPS
    for gi in range(SGU_GROUPS):
        wm = jnp.where(row >= col, sguw_ref[gi], 0.0).astype(BF16)
        mixed = jnp.dot(wm, vn[:, gi * ch:(gi + 1) * ch].astype(BF16), preferred_element_type=F32)
        mixed = mixed + sgub_ref[:, pl.ds(gi, 1)]
        du = z_ref[:, pl.ds(C_DU + gi * ch, ch)]
        mix_ref[:, pl.ds(3 * GROUP_W + gi * ch, ch)] = (du * mixed).astype(mix_ref.dtype)

    @pl.when(c == pl.num_programs(1) - 1)
    def _():
        ret_ref[...] = s_scr[...]
        b31_ref[...] = ext31_scr[pl.ds(HIST31 - (CONV_W - 1), CONV_W - 1), :]
        b3_ref[...] = ext3_scr[pl.ds(HIST3 - (SC_W - 1), SC_W - 1), :]


def _mixer_prompt(z, p):
    cos, sin = _rotary_tables(SEQ, 0)
    d_in, d_q, d_k, d_c = _decay_tables(CHUNK)
    dq_t = jnp.broadcast_to(d_q[:, :, None], (RET_HEADS, CHUNK, RET_D))
    dk_t = jnp.broadcast_to(d_k[:, :, None], (RET_HEADS, CHUNK, RET_D))
    dc_t = jnp.broadcast_to(d_c[:, None, None], (RET_HEADS, RET_D, RET_D))

    def full(shape):
        return pl.BlockSpec(shape, lambda b, c: (0,) * len(shape))

    row = lambda a: a.reshape(1, -1)
    return pl.pallas_call(
        _mixer_prompt_kernel,
        grid=(BATCH, N_CHUNKS),
        in_specs=[
            pl.BlockSpec((CHUNK, D_IN), lambda b, c: (b * N_CHUNKS + c, 0)),
            pl.BlockSpec((CHUNK, RET_D), lambda b, c: (c, 0)),
            pl.BlockSpec((CHUNK, RET_D), lambda b, c: (c, 0)),
            full((RET_HEADS, CHUNK, CHUNK)),
            full((RET_HEADS, CHUNK, RET_D)),
            full((RET_HEADS, CHUNK, RET_D)),
            full((RET_HEADS, RET_D, RET_D)),
            full((1, GROUP_W)),
            full((CONV_W, GROUP_W)),
            full((1, GROUP_W)),
            full((1, GROUP_W)),
            full((1, GROUP_W)),
            full((SC_W, GROUP_W)),
            full((1, GROUP_W)),
            full((1, GROUP_W)),
            full((SGU_GROUPS, CHUNK, CHUNK)),
            full((CHUNK, SGU_GROUPS)),
        ],
        out_specs=[
            pl.BlockSpec((CHUNK, D_MODEL), lambda b, c: (b * N_CHUNKS + c, 0)),
            pl.BlockSpec((None, RET_HEADS, RET_D, RET_D), lambda b, c: (b, 0, 0, 0)),
            pl.BlockSpec((None, CONV_W - 1, GROUP_W), lambda b, c: (b, 0, 0)),
            pl.BlockSpec((None, SC_W - 1, GROUP_W), lambda b, c: (b, 0, 0)),
        ],
        out_shape=[
            jax.ShapeDtypeStruct((T_PROMPT, D_MODEL), BF16),
            jax.ShapeDtypeStruct((BATCH, RET_HEADS, RET_D, RET_D), F32),
            jax.ShapeDtypeStruct((BATCH, CONV_W - 1, GROUP_W), F32),
            jax.ShapeDtypeStruct((BATCH, SC_W - 1, GROUP_W), F32),
        ],
        scratch_shapes=[
            pltpu.VMEM((RET_HEADS, RET_D, RET_D), F32),
            pltpu.VMEM((HIST31 + CHUNK, GROUP_W), F32),
            pltpu.VMEM((HIST3 + CHUNK, GROUP_W), F32),
            pltpu.VMEM((SUBLANES - 1, CHUNK + SUBLANES, GROUP_W), F32),
        ],
        compiler_params=pltpu.CompilerParams(
            dimension_semantics=("arbitrary", "arbitrary"), vmem_limit_bytes=V7X_VMEM_LIMIT),
        name="mixer_prompt",
    )(z, cos, sin, d_in, dq_t, dk_t, dc_t, row(p["ret_norm_g"]), p["conv31_w"], row(p["conv31_b"]),
      row(p["conv_ln_g"]), row(p["conv_ln_b"]), p["sconv_w"], row(p["sgu_ln_g"]), row(p["sgu_ln_b"]),
      p["sgu_w"], p["sgu_b"].T)


EXT31_S = 40
EXT3_S = 8


def _mixer_sample_kernel(z_ref, cos_ref, sin_ref, din_ref, dq_ref, dk_ref, dc_ref, retg_ref,
                         c31w_ref, c31b_ref, clng_ref, clnb_ref, scw_ref, slng_ref, slnb_ref,
                         sguw_ref, sgub_ref, sret_ref, s31_ref, s3_ref,
                         mix_ref, ret_ref, b31_ref, b3_ref, vn_ref,
                         ext31_scr, ext3_scr):
    cos = cos_ref[...]
    sin = sin_ref[...]
    n_hist31 = CONV_W - 1
    n_hist3 = SC_W - 1
    bt = sret_ref.shape[0]

    def zs(col, width):
        return jnp.stack([z_ref[pl.ds(DEC_SEQ * b, DEC_SEQ), pl.ds(col, width)] for b in range(bt)])

    def zrow(s, col, width):
        return jnp.stack([z_ref[pl.ds(DEC_SEQ * b + s, 1), pl.ds(col, width)] for b in range(bt)])

    def put(col, val):
        for b in range(bt):
            mix_ref[pl.ds(DEC_SEQ * b, DEC_SEQ), pl.ds(col, val.shape[-1])] = val[b]

    for h in range(RET_HEADS):
        lo = h * RET_D
        q = zs(C_Q + lo, RET_D)
        k = zs(C_K + lo, RET_D)
        g = zs(C_G + lo, RET_D)
        q = q * cos + pltpu.roll(q, RET_D // 2, axis=2) * sin
        k = (k * cos + pltpu.roll(k, RET_D // 2, axis=2) * sin) * (RET_D ** -0.5)
        s_old = sret_ref[:, h]
        r = lax.dot_general(q, s_old, (((2,), (1,)), ((0,), (0,))),
                            preferred_element_type=F32) * dq_ref[h]
        for s in range(DEC_SEQ):
            k_s = zrow(s, C_K + lo, RET_D)
            k_s = (k_s * cos[s:s + 1] + pltpu.roll(k_s, RET_D // 2, axis=2) * sin[s:s + 1]) * (RET_D ** -0.5)
            v_s = zrow(s, C_V + lo, RET_D)
            score = jnp.sum(q * k_s, axis=-1, keepdims=True) * din_ref[h, s]
            r = r + score * v_s
        kd = k * dk_ref[h]
        v = zs(C_V + lo, RET_D)
        ret_ref[:, h] = s_old * dc_ref[h] + lax.dot_general(
            kd, v, (((1,), (1,)), ((0,), (0,))), preferred_element_type=F32)
        r = r * lax.rsqrt(jnp.mean(r * r, axis=-1, keepdims=True) + EPS)
        r = r * retg_ref[:, pl.ds(lo, RET_D)]
        put(lo, _silu(g) * r)

    glu = zs(C_BL, GROUP_W) * jax.nn.sigmoid(zs(C_BG, GROUP_W))
    ext31_scr[:, pl.ds(0, n_hist31), :] = s31_ref[...]
    ext31_scr[:, pl.ds(n_hist31, DEC_SEQ), :] = glu
    conv = ext31_scr[:, pl.ds(0, DEC_SEQ), :] * c31w_ref[pl.ds(0, 1), :]
    for j in range(1, CONV_W):
        conv = conv + ext31_scr[:, pl.ds(j, DEC_SEQ), :] * c31w_ref[pl.ds(j, 1), :]
    conv = conv + c31b_ref[...]
    put(GROUP_W, _silu(_layernorm_rows(conv, clng_ref[...], clnb_ref[...])))
    b31_ref[...] = ext31_scr[:, pl.ds(DEC_SEQ, n_hist31), :]

    gated = zs(C_CC, GROUP_W) * zs(C_CH, GROUP_W)
    ext3_scr[:, pl.ds(0, n_hist3), :] = s3_ref[...]
    ext3_scr[:, pl.ds(n_hist3, DEC_SEQ), :] = gated
    conv3 = ext3_scr[:, pl.ds(0, DEC_SEQ), :] * scw_ref[pl.ds(0, 1), :]
    for j in range(1, SC_W):
        conv3 = conv3 + ext3_scr[:, pl.ds(j, DEC_SEQ), :] * scw_ref[pl.ds(j, 1), :]
    put(2 * GROUP_W, zs(C_CB, GROUP_W) * conv3)
    b3_ref[...] = ext3_scr[:, pl.ds(DEC_SEQ, n_hist3), :]

    vn = _layernorm_rows(zs(C_DV, GROUP_W), slng_ref[...], slnb_ref[...])
    vn_ref[...] = vn
    mixed = sgub_ref[...]
    for s in range(DEC_SEQ):
        mixed = mixed + sguw_ref[s] * vn_ref[:, pl.ds(s, 1), :]
    put(3 * GROUP_W, zs(C_DU, GROUP_W) * mixed)


def _mixer_sample(z, p, layer, s_ret, s_31, s_3):
    cos, sin = _rotary_tables(DEC_SEQ, PAST_LEN)
    d_in, d_q, d_k, d_c = _decay_tables(DEC_SEQ)
    din_t = jnp.swapaxes(d_in, 1, 2)[:, :, :, None]
    dq_t = jnp.broadcast_to(d_q[:, :, None], (RET_HEADS, DEC_SEQ, RET_D))
    dk_t = jnp.broadcast_to(d_k[:, :, None], (RET_HEADS, DEC_SEQ, RET_D))
    dc_t = jnp.broadcast_to(d_c[:, None, None], (RET_HEADS, RET_D, RET_D))
    ch = GROUP_W // SGU_GROUPS
    w4 = jnp.tril(p["sgu_w"][:, :DEC_SEQ, :DEC_SEQ])
    w_t = jnp.repeat(jnp.transpose(w4, (2, 1, 0)), ch, axis=-1)
    b_t = jnp.repeat(p["sgu_b"][:, :DEC_SEQ].T, ch, axis=-1)
    bt = SAMPLE_BT

    def full(shape):
        return pl.BlockSpec(shape, lambda i: (0,) * len(shape))

    row = lambda a: a.reshape(1, -1)
    return pl.pallas_call(
        _mixer_sample_kernel,
        grid=(DEC_BATCH // bt,),
        in_specs=[
            pl.BlockSpec((bt * DEC_SEQ, D_IN), lambda i: (T_PROMPT // (bt * DEC_SEQ) + i, 0)),
            full((DEC_SEQ, RET_D)),
            full((DEC_SEQ, RET_D)),
            full((RET_HEADS, DEC_SEQ, DEC_SEQ, 1)),
            full((RET_HEADS, DEC_SEQ, RET_D)),
            full((RET_HEADS, DEC_SEQ, RET_D)),
            full((RET_HEADS, RET_D, RET_D)),
            full((1, GROUP_W)),
            full((CONV_W, GROUP_W)),
            full((1, GROUP_W)),
            full((1, GROUP_W)),
            full((1, GROUP_W)),
            full((SC_W, GROUP_W)),
            full((1, GROUP_W)),
            full((1, GROUP_W)),
            full((DEC_SEQ, DEC_SEQ, GROUP_W)),
            full((DEC_SEQ, GROUP_W)),
            pl.BlockSpec((None, bt, RET_HEADS, RET_D, RET_D), lambda i: (layer, i, 0, 0, 0)),
            pl.BlockSpec((None, bt, CONV_W - 1, GROUP_W), lambda i: (layer, i, 0, 0)),
            pl.BlockSpec((None, bt, SC_W - 1, GROUP_W), lambda i: (layer, i, 0, 0)),
        ],
        out_specs=[
            pl.BlockSpec((bt * DEC_SEQ, D_MODEL), lambda i: (i, 0)),
            pl.BlockSpec((bt, RET_HEADS, RET_D, RET_D), lambda i: (i, 0, 0, 0)),
            pl.BlockSpec((bt, CONV_W - 1, GROUP_W), lambda i: (i, 0, 0)),
            pl.BlockSpec((bt, SC_W - 1, GROUP_W), lambda i: (i, 0, 0)),
            pl.BlockSpec((bt, DEC_SEQ, GROUP_W), lambda i: (i, 0, 0)),
        ],
        out_shape=[
            jax.ShapeDtypeStruct((T_SAMPLE, D_MODEL), F32),
            jax.ShapeDtypeStruct((DEC_BATCH, RET_HEADS, RET_D, RET_D), F32),
            jax.ShapeDtypeStruct((DEC_BATCH, CONV_W - 1, GROUP_W), F32),
            jax.ShapeDtypeStruct((DEC_BATCH, SC_W - 1, GROUP_W), F32),
            jax.ShapeDtypeStruct((DEC_BATCH, DEC_SEQ, GROUP_W), F32),
        ],
        scratch_shapes=[
            pltpu.VMEM((bt, EXT31_S, GROUP_W), F32),
            pltpu.VMEM((bt, EXT3_S, GROUP_W), F32),
        ],
        compiler_params=pltpu.CompilerParams(
            dimension_semantics=("arbitrary",), vmem_limit_bytes=V7X_VMEM_LIMIT),
        name="mixer_sample",
    )(z, cos, sin, din_t, dq_t, dk_t, dc_t, row(p["ret_norm_g"]), p["conv31_w"], row(p["conv31_b"]),
      row(p["conv_ln_g"]), row(p["conv_ln_b"]), p["sconv_w"], row(p["sgu_ln_g"]), row(p["sgu_ln_b"]),
      w_t, b_t, s_ret, s_31, s_3)


def _to_token_major(dst, val, pitch=TOK_ROWS):
    n = val.shape[0]
    for a in range(TOK_ROWS):
        dst[pl.ds(a, n, stride=pitch), :] = val[:, a * LANES:(a + 1) * LANES]


STAGE_PITCH = 24
DMA_UNROLL = 8


def _router_kernel(x_ref, g_ref, rw_ref, e_ref, p_ref, c_ref):
    h = _rms_rows(x_ref[...], g_ref[...])
    lane = lax.broadcasted_iota(jnp.int32, (h.shape[0], N_EXPERTS), 1)
    logits = jnp.zeros((h.shape[0], N_EXPERTS), F32)
    for e in range(N_EXPERTS):
        col = jnp.sum(h * rw_ref[pl.ds(e, 1), :], axis=-1, keepdims=True)
        logits = jnp.where(lane == e, col, logits)
    m1 = jnp.max(logits, axis=-1, keepdims=True)
    i1 = jnp.min(jnp.where(logits == m1, lane, N_EXPERTS), axis=-1, keepdims=True)
    rest = jnp.where(lane == i1, -jnp.inf, logits)
    m2 = jnp.max(rest, axis=-1, keepdims=True)
    i2 = jnp.min(jnp.where(rest == m2, lane, N_EXPERTS), axis=-1, keepdims=True)
    e2 = jnp.exp(m2 - m1)
    den = 1.0 + e2
    e_ref[...] = jnp.where(lane == 0, i1, jnp.where(lane == 1, i2, 0))
    p_ref[...] = jnp.where(lane == 0, 1.0 / den, jnp.where(lane == 1, e2 / den, 0.0))
    c_ref[pl.ds(0, 1), :] = jnp.sum(jnp.where(lane == i1, 1.0, 0.0), axis=0, keepdims=True).astype(jnp.int32)
    c_ref[pl.ds(1, 1), :] = jnp.sum(jnp.where(lane == i2, 1.0, 0.0), axis=0, keepdims=True).astype(jnp.int32)


def _router(x, g, rw):
    t, d = x.shape
    return pl.pallas_call(
        _router_kernel,
        grid=(N_TB,),
        in_specs=[
            pl.BlockSpec((MOE_TB, d), lambda i: (i, 0)),
            pl.BlockSpec((1, d), lambda i: (0, 0)),
            pl.BlockSpec((N_EXPERTS, d), lambda i: (0, 0)),
        ],
        out_specs=[
            pl.BlockSpec((MOE_TB, N_EXPERTS), lambda i: (i, 0)),
            pl.BlockSpec((MOE_TB, N_EXPERTS), lambda i: (i, 0)),
            pl.BlockSpec((None, TOP_K, N_EXPERTS), lambda i: (i, 0, 0)),
        ],
        out_shape=[
            jax.ShapeDtypeStruct((t, N_EXPERTS), jnp.int32),
            jax.ShapeDtypeStruct((t, N_EXPERTS), F32),
            jax.ShapeDtypeStruct((N_TB, TOP_K, N_EXPERTS), jnp.int32),
        ],
        compiler_params=pltpu.CompilerParams(
            dimension_semantics=("arbitrary",), vmem_limit_bytes=V7X_VMEM_LIMIT),
        name="moe_router",
    )(x, g.reshape(1, d), rw.T)


def _pick(table, idx):
    lanes = jnp.arange(N_EXPERTS, dtype=jnp.int32)
    return jnp.sum(jnp.where(idx[..., None] == lanes, table, 0), axis=-1)


def _expert_layout(cnt):
    tot = jnp.sum(cnt, axis=(0, 1))
    start = jnp.cumsum(tot) - tot
    blk = jnp.sum(cnt, axis=1)
    blk_base = start[None, :] + jnp.cumsum(blk, axis=0) - blk
    base = jnp.stack([blk_base, blk_base + cnt[:, 0]], axis=1)
    n_sb = (tot + SB - 1) // SB
    sb_end = jnp.cumsum(n_sb)
    n_used = sb_end[-1]
    s_idx = jnp.arange(NS_MAX, dtype=jnp.int32)
    s_cl = jnp.minimum(s_idx, n_used - 1)
    sb_e = jnp.sum(s_cl[:, None] >= sb_end[None, :], axis=-1).astype(jnp.int32)
    local = s_cl - _pick(sb_end - n_sb, sb_e)
    sb_row0 = _pick(start, sb_e) + local * SB
    sb_rows = jnp.where(s_idx < n_used, jnp.clip(_pick(tot, sb_e) - local * SB, 0, SB), 0)
    i32 = lambda a: a.astype(jnp.int32)
    return i32(base), sb_e, i32(sb_row0), i32(sb_rows), i32(n_used).reshape(1)


def _dest_kernel(e_ref, b_ref, d_ref):
    eid = e_ref[...]
    lane = lax.broadcasted_iota(jnp.int32, eid.shape, 1)
    row = lax.broadcasted_iota(jnp.int32, (MOE_TB, MOE_TB), 0)
    col = lax.broadcasted_iota(jnp.int32, (MOE_TB, MOE_TB), 1)
    earlier = jnp.where(row > col, 1.0, 0.0).astype(BF16)
    base = b_ref[...].astype(F32)

    def dest_of(k):
        mine = lane == eid[:, k:k + 1]
        before = jnp.dot(earlier, jnp.where(mine, 1.0, 0.0).astype(BF16), preferred_element_type=F32)
        return jnp.sum(jnp.where(mine, before + base[k:k + 1], 0.0), axis=-1, keepdims=True)

    d_ref[...] = jnp.where(lane == 0, dest_of(0), jnp.where(lane == 1, dest_of(1), 0.0)).astype(jnp.int32)


def _dest_rows(eid, base):
    return pl.pallas_call(
        _dest_kernel,
        grid=(N_TB,),
        in_specs=[
            pl.BlockSpec((MOE_TB, N_EXPERTS), lambda i: (i, 0)),
            pl.BlockSpec((None, TOP_K, N_EXPERTS), lambda i: (i, 0, 0)),
        ],
        out_specs=pl.BlockSpec((MOE_TB, N_EXPERTS), lambda i: (i, 0)),
        out_shape=jax.ShapeDtypeStruct((T_ALL, N_EXPERTS), jnp.int32),
        compiler_params=pltpu.CompilerParams(
            dimension_semantics=("arbitrary",), vmem_limit_bytes=V7X_VMEM_LIMIT),
        name="moe_dest_rows",
    )(eid, base)


def _dispatch_kernel(d_ref, x_ref, g_ref, xg_hbm, stage, zeros, sem, zsem):
    i = pl.program_id(0)
    n = pl.num_programs(0)
    slot = i % 2
    blk_rows = MOE_TB * TOK_ROWS

    def wait_slot(sl):
        for _ in range(TOP_K):
            pltpu.make_async_copy(stage.at[sl, pl.ds(0, blk_rows)], xg_hbm.at[pl.ds(0, blk_rows)], sem.at[sl]).wait()

    @pl.when(i >= 2)
    def _():
        wait_slot(slot)

    _to_token_major(stage.at[slot], _rms_rows(x_ref[...], g_ref[...]), STAGE_PITCH)

    def body(r, carry):
        src = stage.at[slot, pl.ds(pl.multiple_of(r * STAGE_PITCH, 8), TOK_ROWS)]
        for k in range(TOP_K):
            row = d_ref[0, 0, TOP_K * r + k]
            dst = xg_hbm.at[pl.ds(pl.multiple_of(row * TOK_ROWS, TOK_ROWS), TOK_ROWS)]
            pltpu.make_async_copy(src, dst, sem.at[slot]).start()
        return carry

    lax.fori_loop(0, MOE_TB, body, 0, unroll=DMA_UNROLL)

    @pl.when(i == n - 1)
    def _():
        zeros[...] = jnp.zeros_like(zeros)
        tail = [pltpu.make_async_copy(zeros, xg_hbm.at[pl.ds((N_SLOTS + q * (SUB // TOK_ROWS)) * TOK_ROWS, SUB)], zsem)
                for q in range(TOK_ROWS)]
        for cp in tail:
            cp.start()
        for cp in tail:
            cp.wait()
        wait_slot(slot)
        wait_slot(1 - slot)


def _dispatch(x, g, dest_flat):
    t, d = x.shape
    return pl.pallas_call(
        _dispatch_kernel,
        grid=(N_TB,),
        in_specs=[
            pl.BlockSpec((1, 1, TOP_K * MOE_TB), lambda i: (i, 0, 0), memory_space=pltpu.SMEM),
            pl.BlockSpec((MOE_TB, d), lambda i: (i, 0)),
            pl.BlockSpec((1, d), lambda i: (0, 0)),
        ],
        out_specs=pl.BlockSpec(memory_space=pl.ANY),
        out_shape=jax.ShapeDtypeStruct(((N_SLOTS + SUB) * TOK_ROWS, LANES), F32),
        scratch_shapes=[
            pltpu.VMEM((2, MOE_TB * STAGE_PITCH, LANES), F32),
            pltpu.VMEM((SUB, LANES), F32),
            pltpu.SemaphoreType.DMA((2,)),
            pltpu.SemaphoreType.DMA(()),
        ],
        compiler_params=pltpu.CompilerParams(
            dimension_semantics=("arbitrary",), vmem_limit_bytes=V7X_VMEM_LIMIT),
        name="moe_dispatch",
    )(dest_flat, x, g.reshape(1, d))


def _moe_ffn_kernel(se_ref, r0_ref, nr_ref, nu_ref, xg_hbm, w1_ref, w3_ref, w2_ref, y_hbm,
                    x_scr, u_scr, acc, stage, w1b, w3b, w2b, sem_in, sem_out):
    s = pl.program_id(0)
    j = pl.program_id(1)
    nj = pl.num_programs(1)
    rows = nr_ref[s]
    row0 = r0_ref[s]
    nsub = lax.shift_right_logical(rows + (SUB - 1), int(math.log2(SUB)))
    sub_rows = SUB * TOK_ROWS

    is_middle = (j > 0) & (j < nj - 1)
    is_odd = (j & 1) == 1

    def cast_weights():
        ws = (w1_ref[...].astype(BF16), w3_ref[...].astype(BF16))
        w1b[...], w3b[...] = ws
        _stash_w2(w2b, w2_ref, j)
        return ws

    @pl.when((nsub > 0) & jnp.logical_not(is_middle))
    def _():
        cast_weights()

    def sub(r):
        return pl.ds(pl.multiple_of(r * SUB, SUB), SUB)

    def gated(rows_, ws=None):
        w1, w3 = ws if ws is not None else (w1b[...], w3b[...])
        return _gated(x_scr[rows_, :], w1, w3)

    def in_copy(r, slot):
        first = pl.multiple_of((row0 + r * SUB) * TOK_ROWS, TOK_ROWS)
        return pltpu.make_async_copy(xg_hbm.at[pl.ds(first, sub_rows)], stage.at[slot], sem_in.at[slot])

    @pl.when(j == 0)
    def _():
        @pl.when(nsub > 0)
        def _():
            in_copy(0, 0).start()

        def body(r, carry):
            slot = r % 2
            in_copy(r, slot).wait()

            @pl.when(r + 1 < nsub)
            def _():
                in_copy(r + 1, 1 - slot).start()

            for a in range(TOK_ROWS):
                x_scr[sub(r), pl.ds(a * LANES, LANES)] = stage[slot, pl.ds(a, SUB, stride=TOK_ROWS), :].astype(BF16)
            u_scr[sub(r), :] = gated(sub(r))
            acc[sub(r), :] = jnp.zeros((SUB, D_MODEL), F32)
            return carry

        lax.fori_loop(0, nsub, body, 0)

        def clear(r, carry):
            x_scr[sub(r), :] = jnp.zeros((SUB, D_MODEL), BF16)
            u_scr[sub(r), :] = jnp.zeros((SUB, TF), BF16)
            acc[sub(r), :] = jnp.zeros((SUB, D_MODEL), F32)
            return carry

        lax.fori_loop(nsub, SB // SUB, clear, 0)

    @pl.when(is_middle)
    def _():
        for c in range(SB // CHUNK_ROWS):
            @pl.when(rows > c * CHUNK_ROWS)
            def _():
                ws = cast_weights() if c == 0 else None
                rows_c = pl.ds(c * CHUNK_ROWS, CHUNK_ROWS)
                u = gated(rows_c, ws)

                @pl.when(jnp.logical_not(is_odd))
                def _():
                    u_scr[rows_c, :] = u

                @pl.when(is_odd)
                def _():
                    acc[rows_c, :] += _pair_out(u_scr[rows_c, :], u, w2b)

    @pl.when(j == nj - 1)
    def _():
        def out_copies(r, act):
            nv = jnp.clip(rows - r * SUB, 0, SUB)
            bit = SUB
            while bit >= 1:
                off = nv & (-2 * bit)
                src = stage.at[0, pl.ds(pl.multiple_of(off * TOK_ROWS, TOK_ROWS), bit * TOK_ROWS)]
                first = pl.multiple_of((row0 + r * SUB + off) * TOK_ROWS, TOK_ROWS)
                cp = pltpu.make_async_copy(src, y_hbm.at[pl.ds(first, bit * TOK_ROWS)], sem_out)
                pl.when((nv & bit) != 0)(functools.partial(act, cp))
                bit //= 2

        def body(r, carry):
            y = acc[sub(r), :] + _pair_out(u_scr[sub(r), :], gated(sub(r)), w2b)

            @pl.when(r > 0)
            def _():
                out_copies(r - 1, lambda cp: cp.wait())

            _to_token_major(stage.at[0], y)
            out_copies(r, lambda cp: cp.start())
            return carry

        lax.fori_loop(0, nsub, body, 0)

        @pl.when(nsub > 0)
        def _():
            out_copies(nsub - 1, lambda cp: cp.wait())


def _moe_ffn(xg, w1, w3, w2, sb_e, sb_row0, sb_rows, n_used):
    d = w1.shape[1]
    f = w1.shape[2]
    nj = f // TF
    assert nj % 2 == 0, "d_ff tiles are consumed in pairs"

    def jmap(s, j, nu):
        return jnp.where(s < nu[0], j, nj - 1)

    return pl.pallas_call(
        _moe_ffn_kernel,
        grid_spec=pltpu.PrefetchScalarGridSpec(
            num_scalar_prefetch=4,
            grid=(NS_MAX, nj),
            in_specs=[
                pl.BlockSpec(memory_space=pl.ANY),
                pl.BlockSpec((None, d, TF), lambda s, j, se, r0, nr, nu: (se[s], 0, jmap(s, j, nu))),
                pl.BlockSpec((None, d, TF), lambda s, j, se, r0, nr, nu: (se[s], 0, jmap(s, j, nu))),
                pl.BlockSpec((None, TF, d), lambda s, j, se, r0, nr, nu: (se[s], jmap(s, j, nu), 0)),
            ],
            out_specs=pl.BlockSpec(memory_space=pl.ANY),
            scratch_shapes=[
                pltpu.VMEM((SB, d), BF16),
                pltpu.VMEM((SB, TF), BF16),
                pltpu.VMEM((SB, d), F32),
                pltpu.VMEM((2, SUB * TOK_ROWS, LANES), F32),
                pltpu.VMEM((d, TF), BF16),
                pltpu.VMEM((d, TF), BF16),
                pltpu.VMEM((2 * TF, d), BF16),
                pltpu.SemaphoreType.DMA((2,)),
                pltpu.SemaphoreType.DMA(()),
            ],
        ),
        out_shape=jax.ShapeDtypeStruct((N_SLOTS * TOK_ROWS, LANES), F32),
        compiler_params=pltpu.CompilerParams(
            dimension_semantics=("arbitrary", "arbitrary"), vmem_limit_bytes=V7X_VMEM_LIMIT),
        name="moe_ffn",
    )(sb_e, sb_row0, sb_rows, n_used, xg, w1, w3, w2)


N_TB_PROMPT = T_PROMPT // MOE_TB


def _combine_kernel(d_ref, dn_ref, x_ref, p_ref, g_ref, y_hbm, op_ref, os_ref, stage, sem, *, final_norm):
    i = pl.program_id(0)
    n = pl.num_programs(0)
    slot = i % 2
    blk_rows = MOE_TB * TOK_ROWS

    def start_block(dref, sl):
        def body(r, carry):
            for k in range(TOP_K):
                row = dref[0, 0, TOP_K * r + k]
                src = y_hbm.at[pl.ds(pl.multiple_of(row * TOK_ROWS, TOK_ROWS), TOK_ROWS)]
                dst = stage.at[sl, k, pl.ds(pl.multiple_of(r * STAGE_PITCH, 8), TOK_ROWS)]
                pltpu.make_async_copy(src, dst, sem.at[sl]).start()
            return carry

        lax.fori_loop(0, MOE_TB, body, 0, unroll=DMA_UNROLL)

    @pl.when(i == 0)
    def _():
        start_block(d_ref, 0)

    @pl.when(i + 1 < n)
    def _():
        start_block(dn_ref, 1 - slot)

    for k in range(TOP_K):
        pltpu.make_async_copy(y_hbm.at[pl.ds(0, blk_rows)], stage.at[slot, k, pl.ds(0, blk_rows)], sem.at[slot]).wait()
    g0 = p_ref[:, 0:1]
    g1 = p_ref[:, 1:2]
    pieces = []
    for a in range(TOK_ROWS):
        y0 = stage[slot, 0, pl.ds(a, MOE_TB, stride=STAGE_PITCH), :]
        y1 = stage[slot, 1, pl.ds(a, MOE_TB, stride=STAGE_PITCH), :]
        pieces.append(x_ref[:, pl.ds(a * LANES, LANES)] + (y0 * g0 + y1 * g1))
    y = jnp.concatenate(pieces, axis=1)
    if final_norm:
        y = _rms_rows(y, g_ref[...])

    @pl.when(i < N_TB_PROMPT)
    def _():
        op_ref[...] = y

    @pl.when(i >= N_TB_PROMPT)
    def _():
        os_ref[...] = y


def _moe_combine(x, y, dest_flat, gate, g, final_norm):
    t, d = x.shape
    return pl.pallas_call(
        functools.partial(_combine_kernel, final_norm=final_norm),
        grid=(N_TB,),
        in_specs=[
            pl.BlockSpec((1, 1, TOP_K * MOE_TB), lambda i: (i, 0, 0), memory_space=pltpu.SMEM),
            pl.BlockSpec((1, 1, TOP_K * MOE_TB), lambda i: (jnp.minimum(i + 1, N_TB - 1), 0, 0),
                         memory_space=pltpu.SMEM),
            pl.BlockSpec((MOE_TB, d), lambda i: (i, 0)),
            pl.BlockSpec((MOE_TB, N_EXPERTS), lambda i: (i, 0)),
            pl.BlockSpec((1, d), lambda i: (0, 0)),
            pl.BlockSpec(memory_space=pl.ANY),
        ],
        out_specs=[
            pl.BlockSpec((MOE_TB, d), lambda i: (jnp.minimum(i, N_TB_PROMPT - 1), 0)),
            pl.BlockSpec((MOE_TB, d), lambda i: (jnp.maximum(i - N_TB_PROMPT, 0), 0)),
        ],
        out_shape=[
            jax.ShapeDtypeStruct((T_PROMPT, d), F32),
            jax.ShapeDtypeStruct((T_SAMPLE, d), F32),
        ],
        scratch_shapes=[
            pltpu.VMEM((2, TOP_K, MOE_TB * STAGE_PITCH, LANES), F32),
            pltpu.SemaphoreType.DMA((2,)),
        ],
        compiler_params=pltpu.CompilerParams(
            dimension_semantics=("arbitrary",), vmem_limit_bytes=V7X_VMEM_LIMIT),
        name="moe_combine",
    )(dest_flat, dest_flat, x, gate, g.reshape(1, d), y)


def _moe(x, norm_g, rw, w1, w3, w2, out_g, final_norm):
    eid, gate, cnt = _router(x, norm_g, rw)
    base, sb_e, sb_row0, sb_rows, n_used = _expert_layout(cnt)
    dest = _dest_rows(eid, base)
    dest_flat = dest[:, :TOP_K].reshape(N_TB, 1, TOP_K * MOE_TB)
    xg = _dispatch(x, norm_g, dest_flat)
    y = _moe_ffn(xg, w1, w3, w2, sb_e, sb_row0, sb_rows, n_used)
    return _moe_combine(x, y, dest_flat, gate, out_g, final_norm)


def kernel(x_prompt, x_sample, state_ret, state_conv31, state_conv3, mix_norm_g, w_in, ret_norm_g, conv31_w, conv31_b, conv_ln_g, conv_ln_b, sconv_w, sgu_ln_g, sgu_ln_b, sgu_w, sgu_b, w_out, ffn_norm_g, dense_w1, dense_w3, dense_w2, router_w, moe_w1, moe_w3, moe_w2, final_norm_g):
    assert DEPTH == 2, "layer 0 has the dense channel mixer; layer 1's expert mixer ends with the final norm"
    x, hn = _prep(x_prompt.reshape(T_PROMPT, D_MODEL), x_sample.reshape(T_SAMPLE, D_MODEL), mix_norm_g[0])
    rets_p, rets_s, c31_p, c31_s, c3_p, c3_s, vns = [], [], [], [], [], [], []
    for l in range(DEPTH):
        p = dict(ret_norm_g=ret_norm_g[l], conv31_w=conv31_w[l], conv31_b=conv31_b[l],
                 conv_ln_g=conv_ln_g[l], conv_ln_b=conv_ln_b[l], sconv_w=sconv_w[l],
                 sgu_ln_g=sgu_ln_g[l], sgu_ln_b=sgu_ln_b[l], sgu_w=sgu_w[l], sgu_b=sgu_b[l])
        z = _in_proj(hn, w_in, l)
        mix_s, ret_s, b31_s, b3_s, vn_s = _mixer_sample(z, p, l, state_ret, state_conv31, state_conv3)
        mix_p, ret_p, b31_p, b3_p = _mixer_prompt(z, p)
        x = _out_proj_residual(mix_p, mix_s, w_out, l, x)
        j = l // 2
        if l % 2 == 0:
            x, hn = _ffn_dense(x, ffn_norm_g[l], dense_w1[j], dense_w3[j], dense_w2[j], mix_norm_g[l + 1])
        else:
            y_p, y_s = _moe(x, ffn_norm_g[l], router_w[j], moe_w1[j], moe_w3[j], moe_w2[j],
                            final_norm_g, final_norm=True)
        rets_p.append(ret_p); rets_s.append(ret_s)
        c31_p.append(b31_p); c31_s.append(b31_s)
        c3_p.append(b3_p); c3_s.append(b3_s)
        vns.append(vn_s)
    y_prompt = y_p.reshape(BATCH, SEQ, D_MODEL)
    y_sample = y_s.reshape(DEC_BATCH, DEC_SEQ, D_MODEL)
    return (y_prompt, y_sample, jnp.stack(rets_p), jnp.stack(rets_s), jnp.stack(c31_p), jnp.stack(c31_s),
            jnp.stack(c3_p), jnp.stack(c3_s), jnp.stack(vns))
```

```python
import functools
import math

import jax
import jax.numpy as jnp
from jax import lax
from jax.experimental import pallas as pl
from jax.experimental.pallas import tpu as pltpu

F32 = jnp.float32
BF16 = jnp.bfloat16

D_MODEL = 2048
BATCH = 4
SEQ = 2048
DEPTH = 2
DEC_BATCH = 128
DEC_SEQ = 4
PAST_LEN = 16384
GROUP_W = 512
D_IN = 11 * GROUP_W
RET_HEADS = 4
RET_D = 128
CHUNK = 128
ROPE_THETA = 10000.0
CONV_W = 31
SC_W = 3
SGU_GROUPS = 4
D_FF = 5632
N_EXPERTS = 8
TOP_K = 2
EPS = 1e-6

T_PROMPT = BATCH * SEQ
T_SAMPLE = DEC_BATCH * DEC_SEQ
T_ALL = T_PROMPT + T_SAMPLE
N_CHUNKS = SEQ // CHUNK

C_Q, C_K, C_V, C_G, C_BL, C_BG, C_CB, C_CC, C_CH, C_DU, C_DV = (i * GROUP_W for i in range(11))

V7X_VMEM_LIMIT = 56 * 1024 * 1024

TM = 1088
TM_IN = 2176
TM_OUT = 512
N_OUT_PROMPT = T_PROMPT // TM_OUT
NORM_ROWS = 272
TN_IN = 512
TF = 256
SAMPLE_BT = 16

LANES = 128
SUBLANES = 8
TOK_ROWS = D_MODEL // LANES
MOE_TB = 256
N_TB = T_ALL // MOE_TB
N_SLOTS = T_ALL * TOP_K
SUB = 256
SB = 9 * SUB
CHUNK_ROWS = SB // 3
NS_MAX = N_SLOTS // SB + N_EXPERTS


def _rms_rows(x, g):
    ms = jnp.mean(x * x, axis=-1, keepdims=True)
    return (x * lax.rsqrt(ms + EPS)) * g


def _layernorm_rows(x, g, b):
    mu = jnp.mean(x, axis=-1, keepdims=True)
    xc = x - mu
    return xc * lax.rsqrt(jnp.mean(xc * xc, axis=-1, keepdims=True) + EPS) * g + b


def _silu(x):
    return x * jax.nn.sigmoid(x)


def _norm_block_to(x_ref, g_ref, dst_ref):
    rows = x_ref.shape[0]
    g = g_ref[...]

    def body(c, carry):
        r0 = pl.multiple_of(c * NORM_ROWS, NORM_ROWS)
        x = x_ref[pl.ds(r0, NORM_ROWS), :]
        dst_ref[pl.ds(r0, NORM_ROWS), :] = _rms_rows(x, g).astype(dst_ref.dtype)
        return carry

    lax.fori_loop(0, rows // NORM_ROWS, body, 0)


PREP_ROWS = 512
N_PREP_PROMPT = T_PROMPT // PREP_ROWS


def _prep_kernel(xp_ref, xs_ref, g_ref, x_ref, hn_ref):
    def emit(src_ref):
        x = src_ref[...]
        x_ref[...] = x
        hn_ref[...] = _rms_rows(x, g_ref[...]).astype(hn_ref.dtype)

    is_prompt = pl.program_id(0) < N_PREP_PROMPT
    pl.when(is_prompt)(functools.partial(emit, xp_ref))
    pl.when(jnp.logical_not(is_prompt))(functools.partial(emit, xs_ref))


def _prep(x_prompt, x_sample, g):
    d = x_prompt.shape[1]
    blk = lambda f: pl.BlockSpec((PREP_ROWS, d), f)
    return pl.pallas_call(
        _prep_kernel,
        grid=(T_ALL // PREP_ROWS,),
        in_specs=[
            blk(lambda i: (jnp.minimum(i, N_PREP_PROMPT - 1), 0)),
            blk(lambda i: (jnp.maximum(i - N_PREP_PROMPT, 0), 0)),
            pl.BlockSpec((1, d), lambda i: (0, 0)),
        ],
        out_specs=[blk(lambda i: (i, 0)), blk(lambda i: (i, 0))],
        out_shape=[jax.ShapeDtypeStruct((T_ALL, d), F32), jax.ShapeDtypeStruct((T_ALL, d), BF16)],
        compiler_params=pltpu.CompilerParams(
            dimension_semantics=("arbitrary",), vmem_limit_bytes=V7X_VMEM_LIMIT),
        name="prep_rows",
    )(x_prompt, x_sample, g.reshape(1, d))


def _in_proj_kernel(h_ref, w_ref, o_ref):
    o_ref[...] = jnp.dot(h_ref[...], w_ref[...].astype(BF16), preferred_element_type=F32)


def _in_proj(hn, w_all, layer):
    t, d = hn.shape
    n = w_all.shape[2]
    return pl.pallas_call(
        _in_proj_kernel,
        grid=(t // TM_IN, n // TN_IN),
        in_specs=[
            pl.BlockSpec((TM_IN, d), lambda i, j: (i, 0)),
            pl.BlockSpec((None, d, TN_IN), lambda i, j: (layer, 0, j)),
        ],
        out_specs=pl.BlockSpec((TM_IN, TN_IN), lambda i, j: (i, j)),
        out_shape=jax.ShapeDtypeStruct((t, n), F32),
        compiler_params=pltpu.CompilerParams(
            dimension_semantics=("arbitrary", "arbitrary"), vmem_limit_bytes=V7X_VMEM_LIMIT),
        name="in_proj",
    )(hn, w_all)


W_LOAD_ROWS = 512


def _out_proj_kernel(ap_ref, as_ref, w_hbm, r_ref, o_ref, wb, stage, sem, *, layer):
    i = pl.program_id(0)

    @pl.when(i == 0)
    def _():
        for c in range(wb.shape[0] // W_LOAD_ROWS):
            rows = pl.ds(c * W_LOAD_ROWS, W_LOAD_ROWS)
            cp = pltpu.make_async_copy(w_hbm.at[layer, rows], stage, sem)
            cp.start()
            cp.wait()
            wb[rows, :] = stage[...].astype(BF16)

    def emit(a):
        o_ref[...] = r_ref[...] + jnp.dot(a, wb[...], preferred_element_type=F32)

    pl.when(i < N_OUT_PROMPT)(lambda: emit(ap_ref[...]))
    pl.when(i >= N_OUT_PROMPT)(lambda: emit(as_ref[...].astype(BF16)))


def _out_proj_residual(a_prompt, a_sample, w_all, layer, r):
    k = a_prompt.shape[1]
    t = r.shape[0]
    n = w_all.shape[2]
    return pl.pallas_call(
        functools.partial(_out_proj_kernel, layer=layer),
        grid=(t // TM_OUT,),
        in_specs=[
            pl.BlockSpec((TM_OUT, k), lambda i: (jnp.minimum(i, N_OUT_PROMPT - 1), 0)),
            pl.BlockSpec((TM_OUT, k), lambda i: (0, 0)),
            pl.BlockSpec(memory_space=pl.ANY),
            pl.BlockSpec((TM_OUT, n), lambda i: (i, 0)),
        ],
        out_specs=pl.BlockSpec((TM_OUT, n), lambda i: (i, 0)),
        out_shape=jax.ShapeDtypeStruct((t, n), F32),
        scratch_shapes=[
            pltpu.VMEM((k, n), BF16),
            pltpu.VMEM((W_LOAD_ROWS, n), F32),
            pltpu.SemaphoreType.DMA(()),
        ],
        compiler_params=pltpu.CompilerParams(
            dimension_semantics=("arbitrary",), vmem_limit_bytes=V7X_VMEM_LIMIT),
        name="out_proj_residual",
    )(a_prompt, a_sample, w_all, r)


def _swiglu_tile(h, w1_ref, w3_ref, w2_ref):
    a = jnp.dot(h, w1_ref[...].astype(BF16), preferred_element_type=F32)
    b = jnp.dot(h, w3_ref[...].astype(BF16), preferred_element_type=F32)
    u = (_silu(a) * b).astype(BF16)
    return jnp.dot(u, w2_ref[...].astype(BF16), preferred_element_type=F32)


def _ffn_kernel(x_hbm, g_ref, w1_ref, w3_ref, w2_ref, gn_ref, o_ref, hnext_ref, hn_ref, sem):
    @pl.when(pl.program_id(1) == 0)
    def _():
        r0 = pl.multiple_of(pl.program_id(0) * TM, TM)
        cp = pltpu.make_async_copy(x_hbm.at[pl.ds(r0, TM)], o_ref, sem)
        cp.start()
        cp.wait()
        _norm_block_to(o_ref, g_ref, hn_ref)

    o_ref[...] += _swiglu_tile(hn_ref[...], w1_ref, w3_ref, w2_ref)

    @pl.when(pl.program_id(1) == pl.num_programs(1) - 1)
    def _():
        _norm_block_to(o_ref, gn_ref, hnext_ref)


def _ffn_dense(x, g, w1, w3, w2, g_next):
    t, d = x.shape
    f = w1.shape[1]
    return pl.pallas_call(
        _ffn_kernel,
        grid=(t // TM, f // TF),
        in_specs=[
            pl.BlockSpec(memory_space=pl.ANY),
            pl.BlockSpec((1, d), lambda i, j: (0, 0)),
            pl.BlockSpec((d, TF), lambda i, j: (0, j)),
            pl.BlockSpec((d, TF), lambda i, j: (0, j)),
            pl.BlockSpec((TF, d), lambda i, j: (j, 0)),
            pl.BlockSpec((1, d), lambda i, j: (0, 0)),
        ],
        out_specs=[
            pl.BlockSpec((TM, d), lambda i, j: (i, 0)),
            pl.BlockSpec((TM, d), lambda i, j: (i, 0)),
        ],
        out_shape=[jax.ShapeDtypeStruct((t, d), F32), jax.ShapeDtypeStruct((t, d), BF16)],
        scratch_shapes=[pltpu.VMEM((TM, d), BF16), pltpu.SemaphoreType.DMA(())],
        compiler_params=pltpu.CompilerParams(
            dimension_semantics=("arbitrary", "arbitrary"), vmem_limit_bytes=V7X_VMEM_LIMIT),
        name="ffn_dense",
    )(x, g.reshape(1, d), w1, w3, w2, g_next.reshape(1, d))


def _rotary_tables(length, pos0):
    half = RET_D // 2
    inv = ROPE_THETA ** (-jnp.arange(half, dtype=F32) / half)
    pos = jnp.arange(length, dtype=F32) + pos0
    ang = pos[:, None] * inv[None, :]
    cos, sin = jnp.cos(ang), jnp.sin(ang)
    return jnp.concatenate([cos, cos], axis=-1), jnp.concatenate([-sin, sin], axis=-1)


def _decay_tables(c):
    log_g = jnp.log1p(-jnp.exp2(-5.0 - jnp.arange(RET_HEADS, dtype=F32)))
    idx = jnp.arange(c, dtype=F32)
    rel = idx[:, None] - idx[None, :]
    d_in = jnp.where(rel >= 0, jnp.exp(log_g[:, None, None] * jnp.maximum(rel, 0.0)), 0.0)
    d_q = jnp.exp(log_g[:, None] * (idx + 1.0)[None, :])
    d_k = jnp.exp(log_g[:, None] * (c - 1.0 - idx)[None, :])
    d_c = jnp.exp(log_g * c)
    return d_in, d_q, d_k, d_c


HIST31 = 32
HIST3 = 8


def _mixer_prompt_kernel(z_ref, cos_ref, sin_ref, din_ref, dq_ref, dk_ref, dc_ref, retg_ref,
                         c31w_ref, c31b_ref, clng_ref, clnb_ref, scw_ref, slng_ref, slnb_ref,
                         sguw_ref, sgub_ref,
                         mix_ref, ret_ref, b31_ref, b3_ref,
                         s_scr, ext31_scr, ext3_scr, ysh_scr):
    c = pl.program_id(1)

    @pl.when(c == 0)
    def _():
        s_scr[...] = jnp.zeros_like(s_scr)
        ext31_scr[pl.ds(0, HIST31), :] = jnp.zeros((HIST31, GROUP_W), F32)
        ext3_scr[pl.ds(0, HIST3), :] = jnp.zeros((HIST3, GROUP_W), F32)

    cos = cos_ref[...]
    sin = sin_ref[...]

    for h in range(RET_HEADS):
        lo = h * RET_D
        q = z_ref[:, pl.ds(C_Q + lo, RET_D)]
        k = z_ref[:, pl.ds(C_K + lo, RET_D)]
        v = z_ref[:, pl.ds(C_V + lo, RET_D)]
        g = z_ref[:, pl.ds(C_G + lo, RET_D)]
        q = q * cos + pltpu.roll(q, RET_D // 2, axis=1) * sin
        k = (k * cos + pltpu.roll(k, RET_D // 2, axis=1) * sin) * (RET_D ** -0.5)
        qb = q.astype(BF16)
        vb = v.astype(BF16)
        scores = lax.dot_general(qb, k.astype(BF16), (((1,), (1,)), ((), ())),
                                 preferred_element_type=F32) * din_ref[h]
        inner = jnp.dot(scores.astype(BF16), vb, preferred_element_type=F32)
        s_old = s_scr[h]
        cross = jnp.dot(qb, s_old.astype(BF16), preferred_element_type=F32) * dq_ref[h]
        kd = (k * dk_ref[h]).astype(BF16)
        s_scr[h] = s_old * dc_ref[h] + lax.dot_general(kd, vb, (((0,), (0,)), ((), ())),
                                                       preferred_element_type=F32)
        r = inner + cross
        r = r * lax.rsqrt(jnp.mean(r * r, axis=-1, keepdims=True) + EPS)
        r = r * retg_ref[:, pl.ds(lo, RET_D)]
        mix_ref[:, pl.ds(lo, RET_D)] = (_silu(g) * r).astype(mix_ref.dtype)

    glu = z_ref[:, pl.ds(C_BL, GROUP_W)] * jax.nn.sigmoid(z_ref[:, pl.ds(C_BG, GROUP_W)])
    ext31_scr[pl.ds(HIST31, CHUNK), :] = glu
    s0 = HIST31 - (CONV_W - 1)
    conv = None
    for i in range(SUBLANES):
        n_rows = CHUNK if i == 0 else CHUNK + SUBLANES
        y = None
        for m in range(HIST31 // SUBLANES + 1):
            j = SUBLANES * m + i - s0
            if 0 <= j < CONV_W:
                term = ext31_scr[pl.ds(SUBLANES * m, n_rows), :] * c31w_ref[pl.ds(j, 1), :]
                y = term if y is None else y + term
        if i == 0:
            conv = y
        else:
            ysh_scr[i - 1] = y
            conv = conv + ysh_scr[i - 1, pl.ds(i, CHUNK), :]
    conv = conv + c31b_ref[...]
    out_b = _silu(_layernorm_rows(conv, clng_ref[...], clnb_ref[...]))
    mix_ref[:, pl.ds(GROUP_W, GROUP_W)] = out_b.astype(mix_ref.dtype)
    ext31_scr[pl.ds(0, HIST31), :] = ext31_scr[pl.ds(CHUNK, HIST31), :]

    gated = z_ref[:, pl.ds(C_CC, GROUP_W)] * z_ref[:, pl.ds(C_CH, GROUP_W)]
    ext3_scr[pl.ds(HIST3, CHUNK), :] = gated
    conv3 = ext3_scr[pl.ds(HIST3 - (SC_W - 1), CHUNK), :] * scw_ref[pl.ds(0, 1), :]
    for j in range(1, SC_W):
        conv3 = conv3 + ext3_scr[pl.ds(HIST3 - (SC_W - 1) + j, CHUNK), :] * scw_ref[pl.ds(j, 1), :]
    mix_ref[:, pl.ds(2 * GROUP_W, GROUP_W)] = (z_ref[:, pl.ds(C_CB, GROUP_W)] * conv3).astype(mix_ref.dtype)
    ext3_scr[pl.ds(0, HIST3), :] = ext3_scr[pl.ds(CHUNK, HIST3), :]

    vn = _layernorm_rows(z_ref[:, pl.ds(C_DV, GROUP_W)], slng_ref[...], slnb_ref[...])
    row = lax.broadcasted_iota(jnp.int32, (CHUNK, CHUNK), 0)
    col = lax.broadcasted_iota(jnp.int32, (CHUNK, CHUNK), 1)
    ch = GROUP_W // SGU_GROUPS
    for gi in range(SGU_GROUPS):
        wm = jnp.where(row >= col, sguw_ref[gi], 0.0).astype(BF16)
        mixed = jnp.dot(wm, vn[:, gi * ch:(gi + 1) * ch].astype(BF16), preferred_element_type=F32)
        mixed = mixed + sgub_ref[:, pl.ds(gi, 1)]
        du = z_ref[:, pl.ds(C_DU + gi * ch, ch)]
        mix_ref[:, pl.ds(3 * GROUP_W + gi * ch, ch)] = (du * mixed).astype(mix_ref.dtype)

    @pl.when(c == pl.num_programs(1) - 1)
    def _():
        ret_ref[...] = s_scr[...]
        b31_ref[...] = ext31_scr[pl.ds(HIST31 - (CONV_W - 1), CONV_W - 1), :]
        b3_ref[...] = ext3_scr[pl.ds(HIST3 - (SC_W - 1), SC_W - 1), :]


def _mixer_prompt(z, p):
    cos, sin = _rotary_tables(SEQ, 0)
    d_in, d_q, d_k, d_c = _decay_tables(CHUNK)
    dq_t = jnp.broadcast_to(d_q[:, :, None], (RET_HEADS, CHUNK, RET_D))
    dk_t = jnp.broadcast_to(d_k[:, :, None], (RET_HEADS, CHUNK, RET_D))
    dc_t = jnp.broadcast_to(d_c[:, None, None], (RET_HEADS, RET_D, RET_D))

    def full(shape):
        return pl.BlockSpec(shape, lambda b, c: (0,) * len(shape))

    row = lambda a: a.reshape(1, -1)
    return pl.pallas_call(
        _mixer_prompt_kernel,
        grid=(BATCH, N_CHUNKS),
        in_specs=[
            pl.BlockSpec((CHUNK, D_IN), lambda b, c: (b * N_CHUNKS + c, 0)),
            pl.BlockSpec((CHUNK, RET_D), lambda b, c: (c, 0)),
            pl.BlockSpec((CHUNK, RET_D), lambda b, c: (c, 0)),
            full((RET_HEADS, CHUNK, CHUNK)),
            full((RET_HEADS, CHUNK, RET_D)),
            full((RET_HEADS, CHUNK, RET_D)),
            full((RET_HEADS, RET_D, RET_D)),
            full((1, GROUP_W)),
            full((CONV_W, GROUP_W)),
            full((1, GROUP_W)),
            full((1, GROUP_W)),
            full((1, GROUP_W)),
            full((SC_W, GROUP_W)),
            full((1, GROUP_W)),
            full((1, GROUP_W)),
            full((SGU_GROUPS, CHUNK, CHUNK)),
            full((CHUNK, SGU_GROUPS)),
        ],
        out_specs=[
            pl.BlockSpec((CHUNK, D_MODEL), lambda b, c: (b * N_CHUNKS + c, 0)),
            pl.BlockSpec((None, RET_HEADS, RET_D, RET_D), lambda b, c: (b, 0, 0, 0)),
            pl.BlockSpec((None, CONV_W - 1, GROUP_W), lambda b, c: (b, 0, 0)),
            pl.BlockSpec((None, SC_W - 1, GROUP_W), lambda b, c: (b, 0, 0)),
        ],
        out_shape=[
            jax.ShapeDtypeStruct((T_PROMPT, D_MODEL), BF16),
            jax.ShapeDtypeStruct((BATCH, RET_HEADS, RET_D, RET_D), F32),
            jax.ShapeDtypeStruct((BATCH, CONV_W - 1, GROUP_W), F32),
            jax.ShapeDtypeStruct((BATCH, SC_W - 1, GROUP_W), F32),
        ],
        scratch_shapes=[
            pltpu.VMEM((RET_HEADS, RET_D, RET_D), F32),
            pltpu.VMEM((HIST31 + CHUNK, GROUP_W), F32),
            pltpu.VMEM((HIST3 + CHUNK, GROUP_W), F32),
            pltpu.VMEM((SUBLANES - 1, CHUNK + SUBLANES, GROUP_W), F32),
        ],
        compiler_params=pltpu.CompilerParams(
            dimension_semantics=("arbitrary", "arbitrary"), vmem_limit_bytes=V7X_VMEM_LIMIT),
        name="mixer_prompt",
    )(z, cos, sin, d_in, dq_t, dk_t, dc_t, row(p["ret_norm_g"]), p["conv31_w"], row(p["conv31_b"]),
      row(p["conv_ln_g"]), row(p["conv_ln_b"]), p["sconv_w"], row(p["sgu_ln_g"]), row(p["sgu_ln_b"]),
      p["sgu_w"], p["sgu_b"].T)


EXT31_S = 40
EXT3_S = 8


N_SAMPLE_IN = 20


def _mixer_sample_stacked_kernel(*refs):
    ins = refs[:N_SAMPLE_IN]
    prev_ret, prev_b31 = refs[N_SAMPLE_IN:N_SAMPLE_IN + 2]
    mix_ref, ret_hbm, b31_hbm, b3_ref, vn_ref = refs[N_SAMPLE_IN + 2:N_SAMPLE_IN + 7]
    ext31_scr, ext3_scr, ret_scr, b31_scr, wr_sem, cp_sem = refs[N_SAMPLE_IN + 7:]
    i = pl.program_id(0)
    n = pl.num_programs(0)
    slot = i % 2
    bt = ret_scr.shape[1]
    blk = pl.ds(pl.multiple_of(i * bt, bt), bt)

    def writes(sl):
        return (pltpu.make_async_copy(ret_scr.at[sl], ret_hbm.at[1, blk], wr_sem.at[sl, 0]),
                pltpu.make_async_copy(b31_scr.at[sl], b31_hbm.at[1, blk], wr_sem.at[sl, 1]))

    copies = (pltpu.make_async_copy(prev_ret.at[blk], ret_hbm.at[0, blk], cp_sem.at[0]),
              pltpu.make_async_copy(prev_b31.at[blk], b31_hbm.at[0, blk], cp_sem.at[1]))

    @pl.when(i >= 2)
    def _():
        for w in writes(slot):
            w.wait()

    for c in copies:
        c.start()
    _mixer_sample_kernel(*ins, mix_ref, ret_scr.at[slot], b31_scr.at[slot], b3_ref, vn_ref, ext31_scr, ext3_scr)
    ws = writes(slot)
    for w in ws:
        w.start()
    for c in copies:
        c.wait()

    @pl.when(i == n - 1)
    def _():
        for w in ws:
            w.wait()
        for w in writes(1 - slot):
            w.wait()


def _mixer_sample_kernel(z_ref, cos_ref, sin_ref, din_ref, dq_ref, dk_ref, dc_ref, retg_ref,
                         c31w_ref, c31b_ref, clng_ref, clnb_ref, scw_ref, slng_ref, slnb_ref,
                         sguw_ref, sgub_ref, sret_ref, s31_ref, s3_ref,
                         mix_ref, ret_ref, b31_ref, b3_ref, vn_ref,
                         ext31_scr, ext3_scr):
    cos = cos_ref[...]
    sin = sin_ref[...]
    n_hist31 = CONV_W - 1
    n_hist3 = SC_W - 1
    bt = sret_ref.shape[0]

    def zs(col, width):
        return jnp.stack([z_ref[pl.ds(DEC_SEQ * b, DEC_SEQ), pl.ds(col, width)] for b in range(bt)])

    def zrow(s, col, width):
        return jnp.stack([z_ref[pl.ds(DEC_SEQ * b + s, 1), pl.ds(col, width)] for b in range(bt)])

    def put(col, val):
        for b in range(bt):
            mix_ref[pl.ds(DEC_SEQ * b, DEC_SEQ), pl.ds(col, val.shape[-1])] = val[b]

    for h in range(RET_HEADS):
        lo = h * RET_D
        q = zs(C_Q + lo, RET_D)
        k = zs(C_K + lo, RET_D)
        g = zs(C_G + lo, RET_D)
        q = q * cos + pltpu.roll(q, RET_D // 2, axis=2) * sin
        k = (k * cos + pltpu.roll(k, RET_D // 2, axis=2) * sin) * (RET_D ** -0.5)
        s_old = sret_ref[:, h]
        r = lax.dot_general(q, s_old, (((2,), (1,)), ((0,), (0,))),
                            preferred_element_type=F32) * dq_ref[h]
        for s in range(DEC_SEQ):
            k_s = zrow(s, C_K + lo, RET_D)
            k_s = (k_s * cos[s:s + 1] + pltpu.roll(k_s, RET_D // 2, axis=2) * sin[s:s + 1]) * (RET_D ** -0.5)
            v_s = zrow(s, C_V + lo, RET_D)
            score = jnp.sum(q * k_s, axis=-1, keepdims=True) * din_ref[h, s]
            r = r + score * v_s
        kd = k * dk_ref[h]
        v = zs(C_V + lo, RET_D)
        ret_ref[:, h] = s_old * dc_ref[h] + lax.dot_general(
            kd, v, (((1,), (1,)), ((0,), (0,))), preferred_element_type=F32)
        r = r * lax.rsqrt(jnp.mean(r * r, axis=-1, keepdims=True) + EPS)
        r = r * retg_ref[:, pl.ds(lo, RET_D)]
        put(lo, _silu(g) * r)

    glu = zs(C_BL, GROUP_W) * jax.nn.sigmoid(zs(C_BG, GROUP_W))
    ext31_scr[:, pl.ds(0, n_hist31), :] = s31_ref[...]
    ext31_scr[:, pl.ds(n_hist31, DEC_SEQ), :] = glu
    conv = ext31_scr[:, pl.ds(0, DEC_SEQ), :] * c31w_ref[pl.ds(0, 1), :]
    for j in range(1, CONV_W):
        conv = conv + ext31_scr[:, pl.ds(j, DEC_SEQ), :] * c31w_ref[pl.ds(j, 1), :]
    conv = conv + c31b_ref[...]
    put(GROUP_W, _silu(_layernorm_rows(conv, clng_ref[...], clnb_ref[...])))
    b31_ref[...] = ext31_scr[:, pl.ds(DEC_SEQ, n_hist31), :]

    gated = zs(C_CC, GROUP_W) * zs(C_CH, GROUP_W)
    ext3_scr[:, pl.ds(0, n_hist3), :] = s3_ref[...]
    ext3_scr[:, pl.ds(n_hist3, DEC_SEQ), :] = gated
    conv3 = ext3_scr[:, pl.ds(0, DEC_SEQ), :] * scw_ref[pl.ds(0, 1), :]
    for j in range(1, SC_W):
        conv3 = conv3 + ext3_scr[:, pl.ds(j, DEC_SEQ), :] * scw_ref[pl.ds(j, 1), :]
    put(2 * GROUP_W, zs(C_CB, GROUP_W) * conv3)
    b3_ref[...] = ext3_scr[:, pl.ds(DEC_SEQ, n_hist3), :]

    vn = _layernorm_rows(zs(C_DV, GROUP_W), slng_ref[...], slnb_ref[...])
    vn_ref[...] = vn
    mixed = sgub_ref[...]
    for s in range(DEC_SEQ):
        mixed = mixed + sguw_ref[s] * vn_ref[:, pl.ds(s, 1), :]
    put(3 * GROUP_W, zs(C_DU, GROUP_W) * mixed)


def _mixer_sample(z, p, layer, s_ret, s_31, s_3, prev=None):
    cos, sin = _rotary_tables(DEC_SEQ, PAST_LEN)
    d_in, d_q, d_k, d_c = _decay_tables(DEC_SEQ)
    din_t = jnp.swapaxes(d_in, 1, 2)[:, :, :, None]
    dq_t = jnp.broadcast_to(d_q[:, :, None], (RET_HEADS, DEC_SEQ, RET_D))
    dk_t = jnp.broadcast_to(d_k[:, :, None], (RET_HEADS, DEC_SEQ, RET_D))
    dc_t = jnp.broadcast_to(d_c[:, None, None], (RET_HEADS, RET_D, RET_D))
    ch = GROUP_W // SGU_GROUPS
    w4 = jnp.tril(p["sgu_w"][:, :DEC_SEQ, :DEC_SEQ])
    w_t = jnp.repeat(jnp.transpose(w4, (2, 1, 0)), ch, axis=-1)
    b_t = jnp.repeat(p["sgu_b"][:, :DEC_SEQ].T, ch, axis=-1)
    bt = SAMPLE_BT

    def full(shape):
        return pl.BlockSpec(shape, lambda i: (0,) * len(shape))

    row = lambda a: a.reshape(1, -1)
    any_spec = pl.BlockSpec(memory_space=pl.ANY)
    ret_blk, b31_blk = (bt, RET_HEADS, RET_D, RET_D), (bt, CONV_W - 1, GROUP_W)
    ret_all, b31_all = (DEC_BATCH,) + ret_blk[1:], (DEC_BATCH,) + b31_blk[1:]
    stacked = prev is not None
    return pl.pallas_call(
        _mixer_sample_stacked_kernel if stacked else _mixer_sample_kernel,
        grid=(DEC_BATCH // bt,),
        in_specs=[
            pl.BlockSpec((bt * DEC_SEQ, D_IN), lambda i: (T_PROMPT // (bt * DEC_SEQ) + i, 0)),
            full((DEC_SEQ, RET_D)),
            full((DEC_SEQ, RET_D)),
            full((RET_HEADS, DEC_SEQ, DEC_SEQ, 1)),
            full((RET_HEADS, DEC_SEQ, RET_D)),
            full((RET_HEADS, DEC_SEQ, RET_D)),
            full((RET_HEADS, RET_D, RET_D)),
            full((1, GROUP_W)),
            full((CONV_W, GROUP_W)),
            full((1, GROUP_W)),
            full((1, GROUP_W)),
            full((1, GROUP_W)),
            full((SC_W, GROUP_W)),
            full((1, GROUP_W)),
            full((1, GROUP_W)),
            full((DEC_SEQ, DEC_SEQ, GROUP_W)),
            full((DEC_SEQ, GROUP_W)),
            pl.BlockSpec((None, bt, RET_HEADS, RET_D, RET_D), lambda i: (layer, i, 0, 0, 0)),
            pl.BlockSpec((None, bt, CONV_W - 1, GROUP_W), lambda i: (layer, i, 0, 0)),
            pl.BlockSpec((None, bt, SC_W - 1, GROUP_W), lambda i: (layer, i, 0, 0)),
        ] + ([any_spec, any_spec] if stacked else []),
        out_specs=[
            pl.BlockSpec((bt * DEC_SEQ, D_MODEL), lambda i: (i, 0)),
            any_spec if stacked else pl.BlockSpec(ret_blk, lambda i: (i, 0, 0, 0)),
            any_spec if stacked else pl.BlockSpec(b31_blk, lambda i: (i, 0, 0)),
            pl.BlockSpec((bt, SC_W - 1, GROUP_W), lambda i: (i, 0, 0)),
            pl.BlockSpec((bt, DEC_SEQ, GROUP_W), lambda i: (i, 0, 0)),
        ],
        out_shape=[
            jax.ShapeDtypeStruct((T_SAMPLE, D_MODEL), F32),
            jax.ShapeDtypeStruct(((DEPTH,) if stacked else ()) + ret_all, F32),
            jax.ShapeDtypeStruct(((DEPTH,) if stacked else ()) + b31_all, F32),
            jax.ShapeDtypeStruct((DEC_BATCH, SC_W - 1, GROUP_W), F32),
            jax.ShapeDtypeStruct((DEC_BATCH, DEC_SEQ, GROUP_W), F32),
        ],
        scratch_shapes=[
            pltpu.VMEM((bt, EXT31_S, GROUP_W), F32),
            pltpu.VMEM((bt, EXT3_S, GROUP_W), F32),
        ] + ([
            pltpu.VMEM((2,) + ret_blk, F32),
            pltpu.VMEM((2,) + b31_blk, F32),
            pltpu.SemaphoreType.DMA((2, 2)),
            pltpu.SemaphoreType.DMA((2,)),
        ] if stacked else []),
        compiler_params=pltpu.CompilerParams(
            dimension_semantics=("arbitrary",), vmem_limit_bytes=V7X_VMEM_LIMIT),
        name="mixer_sample",
    )(z, cos, sin, din_t, dq_t, dk_t, dc_t, row(p["ret_norm_g"]), p["conv31_w"], row(p["conv31_b"]),
      row(p["conv_ln_g"]), row(p["conv_ln_b"]), p["sconv_w"], row(p["sgu_ln_g"]), row(p["sgu_ln_b"]),
      w_t, b_t, s_ret, s_31, s_3, *(prev if stacked else ()))


def _to_token_major(dst, val, pitch=TOK_ROWS):
    n = val.shape[0]
    for a in range(TOK_ROWS):
        dst[pl.ds(a, n, stride=pitch), :] = val[:, a * LANES:(a + 1) * LANES]


STAGE_PITCH = 24
DMA_UNROLL = 8


def _router_kernel(x_ref, g_ref, rw_ref, e_ref, p_ref, c_ref):
    h = _rms_rows(x_ref[...], g_ref[...])
    lane = lax.broadcasted_iota(jnp.int32, (h.shape[0], N_EXPERTS), 1)
    logits = jnp.zeros((h.shape[0], N_EXPERTS), F32)
    for e in range(N_EXPERTS):
        col = jnp.sum(h * rw_ref[pl.ds(e, 1), :], axis=-1, keepdims=True)
        logits = jnp.where(lane == e, col, logits)
    m1 = jnp.max(logits, axis=-1, keepdims=True)
    i1 = jnp.min(jnp.where(logits == m1, lane, N_EXPERTS), axis=-1, keepdims=True)
    rest = jnp.where(lane == i1, -jnp.inf, logits)
    m2 = jnp.max(rest, axis=-1, keepdims=True)
    i2 = jnp.min(jnp.where(rest == m2, lane, N_EXPERTS), axis=-1, keepdims=True)
    e2 = jnp.exp(m2 - m1)
    den = 1.0 + e2
    e_ref[...] = jnp.where(lane == 0, i1, jnp.where(lane == 1, i2, 0))
    p_ref[...] = jnp.where(lane == 0, 1.0 / den, jnp.where(lane == 1, e2 / den, 0.0))
    c_ref[pl.ds(0, 1), :] = jnp.sum(jnp.where(lane == i1, 1.0, 0.0), axis=0, keepdims=True).astype(jnp.int32)
    c_ref[pl.ds(1, 1), :] = jnp.sum(jnp.where(lane == i2, 1.0, 0.0), axis=0, keepdims=True).astype(jnp.int32)


def _router(x, g, rw):
    t, d = x.shape
    return pl.pallas_call(
        _router_kernel,
        grid=(N_TB,),
        in_specs=[
            pl.BlockSpec((MOE_TB, d), lambda i: (i, 0)),
            pl.BlockSpec((1, d), lambda i: (0, 0)),
            pl.BlockSpec((N_EXPERTS, d), lambda i: (0, 0)),
        ],
        out_specs=[
            pl.BlockSpec((MOE_TB, N_EXPERTS), lambda i: (i, 0)),
            pl.BlockSpec((MOE_TB, N_EXPERTS), lambda i: (i, 0)),
            pl.BlockSpec((None, TOP_K, N_EXPERTS), lambda i: (i, 0, 0)),
        ],
        out_shape=[
            jax.ShapeDtypeStruct((t, N_EXPERTS), jnp.int32),
            jax.ShapeDtypeStruct((t, N_EXPERTS), F32),
            jax.ShapeDtypeStruct((N_TB, TOP_K, N_EXPERTS), jnp.int32),
        ],
        compiler_params=pltpu.CompilerParams(
            dimension_semantics=("arbitrary",), vmem_limit_bytes=V7X_VMEM_LIMIT),
        name="moe_router",
    )(x, g.reshape(1, d), rw.T)


def _pick(table, idx):
    lanes = jnp.arange(N_EXPERTS, dtype=jnp.int32)
    return jnp.sum(jnp.where(idx[..., None] == lanes, table, 0), axis=-1)


def _expert_layout(cnt):
    tot = jnp.sum(cnt, axis=(0, 1))
    start = jnp.cumsum(tot) - tot
    blk = jnp.sum(cnt, axis=1)
    blk_base = start[None, :] + jnp.cumsum(blk, axis=0) - blk
    base = jnp.stack([blk_base, blk_base + cnt[:, 0]], axis=1)
    n_sb = (tot + SB - 1) // SB
    sb_end = jnp.cumsum(n_sb)
    n_used = sb_end[-1]
    s_idx = jnp.arange(NS_MAX, dtype=jnp.int32)
    s_cl = jnp.minimum(s_idx, n_used - 1)
    sb_e = jnp.sum(s_cl[:, None] >= sb_end[None, :], axis=-1).astype(jnp.int32)
    local = s_cl - _pick(sb_end - n_sb, sb_e)
    sb_row0 = _pick(start, sb_e) + local * SB
    sb_rows = jnp.where(s_idx < n_used, jnp.clip(_pick(tot, sb_e) - local * SB, 0, SB), 0)
    i32 = lambda a: a.astype(jnp.int32)
    return i32(base), sb_e, i32(sb_row0), i32(sb_rows), i32(n_used).reshape(1)


def _dest_kernel(e_ref, b_ref, d_ref):
    eid = e_ref[...]
    lane = lax.broadcasted_iota(jnp.int32, eid.shape, 1)
    row = lax.broadcasted_iota(jnp.int32, (MOE_TB, MOE_TB), 0)
    col = lax.broadcasted_iota(jnp.int32, (MOE_TB, MOE_TB), 1)
    earlier = jnp.where(row > col, 1.0, 0.0).astype(BF16)
    base = b_ref[...].astype(F32)

    def dest_of(k):
        mine = lane == eid[:, k:k + 1]
        before = jnp.dot(earlier, jnp.where(mine, 1.0, 0.0).astype(BF16), preferred_element_type=F32)
        return jnp.sum(jnp.where(mine, before + base[k:k + 1], 0.0), axis=-1, keepdims=True)

    d_ref[...] = jnp.where(lane == 0, dest_of(0), jnp.where(lane == 1, dest_of(1), 0.0)).astype(jnp.int32)


def _dest_rows(eid, base):
    return pl.pallas_call(
        _dest_kernel,
        grid=(N_TB,),
        in_specs=[
            pl.BlockSpec((MOE_TB, N_EXPERTS), lambda i: (i, 0)),
            pl.BlockSpec((None, TOP_K, N_EXPERTS), lambda i: (i, 0, 0)),
        ],
        out_specs=pl.BlockSpec((MOE_TB, N_EXPERTS), lambda i: (i, 0)),
        out_shape=jax.ShapeDtypeStruct((T_ALL, N_EXPERTS), jnp.int32),
        compiler_params=pltpu.CompilerParams(
            dimension_semantics=("arbitrary",), vmem_limit_bytes=V7X_VMEM_LIMIT),
        name="moe_dest_rows",
    )(eid, base)


def _dispatch_kernel(d_ref, x_ref, g_ref, xg_hbm, stage, zeros, sem, zsem):
    i = pl.program_id(0)
    n = pl.num_programs(0)
    slot = i % 2
    blk_rows = MOE_TB * TOK_ROWS

    def wait_slot(sl):
        for _ in range(TOP_K):
            pltpu.make_async_copy(stage.at[sl, pl.ds(0, blk_rows)], xg_hbm.at[pl.ds(0, blk_rows)], sem.at[sl]).wait()

    @pl.when(i >= 2)
    def _():
        wait_slot(slot)

    _to_token_major(stage.at[slot], _rms_rows(x_ref[...], g_ref[...]), STAGE_PITCH)

    def body(r, carry):
        src = stage.at[slot, pl.ds(pl.multiple_of(r * STAGE_PITCH, 8), TOK_ROWS)]
        for k in range(TOP_K):
            row = d_ref[0, 0, TOP_K * r + k]
            dst = xg_hbm.at[pl.ds(pl.multiple_of(row * TOK_ROWS, TOK_ROWS), TOK_ROWS)]
            pltpu.make_async_copy(src, dst, sem.at[slot]).start()
        return carry

    lax.fori_loop(0, MOE_TB, body, 0, unroll=DMA_UNROLL)

    @pl.when(i == n - 1)
    def _():
        zeros[...] = jnp.zeros_like(zeros)
        tail = [pltpu.make_async_copy(zeros, xg_hbm.at[pl.ds((N_SLOTS + q * (SUB // TOK_ROWS)) * TOK_ROWS, SUB)], zsem)
                for q in range(TOK_ROWS)]
        for cp in tail:
            cp.start()
        for cp in tail:
            cp.wait()
        wait_slot(slot)
        wait_slot(1 - slot)


def _dispatch(x, g, dest_flat):
    t, d = x.shape
    return pl.pallas_call(
        _dispatch_kernel,
        grid=(N_TB,),
        in_specs=[
            pl.BlockSpec((1, 1, TOP_K * MOE_TB), lambda i: (i, 0, 0), memory_space=pltpu.SMEM),
            pl.BlockSpec((MOE_TB, d), lambda i: (i, 0)),
            pl.BlockSpec((1, d), lambda i: (0, 0)),
        ],
        out_specs=pl.BlockSpec(memory_space=pl.ANY),
        out_shape=jax.ShapeDtypeStruct(((N_SLOTS + SUB) * TOK_ROWS, LANES), F32),
        scratch_shapes=[
            pltpu.VMEM((2, MOE_TB * STAGE_PITCH, LANES), F32),
            pltpu.VMEM((SUB, LANES), F32),
            pltpu.SemaphoreType.DMA((2,)),
            pltpu.SemaphoreType.DMA(()),
        ],
        compiler_params=pltpu.CompilerParams(
            dimension_semantics=("arbitrary",), vmem_limit_bytes=V7X_VMEM_LIMIT),
        name="moe_dispatch",
    )(dest_flat, x, g.reshape(1, d))


def _moe_ffn_kernel(se_ref, r0_ref, nr_ref, xg_hbm, w1_ref, w3_ref, w2_ref, y_hbm,
                    x_scr, acc, stage, w1b, w3b, w2b, sem_in, sem_out):
    s = pl.program_id(0)
    j = pl.program_id(1)
    nj = pl.num_programs(1)
    rows = nr_ref[s]
    row0 = r0_ref[s]
    nsub = lax.shift_right_logical(rows + (SUB - 1), int(math.log2(SUB)))
    sub_rows = SUB * TOK_ROWS

    is_middle = (j > 0) & (j < nj - 1)

    def cast_weights():
        ws = (w1_ref[...].astype(BF16), w3_ref[...].astype(BF16), w2_ref[...].astype(BF16))
        w1b[...], w3b[...], w2b[...] = ws
        return ws

    @pl.when((nsub > 0) & jnp.logical_not(is_middle))
    def _():
        cast_weights()

    def sub(r):
        return pl.ds(pl.multiple_of(r * SUB, SUB), SUB)

    def swiglu(xs, ws=None):
        w1, w3, w2 = ws if ws is not None else (w1b[...], w3b[...], w2b[...])
        a = jnp.dot(xs, w1, preferred_element_type=F32)
        b = jnp.dot(xs, w3, preferred_element_type=F32)
        u = (_silu(a) * b).astype(BF16)
        return jnp.dot(u, w2, preferred_element_type=F32)

    def tile(r):
        return swiglu(x_scr[sub(r), :])

    def in_copy(r, slot):
        first = pl.multiple_of((row0 + r * SUB) * TOK_ROWS, TOK_ROWS)
        return pltpu.make_async_copy(xg_hbm.at[pl.ds(first, sub_rows)], stage.at[slot], sem_in.at[slot])

    @pl.when(j == 0)
    def _():
        @pl.when(nsub > 0)
        def _():
            in_copy(0, 0).start()

        def body(r, carry):
            slot = r % 2
            in_copy(r, slot).wait()

            @pl.when(r + 1 < nsub)
            def _():
                in_copy(r + 1, 1 - slot).start()

            for a in range(TOK_ROWS):
                x_scr[sub(r), pl.ds(a * LANES, LANES)] = stage[slot, pl.ds(a, SUB, stride=TOK_ROWS), :].astype(BF16)
            acc[sub(r), :] = tile(r)
            return carry

        lax.fori_loop(0, nsub, body, 0)

        def clear(r, carry):
            x_scr[sub(r), :] = jnp.zeros((SUB, D_MODEL), BF16)
            acc[sub(r), :] = jnp.zeros((SUB, D_MODEL), F32)
            return carry

        lax.fori_loop(nsub, SB // SUB, clear, 0)

    @pl.when(is_middle)
    def _():
        for c in range(SB // CHUNK_ROWS):
            @pl.when(rows > c * CHUNK_ROWS)
            def _():
                ws = cast_weights() if c == 0 else None
                rows_c = pl.ds(c * CHUNK_ROWS, CHUNK_ROWS)
                acc[rows_c, :] += swiglu(x_scr[rows_c, :], ws)

    @pl.when(j == nj - 1)
    def _():
        def out_copies(r, act):
            nv = jnp.clip(rows - r * SUB, 0, SUB)
            bit = SUB
            while bit >= 1:
                off = nv & (-2 * bit)
                src = stage.at[0, pl.ds(pl.multiple_of(off * TOK_ROWS, TOK_ROWS), bit * TOK_ROWS)]
                first = pl.multiple_of((row0 + r * SUB + off) * TOK_ROWS, TOK_ROWS)
                cp = pltpu.make_async_copy(src, y_hbm.at[pl.ds(first, bit * TOK_ROWS)], sem_out)
                pl.when((nv & bit) != 0)(functools.partial(act, cp))
                bit //= 2

        def body(r, carry):
            y = acc[sub(r), :] + tile(r)

            @pl.when(r > 0)
            def _():
                out_copies(r - 1, lambda cp: cp.wait())

            _to_token_major(stage.at[0], y)
            out_copies(r, lambda cp: cp.start())
            return carry

        lax.fori_loop(0, nsub, body, 0)

        @pl.when(nsub > 0)
        def _():
            out_copies(nsub - 1, lambda cp: cp.wait())


def _moe_ffn(xg, w1, w3, w2, sb_e, sb_row0, sb_rows, n_used):
    d = w1.shape[1]
    f = w1.shape[2]
    nj = f // TF
    return pl.pallas_call(
        _moe_ffn_kernel,
        grid_spec=pltpu.PrefetchScalarGridSpec(
            num_scalar_prefetch=3,
            grid=(n_used[0], nj),
            in_specs=[
                pl.BlockSpec(memory_space=pl.ANY),
                pl.BlockSpec((None, d, TF), lambda s, j, se, r0, nr: (se[s], 0, j)),
                pl.BlockSpec((None, d, TF), lambda s, j, se, r0, nr: (se[s], 0, j)),
                pl.BlockSpec((None, TF, d), lambda s, j, se, r0, nr: (se[s], j, 0)),
            ],
            out_specs=pl.BlockSpec(memory_space=pl.ANY),
            scratch_shapes=[
                pltpu.VMEM((SB, d), BF16),
                pltpu.VMEM((SB, d), F32),
                pltpu.VMEM((2, SUB * TOK_ROWS, LANES), F32),
                pltpu.VMEM((d, TF), BF16),
                pltpu.VMEM((d, TF), BF16),
                pltpu.VMEM((TF, d), BF16),
                pltpu.SemaphoreType.DMA((2,)),
                pltpu.SemaphoreType.DMA(()),
            ],
        ),
        out_shape=jax.ShapeDtypeStruct((N_SLOTS * TOK_ROWS, LANES), F32),
        compiler_params=pltpu.CompilerParams(
            dimension_semantics=("arbitrary", "arbitrary"), vmem_limit_bytes=V7X_VMEM_LIMIT),
        name="moe_ffn",
    )(sb_e, sb_row0, sb_rows, xg, w1, w3, w2)


N_TB_PROMPT = T_PROMPT // MOE_TB


def _combine_kernel(d_ref, dn_ref, x_ref, p_ref, g_ref, y_hbm, op_ref, os_ref, stage, sem, *, final_norm):
    i = pl.program_id(0)
    n = pl.num_programs(0)
    slot = i % 2
    blk_rows = MOE_TB * TOK_ROWS

    def start_block(dref, sl):
        def body(r, carry):
            for k in range(TOP_K):
                row = dref[0, 0, TOP_K * r + k]
                src = y_hbm.at[pl.ds(pl.multiple_of(row * TOK_ROWS, TOK_ROWS), TOK_ROWS)]
                dst = stage.at[sl, k, pl.ds(pl.multiple_of(r * STAGE_PITCH, 8), TOK_ROWS)]
                pltpu.make_async_copy(src, dst, sem.at[sl]).start()
            return carry

        lax.fori_loop(0, MOE_TB, body, 0, unroll=DMA_UNROLL)

    @pl.when(i == 0)
    def _():
        start_block(d_ref, 0)

    @pl.when(i + 1 < n)
    def _():
        start_block(dn_ref, 1 - slot)

    for k in range(TOP_K):
        pltpu.make_async_copy(y_hbm.at[pl.ds(0, blk_rows)], stage.at[slot, k, pl.ds(0, blk_rows)], sem.at[slot]).wait()
    g0 = p_ref[:, 0:1]
    g1 = p_ref[:, 1:2]
    pieces = []
    for a in range(TOK_ROWS):
        y0 = stage[slot, 0, pl.ds(a, MOE_TB, stride=STAGE_PITCH), :]
        y1 = stage[slot, 1, pl.ds(a, MOE_TB, stride=STAGE_PITCH), :]
        pieces.append(x_ref[:, pl.ds(a * LANES, LANES)] + (y0 * g0 + y1 * g1))
    y = jnp.concatenate(pieces, axis=1)
    if final_norm:
        y = _rms_rows(y, g_ref[...])

    @pl.when(i < N_TB_PROMPT)
    def _():
        op_ref[...] = y

    @pl.when(i >= N_TB_PROMPT)
    def _():
        os_ref[...] = y


def _moe_combine(x, y, dest_flat, gate, g, final_norm):
    t, d = x.shape
    return pl.pallas_call(
        functools.partial(_combine_kernel, final_norm=final_norm),
        grid=(N_TB,),
        in_specs=[
            pl.BlockSpec((1, 1, TOP_K * MOE_TB), lambda i: (i, 0, 0), memory_space=pltpu.SMEM),
            pl.BlockSpec((1, 1, TOP_K * MOE_TB), lambda i: (jnp.minimum(i + 1, N_TB - 1), 0, 0),
                         memory_space=pltpu.SMEM),
            pl.BlockSpec((MOE_TB, d), lambda i: (i, 0)),
            pl.BlockSpec((MOE_TB, N_EXPERTS), lambda i: (i, 0)),
            pl.BlockSpec((1, d), lambda i: (0, 0)),
            pl.BlockSpec(memory_space=pl.ANY),
        ],
        out_specs=[
            pl.BlockSpec((MOE_TB, d), lambda i: (jnp.minimum(i, N_TB_PROMPT - 1), 0)),
            pl.BlockSpec((MOE_TB, d), lambda i: (jnp.maximum(i - N_TB_PROMPT, 0), 0)),
        ],
        out_shape=[
            jax.ShapeDtypeStruct((T_PROMPT, d), F32),
            jax.ShapeDtypeStruct((T_SAMPLE, d), F32),
        ],
        scratch_shapes=[
            pltpu.VMEM((2, TOP_K, MOE_TB * STAGE_PITCH, LANES), F32),
            pltpu.SemaphoreType.DMA((2,)),
        ],
        compiler_params=pltpu.CompilerParams(
            dimension_semantics=("arbitrary",), vmem_limit_bytes=V7X_VMEM_LIMIT),
        name="moe_combine",
    )(dest_flat, dest_flat, x, gate, g.reshape(1, d), y)


def _moe(x, norm_g, rw, w1, w3, w2, out_g, final_norm):
    eid, gate, cnt = _router(x, norm_g, rw)
    base, sb_e, sb_row0, sb_rows, n_used = _expert_layout(cnt)
    dest = _dest_rows(eid, base)
    dest_flat = dest[:, :TOP_K].reshape(N_TB, 1, TOP_K * MOE_TB)
    xg = _dispatch(x, norm_g, dest_flat)
    y = _moe_ffn(xg, w1, w3, w2, sb_e, sb_row0, sb_rows, n_used)
    return _moe_combine(x, y, dest_flat, gate, out_g, final_norm)


def kernel(x_prompt, x_sample, state_ret, state_conv31, state_conv3, mix_norm_g, w_in, ret_norm_g, conv31_w, conv31_b, conv_ln_g, conv_ln_b, sconv_w, sgu_ln_g, sgu_ln_b, sgu_w, sgu_b, w_out, ffn_norm_g, dense_w1, dense_w3, dense_w2, router_w, moe_w1, moe_w3, moe_w2, final_norm_g):
    assert DEPTH == 2, "layer 0 has the dense channel mixer; layer 1's expert mixer ends with the final norm"
    x, hn = _prep(x_prompt.reshape(T_PROMPT, D_MODEL), x_sample.reshape(T_SAMPLE, D_MODEL), mix_norm_g[0])
    rets_p, c31_p, c3_p, c3_s, vns = [], [], [], [], []
    ret_s = b31_s = None
    for l in range(DEPTH):
        p = dict(ret_norm_g=ret_norm_g[l], conv31_w=conv31_w[l], conv31_b=conv31_b[l],
                 conv_ln_g=conv_ln_g[l], conv_ln_b=conv_ln_b[l], sconv_w=sconv_w[l],
                 sgu_ln_g=sgu_ln_g[l], sgu_ln_b=sgu_ln_b[l], sgu_w=sgu_w[l], sgu_b=sgu_b[l])
        z = _in_proj(hn, w_in, l)
        mix_s, ret_s, b31_s, b3_s, vn_s = _mixer_sample(
            z, p, l, state_ret, state_conv31, state_conv3, prev=(ret_s, b31_s) if l == DEPTH - 1 else None)
        mix_p, ret_p, b31_p, b3_p = _mixer_prompt(z, p)
        x = _out_proj_residual(mix_p, mix_s, w_out, l, x)
        j = l // 2
        if l % 2 == 0:
            x, hn = _ffn_dense(x, ffn_norm_g[l], dense_w1[j], dense_w3[j], dense_w2[j], mix_norm_g[l + 1])
        else:
            y_p, y_s = _moe(x, ffn_norm_g[l], router_w[j], moe_w1[j], moe_w3[j], moe_w2[j],
                            final_norm_g, final_norm=True)
        rets_p.append(ret_p)
        c31_p.append(b31_p)
        c3_p.append(b3_p); c3_s.append(b3_s)
        vns.append(vn_s)
    y_prompt = y_p.reshape(BATCH, SEQ, D_MODEL)
    y_sample = y_s.reshape(DEC_BATCH, DEC_SEQ, D_MODEL)
    return (y_prompt, y_sample, jnp.stack(rets_p), ret_s, jnp.stack(c31_p), b31_s,
            jnp.stack(c3_p), jnp.stack(c3_s), jnp.stack(vns))
```

```python
import functools
import math

import jax
import jax.numpy as jnp
from jax import lax
from jax.experimental import pallas as pl
from jax.experimental.pallas import tpu as pltpu

F32 = jnp.float32
BF16 = jnp.bfloat16

D_MODEL = 2048
BATCH = 4
SEQ = 2048
DEPTH = 2
DEC_BATCH = 128
DEC_SEQ = 4
PAST_LEN = 16384
GROUP_W = 512
D_IN = 11 * GROUP_W
RET_HEADS = 4
RET_D = 128
CHUNK = 128
ROPE_THETA = 10000.0
CONV_W = 31
SC_W = 3
SGU_GROUPS = 4
D_FF = 5632
N_EXPERTS = 8
TOP_K = 2
EPS = 1e-6

T_PROMPT = BATCH * SEQ
T_SAMPLE = DEC_BATCH * DEC_SEQ
T_ALL = T_PROMPT + T_SAMPLE
N_CHUNKS = SEQ // CHUNK

C_Q, C_K, C_V, C_G, C_BL, C_BG, C_CB, C_CC, C_CH, C_DU, C_DV = (i * GROUP_W for i in range(11))

V7X_VMEM_LIMIT = 56 * 1024 * 1024

TM = 1088
TM_IN = 2176
TM_OUT = 512
N_OUT_PROMPT = T_PROMPT // TM_OUT
NORM_ROWS = 272
TN_IN = 512
TF = 256
SAMPLE_BT = 16

LANES = 128
SUBLANES = 8
TOK_ROWS = D_MODEL // LANES
MOE_TB = 256
N_TB = T_ALL // MOE_TB
N_SLOTS = T_ALL * TOP_K
SUB = 256
SB = 9 * SUB
CHUNK_ROWS = SB // 3
NS_MAX = N_SLOTS // SB + N_EXPERTS


def _rms_rows(x, g):
    ms = jnp.mean(x * x, axis=-1, keepdims=True)
    return (x * lax.rsqrt(ms + EPS)) * g


def _layernorm_rows(x, g, b):
    mu = jnp.mean(x, axis=-1, keepdims=True)
    xc = x - mu
    return xc * lax.rsqrt(jnp.mean(xc * xc, axis=-1, keepdims=True) + EPS) * g + b


def _silu(x):
    return x * jax.nn.sigmoid(x)


def _norm_block_to(x_ref, g_ref, dst_ref):
    rows = x_ref.shape[0]
    g = g_ref[...]

    def body(c, carry):
        r0 = pl.multiple_of(c * NORM_ROWS, NORM_ROWS)
        x = x_ref[pl.ds(r0, NORM_ROWS), :]
        dst_ref[pl.ds(r0, NORM_ROWS), :] = _rms_rows(x, g).astype(dst_ref.dtype)
        return carry

    lax.fori_loop(0, rows // NORM_ROWS, body, 0)


PREP_ROWS = 512
N_PREP_PROMPT = T_PROMPT // PREP_ROWS


def _prep_kernel(xp_ref, xs_ref, g_ref, x_ref, hn_ref):
    def emit(src_ref):
        x = src_ref[...]
        x_ref[...] = x
        hn_ref[...] = _rms_rows(x, g_ref[...]).astype(hn_ref.dtype)

    is_prompt = pl.program_id(0) < N_PREP_PROMPT
    pl.when(is_prompt)(functools.partial(emit, xp_ref))
    pl.when(jnp.logical_not(is_prompt))(functools.partial(emit, xs_ref))


def _prep(x_prompt, x_sample, g):
    d = x_prompt.shape[1]
    blk = lambda f: pl.BlockSpec((PREP_ROWS, d), f)
    return pl.pallas_call(
        _prep_kernel,
        grid=(T_ALL // PREP_ROWS,),
        in_specs=[
            blk(lambda i: (jnp.minimum(i, N_PREP_PROMPT - 1), 0)),
            blk(lambda i: (jnp.maximum(i - N_PREP_PROMPT, 0), 0)),
            pl.BlockSpec((1, d), lambda i: (0, 0)),
        ],
        out_specs=[blk(lambda i: (i, 0)), blk(lambda i: (i, 0))],
        out_shape=[jax.ShapeDtypeStruct((T_ALL, d), F32), jax.ShapeDtypeStruct((T_ALL, d), BF16)],
        compiler_params=pltpu.CompilerParams(
            dimension_semantics=("arbitrary",), vmem_limit_bytes=V7X_VMEM_LIMIT),
        name="prep_rows",
    )(x_prompt, x_sample, g.reshape(1, d))


def _in_proj_kernel(h_ref, w_ref, o_ref):
    o_ref[...] = jnp.dot(h_ref[...], w_ref[...].astype(BF16), preferred_element_type=F32)


def _in_proj(hn, w_all, layer):
    t, d = hn.shape
    n = w_all.shape[2]
    return pl.pallas_call(
        _in_proj_kernel,
        grid=(t // TM_IN, n // TN_IN),
        in_specs=[
            pl.BlockSpec((TM_IN, d), lambda i, j: (i, 0)),
            pl.BlockSpec((None, d, TN_IN), lambda i, j: (layer, 0, j)),
        ],
        out_specs=pl.BlockSpec((TM_IN, TN_IN), lambda i, j: (i, j)),
        out_shape=jax.ShapeDtypeStruct((t, n), F32),
        compiler_params=pltpu.CompilerParams(
            dimension_semantics=("arbitrary", "arbitrary"), vmem_limit_bytes=V7X_VMEM_LIMIT),
        name="in_proj",
    )(hn, w_all)


W_LOAD_ROWS = 512


def _out_proj_kernel(ap_ref, as_ref, w_hbm, r_ref, o_ref, wb, stage, sem, *, layer):
    i = pl.program_id(0)

    @pl.when(i == 0)
    def _():
        for c in range(wb.shape[0] // W_LOAD_ROWS):
            rows = pl.ds(c * W_LOAD_ROWS, W_LOAD_ROWS)
            cp = pltpu.make_async_copy(w_hbm.at[layer, rows], stage, sem)
            cp.start()
            cp.wait()
            wb[rows, :] = stage[...].astype(BF16)

    def emit(a):
        o_ref[...] = r_ref[...] + jnp.dot(a, wb[...], preferred_element_type=F32)

    pl.when(i < N_OUT_PROMPT)(lambda: emit(ap_ref[...]))
    pl.when(i >= N_OUT_PROMPT)(lambda: emit(as_ref[...].astype(BF16)))


def _out_proj_residual(a_prompt, a_sample, w_all, layer, r):
    k = a_prompt.shape[1]
    t = r.shape[0]
    n = w_all.shape[2]
    return pl.pallas_call(
        functools.partial(_out_proj_kernel, layer=layer),
        grid=(t // TM_OUT,),
        in_specs=[
            pl.BlockSpec((TM_OUT, k), lambda i: (jnp.minimum(i, N_OUT_PROMPT - 1), 0)),
            pl.BlockSpec((TM_OUT, k), lambda i: (0, 0)),
            pl.BlockSpec(memory_space=pl.ANY),
            pl.BlockSpec((TM_OUT, n), lambda i: (i, 0)),
        ],
        out_specs=pl.BlockSpec((TM_OUT, n), lambda i: (i, 0)),
        out_shape=jax.ShapeDtypeStruct((t, n), F32),
        scratch_shapes=[
            pltpu.VMEM((k, n), BF16),
            pltpu.VMEM((W_LOAD_ROWS, n), F32),
            pltpu.SemaphoreType.DMA(()),
        ],
        compiler_params=pltpu.CompilerParams(
            dimension_semantics=("arbitrary",), vmem_limit_bytes=V7X_VMEM_LIMIT),
        name="out_proj_residual",
    )(a_prompt, a_sample, w_all, r)


def _swiglu_tile(h, w1_ref, w3_ref, w2_ref):
    a = jnp.dot(h, w1_ref[...].astype(BF16), preferred_element_type=F32)
    b = jnp.dot(h, w3_ref[...].astype(BF16), preferred_element_type=F32)
    u = (_silu(a) * b).astype(BF16)
    return jnp.dot(u, w2_ref[...].astype(BF16), preferred_element_type=F32)


def _ffn_kernel(x_hbm, g_ref, w1_ref, w3_ref, w2_ref, gn_ref, o_ref, hnext_ref, hn_ref, sem):
    @pl.when(pl.program_id(1) == 0)
    def _():
        r0 = pl.multiple_of(pl.program_id(0) * TM, TM)
        cp = pltpu.make_async_copy(x_hbm.at[pl.ds(r0, TM)], o_ref, sem)
        cp.start()
        cp.wait()
        _norm_block_to(o_ref, g_ref, hn_ref)

    o_ref[...] += _swiglu_tile(hn_ref[...], w1_ref, w3_ref, w2_ref)

    @pl.when(pl.program_id(1) == pl.num_programs(1) - 1)
    def _():
        _norm_block_to(o_ref, gn_ref, hnext_ref)


def _ffn_dense(x, g, w1, w3, w2, g_next):
    t, d = x.shape
    f = w1.shape[1]
    return pl.pallas_call(
        _ffn_kernel,
        grid=(t // TM, f // TF),
        in_specs=[
            pl.BlockSpec(memory_space=pl.ANY),
            pl.BlockSpec((1, d), lambda i, j: (0, 0)),
            pl.BlockSpec((d, TF), lambda i, j: (0, j)),
            pl.BlockSpec((d, TF), lambda i, j: (0, j)),
            pl.BlockSpec((TF, d), lambda i, j: (j, 0)),
            pl.BlockSpec((1, d), lambda i, j: (0, 0)),
        ],
        out_specs=[
            pl.BlockSpec((TM, d), lambda i, j: (i, 0)),
            pl.BlockSpec((TM, d), lambda i, j: (i, 0)),
        ],
        out_shape=[jax.ShapeDtypeStruct((t, d), F32), jax.ShapeDtypeStruct((t, d), BF16)],
        scratch_shapes=[pltpu.VMEM((TM, d), BF16), pltpu.SemaphoreType.DMA(())],
        compiler_params=pltpu.CompilerParams(
            dimension_semantics=("arbitrary", "arbitrary"), vmem_limit_bytes=V7X_VMEM_LIMIT),
        name="ffn_dense",
    )(x, g.reshape(1, d), w1, w3, w2, g_next.reshape(1, d))


def _rotary_tables(length, pos0):
    half = RET_D // 2
    inv = ROPE_THETA ** (-jnp.arange(half, dtype=F32) / half)
    pos = jnp.arange(length, dtype=F32) + pos0
    ang = pos[:, None] * inv[None, :]
    cos, sin = jnp.cos(ang), jnp.sin(ang)
    return jnp.concatenate([cos, cos], axis=-1), jnp.concatenate([-sin, sin], axis=-1)


def _decay_tables(c):
    log_g = jnp.log1p(-jnp.exp2(-5.0 - jnp.arange(RET_HEADS, dtype=F32)))
    idx = jnp.arange(c, dtype=F32)
    rel = idx[:, None] - idx[None, :]
    d_in = jnp.where(rel >= 0, jnp.exp(log_g[:, None, None] * jnp.maximum(rel, 0.0)), 0.0)
    d_q = jnp.exp(log_g[:, None] * (idx + 1.0)[None, :])
    d_k = jnp.exp(log_g[:, None] * (c - 1.0 - idx)[None, :])
    d_c = jnp.exp(log_g * c)
    return d_in, d_q, d_k, d_c


HIST31 = 32
HIST3 = 8


def _mixer_prompt_kernel(z_ref, cos_ref, sin_ref, din_ref, dq_ref, dk_ref, dc_ref, retg_ref,
                         c31w_ref, c31b_ref, clng_ref, clnb_ref, scw_ref, slng_ref, slnb_ref,
                         sguw_ref, sgub_ref,
                         mix_ref, ret_ref, b31_ref, b3_ref,
                         s_scr, ext31_scr, ext3_scr, ysh_scr):
    c = pl.program_id(1)

    @pl.when(c == 0)
    def _():
        s_scr[...] = jnp.zeros_like(s_scr)
        ext31_scr[pl.ds(0, HIST31), :] = jnp.zeros((HIST31, GROUP_W), F32)
        ext3_scr[pl.ds(0, HIST3), :] = jnp.zeros((HIST3, GROUP_W), F32)

    cos = cos_ref[...]
    sin = sin_ref[...]

    for h in range(RET_HEADS):
        lo = h * RET_D
        q = z_ref[:, pl.ds(C_Q + lo, RET_D)]
        k = z_ref[:, pl.ds(C_K + lo, RET_D)]
        v = z_ref[:, pl.ds(C_V + lo, RET_D)]
        g = z_ref[:, pl.ds(C_G + lo, RET_D)]
        q = q * cos + pltpu.roll(q, RET_D // 2, axis=1) * sin
        k = (k * cos + pltpu.roll(k, RET_D // 2, axis=1) * sin) * (RET_D ** -0.5)
        qb = q.astype(BF16)
        vb = v.astype(BF16)
        scores = lax.dot_general(qb, k.astype(BF16), (((1,), (1,)), ((), ())),
                                 preferred_element_type=F32) * din_ref[h]
        inner = jnp.dot(scores.astype(BF16), vb, preferred_element_type=F32)
        s_old = s_scr[h]
        cross = jnp.dot(qb, s_old.astype(BF16), preferred_element_type=F32) * dq_ref[h]
        kd = (k * dk_ref[h]).astype(BF16)
        s_scr[h] = s_old * dc_ref[h] + lax.dot_general(kd, vb, (((0,), (0,)), ((), ())),
                                                       preferred_element_type=F32)
        r = inner + cross
        r = r * lax.rsqrt(jnp.mean(r * r, axis=-1, keepdims=True) + EPS)
        r = r * retg_ref[:, pl.ds(lo, RET_D)]
        mix_ref[:, pl.ds(lo, RET_D)] = (_silu(g) * r).astype(mix_ref.dtype)

    glu = z_ref[:, pl.ds(C_BL, GROUP_W)] * jax.nn.sigmoid(z_ref[:, pl.ds(C_BG, GROUP_W)])
    ext31_scr[pl.ds(HIST31, CHUNK), :] = glu
    s0 = HIST31 - (CONV_W - 1)
    conv = None
    for i in range(SUBLANES):
        n_rows = CHUNK if i == 0 else CHUNK + SUBLANES
        y = None
        for m in range(HIST31 // SUBLANES + 1):
            j = SUBLANES * m + i - s0
            if 0 <= j < CONV_W:
                term = ext31_scr[pl.ds(SUBLANES * m, n_rows), :] * c31w_ref[pl.ds(j, 1), :]
                y = term if y is None else y + term
        if i == 0:
            conv = y
        else:
            ysh_scr[i - 1] = y
            conv = conv + ysh_scr[i - 1, pl.ds(i, CHUNK), :]
    conv = conv + c31b_ref[...]
    out_b = _silu(_layernorm_rows(conv, clng_ref[...], clnb_ref[...]))
    mix_ref[:, pl.ds(GROUP_W, GROUP_W)] = out_b.astype(mix_ref.dtype)
    ext31_scr[pl.ds(0, HIST31), :] = ext31_scr[pl.ds(CHUNK, HIST31), :]

    gated = z_ref[:, pl.ds(C_CC, GROUP_W)] * z_ref[:, pl.ds(C_CH, GROUP_W)]
    ext3_scr[pl.ds(HIST3, CHUNK), :] = gated
    conv3 = ext3_scr[pl.ds(HIST3 - (SC_W - 1), CHUNK), :] * scw_ref[pl.ds(0, 1), :]
    for j in range(1, SC_W):
        conv3 = conv3 + ext3_scr[pl.ds(HIST3 - (SC_W - 1) + j, CHUNK), :] * scw_ref[pl.ds(j, 1), :]
    mix_ref[:, pl.ds(2 * GROUP_W, GROUP_W)] = (z_ref[:, pl.ds(C_CB, GROUP_W)] * conv3).astype(mix_ref.dtype)
    ext3_scr[pl.ds(0, HIST3), :] = ext3_scr[pl.ds(CHUNK, HIST3), :]

    vn = _layernorm_rows(z_ref[:, pl.ds(C_DV, GROUP_W)], slng_ref[...], slnb_ref[...])
    row = lax.broadcasted_iota(jnp.int32, (CHUNK, CHUNK), 0)
    col = lax.broadcasted_iota(jnp.int32, (CHUNK, CHUNK), 1)
    ch = GROUP_W // SGU_GROUPS
    for gi in range(SGU_GROUPS):
        wm = jnp.where(row >= col, sguw_ref[gi], 0.0).astype(BF16)
        mixed = jnp.dot(wm, vn[:, gi * ch:(gi + 1) * ch].astype(BF16), preferred_element_type=F32)
        mixed = mixed + sgub_ref[:, pl.ds(gi, 1)]
        du = z_ref[:, pl.ds(C_DU + gi * ch, ch)]
        mix_ref[:, pl.ds(3 * GROUP_W + gi * ch, ch)] = (du * mixed).astype(mix_ref.dtype)

    @pl.when(c == pl.num_programs(1) - 1)
    def _():
        ret_ref[...] = s_scr[...]
        b31_ref[...] = ext31_scr[pl.ds(HIST31 - (CONV_W - 1), CONV_W - 1), :]
        b3_ref[...] = ext3_scr[pl.ds(HIST3 - (SC_W - 1), SC_W - 1), :]


def _mixer_prompt(z, p):
    cos, sin = _rotary_tables(SEQ, 0)
    d_in, d_q, d_k, d_c = _decay_tables(CHUNK)
    dq_t = jnp.broadcast_to(d_q[:, :, None], (RET_HEADS, CHUNK, RET_D))
    dk_t = jnp.broadcast_to(d_k[:, :, None], (RET_HEADS, CHUNK, RET_D))
    dc_t = jnp.broadcast_to(d_c[:, None, None], (RET_HEADS, RET_D, RET_D))

    def full(shape):
        return pl.BlockSpec(shape, lambda b, c: (0,) * len(shape))

    row = lambda a: a.reshape(1, -1)
    return pl.pallas_call(
        _mixer_prompt_kernel,
        grid=(BATCH, N_CHUNKS),
        in_specs=[
            pl.BlockSpec((CHUNK, D_IN), lambda b, c: (b * N_CHUNKS + c, 0)),
            pl.BlockSpec((CHUNK, RET_D), lambda b, c: (c, 0)),
            pl.BlockSpec((CHUNK, RET_D), lambda b, c: (c, 0)),
            full((RET_HEADS, CHUNK, CHUNK)),
            full((RET_HEADS, CHUNK, RET_D)),
            full((RET_HEADS, CHUNK, RET_D)),
            full((RET_HEADS, RET_D, RET_D)),
            full((1, GROUP_W)),
            full((CONV_W, GROUP_W)),
            full((1, GROUP_W)),
            full((1, GROUP_W)),
            full((1, GROUP_W)),
            full((SC_W, GROUP_W)),
            full((1, GROUP_W)),
            full((1, GROUP_W)),
            full((SGU_GROUPS, CHUNK, CHUNK)),
            full((CHUNK, SGU_GROUPS)),
        ],
        out_specs=[
            pl.BlockSpec((CHUNK, D_MODEL), lambda b, c: (b * N_CHUNKS + c, 0)),
            pl.BlockSpec((None, RET_HEADS, RET_D, RET_D), lambda b, c: (b, 0, 0, 0)),
            pl.BlockSpec((None, CONV_W - 1, GROUP_W), lambda b, c: (b, 0, 0)),
            pl.BlockSpec((None, SC_W - 1, GROUP_W), lambda b, c: (b, 0, 0)),
        ],
        out_shape=[
            jax.ShapeDtypeStruct((T_PROMPT, D_MODEL), BF16),
            jax.ShapeDtypeStruct((BATCH, RET_HEADS, RET_D, RET_D), F32),
            jax.ShapeDtypeStruct((BATCH, CONV_W - 1, GROUP_W), F32),
            jax.ShapeDtypeStruct((BATCH, SC_W - 1, GROUP_W), F32),
        ],
        scratch_shapes=[
            pltpu.VMEM((RET_HEADS, RET_D, RET_D), F32),
            pltpu.VMEM((HIST31 + CHUNK, GROUP_W), F32),
            pltpu.VMEM((HIST3 + CHUNK, GROUP_W), F32),
            pltpu.VMEM((SUBLANES - 1, CHUNK + SUBLANES, GROUP_W), F32),
        ],
        compiler_params=pltpu.CompilerParams(
            dimension_semantics=("arbitrary", "arbitrary"), vmem_limit_bytes=V7X_VMEM_LIMIT),
        name="mixer_prompt",
    )(z, cos, sin, d_in, dq_t, dk_t, dc_t, row(p["ret_norm_g"]), p["conv31_w"], row(p["conv31_b"]),
      row(p["conv_ln_g"]), row(p["conv_ln_b"]), p["sconv_w"], row(p["sgu_ln_g"]), row(p["sgu_ln_b"]),
      p["sgu_w"], p["sgu_b"].T)


EXT31_S = 40
EXT3_S = 8


N_SAMPLE_IN = 20


def _mixer_sample_stacked_kernel(*refs):
    ins = refs[:N_SAMPLE_IN]
    prev_ret, prev_b31 = refs[N_SAMPLE_IN:N_SAMPLE_IN + 2]
    mix_ref, ret_hbm, b31_hbm, b3_ref, vn_ref = refs[N_SAMPLE_IN + 2:N_SAMPLE_IN + 7]
    ext31_scr, ext3_scr, ret_scr, b31_scr, wr_sem, cp_sem = refs[N_SAMPLE_IN + 7:]
    i = pl.program_id(0)
    n = pl.num_programs(0)
    slot = i % 2
    bt = ret_scr.shape[1]
    blk = pl.ds(pl.multiple_of(i * bt, bt), bt)

    def writes(sl):
        return (pltpu.make_async_copy(ret_scr.at[sl], ret_hbm.at[1, blk], wr_sem.at[sl, 0]),
                pltpu.make_async_copy(b31_scr.at[sl], b31_hbm.at[1, blk], wr_sem.at[sl, 1]))

    copies = (pltpu.make_async_copy(prev_ret, ret_hbm.at[0, blk], cp_sem.at[0]),
              pltpu.make_async_copy(prev_b31, b31_hbm.at[0, blk], cp_sem.at[1]))

    @pl.when(i >= 2)
    def _():
        for w in writes(slot):
            w.wait()

    for c in copies:
        c.start()
    _mixer_sample_kernel(*ins, mix_ref, ret_scr.at[slot], b31_scr.at[slot], b3_ref, vn_ref, ext31_scr, ext3_scr)
    ws = writes(slot)
    for w in ws:
        w.start()
    for c in copies:
        c.wait()

    @pl.when(i == n - 1)
    def _():
        for w in ws:
            w.wait()
        for w in writes(1 - slot):
            w.wait()


def _mixer_sample_kernel(z_ref, cos_ref, sin_ref, din_ref, dq_ref, dk_ref, dc_ref, retg_ref,
                         c31w_ref, c31b_ref, clng_ref, clnb_ref, scw_ref, slng_ref, slnb_ref,
                         sguw_ref, sgub_ref, sret_ref, s31_ref, s3_ref,
                         mix_ref, ret_ref, b31_ref, b3_ref, vn_ref,
                         ext31_scr, ext3_scr):
    cos = cos_ref[...]
    sin = sin_ref[...]
    n_hist31 = CONV_W - 1
    n_hist3 = SC_W - 1
    bt = sret_ref.shape[0]

    def zs(col, width):
        return jnp.stack([z_ref[pl.ds(DEC_SEQ * b, DEC_SEQ), pl.ds(col, width)] for b in range(bt)])

    def zrow(s, col, width):
        return jnp.stack([z_ref[pl.ds(DEC_SEQ * b + s, 1), pl.ds(col, width)] for b in range(bt)])

    def put(col, val):
        for b in range(bt):
            mix_ref[pl.ds(DEC_SEQ * b, DEC_SEQ), pl.ds(col, val.shape[-1])] = val[b]

    for h in range(RET_HEADS):
        lo = h * RET_D
        q = zs(C_Q + lo, RET_D)
        k = zs(C_K + lo, RET_D)
        g = zs(C_G + lo, RET_D)
        q = q * cos + pltpu.roll(q, RET_D // 2, axis=2) * sin
        k = (k * cos + pltpu.roll(k, RET_D // 2, axis=2) * sin) * (RET_D ** -0.5)
        s_old = sret_ref[:, h]
        r = lax.dot_general(q, s_old, (((2,), (1,)), ((0,), (0,))),
                            preferred_element_type=F32) * dq_ref[h]
        for s in range(DEC_SEQ):
            k_s = zrow(s, C_K + lo, RET_D)
            k_s = (k_s * cos[s:s + 1] + pltpu.roll(k_s, RET_D // 2, axis=2) * sin[s:s + 1]) * (RET_D ** -0.5)
            v_s = zrow(s, C_V + lo, RET_D)
            score = jnp.sum(q * k_s, axis=-1, keepdims=True) * din_ref[h, s]
            r = r + score * v_s
        kd = k * dk_ref[h]
        v = zs(C_V + lo, RET_D)
        ret_ref[:, h] = s_old * dc_ref[h] + lax.dot_general(
            kd, v, (((1,), (1,)), ((0,), (0,))), preferred_element_type=F32)
        r = r * lax.rsqrt(jnp.mean(r * r, axis=-1, keepdims=True) + EPS)
        r = r * retg_ref[:, pl.ds(lo, RET_D)]
        put(lo, _silu(g) * r)

    glu = zs(C_BL, GROUP_W) * jax.nn.sigmoid(zs(C_BG, GROUP_W))
    ext31_scr[:, pl.ds(0, n_hist31), :] = s31_ref[...]
    ext31_scr[:, pl.ds(n_hist31, DEC_SEQ), :] = glu
    conv = ext31_scr[:, pl.ds(0, DEC_SEQ), :] * c31w_ref[pl.ds(0, 1), :]
    for j in range(1, CONV_W):
        conv = conv + ext31_scr[:, pl.ds(j, DEC_SEQ), :] * c31w_ref[pl.ds(j, 1), :]
    conv = conv + c31b_ref[...]
    put(GROUP_W, _silu(_layernorm_rows(conv, clng_ref[...], clnb_ref[...])))
    b31_ref[...] = ext31_scr[:, pl.ds(DEC_SEQ, n_hist31), :]

    gated = zs(C_CC, GROUP_W) * zs(C_CH, GROUP_W)
    ext3_scr[:, pl.ds(0, n_hist3), :] = s3_ref[...]
    ext3_scr[:, pl.ds(n_hist3, DEC_SEQ), :] = gated
    conv3 = ext3_scr[:, pl.ds(0, DEC_SEQ), :] * scw_ref[pl.ds(0, 1), :]
    for j in range(1, SC_W):
        conv3 = conv3 + ext3_scr[:, pl.ds(j, DEC_SEQ), :] * scw_ref[pl.ds(j, 1), :]
    put(2 * GROUP_W, zs(C_CB, GROUP_W) * conv3)
    b3_ref[...] = ext3_scr[:, pl.ds(DEC_SEQ, n_hist3), :]

    vn = _layernorm_rows(zs(C_DV, GROUP_W), slng_ref[...], slnb_ref[...])
    vn_ref[...] = vn
    mixed = sgub_ref[...]
    for s in range(DEC_SEQ):
        mixed = mixed + sguw_ref[s] * vn_ref[:, pl.ds(s, 1), :]
    put(3 * GROUP_W, zs(C_DU, GROUP_W) * mixed)


def _mixer_sample(z, p, layer, s_ret, s_31, s_3, prev=None):
    cos, sin = _rotary_tables(DEC_SEQ, PAST_LEN)
    d_in, d_q, d_k, d_c = _decay_tables(DEC_SEQ)
    din_t = jnp.swapaxes(d_in, 1, 2)[:, :, :, None]
    dq_t = jnp.broadcast_to(d_q[:, :, None], (RET_HEADS, DEC_SEQ, RET_D))
    dk_t = jnp.broadcast_to(d_k[:, :, None], (RET_HEADS, DEC_SEQ, RET_D))
    dc_t = jnp.broadcast_to(d_c[:, None, None], (RET_HEADS, RET_D, RET_D))
    ch = GROUP_W // SGU_GROUPS
    w4 = jnp.tril(p["sgu_w"][:, :DEC_SEQ, :DEC_SEQ])
    w_t = jnp.repeat(jnp.transpose(w4, (2, 1, 0)), ch, axis=-1)
    b_t = jnp.repeat(p["sgu_b"][:, :DEC_SEQ].T, ch, axis=-1)
    bt = SAMPLE_BT

    def full(shape):
        return pl.BlockSpec(shape, lambda i: (0,) * len(shape))

    row = lambda a: a.reshape(1, -1)
    any_spec = pl.BlockSpec(memory_space=pl.ANY)
    ret_blk, b31_blk = (bt, RET_HEADS, RET_D, RET_D), (bt, CONV_W - 1, GROUP_W)
    ret_all, b31_all = (DEC_BATCH,) + ret_blk[1:], (DEC_BATCH,) + b31_blk[1:]
    stacked = prev is not None
    return pl.pallas_call(
        _mixer_sample_stacked_kernel if stacked else _mixer_sample_kernel,
        grid=(DEC_BATCH // bt,),
        in_specs=[
            pl.BlockSpec((bt * DEC_SEQ, D_IN), lambda i: (T_PROMPT // (bt * DEC_SEQ) + i, 0)),
            full((DEC_SEQ, RET_D)),
            full((DEC_SEQ, RET_D)),
            full((RET_HEADS, DEC_SEQ, DEC_SEQ, 1)),
            full((RET_HEADS, DEC_SEQ, RET_D)),
            full((RET_HEADS, DEC_SEQ, RET_D)),
            full((RET_HEADS, RET_D, RET_D)),
            full((1, GROUP_W)),
            full((CONV_W, GROUP_W)),
            full((1, GROUP_W)),
            full((1, GROUP_W)),
            full((1, GROUP_W)),
            full((SC_W, GROUP_W)),
            full((1, GROUP_W)),
            full((1, GROUP_W)),
            full((DEC_SEQ, DEC_SEQ, GROUP_W)),
            full((DEC_SEQ, GROUP_W)),
            pl.BlockSpec((None, bt, RET_HEADS, RET_D, RET_D), lambda i: (layer, i, 0, 0, 0)),
            pl.BlockSpec((None, bt, CONV_W - 1, GROUP_W), lambda i: (layer, i, 0, 0)),
            pl.BlockSpec((None, bt, SC_W - 1, GROUP_W), lambda i: (layer, i, 0, 0)),
        ] + ([pl.BlockSpec(ret_blk, lambda i: (i, 0, 0, 0)), pl.BlockSpec(b31_blk, lambda i: (i, 0, 0))]
             if stacked else []),
        out_specs=[
            pl.BlockSpec((bt * DEC_SEQ, D_MODEL), lambda i: (i, 0)),
            any_spec if stacked else pl.BlockSpec(ret_blk, lambda i: (i, 0, 0, 0)),
            any_spec if stacked else pl.BlockSpec(b31_blk, lambda i: (i, 0, 0)),
            pl.BlockSpec((bt, SC_W - 1, GROUP_W), lambda i: (i, 0, 0)),
            pl.BlockSpec((bt, DEC_SEQ, GROUP_W), lambda i: (i, 0, 0)),
        ],
        out_shape=[
            jax.ShapeDtypeStruct((T_SAMPLE, D_MODEL), F32),
            jax.ShapeDtypeStruct(((DEPTH,) if stacked else ()) + ret_all, F32),
            jax.ShapeDtypeStruct(((DEPTH,) if stacked else ()) + b31_all, F32),
            jax.ShapeDtypeStruct((DEC_BATCH, SC_W - 1, GROUP_W), F32),
            jax.ShapeDtypeStruct((DEC_BATCH, DEC_SEQ, GROUP_W), F32),
        ],
        scratch_shapes=[
            pltpu.VMEM((bt, EXT31_S, GROUP_W), F32),
            pltpu.VMEM((bt, EXT3_S, GROUP_W), F32),
        ] + ([
            pltpu.VMEM((2,) + ret_blk, F32),
            pltpu.VMEM((2,) + b31_blk, F32),
            pltpu.SemaphoreType.DMA((2, 2)),
            pltpu.SemaphoreType.DMA((2,)),
        ] if stacked else []),
        compiler_params=pltpu.CompilerParams(
            dimension_semantics=("arbitrary",), vmem_limit_bytes=V7X_VMEM_LIMIT),
        name="mixer_sample",
    )(z, cos, sin, din_t, dq_t, dk_t, dc_t, row(p["ret_norm_g"]), p["conv31_w"], row(p["conv31_b"]),
      row(p["conv_ln_g"]), row(p["conv_ln_b"]), p["sconv_w"], row(p["sgu_ln_g"]), row(p["sgu_ln_b"]),
      w_t, b_t, s_ret, s_31, s_3, *(prev if stacked else ()))


def _to_token_major(dst, val, pitch=TOK_ROWS):
    n = val.shape[0]
    for a in range(TOK_ROWS):
        dst[pl.ds(a, n, stride=pitch), :] = val[:, a * LANES:(a + 1) * LANES]


STAGE_PITCH = 24
DMA_UNROLL = 8


def _router_kernel(x_ref, g_ref, rw_ref, e_ref, p_ref, c_ref):
    h = _rms_rows(x_ref[...], g_ref[...])
    lane = lax.broadcasted_iota(jnp.int32, (h.shape[0], N_EXPERTS), 1)
    logits = jnp.zeros((h.shape[0], N_EXPERTS), F32)
    for e in range(N_EXPERTS):
        col = jnp.sum(h * rw_ref[pl.ds(e, 1), :], axis=-1, keepdims=True)
        logits = jnp.where(lane == e, col, logits)
    m1 = jnp.max(logits, axis=-1, keepdims=True)
    i1 = jnp.min(jnp.where(logits == m1, lane, N_EXPERTS), axis=-1, keepdims=True)
    rest = jnp.where(lane == i1, -jnp.inf, logits)
    m2 = jnp.max(rest, axis=-1, keepdims=True)
    i2 = jnp.min(jnp.where(rest == m2, lane, N_EXPERTS), axis=-1, keepdims=True)
    e2 = jnp.exp(m2 - m1)
    den = 1.0 + e2
    e_ref[...] = jnp.where(lane == 0, i1, jnp.where(lane == 1, i2, 0))
    p_ref[...] = jnp.where(lane == 0, 1.0 / den, jnp.where(lane == 1, e2 / den, 0.0))
    c_ref[pl.ds(0, 1), :] = jnp.sum(jnp.where(lane == i1, 1.0, 0.0), axis=0, keepdims=True).astype(jnp.int32)
    c_ref[pl.ds(1, 1), :] = jnp.sum(jnp.where(lane == i2, 1.0, 0.0), axis=0, keepdims=True).astype(jnp.int32)


def _router(x, g, rw):
    t, d = x.shape
    return pl.pallas_call(
        _router_kernel,
        grid=(N_TB,),
        in_specs=[
            pl.BlockSpec((MOE_TB, d), lambda i: (i, 0)),
            pl.BlockSpec((1, d), lambda i: (0, 0)),
            pl.BlockSpec((N_EXPERTS, d), lambda i: (0, 0)),
        ],
        out_specs=[
            pl.BlockSpec((MOE_TB, N_EXPERTS), lambda i: (i, 0)),
            pl.BlockSpec((MOE_TB, N_EXPERTS), lambda i: (i, 0)),
            pl.BlockSpec((None, TOP_K, N_EXPERTS), lambda i: (i, 0, 0)),
        ],
        out_shape=[
            jax.ShapeDtypeStruct((t, N_EXPERTS), jnp.int32),
            jax.ShapeDtypeStruct((t, N_EXPERTS), F32),
            jax.ShapeDtypeStruct((N_TB, TOP_K, N_EXPERTS), jnp.int32),
        ],
        compiler_params=pltpu.CompilerParams(
            dimension_semantics=("arbitrary",), vmem_limit_bytes=V7X_VMEM_LIMIT),
        name="moe_router",
    )(x, g.reshape(1, d), rw.T)


def _pick(table, idx):
    lanes = jnp.arange(N_EXPERTS, dtype=jnp.int32)
    return jnp.sum(jnp.where(idx[..., None] == lanes, table, 0), axis=-1)


def _expert_layout(cnt):
    tot = jnp.sum(cnt, axis=(0, 1))
    start = jnp.cumsum(tot) - tot
    blk = jnp.sum(cnt, axis=1)
    blk_base = start[None, :] + jnp.cumsum(blk, axis=0) - blk
    base = jnp.stack([blk_base, blk_base + cnt[:, 0]], axis=1)
    n_sb = (tot + SB - 1) // SB
    sb_end = jnp.cumsum(n_sb)
    n_used = sb_end[-1]
    s_idx = jnp.arange(NS_MAX, dtype=jnp.int32)
    s_cl = jnp.minimum(s_idx, n_used - 1)
    sb_e = jnp.sum(s_cl[:, None] >= sb_end[None, :], axis=-1).astype(jnp.int32)
    local = s_cl - _pick(sb_end - n_sb, sb_e)
    sb_row0 = _pick(start, sb_e) + local * SB
    sb_rows = jnp.where(s_idx < n_used, jnp.clip(_pick(tot, sb_e) - local * SB, 0, SB), 0)
    i32 = lambda a: a.astype(jnp.int32)
    return i32(base), sb_e, i32(sb_row0), i32(sb_rows), i32(n_used).reshape(1)


def _dest_kernel(e_ref, b_ref, d_ref):
    eid = e_ref[...]
    lane = lax.broadcasted_iota(jnp.int32, eid.shape, 1)
    row = lax.broadcasted_iota(jnp.int32, (MOE_TB, MOE_TB), 0)
    col = lax.broadcasted_iota(jnp.int32, (MOE_TB, MOE_TB), 1)
    earlier = jnp.where(row > col, 1.0, 0.0).astype(BF16)
    base = b_ref[...].astype(F32)

    def dest_of(k):
        mine = lane == eid[:, k:k + 1]
        before = jnp.dot(earlier, jnp.where(mine, 1.0, 0.0).astype(BF16), preferred_element_type=F32)
        return jnp.sum(jnp.where(mine, before + base[k:k + 1], 0.0), axis=-1, keepdims=True)

    d_ref[...] = jnp.where(lane == 0, dest_of(0), jnp.where(lane == 1, dest_of(1), 0.0)).astype(jnp.int32)


def _dest_rows(eid, base):
    return pl.pallas_call(
        _dest_kernel,
        grid=(N_TB,),
        in_specs=[
            pl.BlockSpec((MOE_TB, N_EXPERTS), lambda i: (i, 0)),
            pl.BlockSpec((None, TOP_K, N_EXPERTS), lambda i: (i, 0, 0)),
        ],
        out_specs=pl.BlockSpec((MOE_TB, N_EXPERTS), lambda i: (i, 0)),
        out_shape=jax.ShapeDtypeStruct((T_ALL, N_EXPERTS), jnp.int32),
        compiler_params=pltpu.CompilerParams(
            dimension_semantics=("arbitrary",), vmem_limit_bytes=V7X_VMEM_LIMIT),
        name="moe_dest_rows",
    )(eid, base)


def _dispatch_kernel(d_ref, x_ref, g_ref, xg_hbm, stage, zeros, sem, zsem):
    i = pl.program_id(0)
    n = pl.num_programs(0)
    slot = i % 2
    blk_rows = MOE_TB * TOK_ROWS

    def wait_slot(sl):
        for _ in range(TOP_K):
            pltpu.make_async_copy(stage.at[sl, pl.ds(0, blk_rows)], xg_hbm.at[pl.ds(0, blk_rows)], sem.at[sl]).wait()

    @pl.when(i >= 2)
    def _():
        wait_slot(slot)

    _to_token_major(stage.at[slot], _rms_rows(x_ref[...], g_ref[...]), STAGE_PITCH)

    def body(r, carry):
        src = stage.at[slot, pl.ds(pl.multiple_of(r * STAGE_PITCH, 8), TOK_ROWS)]
        for k in range(TOP_K):
            row = d_ref[0, 0, TOP_K * r + k]
            dst = xg_hbm.at[pl.ds(pl.multiple_of(row * TOK_ROWS, TOK_ROWS), TOK_ROWS)]
            pltpu.make_async_copy(src, dst, sem.at[slot]).start()
        return carry

    lax.fori_loop(0, MOE_TB, body, 0, unroll=DMA_UNROLL)

    @pl.when(i == n - 1)
    def _():
        zeros[...] = jnp.zeros_like(zeros)
        tail = [pltpu.make_async_copy(zeros, xg_hbm.at[pl.ds((N_SLOTS + q * (SUB // TOK_ROWS)) * TOK_ROWS, SUB)], zsem)
                for q in range(TOK_ROWS)]
        for cp in tail:
            cp.start()
        for cp in tail:
            cp.wait()
        wait_slot(slot)
        wait_slot(1 - slot)


def _dispatch(x, g, dest_flat):
    t, d = x.shape
    return pl.pallas_call(
        _dispatch_kernel,
        grid=(N_TB,),
        in_specs=[
            pl.BlockSpec((1, 1, TOP_K * MOE_TB), lambda i: (i, 0, 0), memory_space=pltpu.SMEM),
            pl.BlockSpec((MOE_TB, d), lambda i: (i, 0)),
            pl.BlockSpec((1, d), lambda i: (0, 0)),
        ],
        out_specs=pl.BlockSpec(memory_space=pl.ANY),
        out_shape=jax.ShapeDtypeStruct(((N_SLOTS + SUB) * TOK_ROWS, LANES), F32),
        scratch_shapes=[
            pltpu.VMEM((2, MOE_TB * STAGE_PITCH, LANES), F32),
            pltpu.VMEM((SUB, LANES), F32),
            pltpu.SemaphoreType.DMA((2,)),
            pltpu.SemaphoreType.DMA(()),
        ],
        compiler_params=pltpu.CompilerParams(
            dimension_semantics=("arbitrary",), vmem_limit_bytes=V7X_VMEM_LIMIT),
        name="moe_dispatch",
    )(dest_flat, x, g.reshape(1, d))


def _moe_ffn_kernel(se_ref, r0_ref, nr_ref, nu_ref, xg_hbm, w1_ref, w3_ref, w2_ref, y_hbm,
                    x_scr, acc, stage, w1b, w3b, w2b, sem_in, sem_out):
    s = pl.program_id(0)
    j = pl.program_id(1)
    nj = pl.num_programs(1)
    rows = nr_ref[s]
    row0 = r0_ref[s]
    nsub = lax.shift_right_logical(rows + (SUB - 1), int(math.log2(SUB)))
    sub_rows = SUB * TOK_ROWS

    is_middle = (j > 0) & (j < nj - 1)

    def cast_weights():
        ws = (w1_ref[...].astype(BF16), w3_ref[...].astype(BF16), w2_ref[...].astype(BF16))
        w1b[...], w3b[...], w2b[...] = ws
        return ws

    @pl.when((nsub > 0) & jnp.logical_not(is_middle))
    def _():
        cast_weights()

    def sub(r):
        return pl.ds(pl.multiple_of(r * SUB, SUB), SUB)

    def swiglu(xs, ws=None):
        w1, w3, w2 = ws if ws is not None else (w1b[...], w3b[...], w2b[...])
        a = jnp.dot(xs, w1, preferred_element_type=F32)
        b = jnp.dot(xs, w3, preferred_element_type=F32)
        u = (_silu(a) * b).astype(BF16)
        return jnp.dot(u, w2, preferred_element_type=F32)

    def tile(r):
        return swiglu(x_scr[sub(r), :])

    def in_copy(r, slot):
        first = pl.multiple_of((row0 + r * SUB) * TOK_ROWS, TOK_ROWS)
        return pltpu.make_async_copy(xg_hbm.at[pl.ds(first, sub_rows)], stage.at[slot], sem_in.at[slot])

    @pl.when(j == 0)
    def _():
        @pl.when(nsub > 0)
        def _():
            in_copy(0, 0).start()

        def body(r, carry):
            slot = r % 2
            in_copy(r, slot).wait()

            @pl.when(r + 1 < nsub)
            def _():
                in_copy(r + 1, 1 - slot).start()

            for a in range(TOK_ROWS):
                x_scr[sub(r), pl.ds(a * LANES, LANES)] = stage[slot, pl.ds(a, SUB, stride=TOK_ROWS), :].astype(BF16)
            acc[sub(r), :] = tile(r)
            return carry

        lax.fori_loop(0, nsub, body, 0)

        def clear(r, carry):
            x_scr[sub(r), :] = jnp.zeros((SUB, D_MODEL), BF16)
            acc[sub(r), :] = jnp.zeros((SUB, D_MODEL), F32)
            return carry

        lax.fori_loop(nsub, SB // SUB, clear, 0)

    @pl.when(is_middle)
    def _():
        for c in range(SB // CHUNK_ROWS):
            @pl.when(rows > c * CHUNK_ROWS)
            def _():
                ws = cast_weights() if c == 0 else None
                rows_c = pl.ds(c * CHUNK_ROWS, CHUNK_ROWS)
                acc[rows_c, :] += swiglu(x_scr[rows_c, :], ws)

    @pl.when(j == nj - 1)
    def _():
        def out_copies(r, act):
            nv = jnp.clip(rows - r * SUB, 0, SUB)
            bit = SUB
            while bit >= 1:
                off = nv & (-2 * bit)
                src = stage.at[0, pl.ds(pl.multiple_of(off * TOK_ROWS, TOK_ROWS), bit * TOK_ROWS)]
                first = pl.multiple_of((row0 + r * SUB + off) * TOK_ROWS, TOK_ROWS)
                cp = pltpu.make_async_copy(src, y_hbm.at[pl.ds(first, bit * TOK_ROWS)], sem_out)
                pl.when((nv & bit) != 0)(functools.partial(act, cp))
                bit //= 2

        def body(r, carry):
            y = acc[sub(r), :] + tile(r)

            @pl.when(r > 0)
            def _():
                out_copies(r - 1, lambda cp: cp.wait())

            _to_token_major(stage.at[0], y)
            out_copies(r, lambda cp: cp.start())
            return carry

        lax.fori_loop(0, nsub, body, 0)

        @pl.when(nsub > 0)
        def _():
            out_copies(nsub - 1, lambda cp: cp.wait())


def _moe_ffn(xg, w1, w3, w2, sb_e, sb_row0, sb_rows, n_used):
    d = w1.shape[1]
    f = w1.shape[2]
    nj = f // TF

    def jmap(s, j, nu):
        return jnp.where(s < nu[0], j, nj - 1)

    return pl.pallas_call(
        _moe_ffn_kernel,
        grid_spec=pltpu.PrefetchScalarGridSpec(
            num_scalar_prefetch=4,
            grid=(NS_MAX, nj),
            in_specs=[
                pl.BlockSpec(memory_space=pl.ANY),
                pl.BlockSpec((None, d, TF), lambda s, j, se, r0, nr, nu: (se[s], 0, jmap(s, j, nu))),
                pl.BlockSpec((None, d, TF), lambda s, j, se, r0, nr, nu: (se[s], 0, jmap(s, j, nu))),
                pl.BlockSpec((None, TF, d), lambda s, j, se, r0, nr, nu: (se[s], jmap(s, j, nu), 0)),
            ],
            out_specs=pl.BlockSpec(memory_space=pl.ANY),
            scratch_shapes=[
                pltpu.VMEM((SB, d), BF16),
                pltpu.VMEM((SB, d), F32),
                pltpu.VMEM((2, SUB * TOK_ROWS, LANES), F32),
                pltpu.VMEM((d, TF), BF16),
                pltpu.VMEM((d, TF), BF16),
                pltpu.VMEM((TF, d), BF16),
                pltpu.SemaphoreType.DMA((2,)),
                pltpu.SemaphoreType.DMA(()),
            ],
        ),
        out_shape=jax.ShapeDtypeStruct((N_SLOTS * TOK_ROWS, LANES), F32),
        compiler_params=pltpu.CompilerParams(
            dimension_semantics=("arbitrary", "arbitrary"), vmem_limit_bytes=V7X_VMEM_LIMIT),
        name="moe_ffn",
    )(sb_e, sb_row0, sb_rows, n_used, xg, w1, w3, w2)


N_TB_PROMPT = T_PROMPT // MOE_TB


def _combine_kernel(d_ref, dn_ref, x_ref, p_ref, g_ref, y_hbm, op_ref, os_ref, stage, sem, *, final_norm):
    i = pl.program_id(0)
    n = pl.num_programs(0)
    slot = i % 2
    blk_rows = MOE_TB * TOK_ROWS

    def start_block(dref, sl):
        def body(r, carry):
            for k in range(TOP_K):
                row = dref[0, 0, TOP_K * r + k]
                src = y_hbm.at[pl.ds(pl.multiple_of(row * TOK_ROWS, TOK_ROWS), TOK_ROWS)]
                dst = stage.at[sl, k, pl.ds(pl.multiple_of(r * STAGE_PITCH, 8), TOK_ROWS)]
                pltpu.make_async_copy(src, dst, sem.at[sl]).start()
            return carry

        lax.fori_loop(0, MOE_TB, body, 0, unroll=DMA_UNROLL)

    @pl.when(i == 0)
    def _():
        start_block(d_ref, 0)

    @pl.when(i + 1 < n)
    def _():
        start_block(dn_ref, 1 - slot)

    for k in range(TOP_K):
        pltpu.make_async_copy(y_hbm.at[pl.ds(0, blk_rows)], stage.at[slot, k, pl.ds(0, blk_rows)], sem.at[slot]).wait()
    g0 = p_ref[:, 0:1]
    g1 = p_ref[:, 1:2]
    pieces = []
    for a in range(TOK_ROWS):
        y0 = stage[slot, 0, pl.ds(a, MOE_TB, stride=STAGE_PITCH), :]
        y1 = stage[slot, 1, pl.ds(a, MOE_TB, stride=STAGE_PITCH), :]
        pieces.append(x_ref[:, pl.ds(a * LANES, LANES)] + (y0 * g0 + y1 * g1))
    y = jnp.concatenate(pieces, axis=1)
    if final_norm:
        y = _rms_rows(y, g_ref[...])

    @pl.when(i < N_TB_PROMPT)
    def _():
        op_ref[...] = y

    @pl.when(i >= N_TB_PROMPT)
    def _():
        os_ref[...] = y


def _moe_combine(x, y, dest_flat, gate, g, final_norm):
    t, d = x.shape
    return pl.pallas_call(
        functools.partial(_combine_kernel, final_norm=final_norm),
        grid=(N_TB,),
        in_specs=[
            pl.BlockSpec((1, 1, TOP_K * MOE_TB), lambda i: (i, 0, 0), memory_space=pltpu.SMEM),
            pl.BlockSpec((1, 1, TOP_K * MOE_TB), lambda i: (jnp.minimum(i + 1, N_TB - 1), 0, 0),
                         memory_space=pltpu.SMEM),
            pl.BlockSpec((MOE_TB, d), lambda i: (i, 0)),
            pl.BlockSpec((MOE_TB, N_EXPERTS), lambda i: (i, 0)),
            pl.BlockSpec((1, d), lambda i: (0, 0)),
            pl.BlockSpec(memory_space=pl.ANY),
        ],
        out_specs=[
            pl.BlockSpec((MOE_TB, d), lambda i: (jnp.minimum(i, N_TB_PROMPT - 1), 0)),
            pl.BlockSpec((MOE_TB, d), lambda i: (jnp.maximum(i - N_TB_PROMPT, 0), 0)),
        ],
        out_shape=[
            jax.ShapeDtypeStruct((T_PROMPT, d), F32),
            jax.ShapeDtypeStruct((T_SAMPLE, d), F32),
        ],
        scratch_shapes=[
            pltpu.VMEM((2, TOP_K, MOE_TB * STAGE_PITCH, LANES), F32),
            pltpu.SemaphoreType.DMA((2,)),
        ],
        compiler_params=pltpu.CompilerParams(
            dimension_semantics=("arbitrary",), vmem_limit_bytes=V7X_VMEM_LIMIT),
        name="moe_combine",
    )(dest_flat, dest_flat, x, gate, g.reshape(1, d), y)


def _moe(x, norm_g, rw, w1, w3, w2, out_g, final_norm):
    eid, gate, cnt = _router(x, norm_g, rw)
    base, sb_e, sb_row0, sb_rows, n_used = _expert_layout(cnt)
    dest = _dest_rows(eid, base)
    dest_flat = dest[:, :TOP_K].reshape(N_TB, 1, TOP_K * MOE_TB)
    xg = _dispatch(x, norm_g, dest_flat)
    y = _moe_ffn(xg, w1, w3, w2, sb_e, sb_row0, sb_rows, n_used)
    return _moe_combine(x, y, dest_flat, gate, out_g, final_norm)


def kernel(x_prompt, x_sample, state_ret, state_conv31, state_conv3, mix_norm_g, w_in, ret_norm_g, conv31_w, conv31_b, conv_ln_g, conv_ln_b, sconv_w, sgu_ln_g, sgu_ln_b, sgu_w, sgu_b, w_out, ffn_norm_g, dense_w1, dense_w3, dense_w2, router_w, moe_w1, moe_w3, moe_w2, final_norm_g):
    assert DEPTH == 2, "layer 0 has the dense channel mixer; layer 1's expert mixer ends with the final norm"
    x, hn = _prep(x_prompt.reshape(T_PROMPT, D_MODEL), x_sample.reshape(T_SAMPLE, D_MODEL), mix_norm_g[0])
    rets_p, c31_p, c3_p, c3_s, vns = [], [], [], [], []
    ret_s = b31_s = None
    for l in range(DEPTH):
        p = dict(ret_norm_g=ret_norm_g[l], conv31_w=conv31_w[l], conv31_b=conv31_b[l],
                 conv_ln_g=conv_ln_g[l], conv_ln_b=conv_ln_b[l], sconv_w=sconv_w[l],
                 sgu_ln_g=sgu_ln_g[l], sgu_ln_b=sgu_ln_b[l], sgu_w=sgu_w[l], sgu_b=sgu_b[l])
        z = _in_proj(hn, w_in, l)
        mix_s, ret_s, b31_s, b3_s, vn_s = _mixer_sample(
            z, p, l, state_ret, state_conv31, state_conv3, prev=(ret_s, b31_s) if l == DEPTH - 1 else None)
        mix_p, ret_p, b31_p, b3_p = _mixer_prompt(z, p)
        x = _out_proj_residual(mix_p, mix_s, w_out, l, x)
        j = l // 2
        if l % 2 == 0:
            x, hn = _ffn_dense(x, ffn_norm_g[l], dense_w1[j], dense_w3[j], dense_w2[j], mix_norm_g[l + 1])
        else:
            y_p, y_s = _moe(x, ffn_norm_g[l], router_w[j], moe_w1[j], moe_w3[j], moe_w2[j],
                            final_norm_g, final_norm=True)
        rets_p.append(ret_p)
        c31_p.append(b31_p)
        c3_p.append(b3_p); c3_s.append(b3_s)
        vns.append(vn_s)
    y_prompt = y_p.reshape(BATCH, SEQ, D_MODEL)
    y_sample = y_s.reshape(DEC_BATCH, DEC_SEQ, D_MODEL)
    return (y_prompt, y_sample, jnp.stack(rets_p), ret_s, jnp.stack(c31_p), b31_s,
            jnp.stack(c3_p), jnp.stack(c3_s), jnp.stack(vns))
```

```python
import functools
import math

import jax
import jax.numpy as jnp
from jax import lax
from jax.experimental import pallas as pl
from jax.experimental.pallas import tpu as pltpu

F32 = jnp.float32
BF16 = jnp.bfloat16

D_MODEL = 2048
BATCH = 4
SEQ = 2048
DEPTH = 2
DEC_BATCH = 128
DEC_SEQ = 4
PAST_LEN = 16384
GROUP_W = 512
D_IN = 11 * GROUP_W
RET_HEADS = 4
RET_D = 128
CHUNK = 128
ROPE_THETA = 10000.0
CONV_W = 31
SC_W = 3
SGU_GROUPS = 4
D_FF = 5632
N_EXPERTS = 8
TOP_K = 2
EPS = 1e-6

T_PROMPT = BATCH * SEQ
T_SAMPLE = DEC_BATCH * DEC_SEQ
T_ALL = T_PROMPT + T_SAMPLE
N_CHUNKS = SEQ // CHUNK

C_Q, C_K, C_V, C_G, C_BL, C_BG, C_CB, C_CC, C_CH, C_DU, C_DV = (i * GROUP_W for i in range(11))

V7X_VMEM_LIMIT = 56 * 1024 * 1024

TM = 1088
TM_IN = 2176
TM_OUT = 512
N_OUT_PROMPT = T_PROMPT // TM_OUT
NORM_ROWS = 272
TN_IN = 512
TF = 256
SAMPLE_BT = 16

LANES = 128
SUBLANES = 8
TOK_ROWS = D_MODEL // LANES
MOE_TB = 256
N_TB = T_ALL // MOE_TB
N_SLOTS = T_ALL * TOP_K
SUB = 256
SB = 9 * SUB
CHUNK_ROWS = SB // 3
NS_MAX = N_SLOTS // SB + N_EXPERTS


def _rms_rows(x, g):
    ms = jnp.mean(x * x, axis=-1, keepdims=True)
    return (x * lax.rsqrt(ms + EPS)) * g


def _layernorm_rows(x, g, b):
    mu = jnp.mean(x, axis=-1, keepdims=True)
    xc = x - mu
    return xc * lax.rsqrt(jnp.mean(xc * xc, axis=-1, keepdims=True) + EPS) * g + b


def _silu(x):
    return x * jax.nn.sigmoid(x)


def _norm_block_to(x_ref, g_ref, dst_ref):
    rows = x_ref.shape[0]
    g = g_ref[...]

    def body(c, carry):
        r0 = pl.multiple_of(c * NORM_ROWS, NORM_ROWS)
        x = x_ref[pl.ds(r0, NORM_ROWS), :]
        dst_ref[pl.ds(r0, NORM_ROWS), :] = _rms_rows(x, g).astype(dst_ref.dtype)
        return carry

    lax.fori_loop(0, rows // NORM_ROWS, body, 0)


PREP_ROWS = 512
N_PREP_PROMPT = T_PROMPT // PREP_ROWS


def _prep_kernel(xp_ref, xs_ref, g_ref, hn_ref):
    def emit(src_ref):
        hn_ref[...] = _rms_rows(src_ref[...], g_ref[...]).astype(hn_ref.dtype)

    is_prompt = pl.program_id(0) < N_PREP_PROMPT
    pl.when(is_prompt)(functools.partial(emit, xp_ref))
    pl.when(jnp.logical_not(is_prompt))(functools.partial(emit, xs_ref))


def _prep(x_prompt, x_sample, g):
    d = x_prompt.shape[1]
    blk = lambda f: pl.BlockSpec((PREP_ROWS, d), f)
    return pl.pallas_call(
        _prep_kernel,
        grid=(T_ALL // PREP_ROWS,),
        in_specs=[
            blk(lambda i: (jnp.minimum(i, N_PREP_PROMPT - 1), 0)),
            blk(lambda i: (jnp.maximum(i - N_PREP_PROMPT, 0), 0)),
            pl.BlockSpec((1, d), lambda i: (0, 0)),
        ],
        out_specs=blk(lambda i: (i, 0)),
        out_shape=jax.ShapeDtypeStruct((T_ALL, d), BF16),
        compiler_params=pltpu.CompilerParams(
            dimension_semantics=("arbitrary",), vmem_limit_bytes=V7X_VMEM_LIMIT),
        name="prep_rows",
    )(x_prompt, x_sample, g.reshape(1, d))


def _in_proj_kernel(h_ref, w_ref, o_ref):
    o_ref[...] = jnp.dot(h_ref[...], w_ref[...].astype(BF16), preferred_element_type=F32)


def _in_proj(hn, w_all, layer):
    t, d = hn.shape
    n = w_all.shape[2]
    return pl.pallas_call(
        _in_proj_kernel,
        grid=(t // TM_IN, n // TN_IN),
        in_specs=[
            pl.BlockSpec((TM_IN, d), lambda i, j: (i, 0)),
            pl.BlockSpec((None, d, TN_IN), lambda i, j: (layer, 0, j)),
        ],
        out_specs=pl.BlockSpec((TM_IN, TN_IN), lambda i, j: (i, j)),
        out_shape=jax.ShapeDtypeStruct((t, n), F32),
        compiler_params=pltpu.CompilerParams(
            dimension_semantics=("arbitrary", "arbitrary"), vmem_limit_bytes=V7X_VMEM_LIMIT),
        name="in_proj",
    )(hn, w_all)


W_LOAD_ROWS = 512


def _out_proj_kernel(ap_ref, as_ref, w_hbm, rp_ref, rs_ref, o_ref, wb, stage, sem, *, layer):
    i = pl.program_id(0)

    @pl.when(i == 0)
    def _():
        for c in range(wb.shape[0] // W_LOAD_ROWS):
            rows = pl.ds(c * W_LOAD_ROWS, W_LOAD_ROWS)
            cp = pltpu.make_async_copy(w_hbm.at[layer, rows], stage, sem)
            cp.start()
            cp.wait()
            wb[rows, :] = stage[...].astype(BF16)

    def emit(a, r_ref):
        o_ref[...] = r_ref[...] + jnp.dot(a, wb[...], preferred_element_type=F32)

    pl.when(i < N_OUT_PROMPT)(lambda: emit(ap_ref[...], rp_ref))
    pl.when(i >= N_OUT_PROMPT)(lambda: emit(as_ref[...].astype(BF16), rs_ref))


def _out_proj_residual(a_prompt, a_sample, w_all, layer, r_prompt, r_sample, r_sample_block):
    k = a_prompt.shape[1]
    t = T_ALL
    n = w_all.shape[2]
    assert T_SAMPLE == TM_OUT and T_PROMPT % TM_OUT == 0
    return pl.pallas_call(
        functools.partial(_out_proj_kernel, layer=layer),
        grid=(t // TM_OUT,),
        in_specs=[
            pl.BlockSpec((TM_OUT, k), lambda i: (jnp.minimum(i, N_OUT_PROMPT - 1), 0)),
            pl.BlockSpec((TM_OUT, k), lambda i: (0, 0)),
            pl.BlockSpec(memory_space=pl.ANY),
            pl.BlockSpec((TM_OUT, n), lambda i: (jnp.minimum(i, N_OUT_PROMPT - 1), 0)),
            pl.BlockSpec((TM_OUT, n), lambda i: (r_sample_block, 0)),
        ],
        out_specs=pl.BlockSpec((TM_OUT, n), lambda i: (i, 0)),
        out_shape=jax.ShapeDtypeStruct((t, n), F32),
        scratch_shapes=[
            pltpu.VMEM((k, n), BF16),
            pltpu.VMEM((W_LOAD_ROWS, n), F32),
            pltpu.SemaphoreType.DMA(()),
        ],
        compiler_params=pltpu.CompilerParams(
            dimension_semantics=("arbitrary",), vmem_limit_bytes=V7X_VMEM_LIMIT),
        name="out_proj_residual",
    )(a_prompt, a_sample, w_all, r_prompt, r_sample)


def _swiglu_tile(h, w1_ref, w3_ref, w2_ref):
    a = jnp.dot(h, w1_ref[...].astype(BF16), preferred_element_type=F32)
    b = jnp.dot(h, w3_ref[...].astype(BF16), preferred_element_type=F32)
    u = (_silu(a) * b).astype(BF16)
    return jnp.dot(u, w2_ref[...].astype(BF16), preferred_element_type=F32)


def _ffn_kernel(x_hbm, g_ref, w1_ref, w3_ref, w2_ref, gn_ref, o_ref, hnext_ref, hn_ref, sem):
    @pl.when(pl.program_id(1) == 0)
    def _():
        r0 = pl.multiple_of(pl.program_id(0) * TM, TM)
        cp = pltpu.make_async_copy(x_hbm.at[pl.ds(r0, TM)], o_ref, sem)
        cp.start()
        cp.wait()
        _norm_block_to(o_ref, g_ref, hn_ref)

    o_ref[...] += _swiglu_tile(hn_ref[...], w1_ref, w3_ref, w2_ref)

    @pl.when(pl.program_id(1) == pl.num_programs(1) - 1)
    def _():
        _norm_block_to(o_ref, gn_ref, hnext_ref)


def _ffn_dense(x, g, w1, w3, w2, g_next):
    t, d = x.shape
    f = w1.shape[1]
    return pl.pallas_call(
        _ffn_kernel,
        grid=(t // TM, f // TF),
        in_specs=[
            pl.BlockSpec(memory_space=pl.ANY),
            pl.BlockSpec((1, d), lambda i, j: (0, 0)),
            pl.BlockSpec((d, TF), lambda i, j: (0, j)),
            pl.BlockSpec((d, TF), lambda i, j: (0, j)),
            pl.BlockSpec((TF, d), lambda i, j: (j, 0)),
            pl.BlockSpec((1, d), lambda i, j: (0, 0)),
        ],
        out_specs=[
            pl.BlockSpec((TM, d), lambda i, j: (i, 0)),
            pl.BlockSpec((TM, d), lambda i, j: (i, 0)),
        ],
        out_shape=[jax.ShapeDtypeStruct((t, d), F32), jax.ShapeDtypeStruct((t, d), BF16)],
        scratch_shapes=[pltpu.VMEM((TM, d), BF16), pltpu.SemaphoreType.DMA(())],
        compiler_params=pltpu.CompilerParams(
            dimension_semantics=("arbitrary", "arbitrary"), vmem_limit_bytes=V7X_VMEM_LIMIT),
        name="ffn_dense",
    )(x, g.reshape(1, d), w1, w3, w2, g_next.reshape(1, d))


def _rotary_tables(length, pos0):
    half = RET_D // 2
    inv = ROPE_THETA ** (-jnp.arange(half, dtype=F32) / half)
    pos = jnp.arange(length, dtype=F32) + pos0
    ang = pos[:, None] * inv[None, :]
    cos, sin = jnp.cos(ang), jnp.sin(ang)
    return jnp.concatenate([cos, cos], axis=-1), jnp.concatenate([-sin, sin], axis=-1)


def _decay_tables(c):
    log_g = jnp.log1p(-jnp.exp2(-5.0 - jnp.arange(RET_HEADS, dtype=F32)))
    idx = jnp.arange(c, dtype=F32)
    rel = idx[:, None] - idx[None, :]
    d_in = jnp.where(rel >= 0, jnp.exp(log_g[:, None, None] * jnp.maximum(rel, 0.0)), 0.0)
    d_q = jnp.exp(log_g[:, None] * (idx + 1.0)[None, :])
    d_k = jnp.exp(log_g[:, None] * (c - 1.0 - idx)[None, :])
    d_c = jnp.exp(log_g * c)
    return d_in, d_q, d_k, d_c


HIST31 = 32
HIST3 = 8


def _mixer_prompt_kernel(z_ref, cos_ref, sin_ref, din_ref, dq_ref, dk_ref, dc_ref, retg_ref,
                         c31w_ref, c31b_ref, clng_ref, clnb_ref, scw_ref, slng_ref, slnb_ref,
                         sguw_ref, sgub_ref,
                         mix_ref, ret_ref, b31_ref, b3_ref,
                         s_scr, ext31_scr, ext3_scr, ysh_scr):
    c = pl.program_id(1)

    @pl.when(c == 0)
    def _():
        s_scr[...] = jnp.zeros_like(s_scr)
        ext31_scr[pl.ds(0, HIST31), :] = jnp.zeros((HIST31, GROUP_W), F32)
        ext3_scr[pl.ds(0, HIST3), :] = jnp.zeros((HIST3, GROUP_W), F32)

    cos = cos_ref[...]
    sin = sin_ref[...]

    for h in range(RET_HEADS):
        lo = h * RET_D
        q = z_ref[:, pl.ds(C_Q + lo, RET_D)]
        k = z_ref[:, pl.ds(C_K + lo, RET_D)]
        v = z_ref[:, pl.ds(C_V + lo, RET_D)]
        g = z_ref[:, pl.ds(C_G + lo, RET_D)]
        q = q * cos + pltpu.roll(q, RET_D // 2, axis=1) * sin
        k = (k * cos + pltpu.roll(k, RET_D // 2, axis=1) * sin) * (RET_D ** -0.5)
        qb = q.astype(BF16)
        vb = v.astype(BF16)
        scores = lax.dot_general(qb, k.astype(BF16), (((1,), (1,)), ((), ())),
                                 preferred_element_type=F32) * din_ref[h]
        inner = jnp.dot(scores.astype(BF16), vb, preferred_element_type=F32)
        s_old = s_scr[h]
        cross = jnp.dot(qb, s_old.astype(BF16), preferred_element_type=F32) * dq_ref[h]
        kd = (k * dk_ref[h]).astype(BF16)
        s_scr[h] = s_old * dc_ref[h] + lax.dot_general(kd, vb, (((0,), (0,)), ((), ())),
                                                       preferred_element_type=F32)
        r = inner + cross
        r = r * lax.rsqrt(jnp.mean(r * r, axis=-1, keepdims=True) + EPS)
        r = r * retg_ref[:, pl.ds(lo, RET_D)]
        mix_ref[:, pl.ds(lo, RET_D)] = (_silu(g) * r).astype(mix_ref.dtype)

    glu = z_ref[:, pl.ds(C_BL, GROUP_W)] * jax.nn.sigmoid(z_ref[:, pl.ds(C_BG, GROUP_W)])
    ext31_scr[pl.ds(HIST31, CHUNK), :] = glu
    s0 = HIST31 - (CONV_W - 1)
    conv = None
    for i in range(SUBLANES):
        n_rows = CHUNK if i == 0 else CHUNK + SUBLANES
        y = None
        for m in range(HIST31 // SUBLANES + 1):
            j = SUBLANES * m + i - s0
            if 0 <= j < CONV_W:
                term = ext31_scr[pl.ds(SUBLANES * m, n_rows), :] * c31w_ref[pl.ds(j, 1), :]
                y = term if y is None else y + term
        if i == 0:
            conv = y
        else:
            ysh_scr[i - 1] = y
            conv = conv + ysh_scr[i - 1, pl.ds(i, CHUNK), :]
    conv = conv + c31b_ref[...]
    out_b = _silu(_layernorm_rows(conv, clng_ref[...], clnb_ref[...]))
    mix_ref[:, pl.ds(GROUP_W, GROUP_W)] = out_b.astype(mix_ref.dtype)
    ext31_scr[pl.ds(0, HIST31), :] = ext31_scr[pl.ds(CHUNK, HIST31), :]

    gated = z_ref[:, pl.ds(C_CC, GROUP_W)] * z_ref[:, pl.ds(C_CH, GROUP_W)]
    ext3_scr[pl.ds(HIST3, CHUNK), :] = gated
    conv3 = ext3_scr[pl.ds(HIST3 - (SC_W - 1), CHUNK), :] * scw_ref[pl.ds(0, 1), :]
    for j in range(1, SC_W):
        conv3 = conv3 + ext3_scr[pl.ds(HIST3 - (SC_W - 1) + j, CHUNK), :] * scw_ref[pl.ds(j, 1), :]
    mix_ref[:, pl.ds(2 * GROUP_W, GROUP_W)] = (z_ref[:, pl.ds(C_CB, GROUP_W)] * conv3).astype(mix_ref.dtype)
    ext3_scr[pl.ds(0, HIST3), :] = ext3_scr[pl.ds(CHUNK, HIST3), :]

    vn = _layernorm_rows(z_ref[:, pl.ds(C_DV, GROUP_W)], slng_ref[...], slnb_ref[...])
    row = lax.broadcasted_iota(jnp.int32, (CHUNK, CHUNK), 0)
    col = lax.broadcasted_iota(jnp.int32, (CHUNK, CHUNK), 1)
    ch = GROUP_W // SGU_GROUPS
    for gi in range(SGU_GROUPS):
        wm = jnp.where(row >= col, sguw_ref[gi], 0.0).astype(BF16)
        mixed = jnp.dot(wm, vn[:, gi * ch:(gi + 1) * ch].astype(BF16), preferred_element_type=F32)
        mixed = mixed + sgub_ref[:, pl.ds(gi, 1)]
        du = z_ref[:, pl.ds(C_DU + gi * ch, ch)]
        mix_ref[:, pl.ds(3 * GROUP_W + gi * ch, ch)] = (du * mixed).astype(mix_ref.dtype)

    @pl.when(c == pl.num_programs(1) - 1)
    def _():
        ret_ref[...] = s_scr[...]
        b31_ref[...] = ext31_scr[pl.ds(HIST31 - (CONV_W - 1), CONV_W - 1), :]
        b3_ref[...] = ext3_scr[pl.ds(HIST3 - (SC_W - 1), SC_W - 1), :]


def _mixer_prompt(z, p):
    cos, sin = _rotary_tables(SEQ, 0)
    d_in, d_q, d_k, d_c = _decay_tables(CHUNK)
    dq_t = jnp.broadcast_to(d_q[:, :, None], (RET_HEADS, CHUNK, RET_D))
    dk_t = jnp.broadcast_to(d_k[:, :, None], (RET_HEADS, CHUNK, RET_D))
    dc_t = jnp.broadcast_to(d_c[:, None, None], (RET_HEADS, RET_D, RET_D))

    def full(shape):
        return pl.BlockSpec(shape, lambda b, c: (0,) * len(shape))

    row = lambda a: a.reshape(1, -1)
    return pl.pallas_call(
        _mixer_prompt_kernel,
        grid=(BATCH, N_CHUNKS),
        in_specs=[
            pl.BlockSpec((CHUNK, D_IN), lambda b, c: (b * N_CHUNKS + c, 0)),
            pl.BlockSpec((CHUNK, RET_D), lambda b, c: (c, 0)),
            pl.BlockSpec((CHUNK, RET_D), lambda b, c: (c, 0)),
            full((RET_HEADS, CHUNK, CHUNK)),
            full((RET_HEADS, CHUNK, RET_D)),
            full((RET_HEADS, CHUNK, RET_D)),
            full((RET_HEADS, RET_D, RET_D)),
            full((1, GROUP_W)),
            full((CONV_W, GROUP_W)),
            full((1, GROUP_W)),
            full((1, GROUP_W)),
            full((1, GROUP_W)),
            full((SC_W, GROUP_W)),
            full((1, GROUP_W)),
            full((1, GROUP_W)),
            full((SGU_GROUPS, CHUNK, CHUNK)),
            full((CHUNK, SGU_GROUPS)),
        ],
        out_specs=[
            pl.BlockSpec((CHUNK, D_MODEL), lambda b, c: (b * N_CHUNKS + c, 0)),
            pl.BlockSpec((None, RET_HEADS, RET_D, RET_D), lambda b, c: (b, 0, 0, 0)),
            pl.BlockSpec((None, CONV_W - 1, GROUP_W), lambda b, c: (b, 0, 0)),
            pl.BlockSpec((None, SC_W - 1, GROUP_W), lambda b, c: (b, 0, 0)),
        ],
        out_shape=[
            jax.ShapeDtypeStruct((T_PROMPT, D_MODEL), BF16),
            jax.ShapeDtypeStruct((BATCH, RET_HEADS, RET_D, RET_D), F32),
            jax.ShapeDtypeStruct((BATCH, CONV_W - 1, GROUP_W), F32),
            jax.ShapeDtypeStruct((BATCH, SC_W - 1, GROUP_W), F32),
        ],
        scratch_shapes=[
            pltpu.VMEM((RET_HEADS, RET_D, RET_D), F32),
            pltpu.VMEM((HIST31 + CHUNK, GROUP_W), F32),
            pltpu.VMEM((HIST3 + CHUNK, GROUP_W), F32),
            pltpu.VMEM((SUBLANES - 1, CHUNK + SUBLANES, GROUP_W), F32),
        ],
        compiler_params=pltpu.CompilerParams(
            dimension_semantics=("arbitrary", "arbitrary"), vmem_limit_bytes=V7X_VMEM_LIMIT),
        name="mixer_prompt",
    )(z, cos, sin, d_in, dq_t, dk_t, dc_t, row(p["ret_norm_g"]), p["conv31_w"], row(p["conv31_b"]),
      row(p["conv_ln_g"]), row(p["conv_ln_b"]), p["sconv_w"], row(p["sgu_ln_g"]), row(p["sgu_ln_b"]),
      p["sgu_w"], p["sgu_b"].T)


EXT31_S = 40
EXT3_S = 8


N_SAMPLE_IN = 20


def _mixer_sample_stacked_kernel(*refs):
    ins = refs[:N_SAMPLE_IN]
    prev_ret, prev_b31 = refs[N_SAMPLE_IN:N_SAMPLE_IN + 2]
    mix_ref, ret_hbm, b31_hbm, b3_ref, vn_ref = refs[N_SAMPLE_IN + 2:N_SAMPLE_IN + 7]
    ext31_scr, ext3_scr, ret_scr, b31_scr, wr_sem, cp_sem = refs[N_SAMPLE_IN + 7:]
    i = pl.program_id(0)
    n = pl.num_programs(0)
    slot = i % 2
    bt = ret_scr.shape[1]
    blk = pl.ds(pl.multiple_of(i * bt, bt), bt)

    def writes(sl):
        return (pltpu.make_async_copy(ret_scr.at[sl], ret_hbm.at[1, blk], wr_sem.at[sl, 0]),
                pltpu.make_async_copy(b31_scr.at[sl], b31_hbm.at[1, blk], wr_sem.at[sl, 1]))

    copies = (pltpu.make_async_copy(prev_ret, ret_hbm.at[0, blk], cp_sem.at[0]),
              pltpu.make_async_copy(prev_b31, b31_hbm.at[0, blk], cp_sem.at[1]))

    @pl.when(i >= 2)
    def _():
        for w in writes(slot):
            w.wait()

    for c in copies:
        c.start()
    _mixer_sample_kernel(*ins, mix_ref, ret_scr.at[slot], b31_scr.at[slot], b3_ref, vn_ref, ext31_scr, ext3_scr)
    ws = writes(slot)
    for w in ws:
        w.start()
    for c in copies:
        c.wait()

    @pl.when(i == n - 1)
    def _():
        for w in ws:
            w.wait()
        for w in writes(1 - slot):
            w.wait()


def _mixer_sample_kernel(z_ref, cos_ref, sin_ref, din_ref, dq_ref, dk_ref, dc_ref, retg_ref,
                         c31w_ref, c31b_ref, clng_ref, clnb_ref, scw_ref, slng_ref, slnb_ref,
                         sguw_ref, sgub_ref, sret_ref, s31_ref, s3_ref,
                         mix_ref, ret_ref, b31_ref, b3_ref, vn_ref,
                         ext31_scr, ext3_scr):
    cos = cos_ref[...]
    sin = sin_ref[...]
    n_hist31 = CONV_W - 1
    n_hist3 = SC_W - 1
    bt = sret_ref.shape[0]

    def zs(col, width):
        return jnp.stack([z_ref[pl.ds(DEC_SEQ * b, DEC_SEQ), pl.ds(col, width)] for b in range(bt)])

    def zrow(s, col, width):
        return jnp.stack([z_ref[pl.ds(DEC_SEQ * b + s, 1), pl.ds(col, width)] for b in range(bt)])

    def put(col, val):
        for b in range(bt):
            mix_ref[pl.ds(DEC_SEQ * b, DEC_SEQ), pl.ds(col, val.shape[-1])] = val[b]

    for h in range(RET_HEADS):
        lo = h * RET_D
        q = zs(C_Q + lo, RET_D)
        k = zs(C_K + lo, RET_D)
        g = zs(C_G + lo, RET_D)
        q = q * cos + pltpu.roll(q, RET_D // 2, axis=2) * sin
        k = (k * cos + pltpu.roll(k, RET_D // 2, axis=2) * sin) * (RET_D ** -0.5)
        s_old = sret_ref[:, h]
        r = lax.dot_general(q, s_old, (((2,), (1,)), ((0,), (0,))),
                            preferred_element_type=F32) * dq_ref[h]
        for s in range(DEC_SEQ):
            k_s = zrow(s, C_K + lo, RET_D)
            k_s = (k_s * cos[s:s + 1] + pltpu.roll(k_s, RET_D // 2, axis=2) * sin[s:s + 1]) * (RET_D ** -0.5)
            v_s = zrow(s, C_V + lo, RET_D)
            score = jnp.sum(q * k_s, axis=-1, keepdims=True) * din_ref[h, s]
            r = r + score * v_s
        kd = k * dk_ref[h]
        v = zs(C_V + lo, RET_D)
        ret_ref[:, h] = s_old * dc_ref[h] + lax.dot_general(
            kd, v, (((1,), (1,)), ((0,), (0,))), preferred_element_type=F32)
        r = r * lax.rsqrt(jnp.mean(r * r, axis=-1, keepdims=True) + EPS)
        r = r * retg_ref[:, pl.ds(lo, RET_D)]
        put(lo, _silu(g) * r)

    glu = zs(C_BL, GROUP_W) * jax.nn.sigmoid(zs(C_BG, GROUP_W))
    ext31_scr[:, pl.ds(0, n_hist31), :] = s31_ref[...]
    ext31_scr[:, pl.ds(n_hist31, DEC_SEQ), :] = glu
    conv = ext31_scr[:, pl.ds(0, DEC_SEQ), :] * c31w_ref[pl.ds(0, 1), :]
    for j in range(1, CONV_W):
        conv = conv + ext31_scr[:, pl.ds(j, DEC_SEQ), :] * c31w_ref[pl.ds(j, 1), :]
    conv = conv + c31b_ref[...]
    put(GROUP_W, _silu(_layernorm_rows(conv, clng_ref[...], clnb_ref[...])))
    b31_ref[...] = ext31_scr[:, pl.ds(DEC_SEQ, n_hist31), :]

    gated = zs(C_CC, GROUP_W) * zs(C_CH, GROUP_W)
    ext3_scr[:, pl.ds(0, n_hist3), :] = s3_ref[...]
    ext3_scr[:, pl.ds(n_hist3, DEC_SEQ), :] = gated
    conv3 = ext3_scr[:, pl.ds(0, DEC_SEQ), :] * scw_ref[pl.ds(0, 1), :]
    for j in range(1, SC_W):
        conv3 = conv3 + ext3_scr[:, pl.ds(j, DEC_SEQ), :] * scw_ref[pl.ds(j, 1), :]
    put(2 * GROUP_W, zs(C_CB, GROUP_W) * conv3)
    b3_ref[...] = ext3_scr[:, pl.ds(DEC_SEQ, n_hist3), :]

    vn = _layernorm_rows(zs(C_DV, GROUP_W), slng_ref[...], slnb_ref[...])
    vn_ref[...] = vn
    mixed = sgub_ref[...]
    for s in range(DEC_SEQ):
        mixed = mixed + sguw_ref[s] * vn_ref[:, pl.ds(s, 1), :]
    put(3 * GROUP_W, zs(C_DU, GROUP_W) * mixed)


def _mixer_sample(z, p, layer, s_ret, s_31, s_3, prev=None):
    cos, sin = _rotary_tables(DEC_SEQ, PAST_LEN)
    d_in, d_q, d_k, d_c = _decay_tables(DEC_SEQ)
    din_t = jnp.swapaxes(d_in, 1, 2)[:, :, :, None]
    dq_t = jnp.broadcast_to(d_q[:, :, None], (RET_HEADS, DEC_SEQ, RET_D))
    dk_t = jnp.broadcast_to(d_k[:, :, None], (RET_HEADS, DEC_SEQ, RET_D))
    dc_t = jnp.broadcast_to(d_c[:, None, None], (RET_HEADS, RET_D, RET_D))
    ch = GROUP_W // SGU_GROUPS
    w4 = jnp.tril(p["sgu_w"][:, :DEC_SEQ, :DEC_SEQ])
    w_t = jnp.repeat(jnp.transpose(w4, (2, 1, 0)), ch, axis=-1)
    b_t = jnp.repeat(p["sgu_b"][:, :DEC_SEQ].T, ch, axis=-1)
    bt = SAMPLE_BT

    def full(shape):
        return pl.BlockSpec(shape, lambda i: (0,) * len(shape))

    row = lambda a: a.reshape(1, -1)
    any_spec = pl.BlockSpec(memory_space=pl.ANY)
    ret_blk, b31_blk = (bt, RET_HEADS, RET_D, RET_D), (bt, CONV_W - 1, GROUP_W)
    ret_all, b31_all = (DEC_BATCH,) + ret_blk[1:], (DEC_BATCH,) + b31_blk[1:]
    stacked = prev is not None
    return pl.pallas_call(
        _mixer_sample_stacked_kernel if stacked else _mixer_sample_kernel,
        grid=(DEC_BATCH // bt,),
        in_specs=[
            pl.BlockSpec((bt * DEC_SEQ, D_IN), lambda i: (T_PROMPT // (bt * DEC_SEQ) + i, 0)),
            full((DEC_SEQ, RET_D)),
            full((DEC_SEQ, RET_D)),
            full((RET_HEADS, DEC_SEQ, DEC_SEQ, 1)),
            full((RET_HEADS, DEC_SEQ, RET_D)),
            full((RET_HEADS, DEC_SEQ, RET_D)),
            full((RET_HEADS, RET_D, RET_D)),
            full((1, GROUP_W)),
            full((CONV_W, GROUP_W)),
            full((1, GROUP_W)),
            full((1, GROUP_W)),
            full((1, GROUP_W)),
            full((SC_W, GROUP_W)),
            full((1, GROUP_W)),
            full((1, GROUP_W)),
            full((DEC_SEQ, DEC_SEQ, GROUP_W)),
            full((DEC_SEQ, GROUP_W)),
            pl.BlockSpec((None, bt, RET_HEADS, RET_D, RET_D), lambda i: (layer, i, 0, 0, 0)),
            pl.BlockSpec((None, bt, CONV_W - 1, GROUP_W), lambda i: (layer, i, 0, 0)),
            pl.BlockSpec((None, bt, SC_W - 1, GROUP_W), lambda i: (layer, i, 0, 0)),
        ] + ([pl.BlockSpec(ret_blk, lambda i: (i, 0, 0, 0)), pl.BlockSpec(b31_blk, lambda i: (i, 0, 0))]
             if stacked else []),
        out_specs=[
            pl.BlockSpec((bt * DEC_SEQ, D_MODEL), lambda i: (i, 0)),
            any_spec if stacked else pl.BlockSpec(ret_blk, lambda i: (i, 0, 0, 0)),
            any_spec if stacked else pl.BlockSpec(b31_blk, lambda i: (i, 0, 0)),
            pl.BlockSpec((bt, SC_W - 1, GROUP_W), lambda i: (i, 0, 0)),
            pl.BlockSpec((bt, DEC_SEQ, GROUP_W), lambda i: (i, 0, 0)),
        ],
        out_shape=[
            jax.ShapeDtypeStruct((T_SAMPLE, D_MODEL), F32),
            jax.ShapeDtypeStruct(((DEPTH,) if stacked else ()) + ret_all, F32),
            jax.ShapeDtypeStruct(((DEPTH,) if stacked else ()) + b31_all, F32),
            jax.ShapeDtypeStruct((DEC_BATCH, SC_W - 1, GROUP_W), F32),
            jax.ShapeDtypeStruct((DEC_BATCH, DEC_SEQ, GROUP_W), F32),
        ],
        scratch_shapes=[
            pltpu.VMEM((bt, EXT31_S, GROUP_W), F32),
            pltpu.VMEM((bt, EXT3_S, GROUP_W), F32),
        ] + ([
            pltpu.VMEM((2,) + ret_blk, F32),
            pltpu.VMEM((2,) + b31_blk, F32),
            pltpu.SemaphoreType.DMA((2, 2)),
            pltpu.SemaphoreType.DMA((2,)),
        ] if stacked else []),
        compiler_params=pltpu.CompilerParams(
            dimension_semantics=("arbitrary",), vmem_limit_bytes=V7X_VMEM_LIMIT),
        name="mixer_sample",
    )(z, cos, sin, din_t, dq_t, dk_t, dc_t, row(p["ret_norm_g"]), p["conv31_w"], row(p["conv31_b"]),
      row(p["conv_ln_g"]), row(p["conv_ln_b"]), p["sconv_w"], row(p["sgu_ln_g"]), row(p["sgu_ln_b"]),
      w_t, b_t, s_ret, s_31, s_3, *(prev if stacked else ()))


def _to_token_major(dst, val, pitch=TOK_ROWS):
    n = val.shape[0]
    for a in range(TOK_ROWS):
        dst[pl.ds(a, n, stride=pitch), :] = val[:, a * LANES:(a + 1) * LANES]


STAGE_PITCH = 24
DMA_UNROLL = 8


def _router_kernel(x_ref, g_ref, rw_ref, e_ref, p_ref, c_ref):
    h = _rms_rows(x_ref[...], g_ref[...])
    lane = lax.broadcasted_iota(jnp.int32, (h.shape[0], N_EXPERTS), 1)
    logits = jnp.zeros((h.shape[0], N_EXPERTS), F32)
    for e in range(N_EXPERTS):
        col = jnp.sum(h * rw_ref[pl.ds(e, 1), :], axis=-1, keepdims=True)
        logits = jnp.where(lane == e, col, logits)
    m1 = jnp.max(logits, axis=-1, keepdims=True)
    i1 = jnp.min(jnp.where(logits == m1, lane, N_EXPERTS), axis=-1, keepdims=True)
    rest = jnp.where(lane == i1, -jnp.inf, logits)
    m2 = jnp.max(rest, axis=-1, keepdims=True)
    i2 = jnp.min(jnp.where(rest == m2, lane, N_EXPERTS), axis=-1, keepdims=True)
    e2 = jnp.exp(m2 - m1)
    den = 1.0 + e2
    e_ref[...] = jnp.where(lane == 0, i1, jnp.where(lane == 1, i2, 0))
    p_ref[...] = jnp.where(lane == 0, 1.0 / den, jnp.where(lane == 1, e2 / den, 0.0))
    c_ref[pl.ds(0, 1), :] = jnp.sum(jnp.where(lane == i1, 1.0, 0.0), axis=0, keepdims=True).astype(jnp.int32)
    c_ref[pl.ds(1, 1), :] = jnp.sum(jnp.where(lane == i2, 1.0, 0.0), axis=0, keepdims=True).astype(jnp.int32)


def _router(x, g, rw):
    t, d = x.shape
    return pl.pallas_call(
        _router_kernel,
        grid=(N_TB,),
        in_specs=[
            pl.BlockSpec((MOE_TB, d), lambda i: (i, 0)),
            pl.BlockSpec((1, d), lambda i: (0, 0)),
            pl.BlockSpec((N_EXPERTS, d), lambda i: (0, 0)),
        ],
        out_specs=[
            pl.BlockSpec((MOE_TB, N_EXPERTS), lambda i: (i, 0)),
            pl.BlockSpec((MOE_TB, N_EXPERTS), lambda i: (i, 0)),
            pl.BlockSpec((None, TOP_K, N_EXPERTS), lambda i: (i, 0, 0)),
        ],
        out_shape=[
            jax.ShapeDtypeStruct((t, N_EXPERTS), jnp.int32),
            jax.ShapeDtypeStruct((t, N_EXPERTS), F32),
            jax.ShapeDtypeStruct((N_TB, TOP_K, N_EXPERTS), jnp.int32),
        ],
        compiler_params=pltpu.CompilerParams(
            dimension_semantics=("arbitrary",), vmem_limit_bytes=V7X_VMEM_LIMIT),
        name="moe_router",
    )(x, g.reshape(1, d), rw.T)


def _pick(table, idx):
    lanes = jnp.arange(N_EXPERTS, dtype=jnp.int32)
    return jnp.sum(jnp.where(idx[..., None] == lanes, table, 0), axis=-1)


def _expert_layout(cnt):
    tot = jnp.sum(cnt, axis=(0, 1))
    start = jnp.cumsum(tot) - tot
    blk = jnp.sum(cnt, axis=1)
    blk_base = start[None, :] + jnp.cumsum(blk, axis=0) - blk
    base = jnp.stack([blk_base, blk_base + cnt[:, 0]], axis=1)
    n_sb = (tot + SB - 1) // SB
    sb_end = jnp.cumsum(n_sb)
    n_used = sb_end[-1]
    s_idx = jnp.arange(NS_MAX, dtype=jnp.int32)
    s_cl = jnp.minimum(s_idx, n_used - 1)
    sb_e = jnp.sum(s_cl[:, None] >= sb_end[None, :], axis=-1).astype(jnp.int32)
    local = s_cl - _pick(sb_end - n_sb, sb_e)
    sb_row0 = _pick(start, sb_e) + local * SB
    sb_rows = jnp.where(s_idx < n_used, jnp.clip(_pick(tot, sb_e) - local * SB, 0, SB), 0)
    i32 = lambda a: a.astype(jnp.int32)
    return i32(base), sb_e, i32(sb_row0), i32(sb_rows), i32(n_used).reshape(1)


def _dest_kernel(e_ref, b_ref, d_ref):
    eid = e_ref[...]
    lane = lax.broadcasted_iota(jnp.int32, eid.shape, 1)
    row = lax.broadcasted_iota(jnp.int32, (MOE_TB, MOE_TB), 0)
    col = lax.broadcasted_iota(jnp.int32, (MOE_TB, MOE_TB), 1)
    earlier = jnp.where(row > col, 1.0, 0.0).astype(BF16)
    base = b_ref[...].astype(F32)

    def dest_of(k):
        mine = lane == eid[:, k:k + 1]
        before = jnp.dot(earlier, jnp.where(mine, 1.0, 0.0).astype(BF16), preferred_element_type=F32)
        return jnp.sum(jnp.where(mine, before + base[k:k + 1], 0.0), axis=-1, keepdims=True)

    d_ref[...] = jnp.where(lane == 0, dest_of(0), jnp.where(lane == 1, dest_of(1), 0.0)).astype(jnp.int32)


def _dest_rows(eid, base):
    return pl.pallas_call(
        _dest_kernel,
        grid=(N_TB,),
        in_specs=[
            pl.BlockSpec((MOE_TB, N_EXPERTS), lambda i: (i, 0)),
            pl.BlockSpec((None, TOP_K, N_EXPERTS), lambda i: (i, 0, 0)),
        ],
        out_specs=pl.BlockSpec((MOE_TB, N_EXPERTS), lambda i: (i, 0)),
        out_shape=jax.ShapeDtypeStruct((T_ALL, N_EXPERTS), jnp.int32),
        compiler_params=pltpu.CompilerParams(
            dimension_semantics=("arbitrary",), vmem_limit_bytes=V7X_VMEM_LIMIT),
        name="moe_dest_rows",
    )(eid, base)


def _dispatch_kernel(d_ref, x_ref, g_ref, xg_hbm, stage, zeros, sem, zsem):
    i = pl.program_id(0)
    n = pl.num_programs(0)
    slot = i % 2
    blk_rows = MOE_TB * TOK_ROWS

    def wait_slot(sl):
        for _ in range(TOP_K):
            pltpu.make_async_copy(stage.at[sl, pl.ds(0, blk_rows)], xg_hbm.at[pl.ds(0, blk_rows)], sem.at[sl]).wait()

    @pl.when(i >= 2)
    def _():
        wait_slot(slot)

    _to_token_major(stage.at[slot], _rms_rows(x_ref[...], g_ref[...]), STAGE_PITCH)

    def body(r, carry):
        src = stage.at[slot, pl.ds(pl.multiple_of(r * STAGE_PITCH, 8), TOK_ROWS)]
        for k in range(TOP_K):
            row = d_ref[0, 0, TOP_K * r + k]
            dst = xg_hbm.at[pl.ds(pl.multiple_of(row * TOK_ROWS, TOK_ROWS), TOK_ROWS)]
            pltpu.make_async_copy(src, dst, sem.at[slot]).start()
        return carry

    lax.fori_loop(0, MOE_TB, body, 0, unroll=DMA_UNROLL)

    @pl.when(i == n - 1)
    def _():
        zeros[...] = jnp.zeros_like(zeros)
        tail = [pltpu.make_async_copy(zeros, xg_hbm.at[pl.ds((N_SLOTS + q * (SUB // TOK_ROWS)) * TOK_ROWS, SUB)], zsem)
                for q in range(TOK_ROWS)]
        for cp in tail:
            cp.start()
        for cp in tail:
            cp.wait()
        wait_slot(slot)
        wait_slot(1 - slot)


def _dispatch(x, g, dest_flat):
    t, d = x.shape
    return pl.pallas_call(
        _dispatch_kernel,
        grid=(N_TB,),
        in_specs=[
            pl.BlockSpec((1, 1, TOP_K * MOE_TB), lambda i: (i, 0, 0), memory_space=pltpu.SMEM),
            pl.BlockSpec((MOE_TB, d), lambda i: (i, 0)),
            pl.BlockSpec((1, d), lambda i: (0, 0)),
        ],
        out_specs=pl.BlockSpec(memory_space=pl.ANY),
        out_shape=jax.ShapeDtypeStruct(((N_SLOTS + SUB) * TOK_ROWS, LANES), F32),
        scratch_shapes=[
            pltpu.VMEM((2, MOE_TB * STAGE_PITCH, LANES), F32),
            pltpu.VMEM((SUB, LANES), F32),
            pltpu.SemaphoreType.DMA((2,)),
            pltpu.SemaphoreType.DMA(()),
        ],
        compiler_params=pltpu.CompilerParams(
            dimension_semantics=("arbitrary",), vmem_limit_bytes=V7X_VMEM_LIMIT),
        name="moe_dispatch",
    )(dest_flat, x, g.reshape(1, d))


def _moe_ffn_kernel(se_ref, r0_ref, nr_ref, nu_ref, xg_hbm, w1_ref, w3_ref, w2_ref, y_hbm,
                    x_scr, acc, stage, w1b, w3b, w2b, sem_in, sem_out):
    s = pl.program_id(0)
    j = pl.program_id(1)
    nj = pl.num_programs(1)
    rows = nr_ref[s]
    row0 = r0_ref[s]
    nsub = lax.shift_right_logical(rows + (SUB - 1), int(math.log2(SUB)))
    sub_rows = SUB * TOK_ROWS

    is_middle = (j > 0) & (j < nj - 1)

    def cast_weights():
        ws = (w1_ref[...].astype(BF16), w3_ref[...].astype(BF16), w2_ref[...].astype(BF16))
        w1b[...], w3b[...], w2b[...] = ws
        return ws

    @pl.when((nsub > 0) & jnp.logical_not(is_middle))
    def _():
        cast_weights()

    def sub(r):
        return pl.ds(pl.multiple_of(r * SUB, SUB), SUB)

    def swiglu(xs, ws=None):
        w1, w3, w2 = ws if ws is not None else (w1b[...], w3b[...], w2b[...])
        a = jnp.dot(xs, w1, preferred_element_type=F32)
        b = jnp.dot(xs, w3, preferred_element_type=F32)
        u = (_silu(a) * b).astype(BF16)
        return jnp.dot(u, w2, preferred_element_type=F32)

    def tile(r):
        return swiglu(x_scr[sub(r), :])

    def in_copy(r, slot):
        first = pl.multiple_of((row0 + r * SUB) * TOK_ROWS, TOK_ROWS)
        return pltpu.make_async_copy(xg_hbm.at[pl.ds(first, sub_rows)], stage.at[slot], sem_in.at[slot])

    @pl.when(j == 0)
    def _():
        @pl.when(nsub > 0)
        def _():
            in_copy(0, 0).start()

        def body(r, carry):
            slot = r % 2
            in_copy(r, slot).wait()

            @pl.when(r + 1 < nsub)
            def _():
                in_copy(r + 1, 1 - slot).start()

            for a in range(TOK_ROWS):
                x_scr[sub(r), pl.ds(a * LANES, LANES)] = stage[slot, pl.ds(a, SUB, stride=TOK_ROWS), :].astype(BF16)
            acc[sub(r), :] = tile(r)
            return carry

        lax.fori_loop(0, nsub, body, 0)

        def clear(r, carry):
            x_scr[sub(r), :] = jnp.zeros((SUB, D_MODEL), BF16)
            acc[sub(r), :] = jnp.zeros((SUB, D_MODEL), F32)
            return carry

        lax.fori_loop(nsub, SB // SUB, clear, 0)

    @pl.when(is_middle)
    def _():
        for c in range(SB // CHUNK_ROWS):
            @pl.when(rows > c * CHUNK_ROWS)
            def _():
                ws = cast_weights() if c == 0 else None
                rows_c = pl.ds(c * CHUNK_ROWS, CHUNK_ROWS)
                acc[rows_c, :] += swiglu(x_scr[rows_c, :], ws)

    @pl.when(j == nj - 1)
    def _():
        def out_copies(r, act):
            nv = jnp.clip(rows - r * SUB, 0, SUB)
            bit = SUB
            while bit >= 1:
                off = nv & (-2 * bit)
                src = stage.at[0, pl.ds(pl.multiple_of(off * TOK_ROWS, TOK_ROWS), bit * TOK_ROWS)]
                first = pl.multiple_of((row0 + r * SUB + off) * TOK_ROWS, TOK_ROWS)
                cp = pltpu.make_async_copy(src, y_hbm.at[pl.ds(first, bit * TOK_ROWS)], sem_out)
                pl.when((nv & bit) != 0)(functools.partial(act, cp))
                bit //= 2

        def body(r, carry):
            y = acc[sub(r), :] + tile(r)

            @pl.when(r > 0)
            def _():
                out_copies(r - 1, lambda cp: cp.wait())

            _to_token_major(stage.at[0], y)
            out_copies(r, lambda cp: cp.start())
            return carry

        lax.fori_loop(0, nsub, body, 0)

        @pl.when(nsub > 0)
        def _():
            out_copies(nsub - 1, lambda cp: cp.wait())


def _moe_ffn(xg, w1, w3, w2, sb_e, sb_row0, sb_rows, n_used):
    d = w1.shape[1]
    f = w1.shape[2]
    nj = f // TF

    def jmap(s, j, nu):
        return jnp.where(s < nu[0], j, nj - 1)

    return pl.pallas_call(
        _moe_ffn_kernel,
        grid_spec=pltpu.PrefetchScalarGridSpec(
            num_scalar_prefetch=4,
            grid=(NS_MAX, nj),
            in_specs=[
                pl.BlockSpec(memory_space=pl.ANY),
                pl.BlockSpec((None, d, TF), lambda s, j, se, r0, nr, nu: (se[s], 0, jmap(s, j, nu))),
                pl.BlockSpec((None, d, TF), lambda s, j, se, r0, nr, nu: (se[s], 0, jmap(s, j, nu))),
                pl.BlockSpec((None, TF, d), lambda s, j, se, r0, nr, nu: (se[s], jmap(s, j, nu), 0)),
            ],
            out_specs=pl.BlockSpec(memory_space=pl.ANY),
            scratch_shapes=[
                pltpu.VMEM((SB, d), BF16),
                pltpu.VMEM((SB, d), F32),
                pltpu.VMEM((2, SUB * TOK_ROWS, LANES), F32),
                pltpu.VMEM((d, TF), BF16),
                pltpu.VMEM((d, TF), BF16),
                pltpu.VMEM((TF, d), BF16),
                pltpu.SemaphoreType.DMA((2,)),
                pltpu.SemaphoreType.DMA(()),
            ],
        ),
        out_shape=jax.ShapeDtypeStruct((N_SLOTS * TOK_ROWS, LANES), F32),
        compiler_params=pltpu.CompilerParams(
            dimension_semantics=("arbitrary", "arbitrary"), vmem_limit_bytes=V7X_VMEM_LIMIT),
        name="moe_ffn",
    )(sb_e, sb_row0, sb_rows, n_used, xg, w1, w3, w2)


N_TB_PROMPT = T_PROMPT // MOE_TB


def _combine_kernel(d_ref, dn_ref, x_ref, p_ref, g_ref, y_hbm, op_ref, os_ref, stage, sem, *, final_norm):
    i = pl.program_id(0)
    n = pl.num_programs(0)
    slot = i % 2
    blk_rows = MOE_TB * TOK_ROWS

    def start_block(dref, sl):
        def body(r, carry):
            for k in range(TOP_K):
                row = dref[0, 0, TOP_K * r + k]
                src = y_hbm.at[pl.ds(pl.multiple_of(row * TOK_ROWS, TOK_ROWS), TOK_ROWS)]
                dst = stage.at[sl, k, pl.ds(pl.multiple_of(r * STAGE_PITCH, 8), TOK_ROWS)]
                pltpu.make_async_copy(src, dst, sem.at[sl]).start()
            return carry

        lax.fori_loop(0, MOE_TB, body, 0, unroll=DMA_UNROLL)

    @pl.when(i == 0)
    def _():
        start_block(d_ref, 0)

    @pl.when(i + 1 < n)
    def _():
        start_block(dn_ref, 1 - slot)

    for k in range(TOP_K):
        pltpu.make_async_copy(y_hbm.at[pl.ds(0, blk_rows)], stage.at[slot, k, pl.ds(0, blk_rows)], sem.at[slot]).wait()
    g0 = p_ref[:, 0:1]
    g1 = p_ref[:, 1:2]
    pieces = []
    for a in range(TOK_ROWS):
        y0 = stage[slot, 0, pl.ds(a, MOE_TB, stride=STAGE_PITCH), :]
        y1 = stage[slot, 1, pl.ds(a, MOE_TB, stride=STAGE_PITCH), :]
        pieces.append(x_ref[:, pl.ds(a * LANES, LANES)] + (y0 * g0 + y1 * g1))
    y = jnp.concatenate(pieces, axis=1)
    if final_norm:
        y = _rms_rows(y, g_ref[...])

    @pl.when(i < N_TB_PROMPT)
    def _():
        op_ref[...] = y

    @pl.when(i >= N_TB_PROMPT)
    def _():
        os_ref[...] = y


def _moe_combine(x, y, dest_flat, gate, g, final_norm):
    t, d = x.shape
    return pl.pallas_call(
        functools.partial(_combine_kernel, final_norm=final_norm),
        grid=(N_TB,),
        in_specs=[
            pl.BlockSpec((1, 1, TOP_K * MOE_TB), lambda i: (i, 0, 0), memory_space=pltpu.SMEM),
            pl.BlockSpec((1, 1, TOP_K * MOE_TB), lambda i: (jnp.minimum(i + 1, N_TB - 1), 0, 0),
                         memory_space=pltpu.SMEM),
            pl.BlockSpec((MOE_TB, d), lambda i: (i, 0)),
            pl.BlockSpec((MOE_TB, N_EXPERTS), lambda i: (i, 0)),
            pl.BlockSpec((1, d), lambda i: (0, 0)),
            pl.BlockSpec(memory_space=pl.ANY),
        ],
        out_specs=[
            pl.BlockSpec((MOE_TB, d), lambda i: (jnp.minimum(i, N_TB_PROMPT - 1), 0)),
            pl.BlockSpec((MOE_TB, d), lambda i: (jnp.maximum(i - N_TB_PROMPT, 0), 0)),
        ],
        out_shape=[
            jax.ShapeDtypeStruct((T_PROMPT, d), F32),
            jax.ShapeDtypeStruct((T_SAMPLE, d), F32),
        ],
        scratch_shapes=[
            pltpu.VMEM((2, TOP_K, MOE_TB * STAGE_PITCH, LANES), F32),
            pltpu.SemaphoreType.DMA((2,)),
        ],
        compiler_params=pltpu.CompilerParams(
            dimension_semantics=("arbitrary",), vmem_limit_bytes=V7X_VMEM_LIMIT),
        name="moe_combine",
    )(dest_flat, dest_flat, x, gate, g.reshape(1, d), y)


def _moe(x, norm_g, rw, w1, w3, w2, out_g, final_norm):
    eid, gate, cnt = _router(x, norm_g, rw)
    base, sb_e, sb_row0, sb_rows, n_used = _expert_layout(cnt)
    dest = _dest_rows(eid, base)
    dest_flat = dest[:, :TOP_K].reshape(N_TB, 1, TOP_K * MOE_TB)
    xg = _dispatch(x, norm_g, dest_flat)
    y = _moe_ffn(xg, w1, w3, w2, sb_e, sb_row0, sb_rows, n_used)
    return _moe_combine(x, y, dest_flat, gate, out_g, final_norm)


def kernel(x_prompt, x_sample, state_ret, state_conv31, state_conv3, mix_norm_g, w_in, ret_norm_g, conv31_w, conv31_b, conv_ln_g, conv_ln_b, sconv_w, sgu_ln_g, sgu_ln_b, sgu_w, sgu_b, w_out, ffn_norm_g, dense_w1, dense_w3, dense_w2, router_w, moe_w1, moe_w3, moe_w2, final_norm_g):
    assert DEPTH == 2, "layer 0 has the dense channel mixer; layer 1's expert mixer ends with the final norm"
    xp2, xs2 = x_prompt.reshape(T_PROMPT, D_MODEL), x_sample.reshape(T_SAMPLE, D_MODEL)
    hn = _prep(xp2, xs2, mix_norm_g[0])
    res = (xp2, xs2, 0)
    rets_p, c31_p, c3_p, c3_s, vns = [], [], [], [], []
    ret_s = b31_s = None
    for l in range(DEPTH):
        p = dict(ret_norm_g=ret_norm_g[l], conv31_w=conv31_w[l], conv31_b=conv31_b[l],
                 conv_ln_g=conv_ln_g[l], conv_ln_b=conv_ln_b[l], sconv_w=sconv_w[l],
                 sgu_ln_g=sgu_ln_g[l], sgu_ln_b=sgu_ln_b[l], sgu_w=sgu_w[l], sgu_b=sgu_b[l])
        z = _in_proj(hn, w_in, l)
        mix_s, ret_s, b31_s, b3_s, vn_s = _mixer_sample(
            z, p, l, state_ret, state_conv31, state_conv3, prev=(ret_s, b31_s) if l == DEPTH - 1 else None)
        mix_p, ret_p, b31_p, b3_p = _mixer_prompt(z, p)
        x = _out_proj_residual(mix_p, mix_s, w_out, l, *res)
        j = l // 2
        if l % 2 == 0:
            x, hn = _ffn_dense(x, ffn_norm_g[l], dense_w1[j], dense_w3[j], dense_w2[j], mix_norm_g[l + 1])
            res = (x, x, N_OUT_PROMPT)
        else:
            y_p, y_s = _moe(x, ffn_norm_g[l], router_w[j], moe_w1[j], moe_w3[j], moe_w2[j],
                            final_norm_g, final_norm=True)
        rets_p.append(ret_p)
        c31_p.append(b31_p)
        c3_p.append(b3_p); c3_s.append(b3_s)
        vns.append(vn_s)
    y_prompt = y_p.reshape(BATCH, SEQ, D_MODEL)
    y_sample = y_s.reshape(DEC_BATCH, DEC_SEQ, D_MODEL)
    return (y_prompt, y_sample, jnp.stack(rets_p), ret_s, jnp.stack(c31_p), b31_s,
            jnp.stack(c3_p), jnp.stack(c3_s), jnp.stack(vns))
```

```python
import functools
import math

import jax
import jax.numpy as jnp
from jax import lax
from jax.experimental import pallas as pl
from jax.experimental.pallas import tpu as pltpu

F32 = jnp.float32
BF16 = jnp.bfloat16

D_MODEL = 2048
BATCH = 4
SEQ = 2048
DEPTH = 2
DEC_BATCH = 128
DEC_SEQ = 4
PAST_LEN = 16384
GROUP_W = 512
D_IN = 11 * GROUP_W
RET_HEADS = 4
RET_D = 128
CHUNK = 128
ROPE_THETA = 10000.0
CONV_W = 31
SC_W = 3
SGU_GROUPS = 4
D_FF = 5632
N_EXPERTS = 8
TOP_K = 2
EPS = 1e-6

T_PROMPT = BATCH * SEQ
T_SAMPLE = DEC_BATCH * DEC_SEQ
T_ALL = T_PROMPT + T_SAMPLE
N_CHUNKS = SEQ // CHUNK

C_Q, C_K, C_V, C_G, C_BL, C_BG, C_CB, C_CC, C_CH, C_DU, C_DV = (i * GROUP_W for i in range(11))

V7X_VMEM_LIMIT = 56 * 1024 * 1024

TM = 1088
TM_IN = 2176
TM_OUT = 512
N_OUT_PROMPT = T_PROMPT // TM_OUT
NORM_ROWS = 272
TN_IN = 512
TF = 256
SAMPLE_BT = 16

LANES = 128
SUBLANES = 8
TOK_ROWS = D_MODEL // LANES
MOE_TB = 256
N_TB = T_ALL // MOE_TB
N_SLOTS = T_ALL * TOP_K
SUB = 256
SB = 9 * SUB
CHUNK_ROWS = SB // 3
NS_MAX = N_SLOTS // SB + N_EXPERTS


def _rms_rows(x, g):
    ms = jnp.mean(x * x, axis=-1, keepdims=True)
    return (x * lax.rsqrt(ms + EPS)) * g


def _layernorm_rows(x, g, b):
    mu = jnp.mean(x, axis=-1, keepdims=True)
    xc = x - mu
    return xc * lax.rsqrt(jnp.mean(xc * xc, axis=-1, keepdims=True) + EPS) * g + b


def _silu(x):
    return x * jax.nn.sigmoid(x)


def _norm_block_to(x_ref, g_ref, dst_ref):
    rows = x_ref.shape[0]
    g = g_ref[...]

    def body(c, carry):
        r0 = pl.multiple_of(c * NORM_ROWS, NORM_ROWS)
        x = x_ref[pl.ds(r0, NORM_ROWS), :]
        dst_ref[pl.ds(r0, NORM_ROWS), :] = _rms_rows(x, g).astype(dst_ref.dtype)
        return carry

    lax.fori_loop(0, rows // NORM_ROWS, body, 0)


PREP_ROWS = 512
N_PREP_PROMPT = T_PROMPT // PREP_ROWS


def _prep_kernel(xp_ref, xs_ref, g_ref, hn_ref):
    def emit(src_ref):
        hn_ref[...] = _rms_rows(src_ref[...], g_ref[...]).astype(hn_ref.dtype)

    is_prompt = pl.program_id(0) < N_PREP_PROMPT
    pl.when(is_prompt)(functools.partial(emit, xp_ref))
    pl.when(jnp.logical_not(is_prompt))(functools.partial(emit, xs_ref))


def _prep(x_prompt, x_sample, g):
    d = x_prompt.shape[1]
    blk = lambda f: pl.BlockSpec((PREP_ROWS, d), f)
    return pl.pallas_call(
        _prep_kernel,
        grid=(T_ALL // PREP_ROWS,),
        in_specs=[
            blk(lambda i: (jnp.minimum(i, N_PREP_PROMPT - 1), 0)),
            blk(lambda i: (jnp.maximum(i - N_PREP_PROMPT, 0), 0)),
            pl.BlockSpec((1, d), lambda i: (0, 0)),
        ],
        out_specs=blk(lambda i: (i, 0)),
        out_shape=jax.ShapeDtypeStruct((T_ALL, d), BF16),
        compiler_params=pltpu.CompilerParams(
            dimension_semantics=("arbitrary",), vmem_limit_bytes=V7X_VMEM_LIMIT),
        name="prep_rows",
    )(x_prompt, x_sample, g.reshape(1, d))


def _in_proj_kernel(h_ref, w_ref, o_ref):
    o_ref[...] = jnp.dot(h_ref[...], w_ref[...].astype(BF16), preferred_element_type=F32)


def _in_proj(hn, w_all, layer):
    t, d = hn.shape
    n = w_all.shape[2]
    return pl.pallas_call(
        _in_proj_kernel,
        grid=(t // TM_IN, n // TN_IN),
        in_specs=[
            pl.BlockSpec((TM_IN, d), lambda i, j: (i, 0)),
            pl.BlockSpec((None, d, TN_IN), lambda i, j: (layer, 0, j)),
        ],
        out_specs=pl.BlockSpec((TM_IN, TN_IN), lambda i, j: (i, j)),
        out_shape=jax.ShapeDtypeStruct((t, n), F32),
        compiler_params=pltpu.CompilerParams(
            dimension_semantics=("arbitrary", "arbitrary"), vmem_limit_bytes=V7X_VMEM_LIMIT),
        name="in_proj",
    )(hn, w_all)


W_LOAD_ROWS = 512


def _out_proj_kernel(ap_ref, as_ref, w_hbm, rp_ref, rs_ref, o_ref, wb, stage, sem, *, layer):
    i = pl.program_id(0)

    @pl.when(i == 0)
    def _():
        for c in range(wb.shape[0] // W_LOAD_ROWS):
            rows = pl.ds(c * W_LOAD_ROWS, W_LOAD_ROWS)
            cp = pltpu.make_async_copy(w_hbm.at[layer, rows], stage, sem)
            cp.start()
            cp.wait()
            wb[rows, :] = stage[...].astype(BF16)

    def emit(a, r_ref):
        o_ref[...] = r_ref[...] + jnp.dot(a, wb[...], preferred_element_type=F32)

    pl.when(i < N_OUT_PROMPT)(lambda: emit(ap_ref[...], rp_ref))
    pl.when(i >= N_OUT_PROMPT)(lambda: emit(as_ref[...].astype(BF16), rs_ref))


def _out_proj_residual(a_prompt, a_sample, w_all, layer, r_prompt, r_sample, r_sample_block):
    k = a_prompt.shape[1]
    t = T_ALL
    n = w_all.shape[2]
    assert T_SAMPLE == TM_OUT and T_PROMPT % TM_OUT == 0
    return pl.pallas_call(
        functools.partial(_out_proj_kernel, layer=layer),
        grid=(t // TM_OUT,),
        in_specs=[
            pl.BlockSpec((TM_OUT, k), lambda i: (jnp.minimum(i, N_OUT_PROMPT - 1), 0)),
            pl.BlockSpec((TM_OUT, k), lambda i: (0, 0)),
            pl.BlockSpec(memory_space=pl.ANY),
            pl.BlockSpec((TM_OUT, n), lambda i: (jnp.minimum(i, N_OUT_PROMPT - 1), 0)),
            pl.BlockSpec((TM_OUT, n), lambda i: (r_sample_block, 0)),
        ],
        out_specs=pl.BlockSpec((TM_OUT, n), lambda i: (i, 0)),
        out_shape=jax.ShapeDtypeStruct((t, n), F32),
        scratch_shapes=[
            pltpu.VMEM((k, n), BF16),
            pltpu.VMEM((W_LOAD_ROWS, n), F32),
            pltpu.SemaphoreType.DMA(()),
        ],
        compiler_params=pltpu.CompilerParams(
            dimension_semantics=("arbitrary",), vmem_limit_bytes=V7X_VMEM_LIMIT),
        name="out_proj_residual",
    )(a_prompt, a_sample, w_all, r_prompt, r_sample)


def _swiglu_tile(h, w1_ref, w3_ref, w2_ref):
    a = jnp.dot(h, w1_ref[...].astype(BF16), preferred_element_type=F32)
    b = jnp.dot(h, w3_ref[...].astype(BF16), preferred_element_type=F32)
    u = (_silu(a) * b).astype(BF16)
    return jnp.dot(u, w2_ref[...].astype(BF16), preferred_element_type=F32)


def _ffn_kernel(x_hbm, g_ref, w1_ref, w3_ref, w2_ref, gn_ref, o_ref, hnext_ref, hn_ref, sem):
    @pl.when(pl.program_id(1) == 0)
    def _():
        r0 = pl.multiple_of(pl.program_id(0) * TM, TM)
        cp = pltpu.make_async_copy(x_hbm.at[pl.ds(r0, TM)], o_ref, sem)
        cp.start()
        cp.wait()
        _norm_block_to(o_ref, g_ref, hn_ref)

    o_ref[...] += _swiglu_tile(hn_ref[...], w1_ref, w3_ref, w2_ref)

    @pl.when(pl.program_id(1) == pl.num_programs(1) - 1)
    def _():
        _norm_block_to(o_ref, gn_ref, hnext_ref)


def _ffn_dense(x, g, w1, w3, w2, g_next):
    t, d = x.shape
    f = w1.shape[1]
    return pl.pallas_call(
        _ffn_kernel,
        grid=(t // TM, f // TF),
        in_specs=[
            pl.BlockSpec(memory_space=pl.ANY),
            pl.BlockSpec((1, d), lambda i, j: (0, 0)),
            pl.BlockSpec((d, TF), lambda i, j: (0, j)),
            pl.BlockSpec((d, TF), lambda i, j: (0, j)),
            pl.BlockSpec((TF, d), lambda i, j: (j, 0)),
            pl.BlockSpec((1, d), lambda i, j: (0, 0)),
        ],
        out_specs=[
            pl.BlockSpec((TM, d), lambda i, j: (i, 0)),
            pl.BlockSpec((TM, d), lambda i, j: (i, 0)),
        ],
        out_shape=[jax.ShapeDtypeStruct((t, d), F32), jax.ShapeDtypeStruct((t, d), BF16)],
        scratch_shapes=[pltpu.VMEM((TM, d), BF16), pltpu.SemaphoreType.DMA(())],
        compiler_params=pltpu.CompilerParams(
            dimension_semantics=("arbitrary", "arbitrary"), vmem_limit_bytes=V7X_VMEM_LIMIT),
        name="ffn_dense",
    )(x, g.reshape(1, d), w1, w3, w2, g_next.reshape(1, d))


def _rotary_tables(length, pos0):
    half = RET_D // 2
    inv = ROPE_THETA ** (-jnp.arange(half, dtype=F32) / half)
    pos = jnp.arange(length, dtype=F32) + pos0
    ang = pos[:, None] * inv[None, :]
    cos, sin = jnp.cos(ang), jnp.sin(ang)
    return jnp.concatenate([cos, cos], axis=-1), jnp.concatenate([-sin, sin], axis=-1)


def _decay_tables(c):
    log_g = jnp.log1p(-jnp.exp2(-5.0 - jnp.arange(RET_HEADS, dtype=F32)))
    idx = jnp.arange(c, dtype=F32)
    rel = idx[:, None] - idx[None, :]
    d_in = jnp.where(rel >= 0, jnp.exp(log_g[:, None, None] * jnp.maximum(rel, 0.0)), 0.0)
    d_q = jnp.exp(log_g[:, None] * (idx + 1.0)[None, :])
    d_k = jnp.exp(log_g[:, None] * (c - 1.0 - idx)[None, :])
    d_c = jnp.exp(log_g * c)
    return d_in, d_q, d_k, d_c


HIST31 = 32
HIST3 = 8


PROMPT_CHUNKS_PER_STEP = 2


def _mixer_prompt_kernel(z_ref, cos_ref, sin_ref, din_ref, dq_ref, dk_ref, dc_ref, retg_ref,
                         c31w_ref, c31b_ref, clng_ref, clnb_ref, scw_ref, slng_ref, slnb_ref,
                         sguw_ref, sgub_ref,
                         mix_ref, ret_ref, b31_ref, b3_ref,
                         s_scr, ext31_scr, ext3_scr, ysh_scr):
    c = pl.program_id(1)

    @pl.when(c == 0)
    def _():
        s_scr[...] = jnp.zeros_like(s_scr)
        ext31_scr[pl.ds(0, HIST31), :] = jnp.zeros((HIST31, GROUP_W), F32)
        ext3_scr[pl.ds(0, HIST3), :] = jnp.zeros((HIST3, GROUP_W), F32)

    for cc in range(PROMPT_CHUNKS_PER_STEP):
        _mixer_prompt_chunk(pl.ds(cc * CHUNK, CHUNK), z_ref, cos_ref, sin_ref, din_ref, dq_ref, dk_ref, dc_ref,
                            retg_ref, c31w_ref, c31b_ref, clng_ref, clnb_ref, scw_ref, slng_ref, slnb_ref,
                            sguw_ref, sgub_ref, mix_ref, s_scr, ext31_scr, ext3_scr, ysh_scr)

    @pl.when(c == pl.num_programs(1) - 1)
    def _():
        ret_ref[...] = s_scr[...]
        b31_ref[...] = ext31_scr[pl.ds(HIST31 - (CONV_W - 1), CONV_W - 1), :]
        b3_ref[...] = ext3_scr[pl.ds(HIST3 - (SC_W - 1), SC_W - 1), :]


def _mixer_prompt_chunk(rows, z_ref, cos_ref, sin_ref, din_ref, dq_ref, dk_ref, dc_ref, retg_ref,
                        c31w_ref, c31b_ref, clng_ref, clnb_ref, scw_ref, slng_ref, slnb_ref,
                        sguw_ref, sgub_ref, mix_ref, s_scr, ext31_scr, ext3_scr, ysh_scr):
    cos = cos_ref[rows, :]
    sin = sin_ref[rows, :]

    for h in range(RET_HEADS):
        lo = h * RET_D
        q = z_ref[rows, pl.ds(C_Q + lo, RET_D)]
        k = z_ref[rows, pl.ds(C_K + lo, RET_D)]
        v = z_ref[rows, pl.ds(C_V + lo, RET_D)]
        g = z_ref[rows, pl.ds(C_G + lo, RET_D)]
        q = q * cos + pltpu.roll(q, RET_D // 2, axis=1) * sin
        k = (k * cos + pltpu.roll(k, RET_D // 2, axis=1) * sin) * (RET_D ** -0.5)
        qb = q.astype(BF16)
        vb = v.astype(BF16)
        scores = lax.dot_general(qb, k.astype(BF16), (((1,), (1,)), ((), ())),
                                 preferred_element_type=F32) * din_ref[h]
        inner = jnp.dot(scores.astype(BF16), vb, preferred_element_type=F32)
        s_old = s_scr[h]
        cross = jnp.dot(qb, s_old.astype(BF16), preferred_element_type=F32) * dq_ref[h]
        kd = (k * dk_ref[h]).astype(BF16)
        s_scr[h] = s_old * dc_ref[h] + lax.dot_general(kd, vb, (((0,), (0,)), ((), ())),
                                                       preferred_element_type=F32)
        r = inner + cross
        r = r * lax.rsqrt(jnp.mean(r * r, axis=-1, keepdims=True) + EPS)
        r = r * retg_ref[:, pl.ds(lo, RET_D)]
        mix_ref[rows, pl.ds(lo, RET_D)] = (_silu(g) * r).astype(mix_ref.dtype)

    glu = z_ref[rows, pl.ds(C_BL, GROUP_W)] * jax.nn.sigmoid(z_ref[rows, pl.ds(C_BG, GROUP_W)])
    ext31_scr[pl.ds(HIST31, CHUNK), :] = glu
    s0 = HIST31 - (CONV_W - 1)
    conv = None
    for i in range(SUBLANES):
        n_rows = CHUNK if i == 0 else CHUNK + SUBLANES
        y = None
        for m in range(HIST31 // SUBLANES + 1):
            j = SUBLANES * m + i - s0
            if 0 <= j < CONV_W:
                term = ext31_scr[pl.ds(SUBLANES * m, n_rows), :] * c31w_ref[pl.ds(j, 1), :]
                y = term if y is None else y + term
        if i == 0:
            conv = y
        else:
            ysh_scr[i - 1] = y
            conv = conv + ysh_scr[i - 1, pl.ds(i, CHUNK), :]
    conv = conv + c31b_ref[...]
    out_b = _silu(_layernorm_rows(conv, clng_ref[...], clnb_ref[...]))
    mix_ref[rows, pl.ds(GROUP_W, GROUP_W)] = out_b.astype(mix_ref.dtype)
    ext31_scr[pl.ds(0, HIST31), :] = ext31_scr[pl.ds(CHUNK, HIST31), :]

    gated = z_ref[rows, pl.ds(C_CC, GROUP_W)] * z_ref[rows, pl.ds(C_CH, GROUP_W)]
    ext3_scr[pl.ds(HIST3, CHUNK), :] = gated
    conv3 = ext3_scr[pl.ds(HIST3 - (SC_W - 1), CHUNK), :] * scw_ref[pl.ds(0, 1), :]
    for j in range(1, SC_W):
        conv3 = conv3 + ext3_scr[pl.ds(HIST3 - (SC_W - 1) + j, CHUNK), :] * scw_ref[pl.ds(j, 1), :]
    mix_ref[rows, pl.ds(2 * GROUP_W, GROUP_W)] = (z_ref[rows, pl.ds(C_CB, GROUP_W)] * conv3).astype(mix_ref.dtype)
    ext3_scr[pl.ds(0, HIST3), :] = ext3_scr[pl.ds(CHUNK, HIST3), :]

    vn = _layernorm_rows(z_ref[rows, pl.ds(C_DV, GROUP_W)], slng_ref[...], slnb_ref[...])
    row = lax.broadcasted_iota(jnp.int32, (CHUNK, CHUNK), 0)
    col = lax.broadcasted_iota(jnp.int32, (CHUNK, CHUNK), 1)
    ch = GROUP_W // SGU_GROUPS
    for gi in range(SGU_GROUPS):
        wm = jnp.where(row >= col, sguw_ref[gi], 0.0).astype(BF16)
        mixed = jnp.dot(wm, vn[:, gi * ch:(gi + 1) * ch].astype(BF16), preferred_element_type=F32)
        mixed = mixed + sgub_ref[:, pl.ds(gi, 1)]
        du = z_ref[rows, pl.ds(C_DU + gi * ch, ch)]
        mix_ref[rows, pl.ds(3 * GROUP_W + gi * ch, ch)] = (du * mixed).astype(mix_ref.dtype)


def _mixer_prompt(z, p):
    cos, sin = _rotary_tables(SEQ, 0)
    d_in, d_q, d_k, d_c = _decay_tables(CHUNK)
    dq_t = jnp.broadcast_to(d_q[:, :, None], (RET_HEADS, CHUNK, RET_D))
    dk_t = jnp.broadcast_to(d_k[:, :, None], (RET_HEADS, CHUNK, RET_D))
    dc_t = jnp.broadcast_to(d_c[:, None, None], (RET_HEADS, RET_D, RET_D))
    step_rows = PROMPT_CHUNKS_PER_STEP * CHUNK
    n_steps = SEQ // step_rows

    def full(shape):
        return pl.BlockSpec(shape, lambda b, c: (0,) * len(shape))

    row = lambda a: a.reshape(1, -1)
    return pl.pallas_call(
        _mixer_prompt_kernel,
        grid=(BATCH, n_steps),
        in_specs=[
            pl.BlockSpec((step_rows, D_IN), lambda b, c: (b * n_steps + c, 0)),
            pl.BlockSpec((step_rows, RET_D), lambda b, c: (c, 0)),
            pl.BlockSpec((step_rows, RET_D), lambda b, c: (c, 0)),
            full((RET_HEADS, CHUNK, CHUNK)),
            full((RET_HEADS, CHUNK, RET_D)),
            full((RET_HEADS, CHUNK, RET_D)),
            full((RET_HEADS, RET_D, RET_D)),
            full((1, GROUP_W)),
            full((CONV_W, GROUP_W)),
            full((1, GROUP_W)),
            full((1, GROUP_W)),
            full((1, GROUP_W)),
            full((SC_W, GROUP_W)),
            full((1, GROUP_W)),
            full((1, GROUP_W)),
            full((SGU_GROUPS, CHUNK, CHUNK)),
            full((CHUNK, SGU_GROUPS)),
        ],
        out_specs=[
            pl.BlockSpec((step_rows, D_MODEL), lambda b, c: (b * n_steps + c, 0)),
            pl.BlockSpec((None, RET_HEADS, RET_D, RET_D), lambda b, c: (b, 0, 0, 0)),
            pl.BlockSpec((None, CONV_W - 1, GROUP_W), lambda b, c: (b, 0, 0)),
            pl.BlockSpec((None, SC_W - 1, GROUP_W), lambda b, c: (b, 0, 0)),
        ],
        out_shape=[
            jax.ShapeDtypeStruct((T_PROMPT, D_MODEL), BF16),
            jax.ShapeDtypeStruct((BATCH, RET_HEADS, RET_D, RET_D), F32),
            jax.ShapeDtypeStruct((BATCH, CONV_W - 1, GROUP_W), F32),
            jax.ShapeDtypeStruct((BATCH, SC_W - 1, GROUP_W), F32),
        ],
        scratch_shapes=[
            pltpu.VMEM((RET_HEADS, RET_D, RET_D), F32),
            pltpu.VMEM((HIST31 + CHUNK, GROUP_W), F32),
            pltpu.VMEM((HIST3 + CHUNK, GROUP_W), F32),
            pltpu.VMEM((SUBLANES - 1, CHUNK + SUBLANES, GROUP_W), F32),
        ],
        compiler_params=pltpu.CompilerParams(
            dimension_semantics=("arbitrary", "arbitrary"), vmem_limit_bytes=V7X_VMEM_LIMIT),
        name="mixer_prompt",
    )(z, cos, sin, d_in, dq_t, dk_t, dc_t, row(p["ret_norm_g"]), p["conv31_w"], row(p["conv31_b"]),
      row(p["conv_ln_g"]), row(p["conv_ln_b"]), p["sconv_w"], row(p["sgu_ln_g"]), row(p["sgu_ln_b"]),
      p["sgu_w"], p["sgu_b"].T)


EXT31_S = 40
EXT3_S = 8


N_SAMPLE_IN = 20


def _mixer_sample_stacked_kernel(*refs):
    ins = refs[:N_SAMPLE_IN]
    prev_ret, prev_b31 = refs[N_SAMPLE_IN:N_SAMPLE_IN + 2]
    mix_ref, ret_hbm, b31_hbm, b3_ref, vn_ref = refs[N_SAMPLE_IN + 2:N_SAMPLE_IN + 7]
    ext31_scr, ext3_scr, ret_scr, b31_scr, wr_sem, cp_sem = refs[N_SAMPLE_IN + 7:]
    i = pl.program_id(0)
    n = pl.num_programs(0)
    slot = i % 2
    bt = ret_scr.shape[1]
    blk = pl.ds(pl.multiple_of(i * bt, bt), bt)

    def writes(sl):
        return (pltpu.make_async_copy(ret_scr.at[sl], ret_hbm.at[1, blk], wr_sem.at[sl, 0]),
                pltpu.make_async_copy(b31_scr.at[sl], b31_hbm.at[1, blk], wr_sem.at[sl, 1]))

    copies = (pltpu.make_async_copy(prev_ret, ret_hbm.at[0, blk], cp_sem.at[0]),
              pltpu.make_async_copy(prev_b31, b31_hbm.at[0, blk], cp_sem.at[1]))

    @pl.when(i >= 2)
    def _():
        for w in writes(slot):
            w.wait()

    for c in copies:
        c.start()
    _mixer_sample_kernel(*ins, mix_ref, ret_scr.at[slot], b31_scr.at[slot], b3_ref, vn_ref, ext31_scr, ext3_scr)
    ws = writes(slot)
    for w in ws:
        w.start()
    for c in copies:
        c.wait()

    @pl.when(i == n - 1)
    def _():
        for w in ws:
            w.wait()
        for w in writes(1 - slot):
            w.wait()


def _mixer_sample_kernel(z_ref, cos_ref, sin_ref, din_ref, dq_ref, dk_ref, dc_ref, retg_ref,
                         c31w_ref, c31b_ref, clng_ref, clnb_ref, scw_ref, slng_ref, slnb_ref,
                         sguw_ref, sgub_ref, sret_ref, s31_ref, s3_ref,
                         mix_ref, ret_ref, b31_ref, b3_ref, vn_ref,
                         ext31_scr, ext3_scr):
    cos = cos_ref[...]
    sin = sin_ref[...]
    n_hist31 = CONV_W - 1
    n_hist3 = SC_W - 1
    bt = sret_ref.shape[0]

    def zs(col, width):
        return jnp.stack([z_ref[pl.ds(DEC_SEQ * b, DEC_SEQ), pl.ds(col, width)] for b in range(bt)])

    def zrow(s, col, width):
        return jnp.stack([z_ref[pl.ds(DEC_SEQ * b + s, 1), pl.ds(col, width)] for b in range(bt)])

    def put(col, val):
        for b in range(bt):
            mix_ref[pl.ds(DEC_SEQ * b, DEC_SEQ), pl.ds(col, val.shape[-1])] = val[b]

    for h in range(RET_HEADS):
        lo = h * RET_D
        q = zs(C_Q + lo, RET_D)
        k = zs(C_K + lo, RET_D)
        g = zs(C_G + lo, RET_D)
        q = q * cos + pltpu.roll(q, RET_D // 2, axis=2) * sin
        k = (k * cos + pltpu.roll(k, RET_D // 2, axis=2) * sin) * (RET_D ** -0.5)
        s_old = sret_ref[:, h]
        r = lax.dot_general(q, s_old, (((2,), (1,)), ((0,), (0,))),
                            preferred_element_type=F32) * dq_ref[h]
        for s in range(DEC_SEQ):
            k_s = zrow(s, C_K + lo, RET_D)
            k_s = (k_s * cos[s:s + 1] + pltpu.roll(k_s, RET_D // 2, axis=2) * sin[s:s + 1]) * (RET_D ** -0.5)
            v_s = zrow(s, C_V + lo, RET_D)
            score = jnp.sum(q * k_s, axis=-1, keepdims=True) * din_ref[h, s]
            r = r + score * v_s
        kd = k * dk_ref[h]
        v = zs(C_V + lo, RET_D)
        ret_ref[:, h] = s_old * dc_ref[h] + lax.dot_general(
            kd, v, (((1,), (1,)), ((0,), (0,))), preferred_element_type=F32)
        r = r * lax.rsqrt(jnp.mean(r * r, axis=-1, keepdims=True) + EPS)
        r = r * retg_ref[:, pl.ds(lo, RET_D)]
        put(lo, _silu(g) * r)

    glu = zs(C_BL, GROUP_W) * jax.nn.sigmoid(zs(C_BG, GROUP_W))
    ext31_scr[:, pl.ds(0, n_hist31), :] = s31_ref[...]
    ext31_scr[:, pl.ds(n_hist31, DEC_SEQ), :] = glu
    conv = ext31_scr[:, pl.ds(0, DEC_SEQ), :] * c31w_ref[pl.ds(0, 1), :]
    for j in range(1, CONV_W):
        conv = conv + ext31_scr[:, pl.ds(j, DEC_SEQ), :] * c31w_ref[pl.ds(j, 1), :]
    conv = conv + c31b_ref[...]
    put(GROUP_W, _silu(_layernorm_rows(conv, clng_ref[...], clnb_ref[...])))
    b31_ref[...] = ext31_scr[:, pl.ds(DEC_SEQ, n_hist31), :]

    gated = zs(C_CC, GROUP_W) * zs(C_CH, GROUP_W)
    ext3_scr[:, pl.ds(0, n_hist3), :] = s3_ref[...]
    ext3_scr[:, pl.ds(n_hist3, DEC_SEQ), :] = gated
    conv3 = ext3_scr[:, pl.ds(0, DEC_SEQ), :] * scw_ref[pl.ds(0, 1), :]
    for j in range(1, SC_W):
        conv3 = conv3 + ext3_scr[:, pl.ds(j, DEC_SEQ), :] * scw_ref[pl.ds(j, 1), :]
    put(2 * GROUP_W, zs(C_CB, GROUP_W) * conv3)
    b3_ref[...] = ext3_scr[:, pl.ds(DEC_SEQ, n_hist3), :]

    vn = _layernorm_rows(zs(C_DV, GROUP_W), slng_ref[...], slnb_ref[...])
    vn_ref[...] = vn
    mixed = sgub_ref[...]
    for s in range(DEC_SEQ):
        mixed = mixed + sguw_ref[s] * vn_ref[:, pl.ds(s, 1), :]
    put(3 * GROUP_W, zs(C_DU, GROUP_W) * mixed)


def _mixer_sample(z, p, layer, s_ret, s_31, s_3, prev=None):
    cos, sin = _rotary_tables(DEC_SEQ, PAST_LEN)
    d_in, d_q, d_k, d_c = _decay_tables(DEC_SEQ)
    din_t = jnp.swapaxes(d_in, 1, 2)[:, :, :, None]
    dq_t = jnp.broadcast_to(d_q[:, :, None], (RET_HEADS, DEC_SEQ, RET_D))
    dk_t = jnp.broadcast_to(d_k[:, :, None], (RET_HEADS, DEC_SEQ, RET_D))
    dc_t = jnp.broadcast_to(d_c[:, None, None], (RET_HEADS, RET_D, RET_D))
    ch = GROUP_W // SGU_GROUPS
    w4 = jnp.tril(p["sgu_w"][:, :DEC_SEQ, :DEC_SEQ])
    w_t = jnp.repeat(jnp.transpose(w4, (2, 1, 0)), ch, axis=-1)
    b_t = jnp.repeat(p["sgu_b"][:, :DEC_SEQ].T, ch, axis=-1)
    bt = SAMPLE_BT

    def full(shape):
        return pl.BlockSpec(shape, lambda i: (0,) * len(shape))

    row = lambda a: a.reshape(1, -1)
    any_spec = pl.BlockSpec(memory_space=pl.ANY)
    ret_blk, b31_blk = (bt, RET_HEADS, RET_D, RET_D), (bt, CONV_W - 1, GROUP_W)
    ret_all, b31_all = (DEC_BATCH,) + ret_blk[1:], (DEC_BATCH,) + b31_blk[1:]
    stacked = prev is not None
    return pl.pallas_call(
        _mixer_sample_stacked_kernel if stacked else _mixer_sample_kernel,
        grid=(DEC_BATCH // bt,),
        in_specs=[
            pl.BlockSpec((bt * DEC_SEQ, D_IN), lambda i: (T_PROMPT // (bt * DEC_SEQ) + i, 0)),
            full((DEC_SEQ, RET_D)),
            full((DEC_SEQ, RET_D)),
            full((RET_HEADS, DEC_SEQ, DEC_SEQ, 1)),
            full((RET_HEADS, DEC_SEQ, RET_D)),
            full((RET_HEADS, DEC_SEQ, RET_D)),
            full((RET_HEADS, RET_D, RET_D)),
            full((1, GROUP_W)),
            full((CONV_W, GROUP_W)),
            full((1, GROUP_W)),
            full((1, GROUP_W)),
            full((1, GROUP_W)),
            full((SC_W, GROUP_W)),
            full((1, GROUP_W)),
            full((1, GROUP_W)),
            full((DEC_SEQ, DEC_SEQ, GROUP_W)),
            full((DEC_SEQ, GROUP_W)),
            pl.BlockSpec((None, bt, RET_HEADS, RET_D, RET_D), lambda i: (layer, i, 0, 0, 0)),
            pl.BlockSpec((None, bt, CONV_W - 1, GROUP_W), lambda i: (layer, i, 0, 0)),
            pl.BlockSpec((None, bt, SC_W - 1, GROUP_W), lambda i: (layer, i, 0, 0)),
        ] + ([pl.BlockSpec(ret_blk, lambda i: (i, 0, 0, 0)), pl.BlockSpec(b31_blk, lambda i: (i, 0, 0))]
             if stacked else []),
        out_specs=[
            pl.BlockSpec((bt * DEC_SEQ, D_MODEL), lambda i: (i, 0)),
            any_spec if stacked else pl.BlockSpec(ret_blk, lambda i: (i, 0, 0, 0)),
            any_spec if stacked else pl.BlockSpec(b31_blk, lambda i: (i, 0, 0)),
            pl.BlockSpec((bt, SC_W - 1, GROUP_W), lambda i: (i, 0, 0)),
            pl.BlockSpec((bt, DEC_SEQ, GROUP_W), lambda i: (i, 0, 0)),
        ],
        out_shape=[
            jax.ShapeDtypeStruct((T_SAMPLE, D_MODEL), F32),
            jax.ShapeDtypeStruct(((DEPTH,) if stacked else ()) + ret_all, F32),
            jax.ShapeDtypeStruct(((DEPTH,) if stacked else ()) + b31_all, F32),
            jax.ShapeDtypeStruct((DEC_BATCH, SC_W - 1, GROUP_W), F32),
            jax.ShapeDtypeStruct((DEC_BATCH, DEC_SEQ, GROUP_W), F32),
        ],
        scratch_shapes=[
            pltpu.VMEM((bt, EXT31_S, GROUP_W), F32),
            pltpu.VMEM((bt, EXT3_S, GROUP_W), F32),
        ] + ([
            pltpu.VMEM((2,) + ret_blk, F32),
            pltpu.VMEM((2,) + b31_blk, F32),
            pltpu.SemaphoreType.DMA((2, 2)),
            pltpu.SemaphoreType.DMA((2,)),
        ] if stacked else []),
        compiler_params=pltpu.CompilerParams(
            dimension_semantics=("arbitrary",), vmem_limit_bytes=V7X_VMEM_LIMIT),
        name="mixer_sample",
    )(z, cos, sin, din_t, dq_t, dk_t, dc_t, row(p["ret_norm_g"]), p["conv31_w"], row(p["conv31_b"]),
      row(p["conv_ln_g"]), row(p["conv_ln_b"]), p["sconv_w"], row(p["sgu_ln_g"]), row(p["sgu_ln_b"]),
      w_t, b_t, s_ret, s_31, s_3, *(prev if stacked else ()))


def _to_token_major(dst, val, pitch=TOK_ROWS):
    n = val.shape[0]
    for a in range(TOK_ROWS):
        dst[pl.ds(a, n, stride=pitch), :] = val[:, a * LANES:(a + 1) * LANES]


STAGE_PITCH = 24
DMA_UNROLL = 8


def _router_kernel(x_ref, g_ref, rw_ref, e_ref, p_ref, c_ref):
    h = _rms_rows(x_ref[...], g_ref[...])
    lane = lax.broadcasted_iota(jnp.int32, (h.shape[0], N_EXPERTS), 1)
    logits = jnp.zeros((h.shape[0], N_EXPERTS), F32)
    for e in range(N_EXPERTS):
        col = jnp.sum(h * rw_ref[pl.ds(e, 1), :], axis=-1, keepdims=True)
        logits = jnp.where(lane == e, col, logits)
    m1 = jnp.max(logits, axis=-1, keepdims=True)
    i1 = jnp.min(jnp.where(logits == m1, lane, N_EXPERTS), axis=-1, keepdims=True)
    rest = jnp.where(lane == i1, -jnp.inf, logits)
    m2 = jnp.max(rest, axis=-1, keepdims=True)
    i2 = jnp.min(jnp.where(rest == m2, lane, N_EXPERTS), axis=-1, keepdims=True)
    e2 = jnp.exp(m2 - m1)
    den = 1.0 + e2
    e_ref[...] = jnp.where(lane == 0, i1, jnp.where(lane == 1, i2, 0))
    p_ref[...] = jnp.where(lane == 0, 1.0 / den, jnp.where(lane == 1, e2 / den, 0.0))
    c_ref[pl.ds(0, 1), :] = jnp.sum(jnp.where(lane == i1, 1.0, 0.0), axis=0, keepdims=True).astype(jnp.int32)
    c_ref[pl.ds(1, 1), :] = jnp.sum(jnp.where(lane == i2, 1.0, 0.0), axis=0, keepdims=True).astype(jnp.int32)


def _router(x, g, rw):
    t, d = x.shape
    return pl.pallas_call(
        _router_kernel,
        grid=(N_TB,),
        in_specs=[
            pl.BlockSpec((MOE_TB, d), lambda i: (i, 0)),
            pl.BlockSpec((1, d), lambda i: (0, 0)),
            pl.BlockSpec((N_EXPERTS, d), lambda i: (0, 0)),
        ],
        out_specs=[
            pl.BlockSpec((MOE_TB, N_EXPERTS), lambda i: (i, 0)),
            pl.BlockSpec((MOE_TB, N_EXPERTS), lambda i: (i, 0)),
            pl.BlockSpec((None, TOP_K, N_EXPERTS), lambda i: (i, 0, 0)),
        ],
        out_shape=[
            jax.ShapeDtypeStruct((t, N_EXPERTS), jnp.int32),
            jax.ShapeDtypeStruct((t, N_EXPERTS), F32),
            jax.ShapeDtypeStruct((N_TB, TOP_K, N_EXPERTS), jnp.int32),
        ],
        compiler_params=pltpu.CompilerParams(
            dimension_semantics=("arbitrary",), vmem_limit_bytes=V7X_VMEM_LIMIT),
        name="moe_router",
    )(x, g.reshape(1, d), rw.T)


def _pick(table, idx):
    lanes = jnp.arange(N_EXPERTS, dtype=jnp.int32)
    return jnp.sum(jnp.where(idx[..., None] == lanes, table, 0), axis=-1)


def _expert_layout(cnt):
    tot = jnp.sum(cnt, axis=(0, 1))
    start = jnp.cumsum(tot) - tot
    blk = jnp.sum(cnt, axis=1)
    blk_base = start[None, :] + jnp.cumsum(blk, axis=0) - blk
    base = jnp.stack([blk_base, blk_base + cnt[:, 0]], axis=1)
    n_sb = (tot + SB - 1) // SB
    sb_end = jnp.cumsum(n_sb)
    n_used = sb_end[-1]
    s_idx = jnp.arange(NS_MAX, dtype=jnp.int32)
    s_cl = jnp.minimum(s_idx, n_used - 1)
    sb_e = jnp.sum(s_cl[:, None] >= sb_end[None, :], axis=-1).astype(jnp.int32)
    local = s_cl - _pick(sb_end - n_sb, sb_e)
    sb_row0 = _pick(start, sb_e) + local * SB
    sb_rows = jnp.where(s_idx < n_used, jnp.clip(_pick(tot, sb_e) - local * SB, 0, SB), 0)
    i32 = lambda a: a.astype(jnp.int32)
    return i32(base), sb_e, i32(sb_row0), i32(sb_rows), i32(n_used).reshape(1)


def _dest_kernel(e_ref, b_ref, d_ref):
    eid = e_ref[...]
    lane = lax.broadcasted_iota(jnp.int32, eid.shape, 1)
    row = lax.broadcasted_iota(jnp.int32, (MOE_TB, MOE_TB), 0)
    col = lax.broadcasted_iota(jnp.int32, (MOE_TB, MOE_TB), 1)
    earlier = jnp.where(row > col, 1.0, 0.0).astype(BF16)
    base = b_ref[...].astype(F32)

    def dest_of(k):
        mine = lane == eid[:, k:k + 1]
        before = jnp.dot(earlier, jnp.where(mine, 1.0, 0.0).astype(BF16), preferred_element_type=F32)
        return jnp.sum(jnp.where(mine, before + base[k:k + 1], 0.0), axis=-1, keepdims=True)

    d_ref[...] = jnp.where(lane == 0, dest_of(0), jnp.where(lane == 1, dest_of(1), 0.0)).astype(jnp.int32)


def _dest_rows(eid, base):
    return pl.pallas_call(
        _dest_kernel,
        grid=(N_TB,),
        in_specs=[
            pl.BlockSpec((MOE_TB, N_EXPERTS), lambda i: (i, 0)),
            pl.BlockSpec((None, TOP_K, N_EXPERTS), lambda i: (i, 0, 0)),
        ],
        out_specs=pl.BlockSpec((MOE_TB, N_EXPERTS), lambda i: (i, 0)),
        out_shape=jax.ShapeDtypeStruct((T_ALL, N_EXPERTS), jnp.int32),
        compiler_params=pltpu.CompilerParams(
            dimension_semantics=("arbitrary",), vmem_limit_bytes=V7X_VMEM_LIMIT),
        name="moe_dest_rows",
    )(eid, base)


def _dispatch_kernel(d_ref, x_ref, g_ref, xg_hbm, stage, zeros, sem, zsem):
    i = pl.program_id(0)
    n = pl.num_programs(0)
    slot = i % 2
    blk_rows = MOE_TB * TOK_ROWS

    def wait_slot(sl):
        for _ in range(TOP_K):
            pltpu.make_async_copy(stage.at[sl, pl.ds(0, blk_rows)], xg_hbm.at[pl.ds(0, blk_rows)], sem.at[sl]).wait()

    @pl.when(i >= 2)
    def _():
        wait_slot(slot)

    _to_token_major(stage.at[slot], _rms_rows(x_ref[...], g_ref[...]), STAGE_PITCH)

    def body(r, carry):
        src = stage.at[slot, pl.ds(pl.multiple_of(r * STAGE_PITCH, 8), TOK_ROWS)]
        for k in range(TOP_K):
            row = d_ref[0, 0, TOP_K * r + k]
            dst = xg_hbm.at[pl.ds(pl.multiple_of(row * TOK_ROWS, TOK_ROWS), TOK_ROWS)]
            pltpu.make_async_copy(src, dst, sem.at[slot]).start()
        return carry

    lax.fori_loop(0, MOE_TB, body, 0, unroll=DMA_UNROLL)

    @pl.when(i == n - 1)
    def _():
        zeros[...] = jnp.zeros_like(zeros)
        tail = [pltpu.make_async_copy(zeros, xg_hbm.at[pl.ds((N_SLOTS + q * (SUB // TOK_ROWS)) * TOK_ROWS, SUB)], zsem)
                for q in range(TOK_ROWS)]
        for cp in tail:
            cp.start()
        for cp in tail:
            cp.wait()
        wait_slot(slot)
        wait_slot(1 - slot)


def _dispatch(x, g, dest_flat):
    t, d = x.shape
    return pl.pallas_call(
        _dispatch_kernel,
        grid=(N_TB,),
        in_specs=[
            pl.BlockSpec((1, 1, TOP_K * MOE_TB), lambda i: (i, 0, 0), memory_space=pltpu.SMEM),
            pl.BlockSpec((MOE_TB, d), lambda i: (i, 0)),
            pl.BlockSpec((1, d), lambda i: (0, 0)),
        ],
        out_specs=pl.BlockSpec(memory_space=pl.ANY),
        out_shape=jax.ShapeDtypeStruct(((N_SLOTS + SUB) * TOK_ROWS, LANES), F32),
        scratch_shapes=[
            pltpu.VMEM((2, MOE_TB * STAGE_PITCH, LANES), F32),
            pltpu.VMEM((SUB, LANES), F32),
            pltpu.SemaphoreType.DMA((2,)),
            pltpu.SemaphoreType.DMA(()),
        ],
        compiler_params=pltpu.CompilerParams(
            dimension_semantics=("arbitrary",), vmem_limit_bytes=V7X_VMEM_LIMIT),
        name="moe_dispatch",
    )(dest_flat, x, g.reshape(1, d))


def _moe_ffn_kernel(se_ref, r0_ref, nr_ref, nu_ref, xg_hbm, w1_ref, w3_ref, w2_ref, y_hbm,
                    x_scr, acc, stage, w1b, w3b, w2b, sem_in, sem_out):
    s = pl.program_id(0)
    j = pl.program_id(1)
    nj = pl.num_programs(1)
    rows = nr_ref[s]
    row0 = r0_ref[s]
    nsub = lax.shift_right_logical(rows + (SUB - 1), int(math.log2(SUB)))
    sub_rows = SUB * TOK_ROWS

    is_middle = (j > 0) & (j < nj - 1)

    def cast_weights():
        ws = (w1_ref[...].astype(BF16), w3_ref[...].astype(BF16), w2_ref[...].astype(BF16))
        w1b[...], w3b[...], w2b[...] = ws
        return ws

    @pl.when((nsub > 0) & jnp.logical_not(is_middle))
    def _():
        cast_weights()

    def sub(r):
        return pl.ds(pl.multiple_of(r * SUB, SUB), SUB)

    def swiglu(xs, ws=None):
        w1, w3, w2 = ws if ws is not None else (w1b[...], w3b[...], w2b[...])
        a = jnp.dot(xs, w1, preferred_element_type=F32)
        b = jnp.dot(xs, w3, preferred_element_type=F32)
        u = (_silu(a) * b).astype(BF16)
        return jnp.dot(u, w2, preferred_element_type=F32)

    def tile(r):
        return swiglu(x_scr[sub(r), :])

    def in_copy(r, slot):
        first = pl.multiple_of((row0 + r * SUB) * TOK_ROWS, TOK_ROWS)
        return pltpu.make_async_copy(xg_hbm.at[pl.ds(first, sub_rows)], stage.at[slot], sem_in.at[slot])

    @pl.when(j == 0)
    def _():
        @pl.when(nsub > 0)
        def _():
            in_copy(0, 0).start()

        def body(r, carry):
            slot = r % 2
            in_copy(r, slot).wait()

            @pl.when(r + 1 < nsub)
            def _():
                in_copy(r + 1, 1 - slot).start()

            for a in range(TOK_ROWS):
                x_scr[sub(r), pl.ds(a * LANES, LANES)] = stage[slot, pl.ds(a, SUB, stride=TOK_ROWS), :].astype(BF16)
            acc[sub(r), :] = tile(r)
            return carry

        lax.fori_loop(0, nsub, body, 0)

        def clear(r, carry):
            x_scr[sub(r), :] = jnp.zeros((SUB, D_MODEL), BF16)
            acc[sub(r), :] = jnp.zeros((SUB, D_MODEL), F32)
            return carry

        lax.fori_loop(nsub, SB // SUB, clear, 0)

    @pl.when(is_middle)
    def _():
        for c in range(SB // CHUNK_ROWS):
            @pl.when(rows > c * CHUNK_ROWS)
            def _():
                ws = cast_weights() if c == 0 else None
                rows_c = pl.ds(c * CHUNK_ROWS, CHUNK_ROWS)
                acc[rows_c, :] += swiglu(x_scr[rows_c, :], ws)

    @pl.when(j == nj - 1)
    def _():
        def out_copies(r, act):
            nv = jnp.clip(rows - r * SUB, 0, SUB)
            bit = SUB
            while bit >= 1:
                off = nv & (-2 * bit)
                src = stage.at[0, pl.ds(pl.multiple_of(off * TOK_ROWS, TOK_ROWS), bit * TOK_ROWS)]
                first = pl.multiple_of((row0 + r * SUB + off) * TOK_ROWS, TOK_ROWS)
                cp = pltpu.make_async_copy(src, y_hbm.at[pl.ds(first, bit * TOK_ROWS)], sem_out)
                pl.when((nv & bit) != 0)(functools.partial(act, cp))
                bit //= 2

        def body(r, carry):
            y = acc[sub(r), :] + tile(r)

            @pl.when(r > 0)
            def _():
                out_copies(r - 1, lambda cp: cp.wait())

            _to_token_major(stage.at[0], y)
            out_copies(r, lambda cp: cp.start())
            return carry

        lax.fori_loop(0, nsub, body, 0)

        @pl.when(nsub > 0)
        def _():
            out_copies(nsub - 1, lambda cp: cp.wait())


def _moe_ffn(xg, w1, w3, w2, sb_e, sb_row0, sb_rows, n_used):
    d = w1.shape[1]
    f = w1.shape[2]
    nj = f // TF

    def jmap(s, j, nu):
        return jnp.where(s < nu[0], j, nj - 1)

    return pl.pallas_call(
        _moe_ffn_kernel,
        grid_spec=pltpu.PrefetchScalarGridSpec(
            num_scalar_prefetch=4,
            grid=(NS_MAX, nj),
            in_specs=[
                pl.BlockSpec(memory_space=pl.ANY),
                pl.BlockSpec((None, d, TF), lambda s, j, se, r0, nr, nu: (se[s], 0, jmap(s, j, nu))),
                pl.BlockSpec((None, d, TF), lambda s, j, se, r0, nr, nu: (se[s], 0, jmap(s, j, nu))),
                pl.BlockSpec((None, TF, d), lambda s, j, se, r0, nr, nu: (se[s], jmap(s, j, nu), 0)),
            ],
            out_specs=pl.BlockSpec(memory_space=pl.ANY),
            scratch_shapes=[
                pltpu.VMEM((SB, d), BF16),
                pltpu.VMEM((SB, d), F32),
                pltpu.VMEM((2, SUB * TOK_ROWS, LANES), F32),
                pltpu.VMEM((d, TF), BF16),
                pltpu.VMEM((d, TF), BF16),
                pltpu.VMEM((TF, d), BF16),
                pltpu.SemaphoreType.DMA((2,)),
                pltpu.SemaphoreType.DMA(()),
            ],
        ),
        out_shape=jax.ShapeDtypeStruct((N_SLOTS * TOK_ROWS, LANES), F32),
        compiler_params=pltpu.CompilerParams(
            dimension_semantics=("arbitrary", "arbitrary"), vmem_limit_bytes=V7X_VMEM_LIMIT),
        name="moe_ffn",
    )(sb_e, sb_row0, sb_rows, n_used, xg, w1, w3, w2)


N_TB_PROMPT = T_PROMPT // MOE_TB


def _combine_kernel(d_ref, dn_ref, x_ref, p_ref, g_ref, y_hbm, op_ref, os_ref, stage, sem, *, final_norm):
    i = pl.program_id(0)
    n = pl.num_programs(0)
    slot = i % 2
    blk_rows = MOE_TB * TOK_ROWS

    def start_block(dref, sl):
        def body(r, carry):
            for k in range(TOP_K):
                row = dref[0, 0, TOP_K * r + k]
                src = y_hbm.at[pl.ds(pl.multiple_of(row * TOK_ROWS, TOK_ROWS), TOK_ROWS)]
                dst = stage.at[sl, k, pl.ds(pl.multiple_of(r * STAGE_PITCH, 8), TOK_ROWS)]
                pltpu.make_async_copy(src, dst, sem.at[sl]).start()
            return carry

        lax.fori_loop(0, MOE_TB, body, 0, unroll=DMA_UNROLL)

    @pl.when(i == 0)
    def _():
        start_block(d_ref, 0)

    @pl.when(i + 1 < n)
    def _():
        start_block(dn_ref, 1 - slot)

    for k in range(TOP_K):
        pltpu.make_async_copy(y_hbm.at[pl.ds(0, blk_rows)], stage.at[slot, k, pl.ds(0, blk_rows)], sem.at[slot]).wait()
    g0 = p_ref[:, 0:1]
    g1 = p_ref[:, 1:2]
    pieces = []
    for a in range(TOK_ROWS):
        y0 = stage[slot, 0, pl.ds(a, MOE_TB, stride=STAGE_PITCH), :]
        y1 = stage[slot, 1, pl.ds(a, MOE_TB, stride=STAGE_PITCH), :]
        pieces.append(x_ref[:, pl.ds(a * LANES, LANES)] + (y0 * g0 + y1 * g1))
    y = jnp.concatenate(pieces, axis=1)
    if final_norm:
        y = _rms_rows(y, g_ref[...])

    @pl.when(i < N_TB_PROMPT)
    def _():
        op_ref[...] = y

    @pl.when(i >= N_TB_PROMPT)
    def _():
        os_ref[...] = y


def _moe_combine(x, y, dest_flat, gate, g, final_norm):
    t, d = x.shape
    return pl.pallas_call(
        functools.partial(_combine_kernel, final_norm=final_norm),
        grid=(N_TB,),
        in_specs=[
            pl.BlockSpec((1, 1, TOP_K * MOE_TB), lambda i: (i, 0, 0), memory_space=pltpu.SMEM),
            pl.BlockSpec((1, 1, TOP_K * MOE_TB), lambda i: (jnp.minimum(i + 1, N_TB - 1), 0, 0),
                         memory_space=pltpu.SMEM),
            pl.BlockSpec((MOE_TB, d), lambda i: (i, 0)),
            pl.BlockSpec((MOE_TB, N_EXPERTS), lambda i: (i, 0)),
            pl.BlockSpec((1, d), lambda i: (0, 0)),
            pl.BlockSpec(memory_space=pl.ANY),
        ],
        out_specs=[
            pl.BlockSpec((MOE_TB, d), lambda i: (jnp.minimum(i, N_TB_PROMPT - 1), 0)),
            pl.BlockSpec((MOE_TB, d), lambda i: (jnp.maximum(i - N_TB_PROMPT, 0), 0)),
        ],
        out_shape=[
            jax.ShapeDtypeStruct((T_PROMPT, d), F32),
            jax.ShapeDtypeStruct((T_SAMPLE, d), F32),
        ],
        scratch_shapes=[
            pltpu.VMEM((2, TOP_K, MOE_TB * STAGE_PITCH, LANES), F32),
            pltpu.SemaphoreType.DMA((2,)),
        ],
        compiler_params=pltpu.CompilerParams(
            dimension_semantics=("arbitrary",), vmem_limit_bytes=V7X_VMEM_LIMIT),
        name="moe_combine",
    )(dest_flat, dest_flat, x, gate, g.reshape(1, d), y)


def _moe(x, norm_g, rw, w1, w3, w2, out_g, final_norm):
    eid, gate, cnt = _router(x, norm_g, rw)
    base, sb_e, sb_row0, sb_rows, n_used = _expert_layout(cnt)
    dest = _dest_rows(eid, base)
    dest_flat = dest[:, :TOP_K].reshape(N_TB, 1, TOP_K * MOE_TB)
    xg = _dispatch(x, norm_g, dest_flat)
    y = _moe_ffn(xg, w1, w3, w2, sb_e, sb_row0, sb_rows, n_used)
    return _moe_combine(x, y, dest_flat, gate, out_g, final_norm)


def kernel(x_prompt, x_sample, state_ret, state_conv31, state_conv3, mix_norm_g, w_in, ret_norm_g, conv31_w, conv31_b, conv_ln_g, conv_ln_b, sconv_w, sgu_ln_g, sgu_ln_b, sgu_w, sgu_b, w_out, ffn_norm_g, dense_w1, dense_w3, dense_w2, router_w, moe_w1, moe_w3, moe_w2, final_norm_g):
    assert DEPTH == 2, "layer 0 has the dense channel mixer; layer 1's expert mixer ends with the final norm"
    xp2, xs2 = x_prompt.reshape(T_PROMPT, D_MODEL), x_sample.reshape(T_SAMPLE, D_MODEL)
    hn = _prep(xp2, xs2, mix_norm_g[0])
    res = (xp2, xs2, 0)
    rets_p, c31_p, c3_p, c3_s, vns = [], [], [], [], []
    ret_s = b31_s = None
    for l in range(DEPTH):
        p = dict(ret_norm_g=ret_norm_g[l], conv31_w=conv31_w[l], conv31_b=conv31_b[l],
                 conv_ln_g=conv_ln_g[l], conv_ln_b=conv_ln_b[l], sconv_w=sconv_w[l],
                 sgu_ln_g=sgu_ln_g[l], sgu_ln_b=sgu_ln_b[l], sgu_w=sgu_w[l], sgu_b=sgu_b[l])
        z = _in_proj(hn, w_in, l)
        mix_s, ret_s, b31_s, b3_s, vn_s = _mixer_sample(
            z, p, l, state_ret, state_conv31, state_conv3, prev=(ret_s, b31_s) if l == DEPTH - 1 else None)
        mix_p, ret_p, b31_p, b3_p = _mixer_prompt(z, p)
        x = _out_proj_residual(mix_p, mix_s, w_out, l, *res)
        j = l // 2
        if l % 2 == 0:
            x, hn = _ffn_dense(x, ffn_norm_g[l], dense_w1[j], dense_w3[j], dense_w2[j], mix_norm_g[l + 1])
            res = (x, x, N_OUT_PROMPT)
        else:
            y_p, y_s = _moe(x, ffn_norm_g[l], router_w[j], moe_w1[j], moe_w3[j], moe_w2[j],
                            final_norm_g, final_norm=True)
        rets_p.append(ret_p)
        c31_p.append(b31_p)
        c3_p.append(b3_p); c3_s.append(b3_s)
        vns.append(vn_s)
    y_prompt = y_p.reshape(BATCH, SEQ, D_MODEL)
    y_sample = y_s.reshape(DEC_BATCH, DEC_SEQ, D_MODEL)
    return (y_prompt, y_sample, jnp.stack(rets_p), ret_s, jnp.stack(c31_p), b31_s,
            jnp.stack(c3_p), jnp.stack(c3_s), jnp.stack(vns))
```

```python
import functools
import math

import jax
import jax.numpy as jnp
from jax import lax
from jax.experimental import pallas as pl
from jax.experimental.pallas import tpu as pltpu

F32 = jnp.float32
BF16 = jnp.bfloat16

D_MODEL = 2048
BATCH = 4
SEQ = 2048
DEPTH = 2
DEC_BATCH = 128
DEC_SEQ = 4
PAST_LEN = 16384
GROUP_W = 512
D_IN = 11 * GROUP_W
RET_HEADS = 4
RET_D = 128
CHUNK = 128
ROPE_THETA = 10000.0
CONV_W = 31
SC_W = 3
SGU_GROUPS = 4
D_FF = 5632
N_EXPERTS = 8
TOP_K = 2
EPS = 1e-6

T_PROMPT = BATCH * SEQ
T_SAMPLE = DEC_BATCH * DEC_SEQ
T_ALL = T_PROMPT + T_SAMPLE
N_CHUNKS = SEQ // CHUNK

C_Q, C_K, C_V, C_G, C_BL, C_BG, C_CB, C_CC, C_CH, C_DU, C_DV = (i * GROUP_W for i in range(11))

V7X_VMEM_LIMIT = 56 * 1024 * 1024

TM = 1088
TM_IN = 2176
TM_OUT = 512
N_OUT_PROMPT = T_PROMPT // TM_OUT
NORM_ROWS = 272
TN_IN = 512
TF = 256
SAMPLE_BT = 16

LANES = 128
SUBLANES = 8
TOK_ROWS = D_MODEL // LANES
MOE_TB = 256
N_TB = T_ALL // MOE_TB
N_SLOTS = T_ALL * TOP_K
SUB = 256
SB = 9 * SUB
CHUNK_ROWS = SB // 3
NS_MAX = N_SLOTS // SB + N_EXPERTS


def _rms_rows(x, g):
    ms = jnp.mean(x * x, axis=-1, keepdims=True)
    return (x * lax.rsqrt(ms + EPS)) * g


def _layernorm_rows(x, g, b):
    mu = jnp.mean(x, axis=-1, keepdims=True)
    xc = x - mu
    return xc * lax.rsqrt(jnp.mean(xc * xc, axis=-1, keepdims=True) + EPS) * g + b


def _silu(x):
    return x * jax.nn.sigmoid(x)


def _norm_block_to(x_ref, g_ref, dst_ref):
    rows = x_ref.shape[0]
    g = g_ref[...]

    def body(c, carry):
        r0 = pl.multiple_of(c * NORM_ROWS, NORM_ROWS)
        x = x_ref[pl.ds(r0, NORM_ROWS), :]
        dst_ref[pl.ds(r0, NORM_ROWS), :] = _rms_rows(x, g).astype(dst_ref.dtype)
        return carry

    lax.fori_loop(0, rows // NORM_ROWS, body, 0)


PREP_ROWS = 512
N_PREP_PROMPT = T_PROMPT // PREP_ROWS


def _prep_kernel(xp_ref, xs_ref, g_ref, hn_ref):
    def emit(src_ref):
        hn_ref[...] = _rms_rows(src_ref[...], g_ref[...]).astype(hn_ref.dtype)

    is_prompt = pl.program_id(0) < N_PREP_PROMPT
    pl.when(is_prompt)(functools.partial(emit, xp_ref))
    pl.when(jnp.logical_not(is_prompt))(functools.partial(emit, xs_ref))


def _prep(x_prompt, x_sample, g):
    d = x_prompt.shape[1]
    blk = lambda f: pl.BlockSpec((PREP_ROWS, d), f)
    return pl.pallas_call(
        _prep_kernel,
        grid=(T_ALL // PREP_ROWS,),
        in_specs=[
            blk(lambda i: (jnp.minimum(i, N_PREP_PROMPT - 1), 0)),
            blk(lambda i: (jnp.maximum(i - N_PREP_PROMPT, 0), 0)),
            pl.BlockSpec((1, d), lambda i: (0, 0)),
        ],
        out_specs=blk(lambda i: (i, 0)),
        out_shape=jax.ShapeDtypeStruct((T_ALL, d), BF16),
        compiler_params=pltpu.CompilerParams(
            dimension_semantics=("arbitrary",), vmem_limit_bytes=V7X_VMEM_LIMIT),
        name="prep_rows",
    )(x_prompt, x_sample, g.reshape(1, d))


def _in_proj_kernel(h_ref, w_ref, o_ref):
    o_ref[...] = jnp.dot(h_ref[...], w_ref[...].astype(BF16), preferred_element_type=F32)


def _in_proj(hn, w_all, layer):
    t, d = hn.shape
    n = w_all.shape[2]
    return pl.pallas_call(
        _in_proj_kernel,
        grid=(t // TM_IN, n // TN_IN),
        in_specs=[
            pl.BlockSpec((TM_IN, d), lambda i, j: (i, 0)),
            pl.BlockSpec((None, d, TN_IN), lambda i, j: (layer, 0, j)),
        ],
        out_specs=pl.BlockSpec((TM_IN, TN_IN), lambda i, j: (i, j)),
        out_shape=jax.ShapeDtypeStruct((t, n), F32),
        compiler_params=pltpu.CompilerParams(
            dimension_semantics=("arbitrary", "arbitrary"), vmem_limit_bytes=V7X_VMEM_LIMIT),
        name="in_proj",
    )(hn, w_all)


W_LOAD_ROWS = 512


def _out_proj_kernel(ap_ref, as_ref, w_hbm, rp_ref, rs_ref, o_ref, wb, stage, sem, *, layer):
    i = pl.program_id(0)

    @pl.when(i == 0)
    def _():
        for c in range(wb.shape[0] // W_LOAD_ROWS):
            rows = pl.ds(c * W_LOAD_ROWS, W_LOAD_ROWS)
            cp = pltpu.make_async_copy(w_hbm.at[layer, rows], stage, sem)
            cp.start()
            cp.wait()
            wb[rows, :] = stage[...].astype(BF16)

    def emit(a, r_ref):
        o_ref[...] = r_ref[...] + jnp.dot(a, wb[...], preferred_element_type=F32)

    pl.when(i < N_OUT_PROMPT)(lambda: emit(ap_ref[...], rp_ref))
    pl.when(i >= N_OUT_PROMPT)(lambda: emit(as_ref[...].astype(BF16), rs_ref))


def _out_proj_residual(a_prompt, a_sample, w_all, layer, r_prompt, r_sample, r_sample_block):
    k = a_prompt.shape[1]
    t = T_ALL
    n = w_all.shape[2]
    assert T_SAMPLE == TM_OUT and T_PROMPT % TM_OUT == 0
    return pl.pallas_call(
        functools.partial(_out_proj_kernel, layer=layer),
        grid=(t // TM_OUT,),
        in_specs=[
            pl.BlockSpec((TM_OUT, k), lambda i: (jnp.minimum(i, N_OUT_PROMPT - 1), 0)),
            pl.BlockSpec((TM_OUT, k), lambda i: (0, 0)),
            pl.BlockSpec(memory_space=pl.ANY),
            pl.BlockSpec((TM_OUT, n), lambda i: (jnp.minimum(i, N_OUT_PROMPT - 1), 0)),
            pl.BlockSpec((TM_OUT, n), lambda i: (r_sample_block, 0)),
        ],
        out_specs=pl.BlockSpec((TM_OUT, n), lambda i: (i, 0)),
        out_shape=jax.ShapeDtypeStruct((t, n), F32),
        scratch_shapes=[
            pltpu.VMEM((k, n), BF16),
            pltpu.VMEM((W_LOAD_ROWS, n), F32),
            pltpu.SemaphoreType.DMA(()),
        ],
        compiler_params=pltpu.CompilerParams(
            dimension_semantics=("arbitrary",), vmem_limit_bytes=V7X_VMEM_LIMIT),
        name="out_proj_residual",
    )(a_prompt, a_sample, w_all, r_prompt, r_sample)


def _swiglu_tile(h, w1_ref, w3_ref, w2_ref):
    a = jnp.dot(h, w1_ref[...].astype(BF16), preferred_element_type=F32)
    b = jnp.dot(h, w3_ref[...].astype(BF16), preferred_element_type=F32)
    u = (_silu(a) * b).astype(BF16)
    return jnp.dot(u, w2_ref[...].astype(BF16), preferred_element_type=F32)


def _ffn_kernel(x_hbm, g_ref, w1_ref, w3_ref, w2_ref, gn_ref, o_ref, hnext_ref, hn_ref, sem):
    @pl.when(pl.program_id(1) == 0)
    def _():
        r0 = pl.multiple_of(pl.program_id(0) * TM, TM)
        cp = pltpu.make_async_copy(x_hbm.at[pl.ds(r0, TM)], o_ref, sem)
        cp.start()
        cp.wait()
        _norm_block_to(o_ref, g_ref, hn_ref)

    o_ref[...] += _swiglu_tile(hn_ref[...], w1_ref, w3_ref, w2_ref)

    @pl.when(pl.program_id(1) == pl.num_programs(1) - 1)
    def _():
        _norm_block_to(o_ref, gn_ref, hnext_ref)


def _ffn_dense(x, g, w1, w3, w2, g_next):
    t, d = x.shape
    f = w1.shape[1]
    return pl.pallas_call(
        _ffn_kernel,
        grid=(t // TM, f // TF),
        in_specs=[
            pl.BlockSpec(memory_space=pl.ANY),
            pl.BlockSpec((1, d), lambda i, j: (0, 0)),
            pl.BlockSpec((d, TF), lambda i, j: (0, j)),
            pl.BlockSpec((d, TF), lambda i, j: (0, j)),
            pl.BlockSpec((TF, d), lambda i, j: (j, 0)),
            pl.BlockSpec((1, d), lambda i, j: (0, 0)),
        ],
        out_specs=[
            pl.BlockSpec((TM, d), lambda i, j: (i, 0)),
            pl.BlockSpec((TM, d), lambda i, j: (i, 0)),
        ],
        out_shape=[jax.ShapeDtypeStruct((t, d), F32), jax.ShapeDtypeStruct((t, d), BF16)],
        scratch_shapes=[pltpu.VMEM((TM, d), BF16), pltpu.SemaphoreType.DMA(())],
        compiler_params=pltpu.CompilerParams(
            dimension_semantics=("arbitrary", "arbitrary"), vmem_limit_bytes=V7X_VMEM_LIMIT),
        name="ffn_dense",
    )(x, g.reshape(1, d), w1, w3, w2, g_next.reshape(1, d))


def _rotary_tables(length, pos0):
    half = RET_D // 2
    inv = ROPE_THETA ** (-jnp.arange(half, dtype=F32) / half)
    pos = jnp.arange(length, dtype=F32) + pos0
    ang = pos[:, None] * inv[None, :]
    cos, sin = jnp.cos(ang), jnp.sin(ang)
    return jnp.concatenate([cos, cos], axis=-1), jnp.concatenate([-sin, sin], axis=-1)


def _decay_tables(c):
    log_g = jnp.log1p(-jnp.exp2(-5.0 - jnp.arange(RET_HEADS, dtype=F32)))
    idx = jnp.arange(c, dtype=F32)
    rel = idx[:, None] - idx[None, :]
    d_in = jnp.where(rel >= 0, jnp.exp(log_g[:, None, None] * jnp.maximum(rel, 0.0)), 0.0)
    d_q = jnp.exp(log_g[:, None] * (idx + 1.0)[None, :])
    d_k = jnp.exp(log_g[:, None] * (c - 1.0 - idx)[None, :])
    d_c = jnp.exp(log_g * c)
    return d_in, d_q, d_k, d_c


HIST31 = 32
HIST3 = 8


PROMPT_CHUNKS_PER_STEP = 4


def _mixer_prompt_kernel(z_ref, cos_ref, sin_ref, din_ref, dq_ref, dk_ref, dc_ref, retg_ref,
                         c31w_ref, c31b_ref, clng_ref, clnb_ref, scw_ref, slng_ref, slnb_ref,
                         sguw_ref, sgub_ref,
                         mix_ref, ret_ref, b31_ref, b3_ref,
                         s_scr, ext31_scr, ext3_scr, ysh_scr):
    c = pl.program_id(1)

    @pl.when(c == 0)
    def _():
        s_scr[...] = jnp.zeros_like(s_scr)
        ext31_scr[pl.ds(0, HIST31), :] = jnp.zeros((HIST31, GROUP_W), F32)
        ext3_scr[pl.ds(0, HIST3), :] = jnp.zeros((HIST3, GROUP_W), F32)

    for cc in range(PROMPT_CHUNKS_PER_STEP):
        _mixer_prompt_chunk(pl.ds(cc * CHUNK, CHUNK), z_ref, cos_ref, sin_ref, din_ref, dq_ref, dk_ref, dc_ref,
                            retg_ref, c31w_ref, c31b_ref, clng_ref, clnb_ref, scw_ref, slng_ref, slnb_ref,
                            sguw_ref, sgub_ref, mix_ref, s_scr, ext31_scr, ext3_scr, ysh_scr)

    @pl.when(c == pl.num_programs(1) - 1)
    def _():
        ret_ref[...] = s_scr[...]
        b31_ref[...] = ext31_scr[pl.ds(HIST31 - (CONV_W - 1), CONV_W - 1), :]
        b3_ref[...] = ext3_scr[pl.ds(HIST3 - (SC_W - 1), SC_W - 1), :]


def _mixer_prompt_chunk(rows, z_ref, cos_ref, sin_ref, din_ref, dq_ref, dk_ref, dc_ref, retg_ref,
                        c31w_ref, c31b_ref, clng_ref, clnb_ref, scw_ref, slng_ref, slnb_ref,
                        sguw_ref, sgub_ref, mix_ref, s_scr, ext31_scr, ext3_scr, ysh_scr):
    cos = cos_ref[rows, :]
    sin = sin_ref[rows, :]

    for h in range(RET_HEADS):
        lo = h * RET_D
        q = z_ref[rows, pl.ds(C_Q + lo, RET_D)]
        k = z_ref[rows, pl.ds(C_K + lo, RET_D)]
        v = z_ref[rows, pl.ds(C_V + lo, RET_D)]
        g = z_ref[rows, pl.ds(C_G + lo, RET_D)]
        q = q * cos + pltpu.roll(q, RET_D // 2, axis=1) * sin
        k = (k * cos + pltpu.roll(k, RET_D // 2, axis=1) * sin) * (RET_D ** -0.5)
        qb = q.astype(BF16)
        vb = v.astype(BF16)
        scores = lax.dot_general(qb, k.astype(BF16), (((1,), (1,)), ((), ())),
                                 preferred_element_type=F32) * din_ref[h]
        inner = jnp.dot(scores.astype(BF16), vb, preferred_element_type=F32)
        s_old = s_scr[h]
        cross = jnp.dot(qb, s_old.astype(BF16), preferred_element_type=F32) * dq_ref[h]
        kd = (k * dk_ref[h]).astype(BF16)
        s_scr[h] = s_old * dc_ref[h] + lax.dot_general(kd, vb, (((0,), (0,)), ((), ())),
                                                       preferred_element_type=F32)
        r = inner + cross
        r = r * lax.rsqrt(jnp.mean(r * r, axis=-1, keepdims=True) + EPS)
        r = r * retg_ref[:, pl.ds(lo, RET_D)]
        mix_ref[rows, pl.ds(lo, RET_D)] = (_silu(g) * r).astype(mix_ref.dtype)

    glu = z_ref[rows, pl.ds(C_BL, GROUP_W)] * jax.nn.sigmoid(z_ref[rows, pl.ds(C_BG, GROUP_W)])
    ext31_scr[pl.ds(HIST31, CHUNK), :] = glu
    s0 = HIST31 - (CONV_W - 1)
    conv = None
    for i in range(SUBLANES):
        n_rows = CHUNK if i == 0 else CHUNK + SUBLANES
        y = None
        for m in range(HIST31 // SUBLANES + 1):
            j = SUBLANES * m + i - s0
            if 0 <= j < CONV_W:
                term = ext31_scr[pl.ds(SUBLANES * m, n_rows), :] * c31w_ref[pl.ds(j, 1), :]
                y = term if y is None else y + term
        if i == 0:
            conv = y
        else:
            ysh_scr[i - 1] = y
            conv = conv + ysh_scr[i - 1, pl.ds(i, CHUNK), :]
    conv = conv + c31b_ref[...]
    out_b = _silu(_layernorm_rows(conv, clng_ref[...], clnb_ref[...]))
    mix_ref[rows, pl.ds(GROUP_W, GROUP_W)] = out_b.astype(mix_ref.dtype)
    ext31_scr[pl.ds(0, HIST31), :] = ext31_scr[pl.ds(CHUNK, HIST31), :]

    gated = z_ref[rows, pl.ds(C_CC, GROUP_W)] * z_ref[rows, pl.ds(C_CH, GROUP_W)]
    ext3_scr[pl.ds(HIST3, CHUNK), :] = gated
    conv3 = ext3_scr[pl.ds(HIST3 - (SC_W - 1), CHUNK), :] * scw_ref[pl.ds(0, 1), :]
    for j in range(1, SC_W):
        conv3 = conv3 + ext3_scr[pl.ds(HIST3 - (SC_W - 1) + j, CHUNK), :] * scw_ref[pl.ds(j, 1), :]
    mix_ref[rows, pl.ds(2 * GROUP_W, GROUP_W)] = (z_ref[rows, pl.ds(C_CB, GROUP_W)] * conv3).astype(mix_ref.dtype)
    ext3_scr[pl.ds(0, HIST3), :] = ext3_scr[pl.ds(CHUNK, HIST3), :]

    vn = _layernorm_rows(z_ref[rows, pl.ds(C_DV, GROUP_W)], slng_ref[...], slnb_ref[...])
    row = lax.broadcasted_iota(jnp.int32, (CHUNK, CHUNK), 0)
    col = lax.broadcasted_iota(jnp.int32, (CHUNK, CHUNK), 1)
    ch = GROUP_W // SGU_GROUPS
    for gi in range(SGU_GROUPS):
        wm = jnp.where(row >= col, sguw_ref[gi], 0.0).astype(BF16)
        mixed = jnp.dot(wm, vn[:, gi * ch:(gi + 1) * ch].astype(BF16), preferred_element_type=F32)
        mixed = mixed + sgub_ref[:, pl.ds(gi, 1)]
        du = z_ref[rows, pl.ds(C_DU + gi * ch, ch)]
        mix_ref[rows, pl.ds(3 * GROUP_W + gi * ch, ch)] = (du * mixed).astype(mix_ref.dtype)


def _mixer_prompt(z, p):
    cos, sin = _rotary_tables(SEQ, 0)
    d_in, d_q, d_k, d_c = _decay_tables(CHUNK)
    dq_t = jnp.broadcast_to(d_q[:, :, None], (RET_HEADS, CHUNK, RET_D))
    dk_t = jnp.broadcast_to(d_k[:, :, None], (RET_HEADS, CHUNK, RET_D))
    dc_t = jnp.broadcast_to(d_c[:, None, None], (RET_HEADS, RET_D, RET_D))
    step_rows = PROMPT_CHUNKS_PER_STEP * CHUNK
    n_steps = SEQ // step_rows

    def full(shape):
        return pl.BlockSpec(shape, lambda b, c: (0,) * len(shape))

    row = lambda a: a.reshape(1, -1)
    return pl.pallas_call(
        _mixer_prompt_kernel,
        grid=(BATCH, n_steps),
        in_specs=[
            pl.BlockSpec((step_rows, D_IN), lambda b, c: (b * n_steps + c, 0)),
            pl.BlockSpec((step_rows, RET_D), lambda b, c: (c, 0)),
            pl.BlockSpec((step_rows, RET_D), lambda b, c: (c, 0)),
            full((RET_HEADS, CHUNK, CHUNK)),
            full((RET_HEADS, CHUNK, RET_D)),
            full((RET_HEADS, CHUNK, RET_D)),
            full((RET_HEADS, RET_D, RET_D)),
            full((1, GROUP_W)),
            full((CONV_W, GROUP_W)),
            full((1, GROUP_W)),
            full((1, GROUP_W)),
            full((1, GROUP_W)),
            full((SC_W, GROUP_W)),
            full((1, GROUP_W)),
            full((1, GROUP_W)),
            full((SGU_GROUPS, CHUNK, CHUNK)),
            full((CHUNK, SGU_GROUPS)),
        ],
        out_specs=[
            pl.BlockSpec((step_rows, D_MODEL), lambda b, c: (b * n_steps + c, 0)),
            pl.BlockSpec((None, RET_HEADS, RET_D, RET_D), lambda b, c: (b, 0, 0, 0)),
            pl.BlockSpec((None, CONV_W - 1, GROUP_W), lambda b, c: (b, 0, 0)),
            pl.BlockSpec((None, SC_W - 1, GROUP_W), lambda b, c: (b, 0, 0)),
        ],
        out_shape=[
            jax.ShapeDtypeStruct((T_PROMPT, D_MODEL), BF16),
            jax.ShapeDtypeStruct((BATCH, RET_HEADS, RET_D, RET_D), F32),
            jax.ShapeDtypeStruct((BATCH, CONV_W - 1, GROUP_W), F32),
            jax.ShapeDtypeStruct((BATCH, SC_W - 1, GROUP_W), F32),
        ],
        scratch_shapes=[
            pltpu.VMEM((RET_HEADS, RET_D, RET_D), F32),
            pltpu.VMEM((HIST31 + CHUNK, GROUP_W), F32),
            pltpu.VMEM((HIST3 + CHUNK, GROUP_W), F32),
            pltpu.VMEM((SUBLANES - 1, CHUNK + SUBLANES, GROUP_W), F32),
        ],
        compiler_params=pltpu.CompilerParams(
            dimension_semantics=("arbitrary", "arbitrary"), vmem_limit_bytes=V7X_VMEM_LIMIT),
        name="mixer_prompt",
    )(z, cos, sin, d_in, dq_t, dk_t, dc_t, row(p["ret_norm_g"]), p["conv31_w"], row(p["conv31_b"]),
      row(p["conv_ln_g"]), row(p["conv_ln_b"]), p["sconv_w"], row(p["sgu_ln_g"]), row(p["sgu_ln_b"]),
      p["sgu_w"], p["sgu_b"].T)


EXT31_S = 40
EXT3_S = 8


N_SAMPLE_IN = 20


def _mixer_sample_stacked_kernel(*refs):
    ins = refs[:N_SAMPLE_IN]
    prev_ret, prev_b31 = refs[N_SAMPLE_IN:N_SAMPLE_IN + 2]
    mix_ref, ret_hbm, b31_hbm, b3_ref, vn_ref = refs[N_SAMPLE_IN + 2:N_SAMPLE_IN + 7]
    ext31_scr, ext3_scr, ret_scr, b31_scr, wr_sem, cp_sem = refs[N_SAMPLE_IN + 7:]
    i = pl.program_id(0)
    n = pl.num_programs(0)
    slot = i % 2
    bt = ret_scr.shape[1]
    blk = pl.ds(pl.multiple_of(i * bt, bt), bt)

    def writes(sl):
        return (pltpu.make_async_copy(ret_scr.at[sl], ret_hbm.at[1, blk], wr_sem.at[sl, 0]),
                pltpu.make_async_copy(b31_scr.at[sl], b31_hbm.at[1, blk], wr_sem.at[sl, 1]))

    copies = (pltpu.make_async_copy(prev_ret, ret_hbm.at[0, blk], cp_sem.at[0]),
              pltpu.make_async_copy(prev_b31, b31_hbm.at[0, blk], cp_sem.at[1]))

    @pl.when(i >= 2)
    def _():
        for w in writes(slot):
            w.wait()

    for c in copies:
        c.start()
    _mixer_sample_kernel(*ins, mix_ref, ret_scr.at[slot], b31_scr.at[slot], b3_ref, vn_ref, ext31_scr, ext3_scr)
    ws = writes(slot)
    for w in ws:
        w.start()
    for c in copies:
        c.wait()

    @pl.when(i == n - 1)
    def _():
        for w in ws:
            w.wait()
        for w in writes(1 - slot):
            w.wait()


def _mixer_sample_kernel(z_ref, cos_ref, sin_ref, din_ref, dq_ref, dk_ref, dc_ref, retg_ref,
                         c31w_ref, c31b_ref, clng_ref, clnb_ref, scw_ref, slng_ref, slnb_ref,
                         sguw_ref, sgub_ref, sret_ref, s31_ref, s3_ref,
                         mix_ref, ret_ref, b31_ref, b3_ref, vn_ref,
                         ext31_scr, ext3_scr):
    cos = cos_ref[...]
    sin = sin_ref[...]
    n_hist31 = CONV_W - 1
    n_hist3 = SC_W - 1
    bt = sret_ref.shape[0]

    def zs(col, width):
        return jnp.stack([z_ref[pl.ds(DEC_SEQ * b, DEC_SEQ), pl.ds(col, width)] for b in range(bt)])

    def zrow(s, col, width):
        return jnp.stack([z_ref[pl.ds(DEC_SEQ * b + s, 1), pl.ds(col, width)] for b in range(bt)])

    def put(col, val):
        for b in range(bt):
            mix_ref[pl.ds(DEC_SEQ * b, DEC_SEQ), pl.ds(col, val.shape[-1])] = val[b]

    for h in range(RET_HEADS):
        lo = h * RET_D
        q = zs(C_Q + lo, RET_D)
        k = zs(C_K + lo, RET_D)
        g = zs(C_G + lo, RET_D)
        q = q * cos + pltpu.roll(q, RET_D // 2, axis=2) * sin
        k = (k * cos + pltpu.roll(k, RET_D // 2, axis=2) * sin) * (RET_D ** -0.5)
        s_old = sret_ref[:, h]
        r = lax.dot_general(q, s_old, (((2,), (1,)), ((0,), (0,))),
                            preferred_element_type=F32) * dq_ref[h]
        for s in range(DEC_SEQ):
            k_s = zrow(s, C_K + lo, RET_D)
            k_s = (k_s * cos[s:s + 1] + pltpu.roll(k_s, RET_D // 2, axis=2) * sin[s:s + 1]) * (RET_D ** -0.5)
            v_s = zrow(s, C_V + lo, RET_D)
            score = jnp.sum(q * k_s, axis=-1, keepdims=True) * din_ref[h, s]
            r = r + score * v_s
        kd = k * dk_ref[h]
        v = zs(C_V + lo, RET_D)
        ret_ref[:, h] = s_old * dc_ref[h] + lax.dot_general(
            kd, v, (((1,), (1,)), ((0,), (0,))), preferred_element_type=F32)
        r = r * lax.rsqrt(jnp.mean(r * r, axis=-1, keepdims=True) + EPS)
        r = r * retg_ref[:, pl.ds(lo, RET_D)]
        put(lo, _silu(g) * r)

    glu = zs(C_BL, GROUP_W) * jax.nn.sigmoid(zs(C_BG, GROUP_W))
    ext31_scr[:, pl.ds(0, n_hist31), :] = s31_ref[...]
    ext31_scr[:, pl.ds(n_hist31, DEC_SEQ), :] = glu
    conv = ext31_scr[:, pl.ds(0, DEC_SEQ), :] * c31w_ref[pl.ds(0, 1), :]
    for j in range(1, CONV_W):
        conv = conv + ext31_scr[:, pl.ds(j, DEC_SEQ), :] * c31w_ref[pl.ds(j, 1), :]
    conv = conv + c31b_ref[...]
    put(GROUP_W, _silu(_layernorm_rows(conv, clng_ref[...], clnb_ref[...])))
    b31_ref[...] = ext31_scr[:, pl.ds(DEC_SEQ, n_hist31), :]

    gated = zs(C_CC, GROUP_W) * zs(C_CH, GROUP_W)
    ext3_scr[:, pl.ds(0, n_hist3), :] = s3_ref[...]
    ext3_scr[:, pl.ds(n_hist3, DEC_SEQ), :] = gated
    conv3 = ext3_scr[:, pl.ds(0, DEC_SEQ), :] * scw_ref[pl.ds(0, 1), :]
    for j in range(1, SC_W):
        conv3 = conv3 + ext3_scr[:, pl.ds(j, DEC_SEQ), :] * scw_ref[pl.ds(j, 1), :]
    put(2 * GROUP_W, zs(C_CB, GROUP_W) * conv3)
    b3_ref[...] = ext3_scr[:, pl.ds(DEC_SEQ, n_hist3), :]

    vn = _layernorm_rows(zs(C_DV, GROUP_W), slng_ref[...], slnb_ref[...])
    vn_ref[...] = vn
    mixed = sgub_ref[...]
    for s in range(DEC_SEQ):
        mixed = mixed + sguw_ref[s] * vn_ref[:, pl.ds(s, 1), :]
    put(3 * GROUP_W, zs(C_DU, GROUP_W) * mixed)


def _mixer_sample(z, p, layer, s_ret, s_31, s_3, prev=None):
    cos, sin = _rotary_tables(DEC_SEQ, PAST_LEN)
    d_in, d_q, d_k, d_c = _decay_tables(DEC_SEQ)
    din_t = jnp.swapaxes(d_in, 1, 2)[:, :, :, None]
    dq_t = jnp.broadcast_to(d_q[:, :, None], (RET_HEADS, DEC_SEQ, RET_D))
    dk_t = jnp.broadcast_to(d_k[:, :, None], (RET_HEADS, DEC_SEQ, RET_D))
    dc_t = jnp.broadcast_to(d_c[:, None, None], (RET_HEADS, RET_D, RET_D))
    ch = GROUP_W // SGU_GROUPS
    w4 = jnp.tril(p["sgu_w"][:, :DEC_SEQ, :DEC_SEQ])
    w_t = jnp.repeat(jnp.transpose(w4, (2, 1, 0)), ch, axis=-1)
    b_t = jnp.repeat(p["sgu_b"][:, :DEC_SEQ].T, ch, axis=-1)
    bt = SAMPLE_BT

    def full(shape):
        return pl.BlockSpec(shape, lambda i: (0,) * len(shape))

    row = lambda a: a.reshape(1, -1)
    any_spec = pl.BlockSpec(memory_space=pl.ANY)
    ret_blk, b31_blk = (bt, RET_HEADS, RET_D, RET_D), (bt, CONV_W - 1, GROUP_W)
    ret_all, b31_all = (DEC_BATCH,) + ret_blk[1:], (DEC_BATCH,) + b31_blk[1:]
    stacked = prev is not None
    return pl.pallas_call(
        _mixer_sample_stacked_kernel if stacked else _mixer_sample_kernel,
        grid=(DEC_BATCH // bt,),
        in_specs=[
            pl.BlockSpec((bt * DEC_SEQ, D_IN), lambda i: (T_PROMPT // (bt * DEC_SEQ) + i, 0)),
            full((DEC_SEQ, RET_D)),
            full((DEC_SEQ, RET_D)),
            full((RET_HEADS, DEC_SEQ, DEC_SEQ, 1)),
            full((RET_HEADS, DEC_SEQ, RET_D)),
            full((RET_HEADS, DEC_SEQ, RET_D)),
            full((RET_HEADS, RET_D, RET_D)),
            full((1, GROUP_W)),
            full((CONV_W, GROUP_W)),
            full((1, GROUP_W)),
            full((1, GROUP_W)),
            full((1, GROUP_W)),
            full((SC_W, GROUP_W)),
            full((1, GROUP_W)),
            full((1, GROUP_W)),
            full((DEC_SEQ, DEC_SEQ, GROUP_W)),
            full((DEC_SEQ, GROUP_W)),
            pl.BlockSpec((None, bt, RET_HEADS, RET_D, RET_D), lambda i: (layer, i, 0, 0, 0)),
            pl.BlockSpec((None, bt, CONV_W - 1, GROUP_W), lambda i: (layer, i, 0, 0)),
            pl.BlockSpec((None, bt, SC_W - 1, GROUP_W), lambda i: (layer, i, 0, 0)),
        ] + ([pl.BlockSpec(ret_blk, lambda i: (i, 0, 0, 0)), pl.BlockSpec(b31_blk, lambda i: (i, 0, 0))]
             if stacked else []),
        out_specs=[
            pl.BlockSpec((bt * DEC_SEQ, D_MODEL), lambda i: (i, 0)),
            any_spec if stacked else pl.BlockSpec(ret_blk, lambda i: (i, 0, 0, 0)),
            any_spec if stacked else pl.BlockSpec(b31_blk, lambda i: (i, 0, 0)),
            pl.BlockSpec((bt, SC_W - 1, GROUP_W), lambda i: (i, 0, 0)),
            pl.BlockSpec((bt, DEC_SEQ, GROUP_W), lambda i: (i, 0, 0)),
        ],
        out_shape=[
            jax.ShapeDtypeStruct((T_SAMPLE, D_MODEL), F32),
            jax.ShapeDtypeStruct(((DEPTH,) if stacked else ()) + ret_all, F32),
            jax.ShapeDtypeStruct(((DEPTH,) if stacked else ()) + b31_all, F32),
            jax.ShapeDtypeStruct((DEC_BATCH, SC_W - 1, GROUP_W), F32),
            jax.ShapeDtypeStruct((DEC_BATCH, DEC_SEQ, GROUP_W), F32),
        ],
        scratch_shapes=[
            pltpu.VMEM((bt, EXT31_S, GROUP_W), F32),
            pltpu.VMEM((bt, EXT3_S, GROUP_W), F32),
        ] + ([
            pltpu.VMEM((2,) + ret_blk, F32),
            pltpu.VMEM((2,) + b31_blk, F32),
            pltpu.SemaphoreType.DMA((2, 2)),
            pltpu.SemaphoreType.DMA((2,)),
        ] if stacked else []),
        compiler_params=pltpu.CompilerParams(
            dimension_semantics=("arbitrary",), vmem_limit_bytes=V7X_VMEM_LIMIT),
        name="mixer_sample",
    )(z, cos, sin, din_t, dq_t, dk_t, dc_t, row(p["ret_norm_g"]), p["conv31_w"], row(p["conv31_b"]),
      row(p["conv_ln_g"]), row(p["conv_ln_b"]), p["sconv_w"], row(p["sgu_ln_g"]), row(p["sgu_ln_b"]),
      w_t, b_t, s_ret, s_31, s_3, *(prev if stacked else ()))


def _to_token_major(dst, val, pitch=TOK_ROWS):
    n = val.shape[0]
    for a in range(TOK_ROWS):
        dst[pl.ds(a, n, stride=pitch), :] = val[:, a * LANES:(a + 1) * LANES]


STAGE_PITCH = 24
DMA_UNROLL = 8


def _router_kernel(x_ref, g_ref, rw_ref, e_ref, p_ref, c_ref):
    h = _rms_rows(x_ref[...], g_ref[...])
    lane = lax.broadcasted_iota(jnp.int32, (h.shape[0], N_EXPERTS), 1)
    logits = jnp.zeros((h.shape[0], N_EXPERTS), F32)
    for e in range(N_EXPERTS):
        col = jnp.sum(h * rw_ref[pl.ds(e, 1), :], axis=-1, keepdims=True)
        logits = jnp.where(lane == e, col, logits)
    m1 = jnp.max(logits, axis=-1, keepdims=True)
    i1 = jnp.min(jnp.where(logits == m1, lane, N_EXPERTS), axis=-1, keepdims=True)
    rest = jnp.where(lane == i1, -jnp.inf, logits)
    m2 = jnp.max(rest, axis=-1, keepdims=True)
    i2 = jnp.min(jnp.where(rest == m2, lane, N_EXPERTS), axis=-1, keepdims=True)
    e2 = jnp.exp(m2 - m1)
    den = 1.0 + e2
    e_ref[...] = jnp.where(lane == 0, i1, jnp.where(lane == 1, i2, 0))
    p_ref[...] = jnp.where(lane == 0, 1.0 / den, jnp.where(lane == 1, e2 / den, 0.0))
    c_ref[pl.ds(0, 1), :] = jnp.sum(jnp.where(lane == i1, 1.0, 0.0), axis=0, keepdims=True).astype(jnp.int32)
    c_ref[pl.ds(1, 1), :] = jnp.sum(jnp.where(lane == i2, 1.0, 0.0), axis=0, keepdims=True).astype(jnp.int32)


def _router(x, g, rw):
    t, d = x.shape
    return pl.pallas_call(
        _router_kernel,
        grid=(N_TB,),
        in_specs=[
            pl.BlockSpec((MOE_TB, d), lambda i: (i, 0)),
            pl.BlockSpec((1, d), lambda i: (0, 0)),
            pl.BlockSpec((N_EXPERTS, d), lambda i: (0, 0)),
        ],
        out_specs=[
            pl.BlockSpec((MOE_TB, N_EXPERTS), lambda i: (i, 0)),
            pl.BlockSpec((MOE_TB, N_EXPERTS), lambda i: (i, 0)),
            pl.BlockSpec((None, TOP_K, N_EXPERTS), lambda i: (i, 0, 0)),
        ],
        out_shape=[
            jax.ShapeDtypeStruct((t, N_EXPERTS), jnp.int32),
            jax.ShapeDtypeStruct((t, N_EXPERTS), F32),
            jax.ShapeDtypeStruct((N_TB, TOP_K, N_EXPERTS), jnp.int32),
        ],
        compiler_params=pltpu.CompilerParams(
            dimension_semantics=("arbitrary",), vmem_limit_bytes=V7X_VMEM_LIMIT),
        name="moe_router",
    )(x, g.reshape(1, d), rw.T)


def _pick(table, idx):
    lanes = jnp.arange(N_EXPERTS, dtype=jnp.int32)
    return jnp.sum(jnp.where(idx[..., None] == lanes, table, 0), axis=-1)


def _expert_layout(cnt):
    tot = jnp.sum(cnt, axis=(0, 1))
    start = jnp.cumsum(tot) - tot
    blk = jnp.sum(cnt, axis=1)
    blk_base = start[None, :] + jnp.cumsum(blk, axis=0) - blk
    base = jnp.stack([blk_base, blk_base + cnt[:, 0]], axis=1)
    n_sb = (tot + SB - 1) // SB
    sb_end = jnp.cumsum(n_sb)
    n_used = sb_end[-1]
    s_idx = jnp.arange(NS_MAX, dtype=jnp.int32)
    s_cl = jnp.minimum(s_idx, n_used - 1)
    sb_e = jnp.sum(s_cl[:, None] >= sb_end[None, :], axis=-1).astype(jnp.int32)
    local = s_cl - _pick(sb_end - n_sb, sb_e)
    sb_row0 = _pick(start, sb_e) + local * SB
    sb_rows = jnp.where(s_idx < n_used, jnp.clip(_pick(tot, sb_e) - local * SB, 0, SB), 0)
    i32 = lambda a: a.astype(jnp.int32)
    return i32(base), sb_e, i32(sb_row0), i32(sb_rows), i32(n_used).reshape(1)


def _dest_kernel(e_ref, b_ref, d_ref):
    eid = e_ref[...]
    lane = lax.broadcasted_iota(jnp.int32, eid.shape, 1)
    row = lax.broadcasted_iota(jnp.int32, (MOE_TB, MOE_TB), 0)
    col = lax.broadcasted_iota(jnp.int32, (MOE_TB, MOE_TB), 1)
    earlier = jnp.where(row > col, 1.0, 0.0).astype(BF16)
    base = b_ref[...].astype(F32)

    def dest_of(k):
        mine = lane == eid[:, k:k + 1]
        before = jnp.dot(earlier, jnp.where(mine, 1.0, 0.0).astype(BF16), preferred_element_type=F32)
        return jnp.sum(jnp.where(mine, before + base[k:k + 1], 0.0), axis=-1, keepdims=True)

    d_ref[...] = jnp.where(lane == 0, dest_of(0), jnp.where(lane == 1, dest_of(1), 0.0)).astype(jnp.int32)


def _dest_rows(eid, base):
    return pl.pallas_call(
        _dest_kernel,
        grid=(N_TB,),
        in_specs=[
            pl.BlockSpec((MOE_TB, N_EXPERTS), lambda i: (i, 0)),
            pl.BlockSpec((None, TOP_K, N_EXPERTS), lambda i: (i, 0, 0)),
        ],
        out_specs=pl.BlockSpec((MOE_TB, N_EXPERTS), lambda i: (i, 0)),
        out_shape=jax.ShapeDtypeStruct((T_ALL, N_EXPERTS), jnp.int32),
        compiler_params=pltpu.CompilerParams(
            dimension_semantics=("arbitrary",), vmem_limit_bytes=V7X_VMEM_LIMIT),
        name="moe_dest_rows",
    )(eid, base)


def _dispatch_kernel(d_ref, x_ref, g_ref, xg_hbm, stage, zeros, sem, zsem):
    i = pl.program_id(0)
    n = pl.num_programs(0)
    slot = i % 2
    blk_rows = MOE_TB * TOK_ROWS

    def wait_slot(sl):
        for _ in range(TOP_K):
            pltpu.make_async_copy(stage.at[sl, pl.ds(0, blk_rows)], xg_hbm.at[pl.ds(0, blk_rows)], sem.at[sl]).wait()

    @pl.when(i >= 2)
    def _():
        wait_slot(slot)

    _to_token_major(stage.at[slot], _rms_rows(x_ref[...], g_ref[...]), STAGE_PITCH)

    def body(r, carry):
        src = stage.at[slot, pl.ds(pl.multiple_of(r * STAGE_PITCH, 8), TOK_ROWS)]
        for k in range(TOP_K):
            row = d_ref[0, 0, TOP_K * r + k]
            dst = xg_hbm.at[pl.ds(pl.multiple_of(row * TOK_ROWS, TOK_ROWS), TOK_ROWS)]
            pltpu.make_async_copy(src, dst, sem.at[slot]).start()
        return carry

    lax.fori_loop(0, MOE_TB, body, 0, unroll=DMA_UNROLL)

    @pl.when(i == n - 1)
    def _():
        zeros[...] = jnp.zeros_like(zeros)
        tail = [pltpu.make_async_copy(zeros, xg_hbm.at[pl.ds((N_SLOTS + q * (SUB // TOK_ROWS)) * TOK_ROWS, SUB)], zsem)
                for q in range(TOK_ROWS)]
        for cp in tail:
            cp.start()
        for cp in tail:
            cp.wait()
        wait_slot(slot)
        wait_slot(1 - slot)


def _dispatch(x, g, dest_flat):
    t, d = x.shape
    return pl.pallas_call(
        _dispatch_kernel,
        grid=(N_TB,),
        in_specs=[
            pl.BlockSpec((1, 1, TOP_K * MOE_TB), lambda i: (i, 0, 0), memory_space=pltpu.SMEM),
            pl.BlockSpec((MOE_TB, d), lambda i: (i, 0)),
            pl.BlockSpec((1, d), lambda i: (0, 0)),
        ],
        out_specs=pl.BlockSpec(memory_space=pl.ANY),
        out_shape=jax.ShapeDtypeStruct(((N_SLOTS + SUB) * TOK_ROWS, LANES), F32),
        scratch_shapes=[
            pltpu.VMEM((2, MOE_TB * STAGE_PITCH, LANES), F32),
            pltpu.VMEM((SUB, LANES), F32),
            pltpu.SemaphoreType.DMA((2,)),
            pltpu.SemaphoreType.DMA(()),
        ],
        compiler_params=pltpu.CompilerParams(
            dimension_semantics=("arbitrary",), vmem_limit_bytes=V7X_VMEM_LIMIT),
        name="moe_dispatch",
    )(dest_flat, x, g.reshape(1, d))


def _moe_ffn_kernel(se_ref, r0_ref, nr_ref, nu_ref, xg_hbm, w1_ref, w3_ref, w2_ref, y_hbm,
                    x_scr, acc, stage, w1b, w3b, w2b, sem_in, sem_out):
    s = pl.program_id(0)
    j = pl.program_id(1)
    nj = pl.num_programs(1)
    rows = nr_ref[s]
    row0 = r0_ref[s]
    nsub = lax.shift_right_logical(rows + (SUB - 1), int(math.log2(SUB)))
    sub_rows = SUB * TOK_ROWS

    is_middle = (j > 0) & (j < nj - 1)

    def cast_weights():
        ws = (w1_ref[...].astype(BF16), w3_ref[...].astype(BF16), w2_ref[...].astype(BF16))
        w1b[...], w3b[...], w2b[...] = ws
        return ws

    @pl.when((nsub > 0) & jnp.logical_not(is_middle))
    def _():
        cast_weights()

    def sub(r):
        return pl.ds(pl.multiple_of(r * SUB, SUB), SUB)

    def swiglu(xs, ws=None):
        w1, w3, w2 = ws if ws is not None else (w1b[...], w3b[...], w2b[...])
        a = jnp.dot(xs, w1, preferred_element_type=F32)
        b = jnp.dot(xs, w3, preferred_element_type=F32)
        u = (_silu(a) * b).astype(BF16)
        return jnp.dot(u, w2, preferred_element_type=F32)

    def tile(r):
        return swiglu(x_scr[sub(r), :])

    def in_copy(r, slot):
        first = pl.multiple_of((row0 + r * SUB) * TOK_ROWS, TOK_ROWS)
        return pltpu.make_async_copy(xg_hbm.at[pl.ds(first, sub_rows)], stage.at[slot], sem_in.at[slot])

    @pl.when(j == 0)
    def _():
        @pl.when(nsub > 0)
        def _():
            in_copy(0, 0).start()

        def body(r, carry):
            slot = r % 2
            in_copy(r, slot).wait()

            @pl.when(r + 1 < nsub)
            def _():
                in_copy(r + 1, 1 - slot).start()

            for a in range(TOK_ROWS):
                x_scr[sub(r), pl.ds(a * LANES, LANES)] = stage[slot, pl.ds(a, SUB, stride=TOK_ROWS), :].astype(BF16)
            acc[sub(r), :] = tile(r)
            return carry

        lax.fori_loop(0, nsub, body, 0)

        def clear(r, carry):
            x_scr[sub(r), :] = jnp.zeros((SUB, D_MODEL), BF16)
            acc[sub(r), :] = jnp.zeros((SUB, D_MODEL), F32)
            return carry

        lax.fori_loop(nsub, SB // SUB, clear, 0)

    @pl.when(is_middle)
    def _():
        for c in range(SB // CHUNK_ROWS):
            @pl.when(rows > c * CHUNK_ROWS)
            def _():
                ws = cast_weights() if c == 0 else None
                rows_c = pl.ds(c * CHUNK_ROWS, CHUNK_ROWS)
                acc[rows_c, :] += swiglu(x_scr[rows_c, :], ws)

    @pl.when(j == nj - 1)
    def _():
        def out_copies(r, act):
            nv = jnp.clip(rows - r * SUB, 0, SUB)
            bit = SUB
            while bit >= 1:
                off = nv & (-2 * bit)
                src = stage.at[0, pl.ds(pl.multiple_of(off * TOK_ROWS, TOK_ROWS), bit * TOK_ROWS)]
                first = pl.multiple_of((row0 + r * SUB + off) * TOK_ROWS, TOK_ROWS)
                cp = pltpu.make_async_copy(src, y_hbm.at[pl.ds(first, bit * TOK_ROWS)], sem_out)
                pl.when((nv & bit) != 0)(functools.partial(act, cp))
                bit //= 2

        def body(r, carry):
            y = acc[sub(r), :] + tile(r)

            @pl.when(r > 0)
            def _():
                out_copies(r - 1, lambda cp: cp.wait())

            _to_token_major(stage.at[0], y)
            out_copies(r, lambda cp: cp.start())
            return carry

        lax.fori_loop(0, nsub, body, 0)

        @pl.when(nsub > 0)
        def _():
            out_copies(nsub - 1, lambda cp: cp.wait())


def _moe_ffn(xg, w1, w3, w2, sb_e, sb_row0, sb_rows, n_used):
    d = w1.shape[1]
    f = w1.shape[2]
    nj = f // TF

    def jmap(s, j, nu):
        return jnp.where(s < nu[0], j, nj - 1)

    return pl.pallas_call(
        _moe_ffn_kernel,
        grid_spec=pltpu.PrefetchScalarGridSpec(
            num_scalar_prefetch=4,
            grid=(NS_MAX, nj),
            in_specs=[
                pl.BlockSpec(memory_space=pl.ANY),
                pl.BlockSpec((None, d, TF), lambda s, j, se, r0, nr, nu: (se[s], 0, jmap(s, j, nu))),
                pl.BlockSpec((None, d, TF), lambda s, j, se, r0, nr, nu: (se[s], 0, jmap(s, j, nu))),
                pl.BlockSpec((None, TF, d), lambda s, j, se, r0, nr, nu: (se[s], jmap(s, j, nu), 0)),
            ],
            out_specs=pl.BlockSpec(memory_space=pl.ANY),
            scratch_shapes=[
                pltpu.VMEM((SB, d), BF16),
                pltpu.VMEM((SB, d), F32),
                pltpu.VMEM((2, SUB * TOK_ROWS, LANES), F32),
                pltpu.VMEM((d, TF), BF16),
                pltpu.VMEM((d, TF), BF16),
                pltpu.VMEM((TF, d), BF16),
                pltpu.SemaphoreType.DMA((2,)),
                pltpu.SemaphoreType.DMA(()),
            ],
        ),
        out_shape=jax.ShapeDtypeStruct((N_SLOTS * TOK_ROWS, LANES), F32),
        compiler_params=pltpu.CompilerParams(
            dimension_semantics=("arbitrary", "arbitrary"), vmem_limit_bytes=V7X_VMEM_LIMIT),
        name="moe_ffn",
    )(sb_e, sb_row0, sb_rows, n_used, xg, w1, w3, w2)


N_TB_PROMPT = T_PROMPT // MOE_TB


def _combine_kernel(d_ref, dn_ref, x_ref, p_ref, g_ref, y_hbm, op_ref, os_ref, stage, sem, *, final_norm):
    i = pl.program_id(0)
    n = pl.num_programs(0)
    slot = i % 2
    blk_rows = MOE_TB * TOK_ROWS

    def start_block(dref, sl):
        def body(r, carry):
            for k in range(TOP_K):
                row = dref[0, 0, TOP_K * r + k]
                src = y_hbm.at[pl.ds(pl.multiple_of(row * TOK_ROWS, TOK_ROWS), TOK_ROWS)]
                dst = stage.at[sl, k, pl.ds(pl.multiple_of(r * STAGE_PITCH, 8), TOK_ROWS)]
                pltpu.make_async_copy(src, dst, sem.at[sl]).start()
            return carry

        lax.fori_loop(0, MOE_TB, body, 0, unroll=DMA_UNROLL)

    @pl.when(i == 0)
    def _():
        start_block(d_ref, 0)

    @pl.when(i + 1 < n)
    def _():
        start_block(dn_ref, 1 - slot)

    for k in range(TOP_K):
        pltpu.make_async_copy(y_hbm.at[pl.ds(0, blk_rows)], stage.at[slot, k, pl.ds(0, blk_rows)], sem.at[slot]).wait()
    g0 = p_ref[:, 0:1]
    g1 = p_ref[:, 1:2]
    pieces = []
    for a in range(TOK_ROWS):
        y0 = stage[slot, 0, pl.ds(a, MOE_TB, stride=STAGE_PITCH), :]
        y1 = stage[slot, 1, pl.ds(a, MOE_TB, stride=STAGE_PITCH), :]
        pieces.append(x_ref[:, pl.ds(a * LANES, LANES)] + (y0 * g0 + y1 * g1))
    y = jnp.concatenate(pieces, axis=1)
    if final_norm:
        y = _rms_rows(y, g_ref[...])

    @pl.when(i < N_TB_PROMPT)
    def _():
        op_ref[...] = y

    @pl.when(i >= N_TB_PROMPT)
    def _():
        os_ref[...] = y


def _moe_combine(x, y, dest_flat, gate, g, final_norm):
    t, d = x.shape
    return pl.pallas_call(
        functools.partial(_combine_kernel, final_norm=final_norm),
        grid=(N_TB,),
        in_specs=[
            pl.BlockSpec((1, 1, TOP_K * MOE_TB), lambda i: (i, 0, 0), memory_space=pltpu.SMEM),
            pl.BlockSpec((1, 1, TOP_K * MOE_TB), lambda i: (jnp.minimum(i + 1, N_TB - 1), 0, 0),
                         memory_space=pltpu.SMEM),
            pl.BlockSpec((MOE_TB, d), lambda i: (i, 0)),
            pl.BlockSpec((MOE_TB, N_EXPERTS), lambda i: (i, 0)),
            pl.BlockSpec((1, d), lambda i: (0, 0)),
            pl.BlockSpec(memory_space=pl.ANY),
        ],
        out_specs=[
            pl.BlockSpec((MOE_TB, d), lambda i: (jnp.minimum(i, N_TB_PROMPT - 1), 0)),
            pl.BlockSpec((MOE_TB, d), lambda i: (jnp.maximum(i - N_TB_PROMPT, 0), 0)),
        ],
        out_shape=[
            jax.ShapeDtypeStruct((T_PROMPT, d), F32),
            jax.ShapeDtypeStruct((T_SAMPLE, d), F32),
        ],
        scratch_shapes=[
            pltpu.VMEM((2, TOP_K, MOE_TB * STAGE_PITCH, LANES), F32),
            pltpu.SemaphoreType.DMA((2,)),
        ],
        compiler_params=pltpu.CompilerParams(
            dimension_semantics=("arbitrary",), vmem_limit_bytes=V7X_VMEM_LIMIT),
        name="moe_combine",
    )(dest_flat, dest_flat, x, gate, g.reshape(1, d), y)


def _moe(x, norm_g, rw, w1, w3, w2, out_g, final_norm):
    eid, gate, cnt = _router(x, norm_g, rw)
    base, sb_e, sb_row0, sb_rows, n_used = _expert_layout(cnt)
    dest = _dest_rows(eid, base)
    dest_flat = dest[:, :TOP_K].reshape(N_TB, 1, TOP_K * MOE_TB)
    xg = _dispatch(x, norm_g, dest_flat)
    y = _moe_ffn(xg, w1, w3, w2, sb_e, sb_row0, sb_rows, n_used)
    return _moe_combine(x, y, dest_flat, gate, out_g, final_norm)


def kernel(x_prompt, x_sample, state_ret, state_conv31, state_conv3, mix_norm_g, w_in, ret_norm_g, conv31_w, conv31_b, conv_ln_g, conv_ln_b, sconv_w, sgu_ln_g, sgu_ln_b, sgu_w, sgu_b, w_out, ffn_norm_g, dense_w1, dense_w3, dense_w2, router_w, moe_w1, moe_w3, moe_w2, final_norm_g):
    assert DEPTH == 2, "layer 0 has the dense channel mixer; layer 1's expert mixer ends with the final norm"
    xp2, xs2 = x_prompt.reshape(T_PROMPT, D_MODEL), x_sample.reshape(T_SAMPLE, D_MODEL)
    hn = _prep(xp2, xs2, mix_norm_g[0])
    res = (xp2, xs2, 0)
    rets_p, c31_p, c3_p, c3_s, vns = [], [], [], [], []
    ret_s = b31_s = None
    for l in range(DEPTH):
        p = dict(ret_norm_g=ret_norm_g[l], conv31_w=conv31_w[l], conv31_b=conv31_b[l],
                 conv_ln_g=conv_ln_g[l], conv_ln_b=conv_ln_b[l], sconv_w=sconv_w[l],
                 sgu_ln_g=sgu_ln_g[l], sgu_ln_b=sgu_ln_b[l], sgu_w=sgu_w[l], sgu_b=sgu_b[l])
        z = _in_proj(hn, w_in, l)
        mix_s, ret_s, b31_s, b3_s, vn_s = _mixer_sample(
            z, p, l, state_ret, state_conv31, state_conv3, prev=(ret_s, b31_s) if l == DEPTH - 1 else None)
        mix_p, ret_p, b31_p, b3_p = _mixer_prompt(z, p)
        x = _out_proj_residual(mix_p, mix_s, w_out, l, *res)
        j = l // 2
        if l % 2 == 0:
            x, hn = _ffn_dense(x, ffn_norm_g[l], dense_w1[j], dense_w3[j], dense_w2[j], mix_norm_g[l + 1])
            res = (x, x, N_OUT_PROMPT)
        else:
            y_p, y_s = _moe(x, ffn_norm_g[l], router_w[j], moe_w1[j], moe_w3[j], moe_w2[j],
                            final_norm_g, final_norm=True)
        rets_p.append(ret_p)
        c31_p.append(b31_p)
        c3_p.append(b3_p); c3_s.append(b3_s)
        vns.append(vn_s)
    y_prompt = y_p.reshape(BATCH, SEQ, D_MODEL)
    y_sample = y_s.reshape(DEC_BATCH, DEC_SEQ, D_MODEL)
    return (y_prompt, y_sample, jnp.stack(rets_p), ret_s, jnp.stack(c31_p), b31_s,
            jnp.stack(c3_p), jnp.stack(c3_s), jnp.stack(vns))
```
